```python
import math
import functools
import jax
import jax.numpy as jnp
from jax import lax
import numpy as np

D_MODEL = 1024
BATCH = 2
SEQ = 8192
DEPTH = 2

GRID_W = 64
CTX_LEN = 256

HEAD_DIM = 64
GROUP_HEADS = 4
GROUP_WIDTH = GROUP_HEADS * HEAD_DIM
N_GROUPS = 4
MIX_WIDTH = N_GROUPS * GROUP_WIDTH

DA_HEADS = 4
DA_QK_DIM = 32
DA_V_DIM = 64
DA_BLOCK = 128

SWA_HEADS = 4
SWA_KV_HEADS = 2
SWA_DIM = 64
SWA_WINDOW = 128
SWA_BLOCK = SWA_WINDOW

NA_HEADS = 4
NA_DIM = 64
NA_WIN_ROWS = 8
NA_WIN_COLS = 16
NA_COL_BLOCK = 16
NA_BAND_COLS = 2 * NA_WIN_COLS

RET_HEADS = 4
RET_QK_DIM = 64
RET_V_DIM = 64
RET_CHUNK = 128

DA_COLS = DA_HEADS * (4 * DA_QK_DIM + DA_V_DIM)
SWA_COLS = (SWA_HEADS + 2 * SWA_KV_HEADS) * SWA_DIM
NA_COLS = 3 * NA_HEADS * NA_DIM
RET_COLS = RET_HEADS * (2 * RET_QK_DIM + 2 * RET_V_DIM)
IN_WIDTH = DA_COLS + SWA_COLS + NA_COLS + RET_COLS
IN_CUTS = (DA_COLS, DA_COLS + SWA_COLS, DA_COLS + SWA_COLS + NA_COLS)

ROPE_BASE = 10000.0
FFN_DIM = 2816
N_EXPERTS = 8
TOP_K = 2
EXPERT_DIM = 3584
NORM_EPS = 1e-6
NEG_INF = -1e30

kernel_name = 'hybrid_dit_parallel_heads_block'


def _rmsnorm(x, g=None):
    xf = x.astype(jnp.float32)
    y = xf * lax.rsqrt(jnp.mean(xf * xf, axis=-1, keepdims=True) + NORM_EPS)
    if g is not None:
        y = y * g.astype(jnp.float32)
    return y.astype(x.dtype)


def _rope_axis(x, pos):
    dh = x.shape[-1]
    half = dh // 2
    inv = ROPE_BASE ** (-jnp.arange(half, dtype=jnp.float32) * 2.0 / dh)
    ang = pos[:, None] * inv[None, :]
    shape = (pos.shape[0],) + (1,) * (x.ndim - 3) + (half,)
    cos = jnp.cos(ang).reshape(shape).astype(x.dtype)
    sin = jnp.sin(ang).reshape(shape).astype(x.dtype)
    x1, x2 = x[..., :half], x[..., half:]
    return jnp.concatenate([x1 * cos - x2 * sin, x1 * sin + x2 * cos], axis=-1)


def _rope2d(x, row, col):
    h = x.shape[-1] // 2
    return jnp.concatenate([_rope_axis(x[..., :h], row), _rope_axis(x[..., h:], col)], axis=-1)


def _diff_attention(p, pc, row, col, lq1, lk1, lq2, lk2, subln_g, lambda_init, with_ctx):
    B, S, _ = p.shape
    nq = DA_HEADS * 2 * DA_QK_DIM

    def split(t):
        n = t.shape[1]
        q = t[..., :nq].reshape(B, n, DA_HEADS, 2, DA_QK_DIM)
        k = t[..., nq:2 * nq].reshape(B, n, DA_HEADS, 2, DA_QK_DIM)
        v = t[..., 2 * nq:].reshape(B, n, DA_HEADS, DA_V_DIM)
        return q, k, v

    q, k, v = split(p)
    q = _rope2d(q, row, col)
    k = _rope2d(k, row, col)
    qc, kc, vc = split(pc)
    lam = (jnp.exp(jnp.sum(lq1 * lk1)) - jnp.exp(jnp.sum(lq2 * lk2))).astype(jnp.float32) + lambda_init
    scale = DA_QK_DIM ** -0.5

    def attend(qb, keys, vals):
        s = jnp.einsum('bqhtd,bkhtd->bhtqk', qb, keys) * scale
        pr = jax.nn.softmax(s, axis=-1)
        a = pr[:, :, 0] - lam * pr[:, :, 1]
        return jnp.einsum('bhqk,bkhd->bqhd', a, vals)

    k_all = jnp.concatenate([kc, k], axis=1)
    v_all = jnp.concatenate([vc, v], axis=1)
    nb = S // DA_BLOCK
    q_blocks = jnp.swapaxes(q.reshape(B, nb, DA_BLOCK, DA_HEADS, 2, DA_QK_DIM), 0, 1)
    o = lax.map(lambda qb: attend(qb, k_all, v_all), q_blocks)
    o = jnp.swapaxes(o, 0, 1).reshape(B, S, DA_HEADS, DA_V_DIM)

    def head_norm(t):
        return (_rmsnorm(t, subln_g) * (1.0 - lambda_init)).reshape(t.shape[0], t.shape[1], DA_HEADS * DA_V_DIM)

    y = head_norm(o)
    yc = head_norm(attend(qc, kc, vc)) if with_ctx else None
    return y, yc


def _sink_softmax(s, sink):
    m = jnp.maximum(jnp.max(s, axis=-1, keepdims=True), sink)
    e = jnp.exp(s - m)
    return e / (jnp.sum(e, axis=-1, keepdims=True) + jnp.exp(sink - m))


def _window_gqa(p, pc, row, col, sink, with_ctx):
    B, S, _ = p.shape
    G = SWA_HEADS // SWA_KV_HEADS
    nq = SWA_HEADS * SWA_DIM
    nk = SWA_KV_HEADS * SWA_DIM

    def split(t):
        n = t.shape[1]
        q = t[..., :nq].reshape(B, n, SWA_KV_HEADS, G, SWA_DIM)
        k = t[..., nq:nq + nk].reshape(B, n, SWA_KV_HEADS, SWA_DIM)
        v = t[..., nq + nk:].reshape(B, n, SWA_KV_HEADS, SWA_DIM)
        return q, k, v

    q, k, v = split(p)
    q = _rope2d(q, row, col)
    k = _rope2d(k, row, col)
    qc, kc, vc = split(pc)
    sink = sink.astype(jnp.float32).reshape(SWA_KV_HEADS, G, 1, 1)
    scale = SWA_DIM ** -0.5
    nb = S // SWA_BLOCK
    kq = 3 * SWA_BLOCK

    def band(t):
        tp = jnp.pad(t, ((0, 0), (SWA_WINDOW, SWA_WINDOW), (0, 0), (0, 0)))
        tp = tp.reshape(B, nb + 2, SWA_BLOCK, SWA_KV_HEADS, SWA_DIM)
        return jnp.concatenate([tp[:, :-2], tp[:, 1:-1], tp[:, 2:]], axis=2)

    kb, vb = band(k), band(v)
    blk = jnp.arange(nb)[:, None]
    qpos = blk * SWA_BLOCK + jnp.arange(SWA_BLOCK)[None, :]
    kpos = blk * SWA_BLOCK - SWA_WINDOW + jnp.arange(kq)[None, :]
    valid = ((kpos[:, None, :] >= 0) & (kpos[:, None, :] < S)
             & (jnp.abs(qpos[:, :, None] - kpos[:, None, :]) <= SWA_WINDOW))
    qg = q.reshape(B, nb, SWA_BLOCK, SWA_KV_HEADS, G, SWA_DIM)
    s_loc = jnp.einsum('bnqhgd,bnkhd->bnhgqk', qg, kb) * scale
    s_loc = jnp.where(valid[None, :, None, None], s_loc, NEG_INF)
    s_ctx = jnp.einsum('bnqhgd,bkhd->bnhgqk', qg, kc) * scale
    pr = _sink_softmax(jnp.concatenate([s_loc, s_ctx], axis=-1), sink)
    o = (jnp.einsum('bnhgqk,bnkhd->bnqhgd', pr[..., :kq], vb)
         + jnp.einsum('bnhgqk,bkhd->bnqhgd', pr[..., kq:], vc))
    y = o.reshape(B, S, nq)
    yc = None
    if with_ctx:
        sc = jnp.einsum('bqhgd,bkhd->bhgqk', qc, kc) * scale
        yc = jnp.einsum('bhgqk,bkhd->bqhgd', _sink_softmax(sc, sink), vc).reshape(B, qc.shape[1], nq)
    return y, yc


def _neighborhood_attention(p, pc, rpb, with_ctx):
    B, S, _ = p.shape
    rows = S // GRID_W
    wr = min(NA_WIN_ROWS, rows)
    ncb = GRID_W // NA_COL_BLOCK
    nd = NA_HEADS * NA_DIM

    def split(t):
        n = t.shape[1]
        return [t[..., i * nd:(i + 1) * nd].reshape(B, n, NA_HEADS, NA_DIM) for i in range(3)]

    q, k, v = split(p)
    qc, kc, vc = split(pc)
    r = jnp.arange(rows)
    row_idx = jnp.clip(r - wr // 2, 0, rows - wr)[:, None] + jnp.arange(wr)[None, :]
    band_idx = (jnp.clip(jnp.arange(ncb) * NA_COL_BLOCK - NA_WIN_COLS // 2, 0, GRID_W - NA_BAND_COLS)[:, None]
                + jnp.arange(NA_BAND_COLS)[None, :])
    nkeys = wr * NA_BAND_COLS
    tok_idx = (row_idx[:, None, :, None] * GRID_W + band_idx[None, :, None, :]).reshape(rows, ncb, nkeys)
    kb = jnp.take(k, tok_idx, axis=1)
    vb = jnp.take(v, tok_idx, axis=1)
    q_col = jnp.arange(ncb)[:, None] * NA_COL_BLOCK + jnp.arange(NA_COL_BLOCK)[None, :]
    col_start = jnp.clip(q_col - NA_WIN_COLS // 2, 0, GRID_W - NA_WIN_COLS)
    col_ok = ((band_idx[:, None, :] >= col_start[:, :, None])
              & (band_idx[:, None, :] < col_start[:, :, None] + NA_WIN_COLS))
    mask = jnp.broadcast_to(col_ok[:, :, None, :], (ncb, NA_COL_BLOCK, wr, NA_BAND_COLS)).reshape(ncb, NA_COL_BLOCK, nkeys)
    d_row = row_idx - r[:, None] + (NA_WIN_ROWS - 1)
    d_col = jnp.clip(band_idx[:, None, :] - q_col[:, :, None], 1 - NA_WIN_COLS, NA_WIN_COLS - 1) + (NA_WIN_COLS - 1)
    bias = rpb[:, d_row[:, None, None, :, None], d_col[None, :, :, None, :]]
    bias = jnp.transpose(bias.reshape(NA_HEADS, rows, ncb, NA_COL_BLOCK, nkeys), (1, 2, 0, 3, 4)).astype(jnp.float32)
    scale = NA_DIM ** -0.5
    qg = q.reshape(B, rows, ncb, NA_COL_BLOCK, NA_HEADS, NA_DIM)
    s_loc = jnp.einsum('brjqhd,brjkhd->brjhqk', qg, kb) * scale + bias
    s_loc = jnp.where(mask[None, None, :, None], s_loc, NEG_INF)
    s_ctx = jnp.einsum('brjqhd,bkhd->brjhqk', qg, kc) * scale
    pr = jax.nn.softmax(jnp.concatenate([s_loc, s_ctx], axis=-1), axis=-1)
    o = (jnp.einsum('brjhqk,brjkhd->brjqhd', pr[..., :nkeys], vb)
         + jnp.einsum('brjhqk,bkhd->brjqhd', pr[..., nkeys:], vc))
    y = o.reshape(B, S, nd)
    yc = None
    if with_ctx:
        sc = jax.nn.softmax(jnp.einsum('bqhd,bkhd->bhqk', qc, kc) * scale, axis=-1)
        yc = jnp.einsum('bhqk,bkhd->bqhd', sc, vc).reshape(B, qc.shape[1], nd)
    return y, yc


def _ret_scan(q, k, v, log_g, s0, diag):
    B, N, H, dk = q.shape
    dv = v.shape[-1]
    C = RET_CHUNK
    nc = N // C
    qc = q.reshape(B, nc, C, H, dk)
    kc = k.reshape(B, nc, C, H, dk)
    vc = v.reshape(B, nc, C, H, dv)
    i = jnp.arange(C, dtype=jnp.float32)
    dist = i[:, None] - i[None, :]
    keep = (dist >= 0) if diag else (dist > 0)
    decay = jnp.where(keep[None], jnp.exp(jnp.where(keep, dist, 0.0)[None] * log_g[:, None, None]), 0.0)
    att = jnp.einsum('bnihd,bnjhd->bnhij', qc, kc) * decay
    intra = jnp.einsum('bnhij,bnjhe->bnihe', att, vc)
    zeta = jnp.exp((C - 1 - i)[None, :] * log_g[:, None])
    u = jnp.einsum('bnjhd,bnjhe,hj->nbhde', kc, vc, zeta)
    g_chunk = jnp.exp(C * log_g)[None, :, None, None]

    def step(s, u_n):
        return g_chunk * s + u_n, s

    s_last, s_prev = lax.scan(step, s0, u)
    xi = jnp.exp((i + 1)[None, :] * log_g[:, None])
    cross = jnp.einsum('bnihd,hi,nbhde->bnihe', qc, xi, s_prev)
    return (intra + cross).reshape(B, N, H, dv), s_last


def _ret_state(k, v, log_g):
    n = k.shape[1]
    w = jnp.exp((n - 1 - jnp.arange(n, dtype=jnp.float32))[None, :] * log_g[:, None])
    return jnp.einsum('bjhd,bjhe,hj->bhde', k, v, w)


def _retention(p, pc, row, col, gam_f, gam_b, with_ctx):
    B, S, _ = p.shape
    nqk = RET_HEADS * RET_QK_DIM
    nv = RET_HEADS * RET_V_DIM

    def split(t):
        n = t.shape[1]
        q = t[..., :nqk].reshape(B, n, RET_HEADS, RET_QK_DIM)
        k = t[..., nqk:2 * nqk].reshape(B, n, RET_HEADS, RET_QK_DIM) * (RET_QK_DIM ** -0.5)
        v = t[..., 2 * nqk:2 * nqk + nv].reshape(B, n, RET_HEADS, RET_V_DIM)
        g = t[..., 2 * nqk + nv:]
        return q, k, v, g

    q, k, v, g = split(p)
    q = _rope2d(q, row, col)
    k = _rope2d(k, row, col)
    qc, kc, vc, gc = split(pc)
    log_f = jax.nn.log_sigmoid(gam_f.astype(jnp.float32))
    log_b = jax.nn.log_sigmoid(gam_b.astype(jnp.float32))
    flip = lambda t: jnp.flip(t, axis=1)
    if with_ctx:
        s0 = jnp.zeros((B, RET_HEADS, RET_QK_DIM, RET_V_DIM), jnp.float32)
        oc_f, s_f = _ret_scan(qc, kc, vc, log_f, s0, True)
        oc_b, s_b = _ret_scan(flip(qc), flip(kc), flip(vc), log_b, s0, False)
    else:
        s_f = _ret_state(kc, vc, log_f)
        s_b = _ret_state(flip(kc), flip(vc), log_b)
    o_f, _ = _ret_scan(q, k, v, log_f, s_f, True)
    o_b, _ = _ret_scan(flip(q), flip(k), flip(v), log_b, s_b, False)

    def gate_out(o, gate):
        return (_rmsnorm(o) * jax.nn.silu(gate).reshape(o.shape)).reshape(o.shape[0], o.shape[1], nv)

    y = gate_out(o_f + flip(o_b), g)
    yc = gate_out(oc_f + flip(oc_b), gc) if with_ctx else None
    return y, yc


def _swiglu_ffn(h, w_gate, w_up, w_down):
    return (jax.nn.silu(h @ w_gate) * (h @ w_up)) @ w_down


def _moe_ffn(h, router, w_gate, w_up, w_down):
    logits = jnp.einsum('bsd,de->bse', h, router).astype(jnp.float32)
    top_v, top_i = lax.top_k(logits, TOP_K)
    w = jax.nn.softmax(top_v, axis=-1)
    gates = jnp.sum(jax.nn.one_hot(top_i, N_EXPERTS, dtype=jnp.float32) * w[..., None], axis=-2)
    y = jnp.zeros_like(h)
    for e in range(N_EXPERTS):
        y = y + gates[..., e:e + 1].astype(h.dtype) * _swiglu_ffn(h, w_gate[e], w_up[e], w_down[e])
    return y


def _layer(x, xc, c, c_ctx, row, col, w_mod, b_mod, g_attn_pre, g_attn_post, g_ffn_pre, g_ffn_post,
           w_in, w_out, lq1, lk1, lq2, lk2, subln_g, sink, rpb, gam_f, gam_b, ffn, lambda_init, with_ctx):
    mod = jax.nn.silu(c) @ w_mod + b_mod
    mod_c = jax.nn.silu(c_ctx) @ w_mod + b_mod
    sh1, sc1, gt1, sh2, sc2, gt2 = jnp.split(mod[:, None, :], 6, axis=-1)
    csh1, csc1, cgt1, csh2, csc2, cgt2 = jnp.split(mod_c, 6, axis=-1)

    h = _rmsnorm(x, g_attn_pre) * (1 + sc1) + sh1
    hc = _rmsnorm(xc, g_attn_pre) * (1 + csc1) + csh1
    p = (h @ w_in).astype(jnp.float32)
    pc = (hc @ w_in).astype(jnp.float32)
    pa, pb, pn, pr = jnp.split(p, IN_CUTS, axis=-1)
    pca, pcb, pcn, pcr = jnp.split(pc, IN_CUTS, axis=-1)
    ya, yca = _diff_attention(pa, pca, row, col, lq1, lk1, lq2, lk2, subln_g, lambda_init, with_ctx)
    yb, ycb = _window_gqa(pb, pcb, row, col, sink, with_ctx)
    yn, ycn = _neighborhood_attention(pn, pcn, rpb, with_ctx)
    yr, ycr = _retention(pr, pcr, row, col, gam_f, gam_b, with_ctx)
    y = jnp.concatenate([ya, yb, yn, yr], axis=-1).astype(x.dtype) @ w_out
    x = x + gt1 * _rmsnorm(y, g_attn_post)

    h = _rmsnorm(x, g_ffn_pre) * (1 + sc2) + sh2
    x = x + gt2 * _rmsnorm(ffn(h), g_ffn_post)

    if with_ctx:
        yc = jnp.concatenate([yca, ycb, ycn, ycr], axis=-1).astype(xc.dtype) @ w_out
        xc = xc + cgt1 * _rmsnorm(yc, g_attn_post)
        hc = _rmsnorm(xc, g_ffn_pre) * (1 + csc2) + csh2
        xc = xc + cgt2 * _rmsnorm(ffn(hc), g_ffn_post)
    return x, xc


def setup_inputs(seed: int = 0) -> dict:
    key = jax.random.key(seed)
    ks = jax.random.split(key, 32)
    D = D_MODEL
    n_dense = (DEPTH + 1) // 2
    n_moe = DEPTH // 2

    def nrm(k, shape, s):
        return s * jax.random.normal(k, shape, jnp.float32)

    gam = 1.0 - 2.0 ** (-5.0 - jnp.arange(RET_HEADS, dtype=jnp.float32))
    gam_logit = jnp.log(gam) - jnp.log1p(-gam)
    return {
        'x': nrm(ks[0], (BATCH, SEQ, D), 1.0),
        'c': nrm(ks[1], (BATCH, D), 1.0),
        'ctx': nrm(ks[2], (BATCH, CTX_LEN, D), 1.0),
        'c_ctx': nrm(ks[3], (D,), 1.0),
        'w_mod': nrm(ks[4], (DEPTH, D, 6 * D), 0.5 * D ** -0.5),
        'b_mod': nrm(ks[5], (DEPTH, 6 * D), 0.02),
        'g_attn_pre': 1.0 + nrm(ks[6], (DEPTH, D), 0.02),
        'g_attn_post': 1.0 + nrm(ks[7], (DEPTH, D), 0.02),
        'g_ffn_pre': 1.0 + nrm(ks[8], (DEPTH, D), 0.02),
        'g_ffn_post': 1.0 + nrm(ks[9], (DEPTH, D), 0.02),
        'w_in': nrm(ks[10], (DEPTH, D, IN_WIDTH), D ** -0.5),
        'w_out': nrm(ks[11], (DEPTH, MIX_WIDTH, D), MIX_WIDTH ** -0.5),
        'da_lambda_q1': nrm(ks[12], (DEPTH, DA_QK_DIM), 0.1),
        'da_lambda_k1': nrm(ks[13], (DEPTH, DA_QK_DIM), 0.1),
        'da_lambda_q2': nrm(ks[14], (DEPTH, DA_QK_DIM), 0.1),
        'da_lambda_k2': nrm(ks[15], (DEPTH, DA_QK_DIM), 0.1),
        'da_subln': 1.0 + nrm(ks[16], (DEPTH, DA_V_DIM), 0.02),
        'swa_sink': nrm(ks[17], (DEPTH, SWA_HEADS), 0.5),
        'na_rpb': nrm(ks[18], (DEPTH, NA_HEADS, 2 * NA_WIN_ROWS - 1, 2 * NA_WIN_COLS - 1), 0.1),
        'ret_gamma_fwd': gam_logit[None, :] + nrm(ks[19], (DEPTH, RET_HEADS), 0.05),
        'ret_gamma_bwd': gam_logit[None, :] + nrm(ks[20], (DEPTH, RET_HEADS), 0.05),
        'ffn_w_gate': nrm(ks[21], (n_dense, D, FFN_DIM), D ** -0.5),
        'ffn_w_up': nrm(ks[22], (n_dense, D, FFN_DIM), D ** -0.5),
        'ffn_w_down': nrm(ks[23], (n_dense, FFN_DIM, D), FFN_DIM ** -0.5),
        'moe_router': nrm(ks[24], (n_moe, D, N_EXPERTS), D ** -0.5),
        'moe_w_gate': nrm(ks[25], (n_moe, N_EXPERTS, D, EXPERT_DIM), D ** -0.5),
        'moe_w_up': nrm(ks[26], (n_moe, N_EXPERTS, D, EXPERT_DIM), D ** -0.5),
        'moe_w_down': nrm(ks[27], (n_moe, N_EXPERTS, EXPERT_DIM, D), EXPERT_DIM ** -0.5),
    }


def reference(x, c, ctx, c_ctx, w_mod, b_mod, g_attn_pre, g_attn_post, g_ffn_pre, g_ffn_post, w_in, w_out,
              da_lambda_q1, da_lambda_k1, da_lambda_q2, da_lambda_k2, da_subln, swa_sink, na_rpb,
              ret_gamma_fwd, ret_gamma_bwd, ffn_w_gate, ffn_w_up, ffn_w_down,
              moe_router, moe_w_gate, moe_w_up, moe_w_down):
    S = x.shape[1]
    t = jnp.arange(S)
    row = (t // GRID_W).astype(jnp.float32)
    col = (t % GRID_W).astype(jnp.float32)
    xc = ctx
    for l in range(DEPTH):
        if l % 2 == 0:
            ffn = functools.partial(_swiglu_ffn, w_gate=ffn_w_gate[l // 2], w_up=ffn_w_up[l // 2],
                                    w_down=ffn_w_down[l // 2])
        else:
            ffn = functools.partial(_moe_ffn, router=moe_router[l // 2], w_gate=moe_w_gate[l // 2],
                                    w_up=moe_w_up[l // 2], w_down=moe_w_down[l // 2])
        lambda_init = 0.8 - 0.6 * math.exp(-0.3 * l)
        x, xc = _layer(x, xc, c, c_ctx, row, col, w_mod[l], b_mod[l], g_attn_pre[l], g_attn_post[l],
                       g_ffn_pre[l], g_ffn_post[l], w_in[l], w_out[l],
                       da_lambda_q1[l], da_lambda_k1[l], da_lambda_q2[l], da_lambda_k2[l], da_subln[l],
                       swa_sink[l], na_rpb[l], ret_gamma_fwd[l], ret_gamma_bwd[l], ffn, lambda_init,
                       l < DEPTH - 1)
    return x
```

```python
import functools
import math

import jax
import jax.numpy as jnp
from jax import lax
from jax.experimental import pallas as pl
from jax.experimental.pallas import tpu as pltpu

F32 = jnp.float32
BF16 = jnp.bfloat16

GRID_W = 64
HEAD_DIM = 64
N_HEADS = 4
GROUP_W = N_HEADS * HEAD_DIM
DA_QK = 32
SWA_WINDOW = 128
NA_WIN_ROWS = 8
NA_WIN_COLS = 16
NA_TILE_ROWS = 4
RET_CHUNK = 128
ROPE_BASE = 10000.0
TOP_K = 2
EPS = 1e-6
NEG = -1e30
LOG2E = 1.4426950408889634
LANES = 128
VMEM_LIMIT = 56 * 1024 * 1024

CB_DA_Q, CB_DA_K, CB_DA_V = 0, 1, 2
CB_SWA_Q = 3
CB_SWA_K128, CB_SWA_V128 = 8, 9
CB_NA_Q, CB_NA_K, CB_NA_V = 5, 6, 7
CB_RET_Q, CB_RET_K, CB_RET_V, CB_RET_G = 8, 9, 10, 11
IN_WIDTH = 3072


def _params(sem):
    return pltpu.CompilerParams(dimension_semantics=sem, vmem_limit_bytes=VMEM_LIMIT)


def _silu(v):
    return v / (1.0 + jnp.exp(-v))


def _head_mask(shape, head_w, h, dim=1):
    return lax.broadcasted_iota(jnp.int32, shape, dim) // head_w == h


def _lane_pick(a, b):
    lane = lax.broadcasted_iota(jnp.int32, a.shape, 1)
    return jnp.where(lane < HEAD_DIM, a, b)


def _per_head_full(vals):
    return jnp.concatenate([_lane_pick(vals[0], vals[1]), _lane_pick(vals[2], vals[3])], axis=1)


def _group_mean_sq(a):
    sq = a * a
    r = lax.broadcasted_iota(jnp.int32, (GROUP_W, GROUP_W), 0) // HEAD_DIM
    c = lax.broadcasted_iota(jnp.int32, (GROUP_W, GROUP_W), 1) // HEAD_DIM
    g = jnp.where(r == c, 1.0, 0.0).astype(BF16)
    hi = sq.astype(BF16)
    lo = (sq - hi.astype(F32)).astype(BF16)
    tot = jnp.dot(hi, g, preferred_element_type=F32) + jnp.dot(lo, g, preferred_element_type=F32)
    return tot * (1.0 / HEAD_DIM)


def _row_mod(mod_ref, rows_are_ctx, j):
    return jnp.where(rows_are_ctx, mod_ref[0, 1, j:j + 1, :], mod_ref[0, 0, j:j + 1, :])


def _mod_kernel(c_ref, w_ref, b_ref, o_ref):
    s = _silu(c_ref[...])
    o_ref[0] = jnp.dot(s.astype(BF16), w_ref[0].astype(BF16), preferred_element_type=F32) + b_ref[0]


def _modulation(c8, w_mod, b_mod):
    L, D, W = w_mod.shape
    tn = 1536
    return pl.pallas_call(
        _mod_kernel,
        grid=(L, W // tn),
        in_specs=[pl.BlockSpec((8, D), lambda l, j: (0, 0)),
                  pl.BlockSpec((1, D, tn), lambda l, j: (l, 0, j)),
                  pl.BlockSpec((1, 1, tn), lambda l, j: (l, 0, j))],
        out_specs=pl.BlockSpec((1, 8, tn), lambda l, j: (l, 0, j)),
        out_shape=jax.ShapeDtypeStruct((L, 8, W), F32),
        compiler_params=_params(("arbitrary", "arbitrary")),
        name="modulation",
    )(c8, w_mod, b_mod.reshape(L, 1, W))


def _lane_block_ops():
    da_s = DA_QK ** -0.5 * LOG2E
    s64 = HEAD_DIM ** -0.5
    ops = []
    ops += [("r32", da_s)] * 2 + [("r32", 1.0)] * 2 + [(None, 1.0)] * 2
    ops += [("r64", s64 * LOG2E)] * 2 + [("r64", 1.0)] + [(None, 1.0)]
    ops += [(None, s64 * LOG2E)] * 2 + [(None, 1.0)] * 4
    ops += [("r64", 1.0)] * 2 + [("r64", s64)] * 2 + [(None, 1.0)] * 4
    return ops


def _rope(x, cos, sin_signed, w):
    lane = lax.broadcasted_iota(jnp.int32, x.shape, 1)
    first = (lane % (2 * w)) < w
    xr = jnp.where(first, pltpu.roll(x, LANES - w, 1), pltpu.roll(x, w, 1))
    return x * cos + xr * sin_signed


def _inproj_kernel(x_ref, mod_ref, g_ref, w_ref, c32_ref, s32_ref, c64_ref, s64_ref, o_ref, *, tm, seq):
    i = pl.program_id(1)
    x = x_ref[0]
    rows = i * tm + lax.broadcasted_iota(jnp.int32, (tm, 1), 0)
    is_ctx = rows >= seq
    ms = jnp.mean(x * x, axis=-1, keepdims=True)
    h = x * lax.rsqrt(ms + EPS) * g_ref[...]
    h = h * (1.0 + _row_mod(mod_ref, is_ctx, 1)) + _row_mod(mod_ref, is_ctx, 0)
    hb = h.astype(BF16)
    ops = _lane_block_ops()
    for cb in range(IN_WIDTH // 256):
        p = jnp.dot(hb, w_ref[:, cb * 256:(cb + 1) * 256], preferred_element_type=F32)
        halves = []
        for hf in range(2):
            kind, scale = ops[cb * 2 + hf]
            ph = p[:, hf * LANES:(hf + 1) * LANES]
            if kind == "r32":
                ph = _rope(ph, c32_ref[...], s32_ref[...], DA_QK // 4)
            elif kind == "r64":
                ph = _rope(ph, c64_ref[...], s64_ref[...], HEAD_DIM // 4)
            if scale != 1.0:
                ph = ph * scale
            halves.append(ph)
        o_ref[0, :, cb * 256:(cb + 1) * 256] = jnp.concatenate(halves, axis=1).astype(BF16)


def _inproj(xa, modv, g, w_in_b, tabs, seq, tm):
    B, T, D = xa.shape
    kern = functools.partial(_inproj_kernel, tm=tm, seq=seq)
    tab_spec = pl.BlockSpec((tm, LANES), lambda b, i: (i, 0))
    return pl.pallas_call(
        kern,
        grid=(B, T // tm),
        in_specs=[pl.BlockSpec((1, tm, D), lambda b, i: (b, i, 0)),
                  pl.BlockSpec((1, 2, 8, D), lambda b, i: (b, 0, 0, 0)),
                  pl.BlockSpec((1, D), lambda b, i: (0, 0)),
                  pl.BlockSpec((D, IN_WIDTH), lambda b, i: (0, 0)),
                  tab_spec, tab_spec, tab_spec, tab_spec],
        out_specs=pl.BlockSpec((1, tm, IN_WIDTH), lambda b, i: (b, i, 0)),
        out_shape=jax.ShapeDtypeStruct((B, T, IN_WIDTH), BF16),
        compiler_params=_params(("arbitrary", "arbitrary")),
        name="inproj",
    )(xa, modv, g.reshape(1, D), w_in_b, *tabs)


def _rope_tables(seq, ctx):
    t = jnp.arange(seq)
    row = (t // GRID_W).astype(F32)
    col = (t % GRID_W).astype(F32)
    out = []
    for dh in (DA_QK, HEAD_DIM):
        half = dh // 2
        quarter = half // 2
        lane = jnp.arange(LANES)
        d = lane % dh
        use_col = (d // half) == 1
        idx = (d % quarter).astype(F32)
        inv = ROPE_BASE ** (-idx * 2.0 / half)
        pos = jnp.where(use_col[None, :], col[:, None], row[:, None])
        ang = pos * inv[None, :]
        first = (d % half) < quarter
        cos = jnp.cos(ang)
        sin = jnp.where(first[None, :], -jnp.sin(ang), jnp.sin(ang))
        cos = jnp.concatenate([cos, jnp.ones((ctx, LANES), F32)], axis=0)
        sin = jnp.concatenate([sin, jnp.zeros((ctx, LANES), F32)], axis=0)
        out += [cos, sin]
    return out


def _da_kernel(lamp_ref, q_ref, k_ref, v_ref, g_ref, o_ref, qm_sc, m_sc, l_sc, acc_sc,
               *, tq, tk, lambda_init):
    qi = pl.program_id(1)
    ki = pl.program_id(2)
    nq = pl.num_programs(1)
    nk = pl.num_programs(2)
    ctx_q = qi == nq - 1

    @pl.when(ki == 0)
    def _():
        q = q_ref[0]
        for j in range(8):
            qm_sc[j * tq:(j + 1) * tq, :] = jnp.where(_head_mask(q.shape, DA_QK, j), q, jnp.zeros_like(q))
        m_sc[...] = jnp.full(m_sc.shape, NEG, F32)
        l_sc[...] = jnp.zeros(l_sc.shape, F32)
        acc_sc[...] = jnp.zeros(acc_sc.shape, F32)

    @pl.when(jnp.logical_or(jnp.logical_not(ctx_q), ki == nk - 1))
    def _():
        k = k_ref[0]
        v = v_ref[0]
        s = lax.dot_general(qm_sc[...], k, (((1,), (1,)), ((), ())), preferred_element_type=F32)
        vm = jnp.concatenate(
            [jnp.where(_head_mask(v.shape, HEAD_DIM, h), v, jnp.zeros_like(v)) for h in range(N_HEADS)], axis=0)
        for t in range(2):
            ps, alphas = [], []
            for h in range(N_HEADS):
                j = 2 * h + t
                sj = s[j * tq:(j + 1) * tq]
                m_prev = m_sc[j]
                m_new = jnp.maximum(m_prev, jnp.max(sj, axis=1, keepdims=True))
                alpha = jnp.exp2(m_prev - m_new)
                p = jnp.exp2(sj - jnp.concatenate([m_new] * (tk // LANES), axis=1))
                l_sc[j] = alpha * l_sc[j] + jnp.sum(p, axis=1, keepdims=True)
                m_sc[j] = m_new
                ps.append(p.astype(BF16))
                alphas.append(alpha)
            pv = jnp.dot(jnp.concatenate(ps, axis=1), vm, preferred_element_type=F32)
            acc_sc[t] = _per_head_full(alphas) * acc_sc[t] + pv

    @pl.when(ki == nk - 1)
    def _():
        lp = lamp_ref[...]
        lam = (jnp.exp(jnp.sum(lp[0:1] * lp[1:2], axis=1, keepdims=True))
               - jnp.exp(jnp.sum(lp[2:3] * lp[3:4], axis=1, keepdims=True))) + lambda_init
        o0 = acc_sc[0] / _per_head_full([l_sc[2 * h] for h in range(N_HEADS)])
        o1 = acc_sc[1] / _per_head_full([l_sc[2 * h + 1] for h in range(N_HEADS)])
        a = o0 - lam * o1
        y = a * lax.rsqrt(_group_mean_sq(a) + EPS) * g_ref[...]
        o_ref[0] = (y * (1.0 - lambda_init)).astype(BF16)


def _diff_attention(P, lamp, subln_full, lambda_init, t):
    B, T, _ = P.shape
    nt = T // t
    kern = functools.partial(_da_kernel, tq=t, tk=t, lambda_init=lambda_init)

    def kv_map(cb):
        return lambda b, qi, ki: (b, jnp.where(qi == nt - 1, nt - 1, ki), cb)

    return pl.pallas_call(
        kern,
        grid=(B, nt, nt),
        in_specs=[pl.BlockSpec((8, LANES), lambda b, qi, ki: (0, 0)),
                  pl.BlockSpec((1, t, GROUP_W), lambda b, qi, ki: (b, qi, CB_DA_Q)),
                  pl.BlockSpec((1, t, GROUP_W), kv_map(CB_DA_K)),
                  pl.BlockSpec((1, t, GROUP_W), kv_map(CB_DA_V)),
                  pl.BlockSpec((1, GROUP_W), lambda b, qi, ki: (0, 0))],
        out_specs=pl.BlockSpec((1, t, GROUP_W), lambda b, qi, ki: (b, qi, 0)),
        out_shape=jax.ShapeDtypeStruct((B, T, GROUP_W), BF16),
        scratch_shapes=[pltpu.VMEM((8 * t, GROUP_W), BF16),
                        pltpu.VMEM((8, t, LANES), F32),
                        pltpu.VMEM((8, t, LANES), F32),
                        pltpu.VMEM((2, t, GROUP_W), F32)],
        compiler_params=_params(("arbitrary", "arbitrary", "arbitrary")),
        name="diff_attention",
    )(lamp, P, P, P, subln_full)


def _swa_kernel(sink_ref, q_ref, kp_ref, kc_ref, kn_ref, kx_ref, vp_ref, vc_ref, vn_ref, vx_ref, o_ref,
                *, seq, blk):
    i = pl.program_id(1)
    q = q_ref[0]
    kall = jnp.concatenate([kp_ref[0], kc_ref[0], kn_ref[0], kx_ref[0]], axis=0)
    vall = jnp.concatenate([vp_ref[0], vc_ref[0], vn_ref[0], vx_ref[0]], axis=0)
    nloc = 3 * blk
    nkeys = kall.shape[0]
    qs = []
    for g in range(2):
        qg = q[:, g * LANES:(g + 1) * LANES]
        for kvh in range(2):
            qs.append(jnp.where(_head_mask(qg.shape, HEAD_DIM, kvh), qg, jnp.zeros_like(qg)))
    s = lax.dot_general(jnp.concatenate(qs, axis=0), kall, (((1,), (1,)), ((), ())),
                        preferred_element_type=F32)
    qpos = i * blk + lax.broadcasted_iota(jnp.int32, (blk, nkeys), 0)
    c = lax.broadcasted_iota(jnp.int32, (blk, nkeys), 1)
    kpos = (i - 1) * blk + c
    valid = ((kpos >= 0) & (kpos < seq) & (jnp.abs(qpos - kpos) <= SWA_WINDOW) & (qpos < seq)) | (c >= nloc)
    vms = [jnp.where(_head_mask(vall.shape, HEAD_DIM, kvh), vall, jnp.zeros_like(vall)) for kvh in range(2)]
    vm = jnp.concatenate(vms, axis=0)
    for g in range(2):
        ps = []
        for kvh in range(2):
            j = 2 * g + kvh
            sink = sink_ref[2 * kvh + g] * LOG2E
            sj = jnp.where(valid, s[j * blk:(j + 1) * blk], NEG)
            m = jnp.maximum(jnp.max(sj, axis=1, keepdims=True), sink)
            e = jnp.exp2(sj - m)
            den = jnp.sum(e, axis=1, keepdims=True) + jnp.exp2(sink - m)
            ps.append((e / den).astype(BF16))
        o_ref[0, :, g * LANES:(g + 1) * LANES] = jnp.dot(
            jnp.concatenate(ps, axis=1), vm, preferred_element_type=F32).astype(BF16)


def _window_attention(P, sink, seq, ctx):
    B, T, _ = P.shape
    blk = SWA_WINDOW
    nt = T // blk
    kern = functools.partial(_swa_kernel, seq=seq, blk=blk)

    def nb(cb, off):
        return pl.BlockSpec((1, blk, LANES), lambda b, i: (b, jnp.clip(i + off, 0, nt - 1), cb))

    def cx(cb):
        return pl.BlockSpec((1, ctx, LANES), lambda b, i: (b, seq // ctx, cb))

    return pl.pallas_call(
        kern,
        grid=(B, nt),
        in_specs=[pl.BlockSpec(memory_space=pltpu.SMEM),
                  pl.BlockSpec((1, blk, GROUP_W), lambda b, i: (b, i, CB_SWA_Q)),
                  nb(CB_SWA_K128, -1), nb(CB_SWA_K128, 0), nb(CB_SWA_K128, 1), cx(CB_SWA_K128),
                  nb(CB_SWA_V128, -1), nb(CB_SWA_V128, 0), nb(CB_SWA_V128, 1), cx(CB_SWA_V128)],
        out_specs=pl.BlockSpec((1, blk, GROUP_W), lambda b, i: (b, i, 0)),
        out_shape=jax.ShapeDtypeStruct((B, T, GROUP_W), BF16),
        compiler_params=_params(("arbitrary", "arbitrary")),
        name="window_attention",
    )(sink, P, P, P, P, P, P, P, P, P)


def _na_kernel(q_ref, k0_ref, k1_ref, k2_ref, kx_ref, v0_ref, v1_ref, v2_ref, vx_ref, bias_ref, o_ref, *, tq):
    q = q_ref[0]
    kall = jnp.concatenate([k0_ref[0], k1_ref[0], k2_ref[0], kx_ref[0]], axis=0)
    vall = jnp.concatenate([v0_ref[0], v1_ref[0], v2_ref[0], vx_ref[0]], axis=0)
    nloc = 3 * tq
    qs = jnp.concatenate(
        [jnp.where(_head_mask(q.shape, HEAD_DIM, h), q, jnp.zeros_like(q)) for h in range(N_HEADS)], axis=0)
    s = lax.dot_general(qs, kall, (((1,), (1,)), ((), ())), preferred_element_type=F32)
    ps = []
    for h in range(N_HEADS):
        sh = s[h * tq:(h + 1) * tq]
        s_loc = sh[:, :nloc] + bias_ref[0, h]
        s_ctx = sh[:, nloc:]
        m = jnp.maximum(jnp.max(s_loc, axis=1, keepdims=True), jnp.max(s_ctx, axis=1, keepdims=True))
        e_loc = jnp.exp2(s_loc - m)
        e_ctx = jnp.exp2(s_ctx - m)
        den = jnp.sum(e_loc, axis=1, keepdims=True) + jnp.sum(e_ctx, axis=1, keepdims=True)
        ps.append(jnp.concatenate([e_loc / den, e_ctx / den], axis=1).astype(BF16))
    vm = jnp.concatenate(
        [jnp.where(_head_mask(vall.shape, HEAD_DIM, h), vall, jnp.zeros_like(vall)) for h in range(N_HEADS)],
        axis=0)
    o_ref[0] = jnp.dot(jnp.concatenate(ps, axis=1), vm, preferred_element_type=F32).astype(BF16)


def _na_bias_tables(rpb, rows):
    tr = NA_TILE_ROWS
    nrt = rows // tr
    qr = jnp.arange(tr)[:, None, None, None]
    qc = jnp.arange(GRID_W)[None, :, None, None]
    kr = jnp.arange(3 * tr)[None, None, :, None]
    kc = jnp.arange(GRID_W)[None, None, None, :]
    tabs = []
    for t_idx, ws in ((0, 0), (1, 0), (nrt - 1, nrt - 3)):
        r = t_idx * tr + qr
        krow = ws * tr + kr
        r0 = jnp.clip(r - NA_WIN_ROWS // 2, 0, rows - NA_WIN_ROWS)
        c0 = jnp.clip(qc - NA_WIN_COLS // 2, 0, GRID_W - NA_WIN_COLS)
        ok = (krow >= r0) & (krow < r0 + NA_WIN_ROWS) & (kc >= c0) & (kc < c0 + NA_WIN_COLS)
        d_row = jnp.clip(krow - r + (NA_WIN_ROWS - 1), 0, 2 * NA_WIN_ROWS - 2)
        d_col = jnp.clip(kc - qc, 1 - NA_WIN_COLS, NA_WIN_COLS - 1) + (NA_WIN_COLS - 1)
        bias = rpb[:, d_row, d_col].astype(F32) * LOG2E
        bias = jnp.where(ok[None], bias, NEG)
        tabs.append(bias.reshape(N_HEADS, tr * GRID_W, 3 * tr * GRID_W))
    tabs.append(jnp.full_like(tabs[0], NEG))
    return jnp.stack(tabs, axis=0)


def _neighborhood_attention(P, bias_tabs, seq, ctx):
    B, T, _ = P.shape
    tq = NA_TILE_ROWS * GRID_W
    nst = seq // tq
    nt = T // tq
    kern = functools.partial(_na_kernel, tq=tq)

    def win(i):
        return jnp.clip(i - 1, 0, nst - 3)

    def loc(cb, off):
        return pl.BlockSpec((1, tq, GROUP_W), lambda b, i: (b, win(i) + off, cb))

    def cx(cb):
        return pl.BlockSpec((1, ctx, GROUP_W), lambda b, i: (b, seq // ctx, cb))

    def variant(b, i):
        return (jnp.where(i >= nst, 3, jnp.minimum(i, nst - 1) - win(i)), 0, 0, 0)

    return pl.pallas_call(
        kern,
        grid=(B, nt),
        in_specs=[pl.BlockSpec((1, tq, GROUP_W), lambda b, i: (b, i, CB_NA_Q)),
                  loc(CB_NA_K, 0), loc(CB_NA_K, 1), loc(CB_NA_K, 2), cx(CB_NA_K),
                  loc(CB_NA_V, 0), loc(CB_NA_V, 1), loc(CB_NA_V, 2), cx(CB_NA_V),
                  pl.BlockSpec((1, N_HEADS, tq, 3 * tq), variant)],
        out_specs=pl.BlockSpec((1, tq, GROUP_W), lambda b, i: (b, i, 0)),
        out_shape=jax.ShapeDtypeStruct((B, T, GROUP_W), BF16),
        compiler_params=_params(("arbitrary", "arbitrary")),
        name="neighborhood_attention",
    )(P, P, P, P, P, P, P, P, P, bias_tabs)


def _ret_direction(q, k, v, state_ref, lgs_ref, lgv, forward, base):
    C = q.shape[0]
    i = lax.broadcasted_iota(jnp.int32, (C, C), 0)
    j = lax.broadcasted_iota(jnp.int32, (C, C), 1)
    dist = (i - j if forward else j - i).astype(F32)
    keep = dist >= 0 if forward else dist > 0
    dist = jnp.where(keep, dist, 0.0)
    qs = jnp.concatenate(
        [jnp.where(_head_mask(q.shape, HEAD_DIM, h), q, jnp.zeros_like(q)) for h in range(N_HEADS)], axis=0)
    s = lax.dot_general(qs, k, (((1,), (1,)), ((), ())), preferred_element_type=F32)
    atts = []
    for h in range(N_HEADS):
        decay = jnp.where(keep, jnp.exp(dist * lgs_ref[base + h]), 0.0)
        atts.append((s[h * C:(h + 1) * C] * decay).astype(BF16))
    vm = jnp.concatenate(
        [jnp.where(_head_mask(v.shape, HEAD_DIM, h), v, jnp.zeros_like(v)) for h in range(N_HEADS)], axis=0)
    intra = jnp.dot(jnp.concatenate(atts, axis=1), vm, preferred_element_type=F32)
    r = lax.broadcasted_iota(jnp.int32, (C, 1), 0).astype(F32)
    xi = jnp.exp((r + 1.0 if forward else C - r) * lgv)
    zeta = jnp.exp((C - 1.0 - r if forward else r) * lgv)
    state = state_ref[...]
    cross = jnp.dot((q.astype(F32) * xi).astype(BF16), state.astype(BF16), preferred_element_type=F32)
    kz_t = (k.astype(F32) * zeta).T.astype(BF16)
    u = jnp.dot(kz_t, v, preferred_element_type=F32)
    rr = lax.broadcasted_iota(jnp.int32, u.shape, 0) // HEAD_DIM
    cc = lax.broadcasted_iota(jnp.int32, u.shape, 1) // HEAD_DIM
    state_ref[...] = jnp.where(rr == cc, jnp.exp(C * lgv) * state + u, 0.0)
    return intra + cross


def _ret_kernel(lgs_ref, lgv_ref, qf_ref, kf_ref, vf_ref, qb_ref, kb_ref, vb_ref, of_ref, ob_ref, sf_sc, sb_sc):
    @pl.when(pl.program_id(1) == 0)
    def _():
        sf_sc[...] = jnp.zeros(sf_sc.shape, F32)
        sb_sc[...] = jnp.zeros(sb_sc.shape, F32)

    of_ref[0] = _ret_direction(qf_ref[0], kf_ref[0], vf_ref[0], sf_sc, lgs_ref, lgv_ref[0:1], True, 0)
    ob_ref[0] = _ret_direction(qb_ref[0], kb_ref[0], vb_ref[0], sb_sc, lgs_ref, lgv_ref[1:2], False, N_HEADS)


def _retention(P, lgs, lgv, seq, ctx):
    B, T, _ = P.shape
    C = RET_CHUNK
    ns, nc = seq // C, ctx // C
    nt = ns + nc

    def fwd(n):
        return jnp.where(n < nc, ns + n, n - nc)

    def bwd(n):
        return jnp.where(n < nc, nt - 1 - n, nt - 1 - n)

    def spec(order, cb):
        return pl.BlockSpec((1, C, GROUP_W), lambda b, n: (b, order(n), cb))

    return pl.pallas_call(
        _ret_kernel,
        grid=(B, nt),
        in_specs=[pl.BlockSpec(memory_space=pltpu.SMEM),
                  pl.BlockSpec((8, GROUP_W), lambda b, n: (0, 0)),
                  spec(fwd, CB_RET_Q), spec(fwd, CB_RET_K), spec(fwd, CB_RET_V),
                  spec(bwd, CB_RET_Q), spec(bwd, CB_RET_K), spec(bwd, CB_RET_V)],
        out_specs=[pl.BlockSpec((1, C, GROUP_W), lambda b, n: (b, fwd(n), 0)),
                   pl.BlockSpec((1, C, GROUP_W), lambda b, n: (b, bwd(n), 0))],
        out_shape=[jax.ShapeDtypeStruct((B, T, GROUP_W), F32)] * 2,
        scratch_shapes=[pltpu.VMEM((GROUP_W, GROUP_W), F32)] * 2,
        compiler_params=_params(("arbitrary", "arbitrary")),
        name="retention",
    )(lgs, lgv, P, P, P, P, P, P)


def _outproj_kernel(*refs, tm, seq, n_exp):
    with_router = n_exp > 0
    if with_router:
        (ya_ref, yb_ref, yn_ref, of_ref, ob_ref, gt_ref, x_ref, mod_ref, w_ref, gpost_ref, gpre_ref, r_ref,
         x1_ref, h2_ref, eid_ref, gw_ref) = refs
    else:
        (ya_ref, yb_ref, yn_ref, of_ref, ob_ref, gt_ref, x_ref, mod_ref, w_ref, gpost_ref, gpre_ref,
         x1_ref, h2_ref) = refs
    i = pl.program_id(1)
    rows = i * tm + lax.broadcasted_iota(jnp.int32, (tm, 1), 0)
    is_ctx = rows >= seq
    o = of_ref[0] + ob_ref[0]
    yr = o * lax.rsqrt(_group_mean_sq(o) + EPS) * _silu(gt_ref[0].astype(F32))
    y = (jnp.dot(ya_ref[0], w_ref[0:256, :], preferred_element_type=F32)
         + jnp.dot(yb_ref[0], w_ref[256:512, :], preferred_element_type=F32)
         + jnp.dot(yn_ref[0], w_ref[512:768, :], preferred_element_type=F32)
         + jnp.dot(yr.astype(BF16), w_ref[768:1024, :], preferred_element_type=F32))
    yn = y * lax.rsqrt(jnp.mean(y * y, axis=-1, keepdims=True) + EPS) * gpost_ref[...]
    x1 = x_ref[0] + _row_mod(mod_ref, is_ctx, 2) * yn
    x1_ref[0] = x1
    h = x1 * lax.rsqrt(jnp.mean(x1 * x1, axis=-1, keepdims=True) + EPS) * gpre_ref[...]
    h = h * (1.0 + _row_mod(mod_ref, is_ctx, 4)) + _row_mod(mod_ref, is_ctx, 3)
    h2_ref[0] = h.astype(h2_ref.dtype)
    if with_router:
        logits = jnp.dot(h, r_ref[...], preferred_element_type=F32, precision=lax.Precision.HIGHEST)
        lane = lax.broadcasted_iota(jnp.int32, logits.shape, 1)
        logits = jnp.where(lane < n_exp, logits, NEG)
        v1 = jnp.max(logits, axis=1, keepdims=True)
        i1 = jnp.min(jnp.where(logits == v1, lane, LANES), axis=1, keepdims=True)
        rest = jnp.where(lane == i1, NEG, logits)
        v2 = jnp.max(rest, axis=1, keepdims=True)
        i2 = jnp.min(jnp.where(rest == v2, lane, LANES), axis=1, keepdims=True)
        e2 = jnp.exp(v2 - v1)
        w1 = 1.0 / (1.0 + e2)
        w2 = e2 / (1.0 + e2)
        eid_ref[0] = jnp.where(lane == 0, i1, jnp.where(lane == 1, i2, 0))
        gw_ref[0] = jnp.where(lane == 0, w1, jnp.where(lane == 1, w2, 0.0))


def _outproj(ya, yb, yn, of, ob, P, xa, modv, w_out_b, g_post, g_pre2, rows, seq, tm, router=None, n_exp=0):
    B, _, D = xa.shape
    with_router = router is not None
    kern = functools.partial(_outproj_kernel, tm=tm, seq=seq, n_exp=n_exp)
    tok = lambda w: pl.BlockSpec((1, tm, w), lambda b, i: (b, i, 0))
    in_specs = [tok(GROUP_W), tok(GROUP_W), tok(GROUP_W), tok(GROUP_W), tok(GROUP_W),
                pl.BlockSpec((1, tm, GROUP_W), lambda b, i: (b, i, CB_RET_G)),
                tok(D),
                pl.BlockSpec((1, 2, 8, D), lambda b, i: (b, 0, 0, 0)),
                pl.BlockSpec((D, D), lambda b, i: (0, 0)),
                pl.BlockSpec((1, D), lambda b, i: (0, 0)),
                pl.BlockSpec((1, D), lambda b, i: (0, 0))]
    args = [ya, yb, yn, of, ob, P, xa, modv, w_out_b, g_post.reshape(1, D), g_pre2.reshape(1, D)]
    out_specs = [tok(D), tok(D)]
    out_shape = [jax.ShapeDtypeStruct((B, rows, D), F32),
                 jax.ShapeDtypeStruct((B, rows, D), F32 if with_router else BF16)]
    if with_router:
        in_specs.append(pl.BlockSpec((D, LANES), lambda b, i: (0, 0)))
        args.append(router)
        out_specs += [tok(LANES), tok(LANES)]
        out_shape += [jax.ShapeDtypeStruct((B, rows, LANES), jnp.int32),
                      jax.ShapeDtypeStruct((B, rows, LANES), F32)]
    return pl.pallas_call(
        kern,
        grid=(B, rows // tm),
        in_specs=in_specs,
        out_specs=out_specs,
        out_shape=out_shape,
        compiler_params=_params(("arbitrary", "arbitrary")),
        name="outproj_router" if with_router else "outproj",
    )(*args)


def _ffn_kernel(h_ref, x_ref, mod_ref, wg_ref, wu_ref, wd_ref, gpost_ref, o_ref, *, tm, seq, fc):
    i = pl.program_id(1)
    rows = i * tm + lax.broadcasted_iota(jnp.int32, (tm, 1), 0)
    is_ctx = rows >= seq
    h = h_ref[0]
    acc = jnp.zeros((tm, o_ref.shape[-1]), F32)
    for c in range(wg_ref.shape[1] // fc):
        g = jnp.dot(h, wg_ref[:, c * fc:(c + 1) * fc], preferred_element_type=F32)
        u = jnp.dot(h, wu_ref[:, c * fc:(c + 1) * fc], preferred_element_type=F32)
        a = (_silu(g) * u).astype(BF16)
        acc = acc + jnp.dot(a, wd_ref[c * fc:(c + 1) * fc, :], preferred_element_type=F32)
    yn = acc * lax.rsqrt(jnp.mean(acc * acc, axis=-1, keepdims=True) + EPS) * gpost_ref[...]
    o_ref[0] = x_ref[0] + _row_mod(mod_ref, is_ctx, 5) * yn


def _dense_ffn(h2, x1, modv, wg, wu, wd, g_post, seq, tm):
    B, T, D = x1.shape
    F = wg.shape[1]
    kern = functools.partial(_ffn_kernel, tm=tm, seq=seq, fc=256)
    const = lambda shape: pl.BlockSpec(shape, lambda b, i: (0, 0), pipeline_mode=pl.Buffered(1))
    return pl.pallas_call(
        kern,
        grid=(B, T // tm),
        in_specs=[pl.BlockSpec((1, tm, D), lambda b, i: (b, i, 0)),
                  pl.BlockSpec((1, tm, D), lambda b, i: (b, i, 0)),
                  pl.BlockSpec((1, 2, 8, D), lambda b, i: (b, 0, 0, 0)),
                  const((D, F)), const((D, F)), const((F, D)),
                  pl.BlockSpec((1, D), lambda b, i: (0, 0))],
        out_specs=pl.BlockSpec((1, tm, D), lambda b, i: (b, i, 0)),
        out_shape=jax.ShapeDtypeStruct((B, T, D), F32),
        compiler_params=_params(("arbitrary", "arbitrary")),
        name="dense_ffn",
    )(h2, x1, modv, wg, wu, wd, g_post.reshape(1, D))


def _row_copy(src_hbm, row, dst_vmem, r, sem):
    return pltpu.make_async_copy(src_hbm.at[pl.ds(row, 1)], dst_vmem.at[pl.ds(r, 1)], sem)


def _gather_kernel(src_ref, h_hbm, o_ref, sem, *, tr):
    def issue(r, c):
        _row_copy(h_hbm, src_ref[0, 0, r], o_ref, r, sem).start()
        return c

    lax.fori_loop(0, tr, issue, 0)

    def drain(r, c):
        _row_copy(h_hbm, src_ref[0, 0, r], o_ref, r, sem).wait()
        return c

    lax.fori_loop(0, tr, drain, 0)


def _gather_rows(h_flat, src, n_rows, tr):
    D = h_flat.shape[1]
    kern = functools.partial(_gather_kernel, tr=tr)
    return pl.pallas_call(
        kern,
        grid=(n_rows // tr,),
        in_specs=[pl.BlockSpec((1, 1, tr), lambda i: (i, 0, 0), memory_space=pltpu.SMEM),
                  pl.BlockSpec(memory_space=pl.ANY)],
        out_specs=pl.BlockSpec((tr, D), lambda i: (i, 0)),
        out_shape=jax.ShapeDtypeStruct((n_rows, D), h_flat.dtype),
        scratch_shapes=[pltpu.SemaphoreType.DMA(())],
        compiler_params=_params(("arbitrary",)),
        name="moe_gather",
    )(src.reshape(n_rows // tr, 1, tr), h_flat)


def _moe_ffn_kernel(te_ref, tv_ref, x_ref, wg_ref, wu_ref, wd_ref, o_ref, acc_sc):
    t = pl.program_id(0)
    f = pl.program_id(1)
    nf = pl.num_programs(1)

    @pl.when(f == 0)
    def _():
        acc_sc[...] = jnp.zeros(acc_sc.shape, F32)

    @pl.when(tv_ref[t] > 0)
    def _():
        x = x_ref[...].astype(BF16)
        g = jnp.dot(x, wg_ref[0], preferred_element_type=F32)
        u = jnp.dot(x, wu_ref[0], preferred_element_type=F32)
        a = (_silu(g) * u).astype(BF16)
        acc_sc[...] += jnp.dot(a, wd_ref[0], preferred_element_type=F32)

    @pl.when(f == nf - 1)
    def _():
        o_ref[...] = acc_sc[...]


def _moe_ffn(xs, tile_expert, tile_valid, wg, wu, wd, tr, tf):
    R, D = xs.shape
    F = wg.shape[2]
    return pl.pallas_call(
        _moe_ffn_kernel,
        grid_spec=pltpu.PrefetchScalarGridSpec(
            num_scalar_prefetch=2,
            grid=(R // tr, F // tf),
            in_specs=[pl.BlockSpec((tr, D), lambda t, f, te, tv: (t, 0)),
                      pl.BlockSpec((1, D, tf), lambda t, f, te, tv: (te[t], 0, f)),
                      pl.BlockSpec((1, D, tf), lambda t, f, te, tv: (te[t], 0, f)),
                      pl.BlockSpec((1, tf, D), lambda t, f, te, tv: (te[t], f, 0))],
            out_specs=pl.BlockSpec((tr, D), lambda t, f, te, tv: (t, 0)),
            scratch_shapes=[pltpu.VMEM((tr, D), F32)]),
        out_shape=jax.ShapeDtypeStruct((R, D), F32),
        compiler_params=_params(("arbitrary", "arbitrary")),
        name="moe_ffn",
    )(tile_expert, tile_valid, xs, wg, wu, wd)


def _combine_kernel(pos_ref, ys_hbm, gw_ref, x_ref, mod_ref, gpost_ref, o_ref, y1_sc, y2_sc, sem, *, tm, seq):
    def issue(r, c):
        _row_copy(ys_hbm, pos_ref[0, 0, r], y1_sc, r, sem).start()
        _row_copy(ys_hbm, pos_ref[0, 1, r], y2_sc, r, sem).start()
        return c

    lax.fori_loop(0, tm, issue, 0)

    def drain(r, c):
        _row_copy(ys_hbm, pos_ref[0, 0, r], y1_sc, r, sem).wait()
        _row_copy(ys_hbm, pos_ref[0, 1, r], y2_sc, r, sem).wait()
        return c

    lax.fori_loop(0, tm, drain, 0)
    rows = pl.program_id(1) * tm + lax.broadcasted_iota(jnp.int32, (tm, 1), 0)
    gw = gw_ref[0]
    y = gw[:, 0:1] * y1_sc[...] + gw[:, 1:2] * y2_sc[...]
    yn = y * lax.rsqrt(jnp.mean(y * y, axis=-1, keepdims=True) + EPS) * gpost_ref[...]
    o_ref[0] = x_ref[0] + _row_mod(mod_ref, rows >= seq, 5) * yn


def _moe_combine(pos, ys, gw, x1, modv, g_post, seq, tm):
    B, R, D = x1.shape
    ntile = R // tm
    kern = functools.partial(_combine_kernel, tm=tm, seq=seq)
    return pl.pallas_call(
        kern,
        grid=(B, ntile),
        in_specs=[pl.BlockSpec((1, TOP_K, tm), lambda b, i: (b * ntile + i, 0, 0), memory_space=pltpu.SMEM),
                  pl.BlockSpec(memory_space=pl.ANY),
                  pl.BlockSpec((1, tm, LANES), lambda b, i: (b, i, 0)),
                  pl.BlockSpec((1, tm, D), lambda b, i: (b, i, 0)),
                  pl.BlockSpec((1, 2, 8, D), lambda b, i: (b, 0, 0, 0)),
                  pl.BlockSpec((1, D), lambda b, i: (0, 0))],
        out_specs=pl.BlockSpec((1, tm, D), lambda b, i: (b, i, 0)),
        out_shape=jax.ShapeDtypeStruct((B, R, D), F32),
        scratch_shapes=[pltpu.VMEM((tm, D), F32), pltpu.VMEM((tm, D), F32), pltpu.SemaphoreType.DMA(())],
        compiler_params=_params(("arbitrary", "arbitrary")),
        name="moe_combine",
    )(pos.reshape(B * ntile, tm, TOP_K).transpose(0, 2, 1), ys, gw, x1, modv, g_post.reshape(1, D))


def _moe_routing(eid, n_exp, tr):
    N = eid.shape[0]
    e_flat = eid.reshape(-1)
    onehot = (e_flat[:, None] == jnp.arange(n_exp)[None, :]).astype(jnp.int32)
    rank = jnp.sum((jnp.cumsum(onehot, axis=0) - 1) * onehot, axis=1)
    counts = jnp.sum(onehot, axis=0)
    padded = ((counts + tr - 1) // tr) * tr
    ends = jnp.cumsum(padded)
    starts = ends - padded
    dest = starts[e_flat] + rank
    n_rows = (TOP_K * N // tr + n_exp) * tr
    src = jnp.zeros((n_rows,), jnp.int32).at[dest].set(jnp.arange(TOP_K * N, dtype=jnp.int32) // TOP_K)
    tile_start = jnp.arange(n_rows // tr, dtype=jnp.int32) * tr
    tile_expert = jnp.minimum(jnp.searchsorted(ends, tile_start, side="right"), n_exp - 1).astype(jnp.int32)
    tile_valid = (tile_start < ends[-1]).astype(jnp.int32)
    pos = dest.reshape(N, TOP_K).astype(jnp.int32)
    return src, pos, tile_expert, tile_valid, n_rows


def _swa_perm():
    idx = []
    for g in range(2):
        for kvh in range(2):
            idx += [(kvh * 2 + g) * HEAD_DIM + d for d in range(HEAD_DIM)]
    return jnp.array(idx, jnp.int32)


def kernel(x, c, ctx, c_ctx, w_mod, b_mod, g_attn_pre, g_attn_post, g_ffn_pre, g_ffn_post, w_in, w_out,
           da_lambda_q1, da_lambda_k1, da_lambda_q2, da_lambda_k2, da_subln, swa_sink, na_rpb,
           ret_gamma_fwd, ret_gamma_bwd, ffn_w_gate, ffn_w_up, ffn_w_down,
           moe_router, moe_w_gate, moe_w_up, moe_w_down):
    B, S, D = x.shape
    CTX = ctx.shape[1]
    T = S + CTX
    L = w_mod.shape[0]
    n_exp = moe_router.shape[-1]
    rows = S // GRID_W
    t_attn = 256

    c8 = jnp.zeros((8, D), F32).at[:B].set(c).at[B].set(c_ctx)
    mod = _modulation(c8, w_mod, b_mod).reshape(L, 8, 6, D)
    mod = jnp.pad(mod, ((0, 0), (0, 0), (0, 2), (0, 0)))
    modv = jnp.stack([mod[:, :B], jnp.broadcast_to(mod[:, B:B + 1], (L, B, 8, D))], axis=2)

    tabs = _rope_tables(S, CTX)
    perm = _swa_perm()
    swa0 = CB_SWA_Q * 256
    xa = jnp.concatenate([x, ctx], axis=1)

    for l in range(L):
        lambda_init = 0.8 - 0.6 * math.exp(-0.3 * l)
        last = l == L - 1
        w_in_l = w_in[l]
        w_in_b = jnp.concatenate([w_in_l[:, :swa0], w_in_l[:, swa0:swa0 + 256][:, perm], w_in_l[:, swa0 + 256:]],
                                 axis=1).astype(BF16)
        w_out_l = w_out[l]
        w_out_b = jnp.concatenate([w_out_l[:256], w_out_l[256:512][perm], w_out_l[512:]], axis=0).astype(BF16)

        P = _inproj(xa, modv[l], g_attn_pre[l], w_in_b, tabs, S, 256)

        lamp = jnp.zeros((8, LANES), F32)
        for r, v in enumerate((da_lambda_q1[l], da_lambda_k1[l], da_lambda_q2[l], da_lambda_k2[l])):
            lamp = lamp.at[r, :DA_QK].set(v)
        subln_full = jnp.tile(da_subln[l], N_HEADS).reshape(1, GROUP_W)
        ya = _diff_attention(P, lamp, subln_full, lambda_init, t_attn)
        yb = _window_attention(P, swa_sink[l].astype(F32), S, CTX)
        yn = _neighborhood_attention(P, _na_bias_tables(na_rpb[l], rows), S, CTX)
        lg = jnp.stack([jax.nn.log_sigmoid(ret_gamma_fwd[l].astype(F32)),
                        jax.nn.log_sigmoid(ret_gamma_bwd[l].astype(F32))])
        lgv = jnp.zeros((8, GROUP_W), F32).at[:2].set(jnp.repeat(lg, HEAD_DIM, axis=1))
        of, ob = _retention(P, lg.reshape(-1), lgv, S, CTX)

        R = S if last else T
        e = l // 2
        if l % 2 == 0:
            x1, h2 = _outproj(ya, yb, yn, of, ob, P, xa, modv[l], w_out_b, g_attn_post[l], g_ffn_pre[l],
                              R, S, 256)
            xa = _dense_ffn(h2, x1, modv[l], ffn_w_gate[e].astype(BF16), ffn_w_up[e].astype(BF16),
                            ffn_w_down[e].astype(BF16), g_ffn_post[l], S, 256)
        else:
            router = jnp.zeros((D, LANES), F32).at[:, :n_exp].set(moe_router[e])
            x1, h2, eid, gw = _outproj(ya, yb, yn, of, ob, P, xa, modv[l], w_out_b, g_attn_post[l],
                                       g_ffn_pre[l], R, S, 256, router=router, n_exp=n_exp)
            tr = 512
            src, pos, tile_expert, tile_valid, n_rows = _moe_routing(
                eid.reshape(B * R, LANES)[:, :TOP_K], n_exp, tr)
            xs = _gather_rows(h2.reshape(B * R, D), src, n_rows, 256)
            ys = _moe_ffn(xs, tile_expert, tile_valid, moe_w_gate[e].astype(BF16), moe_w_up[e].astype(BF16),
                          moe_w_down[e].astype(BF16), tr, 512)
            xa = _moe_combine(pos, ys, gw, x1, modv[l], g_ffn_post[l], S, 256)
    return xa[:, :S]
```

```python
import functools
import math

import jax
import jax.numpy as jnp
from jax import lax
from jax.experimental import pallas as pl
from jax.experimental.pallas import tpu as pltpu

F32 = jnp.float32
BF16 = jnp.bfloat16

GRID_W = 64
HEAD_DIM = 64
N_HEADS = 4
GROUP_W = N_HEADS * HEAD_DIM
DA_QK = 32
SWA_WINDOW = 128
NA_WIN_ROWS = 8
NA_WIN_COLS = 16
NA_TILE_ROWS = 4
RET_CHUNK = 128
ROPE_BASE = 10000.0
TOP_K = 2
EPS = 1e-6
NEG = -1e30
LOG2E = 1.4426950408889634
LANES = 128
VMEM_LIMIT = 56 * 1024 * 1024

CB_DA_Q, CB_DA_K, CB_DA_V = 0, 1, 2
CB_SWA_Q = 3
CB_SWA_K128, CB_SWA_V128 = 8, 9
CB_NA_Q, CB_NA_K, CB_NA_V = 5, 6, 7
CB_RET_Q, CB_RET_K, CB_RET_V, CB_RET_G = 8, 9, 10, 11
IN_WIDTH = 3072


def _params(sem):
    return pltpu.CompilerParams(dimension_semantics=sem, vmem_limit_bytes=VMEM_LIMIT)


def _silu(v):
    return v / (1.0 + jnp.exp(-v))


def _head_mask(shape, head_w, h, dim=1):
    return lax.broadcasted_iota(jnp.int32, shape, dim) // head_w == h


def _lane_pick(a, b):
    lane = lax.broadcasted_iota(jnp.int32, a.shape, 1)
    return jnp.where(lane < HEAD_DIM, a, b)


def _per_head_full(vals):
    return jnp.concatenate([_lane_pick(vals[0], vals[1]), _lane_pick(vals[2], vals[3])], axis=1)


def _group_mean_sq(a):
    sq = a * a
    r = lax.broadcasted_iota(jnp.int32, (GROUP_W, GROUP_W), 0) // HEAD_DIM
    c = lax.broadcasted_iota(jnp.int32, (GROUP_W, GROUP_W), 1) // HEAD_DIM
    g = jnp.where(r == c, 1.0, 0.0).astype(BF16)
    hi = sq.astype(BF16)
    lo = (sq - hi.astype(F32)).astype(BF16)
    tot = jnp.dot(hi, g, preferred_element_type=F32) + jnp.dot(lo, g, preferred_element_type=F32)
    return tot * (1.0 / HEAD_DIM)


def _row_mod(mod_ref, rows_are_ctx, j):
    return jnp.where(rows_are_ctx, mod_ref[0, 1, j:j + 1, :], mod_ref[0, 0, j:j + 1, :])


def _mod_kernel(c_ref, w_ref, b_ref, o_ref):
    s = _silu(c_ref[...])
    o_ref[0] = jnp.dot(s.astype(BF16), w_ref[0].astype(BF16), preferred_element_type=F32) + b_ref[0]


def _modulation(c8, w_mod, b_mod):
    L, D, W = w_mod.shape
    tn = 1536
    return pl.pallas_call(
        _mod_kernel,
        grid=(L, W // tn),
        in_specs=[pl.BlockSpec((8, D), lambda l, j: (0, 0)),
                  pl.BlockSpec((1, D, tn), lambda l, j: (l, 0, j)),
                  pl.BlockSpec((1, 1, tn), lambda l, j: (l, 0, j))],
        out_specs=pl.BlockSpec((1, 8, tn), lambda l, j: (l, 0, j)),
        out_shape=jax.ShapeDtypeStruct((L, 8, W), F32),
        compiler_params=_params(("arbitrary", "arbitrary")),
        name="modulation",
    )(c8, w_mod, b_mod.reshape(L, 1, W))


def _lane_block_ops():
    da_s = DA_QK ** -0.5 * LOG2E
    s64 = HEAD_DIM ** -0.5
    ops = []
    ops += [("r32", da_s)] * 2 + [("r32", 1.0)] * 2 + [(None, 1.0)] * 2
    ops += [("r64", s64 * LOG2E)] * 2 + [("r64", 1.0)] + [(None, 1.0)]
    ops += [(None, s64 * LOG2E)] * 2 + [(None, 1.0)] * 4
    ops += [("r64", 1.0)] * 2 + [("r64", s64)] * 2 + [(None, 1.0)] * 4
    return ops


def _rope(x, cos, sin_signed, w):
    lane = lax.broadcasted_iota(jnp.int32, x.shape, 1)
    first = (lane % (2 * w)) < w
    xr = jnp.where(first, pltpu.roll(x, LANES - w, 1), pltpu.roll(x, w, 1))
    return x * cos + xr * sin_signed


def _inproj_kernel(x_ref, mod_ref, g_ref, w_ref, c32_ref, s32_ref, c64_ref, s64_ref, o_ref, qt_ref, vt_ref,
                   *, tm, seq):
    i = pl.program_id(1)
    x = x_ref[0]
    rows = i * tm + lax.broadcasted_iota(jnp.int32, (tm, 1), 0)
    is_ctx = rows >= seq
    ms = jnp.mean(x * x, axis=-1, keepdims=True)
    h = x * lax.rsqrt(ms + EPS) * g_ref[...]
    h = h * (1.0 + _row_mod(mod_ref, is_ctx, 1)) + _row_mod(mod_ref, is_ctx, 0)
    hb = h.astype(BF16)
    ops = _lane_block_ops()
    for cb in range(IN_WIDTH // 256):
        p = jnp.dot(hb, w_ref[:, cb * 256:(cb + 1) * 256], preferred_element_type=F32)
        halves = []
        for hf in range(2):
            kind, scale = ops[cb * 2 + hf]
            ph = p[:, hf * LANES:(hf + 1) * LANES]
            if kind == "r32":
                ph = _rope(ph, c32_ref[...], s32_ref[...], DA_QK // 4)
            elif kind == "r64":
                ph = _rope(ph, c64_ref[...], s64_ref[...], HEAD_DIM // 4)
            if scale != 1.0:
                ph = ph * scale
            halves.append(ph)
        full = jnp.concatenate(halves, axis=1)
        o_ref[0, :, cb * 256:(cb + 1) * 256] = full.astype(BF16)
        if cb == CB_DA_Q:
            qt_ref[0] = full.T.astype(BF16)
        elif cb == CB_DA_V:
            vt_ref[0] = full.T.astype(BF16)


def _inproj(xa, modv, g, w_in_b, tabs, seq, tm):
    B, T, D = xa.shape
    kern = functools.partial(_inproj_kernel, tm=tm, seq=seq)
    tab_spec = pl.BlockSpec((tm, LANES), lambda b, i: (i, 0))
    return pl.pallas_call(
        kern,
        grid=(B, T // tm),
        in_specs=[pl.BlockSpec((1, tm, D), lambda b, i: (b, i, 0)),
                  pl.BlockSpec((1, 2, 8, D), lambda b, i: (b, 0, 0, 0)),
                  pl.BlockSpec((1, D), lambda b, i: (0, 0)),
                  pl.BlockSpec((D, IN_WIDTH), lambda b, i: (0, 0)),
                  tab_spec, tab_spec, tab_spec, tab_spec],
        out_specs=[pl.BlockSpec((1, tm, IN_WIDTH), lambda b, i: (b, i, 0)),
                   pl.BlockSpec((1, GROUP_W, tm), lambda b, i: (b, 0, i)),
                   pl.BlockSpec((1, GROUP_W, tm), lambda b, i: (b, 0, i))],
        out_shape=[jax.ShapeDtypeStruct((B, T, IN_WIDTH), BF16),
                   jax.ShapeDtypeStruct((B, GROUP_W, T), BF16),
                   jax.ShapeDtypeStruct((B, GROUP_W, T), BF16)],
        compiler_params=_params(("arbitrary", "arbitrary")),
        name="inproj",
    )(xa, modv, g.reshape(1, D), w_in_b, *tabs)


def _rope_tables(seq, ctx):
    t = jnp.arange(seq)
    row = (t // GRID_W).astype(F32)
    col = (t % GRID_W).astype(F32)
    out = []
    for dh in (DA_QK, HEAD_DIM):
        half = dh // 2
        quarter = half // 2
        lane = jnp.arange(LANES)
        d = lane % dh
        use_col = (d // half) == 1
        idx = (d % quarter).astype(F32)
        inv = ROPE_BASE ** (-idx * 2.0 / half)
        pos = jnp.where(use_col[None, :], col[:, None], row[:, None])
        ang = pos * inv[None, :]
        first = (d % half) < quarter
        cos = jnp.cos(ang)
        sin = jnp.where(first[None, :], -jnp.sin(ang), jnp.sin(ang))
        cos = jnp.concatenate([cos, jnp.ones((ctx, LANES), F32)], axis=0)
        sin = jnp.concatenate([sin, jnp.zeros((ctx, LANES), F32)], axis=0)
        out += [cos, sin]
    return out


def _rows_per_head(vals, tq):
    return jnp.concatenate([jnp.broadcast_to(v, (HEAD_DIM, tq)) for v in vals], axis=0)


def _da_kernel(lamp_ref, qt_ref, k_ref, vt_ref, g_ref, o_ref, qm_sc, m_sc, l_sc, acc_sc, *, tq, lambda_init):
    ki = pl.program_id(2)
    nk = pl.num_programs(2)

    @pl.when(ki == 0)
    def _():
        qt = qt_ref[0]
        for j in range(2 * N_HEADS):
            qm_sc[j] = jnp.where(_head_mask(qt.shape, DA_QK, j, 0), qt, jnp.zeros_like(qt))
        m_sc[...] = jnp.full(m_sc.shape, NEG, F32)
        l_sc[...] = jnp.zeros(l_sc.shape, F32)
        acc_sc[...] = jnp.zeros(acc_sc.shape, F32)

    k = k_ref[0]
    vt = vt_ref[0]
    vm = jnp.concatenate(
        [jnp.where(_head_mask(vt.shape, HEAD_DIM, h, 0), vt, jnp.zeros_like(vt)) for h in range(N_HEADS)], axis=1)
    for t in range(2):
        ps, alphas = [], []
        for h in range(N_HEADS):
            j = 2 * h + t
            st = jnp.dot(k, qm_sc[j], preferred_element_type=F32)
            m_prev = m_sc[j]
            m_new = jnp.maximum(m_prev, jnp.max(st, axis=0, keepdims=True))
            alpha = jnp.exp2(m_prev - m_new)
            p = jnp.exp2(st - m_new)
            l_sc[j] = alpha * l_sc[j] + jnp.sum(p, axis=0, keepdims=True)
            m_sc[j] = m_new
            ps.append(p.astype(BF16))
            alphas.append(alpha)
        pv = jnp.dot(vm, jnp.concatenate(ps, axis=0), preferred_element_type=F32)
        acc_sc[t] = _rows_per_head(alphas, tq) * acc_sc[t] + pv

    @pl.when(ki == nk - 1)
    def _():
        lp = lamp_ref[...]
        lam = (jnp.exp(jnp.sum(lp[0:1] * lp[1:2], axis=1, keepdims=True))
               - jnp.exp(jnp.sum(lp[2:3] * lp[3:4], axis=1, keepdims=True))) + lambda_init
        o0 = acc_sc[0] / _rows_per_head([l_sc[2 * h] for h in range(N_HEADS)], tq)
        o1 = acc_sc[1] / _rows_per_head([l_sc[2 * h + 1] for h in range(N_HEADS)], tq)
        a = (o0 - lam * o1).T
        y = a * lax.rsqrt(_group_mean_sq(a) + EPS) * g_ref[...]
        o_ref[0] = (y * (1.0 - lambda_init)).astype(BF16)


def _diff_attention(P, QT, VT, lamp, subln_full, lambda_init, q0, nq, tq, k0, nk, tk):
    B = P.shape[0]
    kern = functools.partial(_da_kernel, tq=tq, lambda_init=lambda_init)
    return pl.pallas_call(
        kern,
        grid=(B, nq, nk),
        in_specs=[pl.BlockSpec((8, LANES), lambda b, qi, ki: (0, 0)),
                  pl.BlockSpec((1, GROUP_W, tq), lambda b, qi, ki: (b, 0, q0 + qi)),
                  pl.BlockSpec((1, tk, GROUP_W), lambda b, qi, ki: (b, k0 + ki, CB_DA_K)),
                  pl.BlockSpec((1, GROUP_W, tk), lambda b, qi, ki: (b, 0, k0 + ki)),
                  pl.BlockSpec((1, GROUP_W), lambda b, qi, ki: (0, 0))],
        out_specs=pl.BlockSpec((1, tq, GROUP_W), lambda b, qi, ki: (b, qi, 0)),
        out_shape=jax.ShapeDtypeStruct((B, nq * tq, GROUP_W), BF16),
        scratch_shapes=[pltpu.VMEM((2 * N_HEADS, GROUP_W, tq), BF16),
                        pltpu.VMEM((2 * N_HEADS, 1, tq), F32),
                        pltpu.VMEM((2 * N_HEADS, 1, tq), F32),
                        pltpu.VMEM((2, GROUP_W, tq), F32)],
        compiler_params=_params(("arbitrary", "arbitrary", "arbitrary")),
        name="diff_attention",
    )(lamp, QT, P, VT, subln_full)


def _swa_kernel(sink_ref, q_ref, kp_ref, kc_ref, kn_ref, kx_ref, vp_ref, vc_ref, vn_ref, vx_ref, o_ref,
                *, seq, blk):
    i = pl.program_id(1)
    q = q_ref[0]
    kall = jnp.concatenate([kp_ref[0], kc_ref[0], kn_ref[0], kx_ref[0]], axis=0)
    vall = jnp.concatenate([vp_ref[0], vc_ref[0], vn_ref[0], vx_ref[0]], axis=0)
    nloc = 3 * blk
    nkeys = kall.shape[0]
    qs = []
    for g in range(2):
        qg = q[:, g * LANES:(g + 1) * LANES]
        for kvh in range(2):
            qs.append(jnp.where(_head_mask(qg.shape, HEAD_DIM, kvh), qg, jnp.zeros_like(qg)))
    s = lax.dot_general(jnp.concatenate(qs, axis=0), kall, (((1,), (1,)), ((), ())),
                        preferred_element_type=F32)
    qpos = i * blk + lax.broadcasted_iota(jnp.int32, (blk, nkeys), 0)
    c = lax.broadcasted_iota(jnp.int32, (blk, nkeys), 1)
    kpos = (i - 1) * blk + c
    valid = ((kpos >= 0) & (kpos < seq) & (jnp.abs(qpos - kpos) <= SWA_WINDOW) & (qpos < seq)) | (c >= nloc)
    vms = [jnp.where(_head_mask(vall.shape, HEAD_DIM, kvh), vall, jnp.zeros_like(vall)) for kvh in range(2)]
    vm = jnp.concatenate(vms, axis=0)
    for g in range(2):
        ps = []
        for kvh in range(2):
            j = 2 * g + kvh
            sink = sink_ref[2 * kvh + g] * LOG2E
            sj = jnp.where(valid, s[j * blk:(j + 1) * blk], NEG)
            m = jnp.maximum(jnp.max(sj, axis=1, keepdims=True), sink)
            e = jnp.exp2(sj - m)
            den = jnp.sum(e, axis=1, keepdims=True) + jnp.exp2(sink - m)
            ps.append((e / den).astype(BF16))
        o_ref[0, :, g * LANES:(g + 1) * LANES] = jnp.dot(
            jnp.concatenate(ps, axis=1), vm, preferred_element_type=F32).astype(BF16)


def _window_attention(P, sink, seq, ctx):
    B, T, _ = P.shape
    blk = SWA_WINDOW
    nt = T // blk
    kern = functools.partial(_swa_kernel, seq=seq, blk=blk)

    def nb(cb, off):
        return pl.BlockSpec((1, blk, LANES), lambda b, i: (b, jnp.clip(i + off, 0, nt - 1), cb))

    def cx(cb):
        return pl.BlockSpec((1, ctx, LANES), lambda b, i: (b, seq // ctx, cb))

    return pl.pallas_call(
        kern,
        grid=(B, nt),
        in_specs=[pl.BlockSpec(memory_space=pltpu.SMEM),
                  pl.BlockSpec((1, blk, GROUP_W), lambda b, i: (b, i, CB_SWA_Q)),
                  nb(CB_SWA_K128, -1), nb(CB_SWA_K128, 0), nb(CB_SWA_K128, 1), cx(CB_SWA_K128),
                  nb(CB_SWA_V128, -1), nb(CB_SWA_V128, 0), nb(CB_SWA_V128, 1), cx(CB_SWA_V128)],
        out_specs=pl.BlockSpec((1, blk, GROUP_W), lambda b, i: (b, i, 0)),
        out_shape=jax.ShapeDtypeStruct((B, T, GROUP_W), BF16),
        compiler_params=_params(("arbitrary", "arbitrary")),
        name="window_attention",
    )(sink, P, P, P, P, P, P, P, P, P)


def _na_kernel(q_ref, k0_ref, k1_ref, k2_ref, kx_ref, v0_ref, v1_ref, v2_ref, vx_ref, bias_ref, o_ref,
               *, tq, n_seq_tiles):
    ctx_tile = pl.program_id(1) >= n_seq_tiles
    q = q_ref[0]
    kall = jnp.concatenate([k0_ref[0], k1_ref[0], k2_ref[0], kx_ref[0]], axis=0)
    vall = jnp.concatenate([v0_ref[0], v1_ref[0], v2_ref[0], vx_ref[0]], axis=0)
    nloc = 3 * tq
    qs = jnp.concatenate(
        [jnp.where(_head_mask(q.shape, HEAD_DIM, h), q, jnp.zeros_like(q)) for h in range(N_HEADS)], axis=0)
    s = lax.dot_general(qs, kall, (((1,), (1,)), ((), ())), preferred_element_type=F32)
    ps = []
    for h in range(N_HEADS):
        sh = s[h * tq:(h + 1) * tq]
        s_loc = jnp.where(ctx_tile, NEG, sh[:, :nloc] + bias_ref[0, h])
        s_ctx = sh[:, nloc:]
        m = jnp.maximum(jnp.max(s_loc, axis=1, keepdims=True), jnp.max(s_ctx, axis=1, keepdims=True))
        e_loc = jnp.exp2(s_loc - m)
        e_ctx = jnp.exp2(s_ctx - m)
        den = jnp.sum(e_loc, axis=1, keepdims=True) + jnp.sum(e_ctx, axis=1, keepdims=True)
        ps.append(jnp.concatenate([e_loc / den, e_ctx / den], axis=1).astype(BF16))
    vm = jnp.concatenate(
        [jnp.where(_head_mask(vall.shape, HEAD_DIM, h), vall, jnp.zeros_like(vall)) for h in range(N_HEADS)],
        axis=0)
    o_ref[0] = jnp.dot(jnp.concatenate(ps, axis=1), vm, preferred_element_type=F32).astype(BF16)


def _na_bias_kernel(rt_ref, o_ref, sv_sc, *, rows):
    tr = NA_TILE_ROWS
    nrt = rows // tr
    nl = 3 * tr * GRID_W
    n_dcol = 2 * NA_WIN_COLS - 1
    rt = rt_ref[0] * LOG2E
    hi = rt.astype(BF16)
    r1 = rt - hi.astype(F32)
    mid = r1.astype(BF16)
    lo = (r1 - mid.astype(F32)).astype(BF16)
    lane = lax.broadcasted_iota(jnp.int32, (GRID_W, nl), 1)
    qc = lax.broadcasted_iota(jnp.int32, (GRID_W, nl), 0)
    kc = lane % GRID_W
    kr = lane // GRID_W
    dc = jnp.clip(kc - qc, 1 - NA_WIN_COLS, NA_WIN_COLS - 1) + (NA_WIN_COLS - 1)
    c0 = jnp.clip(qc - NA_WIN_COLS // 2, 0, GRID_W - NA_WIN_COLS)
    col_ok = (kc >= c0) & (kc < c0 + NA_WIN_COLS)
    irow = lax.broadcasted_iota(jnp.int32, (LANES, nl), 0)
    krl = lax.broadcasted_iota(jnp.int32, (LANES, nl), 1) // GRID_W
    for v, (t_idx, ws) in enumerate(((0, 0), (1, 0), (nrt - 1, nrt - 3))):
        for qr in range(tr):
            r = t_idx * tr + qr
            r0 = min(max(r - NA_WIN_ROWS // 2, 0), rows - NA_WIN_ROWS)
            d_row = jnp.clip(ws * tr + krl - r + (NA_WIN_ROWS - 1), 0, 2 * NA_WIN_ROWS - 2)
            onehot = jnp.where(irow == d_row, 1.0, 0.0).astype(BF16)
            sv_sc[...] = (jnp.dot(hi, onehot, preferred_element_type=F32)
                          + jnp.dot(mid, onehot, preferred_element_type=F32)
                          + jnp.dot(lo, onehot, preferred_element_type=F32))

            def pick(j, acc):
                return jnp.where(dc == j, sv_sc[pl.ds(j, 1), :], acc)

            acc = lax.fori_loop(0, n_dcol, pick, jnp.zeros((GRID_W, nl), F32))
            krow = ws * tr + kr
            ok = col_ok & (krow >= r0) & (krow < r0 + NA_WIN_ROWS)
            o_ref[v, 0, qr * GRID_W:(qr + 1) * GRID_W, :] = jnp.where(ok, acc, NEG)


def _na_bias_tables(rpb, rows):
    H, nr, ncol = rpb.shape
    rt = jnp.zeros((H, 32, LANES), F32).at[:, :ncol, :nr].set(jnp.swapaxes(rpb.astype(F32), 1, 2))
    tq = NA_TILE_ROWS * GRID_W
    return pl.pallas_call(
        functools.partial(_na_bias_kernel, rows=rows),
        grid=(H,),
        in_specs=[pl.BlockSpec((1, 32, LANES), lambda h: (h, 0, 0))],
        out_specs=pl.BlockSpec((3, 1, tq, 3 * tq), lambda h: (0, h, 0, 0)),
        out_shape=jax.ShapeDtypeStruct((3, H, tq, 3 * tq), F32),
        scratch_shapes=[pltpu.VMEM((32, 3 * tq), F32)],
        compiler_params=_params(("arbitrary",)),
        name="na_bias",
    )(rt)


def _neighborhood_attention(P, bias_tabs, seq, ctx):
    B, T, _ = P.shape
    tq = NA_TILE_ROWS * GRID_W
    nst = seq // tq
    nt = T // tq
    kern = functools.partial(_na_kernel, tq=tq, n_seq_tiles=nst)

    def win(i):
        return jnp.clip(i - 1, 0, nst - 3)

    def loc(cb, off):
        return pl.BlockSpec((1, tq, GROUP_W), lambda b, i: (b, win(i) + off, cb))

    def cx(cb):
        return pl.BlockSpec((1, ctx, GROUP_W), lambda b, i: (b, seq // ctx, cb))

    def variant(b, i):
        return (jnp.where(i >= nst, 0, i - win(i)), 0, 0, 0)

    return pl.pallas_call(
        kern,
        grid=(B, nt),
        in_specs=[pl.BlockSpec((1, tq, GROUP_W), lambda b, i: (b, i, CB_NA_Q)),
                  loc(CB_NA_K, 0), loc(CB_NA_K, 1), loc(CB_NA_K, 2), cx(CB_NA_K),
                  loc(CB_NA_V, 0), loc(CB_NA_V, 1), loc(CB_NA_V, 2), cx(CB_NA_V),
                  pl.BlockSpec((1, N_HEADS, tq, 3 * tq), variant)],
        out_specs=pl.BlockSpec((1, tq, GROUP_W), lambda b, i: (b, i, 0)),
        out_shape=jax.ShapeDtypeStruct((B, T, GROUP_W), BF16),
        compiler_params=_params(("arbitrary", "arbitrary")),
        name="neighborhood_attention",
    )(P, P, P, P, P, P, P, P, P, bias_tabs)


def _ret_direction(q, k, v, state_ref, lgs_ref, lgv, forward, base):
    C = q.shape[0]
    i = lax.broadcasted_iota(jnp.int32, (C, C), 0)
    j = lax.broadcasted_iota(jnp.int32, (C, C), 1)
    dist = (i - j if forward else j - i).astype(F32)
    keep = dist >= 0 if forward else dist > 0
    dist = jnp.where(keep, dist, 0.0)
    qs = jnp.concatenate(
        [jnp.where(_head_mask(q.shape, HEAD_DIM, h), q, jnp.zeros_like(q)) for h in range(N_HEADS)], axis=0)
    s = lax.dot_general(qs, k, (((1,), (1,)), ((), ())), preferred_element_type=F32)
    atts = []
    for h in range(N_HEADS):
        decay = jnp.where(keep, jnp.exp(dist * lgs_ref[base + h]), 0.0)
        atts.append((s[h * C:(h + 1) * C] * decay).astype(BF16))
    vm = jnp.concatenate(
        [jnp.where(_head_mask(v.shape, HEAD_DIM, h), v, jnp.zeros_like(v)) for h in range(N_HEADS)], axis=0)
    intra = jnp.dot(jnp.concatenate(atts, axis=1), vm, preferred_element_type=F32)
    r = lax.broadcasted_iota(jnp.int32, (C, 1), 0).astype(F32)
    xi = jnp.exp((r + 1.0 if forward else C - r) * lgv)
    zeta = jnp.exp((C - 1.0 - r if forward else r) * lgv)
    state = state_ref[...]
    cross = jnp.dot((q.astype(F32) * xi).astype(BF16), state.astype(BF16), preferred_element_type=F32)
    kz_t = (k.astype(F32) * zeta).T.astype(BF16)
    u = jnp.dot(kz_t, v, preferred_element_type=F32)
    rr = lax.broadcasted_iota(jnp.int32, u.shape, 0) // HEAD_DIM
    cc = lax.broadcasted_iota(jnp.int32, u.shape, 1) // HEAD_DIM
    state_ref[...] = jnp.where(rr == cc, jnp.exp(C * lgv) * state + u, 0.0)
    return intra + cross


def _ret_kernel(lgs_ref, lgv_ref, qf_ref, kf_ref, vf_ref, qb_ref, kb_ref, vb_ref, of_ref, ob_ref, sf_sc, sb_sc):
    @pl.when(pl.program_id(1) == 0)
    def _():
        sf_sc[...] = jnp.zeros(sf_sc.shape, F32)
        sb_sc[...] = jnp.zeros(sb_sc.shape, F32)

    of_ref[0] = _ret_direction(qf_ref[0], kf_ref[0], vf_ref[0], sf_sc, lgs_ref, lgv_ref[0:1], True, 0)
    ob_ref[0] = _ret_direction(qb_ref[0], kb_ref[0], vb_ref[0], sb_sc, lgs_ref, lgv_ref[1:2], False, N_HEADS)


def _retention(P, lgs, lgv, seq, ctx):
    B, T, _ = P.shape
    C = RET_CHUNK
    ns, nc = seq // C, ctx // C
    nt = ns + nc

    def fwd(n):
        return jnp.where(n < nc, ns + n, n - nc)

    def bwd(n):
        return jnp.where(n < nc, nt - 1 - n, nt - 1 - n)

    def spec(order, cb):
        return pl.BlockSpec((1, C, GROUP_W), lambda b, n: (b, order(n), cb))

    return pl.pallas_call(
        _ret_kernel,
        grid=(B, nt),
        in_specs=[pl.BlockSpec(memory_space=pltpu.SMEM),
                  pl.BlockSpec((8, GROUP_W), lambda b, n: (0, 0)),
                  spec(fwd, CB_RET_Q), spec(fwd, CB_RET_K), spec(fwd, CB_RET_V),
                  spec(bwd, CB_RET_Q), spec(bwd, CB_RET_K), spec(bwd, CB_RET_V)],
        out_specs=[pl.BlockSpec((1, C, GROUP_W), lambda b, n: (b, fwd(n), 0)),
                   pl.BlockSpec((1, C, GROUP_W), lambda b, n: (b, bwd(n), 0))],
        out_shape=[jax.ShapeDtypeStruct((B, T, GROUP_W), F32)] * 2,
        scratch_shapes=[pltpu.VMEM((GROUP_W, GROUP_W), F32)] * 2,
        compiler_params=_params(("arbitrary", "arbitrary")),
        name="retention",
    )(lgs, lgv, P, P, P, P, P, P)


def _outproj_kernel(*refs, tm, seq, n_exp):
    with_router = n_exp > 0
    if with_router:
        (ya_ref, yb_ref, yn_ref, of_ref, ob_ref, gt_ref, x_ref, mod_ref, w_ref, gpost_ref, gpre_ref, r_ref,
         x1_ref, h2_ref, eid_ref, gw_ref) = refs
    else:
        (ya_ref, yb_ref, yn_ref, of_ref, ob_ref, gt_ref, x_ref, mod_ref, w_ref, gpost_ref, gpre_ref,
         x1_ref, h2_ref) = refs
    i = pl.program_id(1)
    rows = i * tm + lax.broadcasted_iota(jnp.int32, (tm, 1), 0)
    is_ctx = rows >= seq
    o = of_ref[0] + ob_ref[0]
    yr = o * lax.rsqrt(_group_mean_sq(o) + EPS) * _silu(gt_ref[0].astype(F32))
    y = (jnp.dot(ya_ref[0], w_ref[0:256, :], preferred_element_type=F32)
         + jnp.dot(yb_ref[0], w_ref[256:512, :], preferred_element_type=F32)
         + jnp.dot(yn_ref[0], w_ref[512:768, :], preferred_element_type=F32)
         + jnp.dot(yr.astype(BF16), w_ref[768:1024, :], preferred_element_type=F32))
    yn = y * lax.rsqrt(jnp.mean(y * y, axis=-1, keepdims=True) + EPS) * gpost_ref[...]
    x1 = x_ref[0] + _row_mod(mod_ref, is_ctx, 2) * yn
    x1_ref[0] = x1
    h = x1 * lax.rsqrt(jnp.mean(x1 * x1, axis=-1, keepdims=True) + EPS) * gpre_ref[...]
    h = h * (1.0 + _row_mod(mod_ref, is_ctx, 4)) + _row_mod(mod_ref, is_ctx, 3)
    h2_ref[0] = h.astype(h2_ref.dtype)
    if with_router:
        logits = jnp.dot(h, r_ref[...], preferred_element_type=F32, precision=lax.Precision.HIGHEST)
        lane = lax.broadcasted_iota(jnp.int32, logits.shape, 1)
        logits = jnp.where(lane < n_exp, logits, NEG)
        v1 = jnp.max(logits, axis=1, keepdims=True)
        i1 = jnp.min(jnp.where(logits == v1, lane, LANES), axis=1, keepdims=True)
        rest = jnp.where(lane == i1, NEG, logits)
        v2 = jnp.max(rest, axis=1, keepdims=True)
        i2 = jnp.min(jnp.where(rest == v2, lane, LANES), axis=1, keepdims=True)
        e2 = jnp.exp(v2 - v1)
        w1 = 1.0 / (1.0 + e2)
        w2 = e2 / (1.0 + e2)
        eid_ref[0] = jnp.where(lane == 0, i1, jnp.where(lane == 1, i2, 0))
        gw_ref[0] = jnp.where(lane == 0, w1, jnp.where(lane == 1, w2, 0.0))


def _outproj(ya, yb, yn, of, ob, P, xa, modv, w_out_b, g_post, g_pre2, rows, seq, tm, router=None, n_exp=0):
    B, _, D = xa.shape
    with_router = router is not None
    kern = functools.partial(_outproj_kernel, tm=tm, seq=seq, n_exp=n_exp)
    tok = lambda w: pl.BlockSpec((1, tm, w), lambda b, i: (b, i, 0))
    in_specs = [tok(GROUP_W), tok(GROUP_W), tok(GROUP_W), tok(GROUP_W), tok(GROUP_W),
                pl.BlockSpec((1, tm, GROUP_W), lambda b, i: (b, i, CB_RET_G)),
                tok(D),
                pl.BlockSpec((1, 2, 8, D), lambda b, i: (b, 0, 0, 0)),
                pl.BlockSpec((D, D), lambda b, i: (0, 0)),
                pl.BlockSpec((1, D), lambda b, i: (0, 0)),
                pl.BlockSpec((1, D), lambda b, i: (0, 0))]
    args = [ya, yb, yn, of, ob, P, xa, modv, w_out_b, g_post.reshape(1, D), g_pre2.reshape(1, D)]
    out_specs = [tok(D), tok(D)]
    out_shape = [jax.ShapeDtypeStruct((B, rows, D), F32),
                 jax.ShapeDtypeStruct((B, rows, D), F32 if with_router else BF16)]
    if with_router:
        in_specs.append(pl.BlockSpec((D, LANES), lambda b, i: (0, 0)))
        args.append(router)
        out_specs += [tok(LANES), tok(LANES)]
        out_shape += [jax.ShapeDtypeStruct((B, rows, LANES), jnp.int32),
                      jax.ShapeDtypeStruct((B, rows, LANES), F32)]
    return pl.pallas_call(
        kern,
        grid=(B, rows // tm),
        in_specs=in_specs,
        out_specs=out_specs,
        out_shape=out_shape,
        compiler_params=_params(("arbitrary", "arbitrary")),
        name="outproj_router" if with_router else "outproj",
    )(*args)


def _ffn_kernel(h_ref, x_ref, mod_ref, wg_ref, wu_ref, wd_ref, gpost_ref, o_ref, *, tm, seq, fc):
    i = pl.program_id(1)
    rows = i * tm + lax.broadcasted_iota(jnp.int32, (tm, 1), 0)
    is_ctx = rows >= seq
    h = h_ref[0]
    acc = jnp.zeros((tm, o_ref.shape[-1]), F32)
    for c in range(wg_ref.shape[1] // fc):
        g = jnp.dot(h, wg_ref[:, c * fc:(c + 1) * fc], preferred_element_type=F32)
        u = jnp.dot(h, wu_ref[:, c * fc:(c + 1) * fc], preferred_element_type=F32)
        a = (_silu(g) * u).astype(BF16)
        acc = acc + jnp.dot(a, wd_ref[c * fc:(c + 1) * fc, :], preferred_element_type=F32)
    yn = acc * lax.rsqrt(jnp.mean(acc * acc, axis=-1, keepdims=True) + EPS) * gpost_ref[...]
    o_ref[0] = x_ref[0] + _row_mod(mod_ref, is_ctx, 5) * yn


def _dense_ffn(h2, x1, modv, wg, wu, wd, g_post, seq, tm):
    B, T, D = x1.shape
    F = wg.shape[1]
    kern = functools.partial(_ffn_kernel, tm=tm, seq=seq, fc=256)
    const = lambda shape: pl.BlockSpec(shape, lambda b, i: (0, 0), pipeline_mode=pl.Buffered(1))
    return pl.pallas_call(
        kern,
        grid=(B, T // tm),
        in_specs=[pl.BlockSpec((1, tm, D), lambda b, i: (b, i, 0)),
                  pl.BlockSpec((1, tm, D), lambda b, i: (b, i, 0)),
                  pl.BlockSpec((1, 2, 8, D), lambda b, i: (b, 0, 0, 0)),
                  const((D, F)), const((D, F)), const((F, D)),
                  pl.BlockSpec((1, D), lambda b, i: (0, 0))],
        out_specs=pl.BlockSpec((1, tm, D), lambda b, i: (b, i, 0)),
        out_shape=jax.ShapeDtypeStruct((B, T, D), F32),
        compiler_params=_params(("arbitrary", "arbitrary")),
        name="dense_ffn",
    )(h2, x1, modv, wg, wu, wd, g_post.reshape(1, D))


def _row_copy(src_hbm, row, dst_vmem, r, sem):
    return pltpu.make_async_copy(src_hbm.at[pl.ds(row, 1)], dst_vmem.at[pl.ds(r, 1)], sem)


def _gather_kernel(src_ref, h_hbm, o_ref, sem, *, tr):
    def issue(r, c):
        _row_copy(h_hbm, src_ref[0, 0, r], o_ref, r, sem).start()
        return c

    lax.fori_loop(0, tr, issue, 0)

    def drain(r, c):
        _row_copy(h_hbm, src_ref[0, 0, r], o_ref, r, sem).wait()
        return c

    lax.fori_loop(0, tr, drain, 0)


def _gather_rows(h_flat, src, n_rows, tr):
    D = h_flat.shape[1]
    kern = functools.partial(_gather_kernel, tr=tr)
    return pl.pallas_call(
        kern,
        grid=(n_rows // tr,),
        in_specs=[pl.BlockSpec((1, 1, tr), lambda i: (i, 0, 0), memory_space=pltpu.SMEM),
                  pl.BlockSpec(memory_space=pl.ANY)],
        out_specs=pl.BlockSpec((tr, D), lambda i: (i, 0)),
        out_shape=jax.ShapeDtypeStruct((n_rows, D), h_flat.dtype),
        scratch_shapes=[pltpu.SemaphoreType.DMA(())],
        compiler_params=_params(("arbitrary",)),
        name="moe_gather",
    )(src.reshape(n_rows // tr, 1, tr), h_flat)


def _moe_ffn_kernel(te_ref, tv_ref, x_ref, wg_ref, wu_ref, wd_ref, o_ref, acc_sc):
    t = pl.program_id(0)
    f = pl.program_id(1)
    nf = pl.num_programs(1)

    @pl.when(f == 0)
    def _():
        acc_sc[...] = jnp.zeros(acc_sc.shape, F32)

    @pl.when(tv_ref[t] > 0)
    def _():
        x = x_ref[...].astype(BF16)
        g = jnp.dot(x, wg_ref[0], preferred_element_type=F32)
        u = jnp.dot(x, wu_ref[0], preferred_element_type=F32)
        a = (_silu(g) * u).astype(BF16)
        acc_sc[...] += jnp.dot(a, wd_ref[0], preferred_element_type=F32)

    @pl.when(f == nf - 1)
    def _():
        o_ref[...] = acc_sc[...]


def _moe_ffn(xs, tile_expert, tile_valid, wg, wu, wd, tr, tf):
    R, D = xs.shape
    F = wg.shape[2]
    return pl.pallas_call(
        _moe_ffn_kernel,
        grid_spec=pltpu.PrefetchScalarGridSpec(
            num_scalar_prefetch=2,
            grid=(R // tr, F // tf),
            in_specs=[pl.BlockSpec((tr, D), lambda t, f, te, tv: (t, 0)),
                      pl.BlockSpec((1, D, tf), lambda t, f, te, tv: (te[t], 0, f)),
                      pl.BlockSpec((1, D, tf), lambda t, f, te, tv: (te[t], 0, f)),
                      pl.BlockSpec((1, tf, D), lambda t, f, te, tv: (te[t], f, 0))],
            out_specs=pl.BlockSpec((tr, D), lambda t, f, te, tv: (t, 0)),
            scratch_shapes=[pltpu.VMEM((tr, D), F32)]),
        out_shape=jax.ShapeDtypeStruct((R, D), F32),
        compiler_params=_params(("arbitrary", "arbitrary")),
        name="moe_ffn",
    )(tile_expert, tile_valid, xs, wg, wu, wd)


def _combine_kernel(pos_ref, ys_hbm, gw_ref, x_ref, mod_ref, gpost_ref, o_ref, y1_sc, y2_sc, sem, *, tm, seq):
    def issue(r, c):
        _row_copy(ys_hbm, pos_ref[0, 0, r], y1_sc, r, sem).start()
        _row_copy(ys_hbm, pos_ref[0, 1, r], y2_sc, r, sem).start()
        return c

    lax.fori_loop(0, tm, issue, 0)

    def drain(r, c):
        _row_copy(ys_hbm, pos_ref[0, 0, r], y1_sc, r, sem).wait()
        _row_copy(ys_hbm, pos_ref[0, 1, r], y2_sc, r, sem).wait()
        return c

    lax.fori_loop(0, tm, drain, 0)
    rows = pl.program_id(1) * tm + lax.broadcasted_iota(jnp.int32, (tm, 1), 0)
    gw = gw_ref[0]
    y = gw[:, 0:1] * y1_sc[...] + gw[:, 1:2] * y2_sc[...]
    yn = y * lax.rsqrt(jnp.mean(y * y, axis=-1, keepdims=True) + EPS) * gpost_ref[...]
    o_ref[0] = x_ref[0] + _row_mod(mod_ref, rows >= seq, 5) * yn


def _moe_combine(pos, ys, gw, x1, modv, g_post, seq, tm):
    B, R, D = x1.shape
    ntile = R // tm
    kern = functools.partial(_combine_kernel, tm=tm, seq=seq)
    return pl.pallas_call(
        kern,
        grid=(B, ntile),
        in_specs=[pl.BlockSpec((1, TOP_K, tm), lambda b, i: (b * ntile + i, 0, 0), memory_space=pltpu.SMEM),
                  pl.BlockSpec(memory_space=pl.ANY),
                  pl.BlockSpec((1, tm, LANES), lambda b, i: (b, i, 0)),
                  pl.BlockSpec((1, tm, D), lambda b, i: (b, i, 0)),
                  pl.BlockSpec((1, 2, 8, D), lambda b, i: (b, 0, 0, 0)),
                  pl.BlockSpec((1, D), lambda b, i: (0, 0))],
        out_specs=pl.BlockSpec((1, tm, D), lambda b, i: (b, i, 0)),
        out_shape=jax.ShapeDtypeStruct((B, R, D), F32),
        scratch_shapes=[pltpu.VMEM((tm, D), F32), pltpu.VMEM((tm, D), F32), pltpu.SemaphoreType.DMA(())],
        compiler_params=_params(("arbitrary", "arbitrary")),
        name="moe_combine",
    )(pos.reshape(B * ntile, tm, TOP_K).transpose(0, 2, 1), ys, gw, x1, modv, g_post.reshape(1, D))


def _moe_routing(eid, n_exp, tr):
    N = eid.shape[0]
    e_flat = eid.reshape(-1)
    onehot = (e_flat[:, None] == jnp.arange(n_exp)[None, :]).astype(jnp.int32)
    rank = jnp.sum((jnp.cumsum(onehot, axis=0) - 1) * onehot, axis=1)
    counts = jnp.sum(onehot, axis=0)
    padded = ((counts + tr - 1) // tr) * tr
    ends = jnp.cumsum(padded)
    starts = ends - padded
    dest = starts[e_flat] + rank
    n_rows = (TOP_K * N // tr + n_exp) * tr
    src = jnp.zeros((n_rows,), jnp.int32).at[dest].set(jnp.arange(TOP_K * N, dtype=jnp.int32) // TOP_K)
    tile_start = jnp.arange(n_rows // tr, dtype=jnp.int32) * tr
    tile_expert = jnp.minimum(jnp.sum((tile_start[:, None] >= ends[None, :]).astype(jnp.int32), axis=1), n_exp - 1)
    tile_valid = (tile_start < ends[-1]).astype(jnp.int32)
    pos = dest.reshape(N, TOP_K).astype(jnp.int32)
    return src, pos, tile_expert, tile_valid, n_rows


def _swa_perm():
    idx = []
    for g in range(2):
        for kvh in range(2):
            idx += [(kvh * 2 + g) * HEAD_DIM + d for d in range(HEAD_DIM)]
    return jnp.array(idx, jnp.int32)


def kernel(x, c, ctx, c_ctx, w_mod, b_mod, g_attn_pre, g_attn_post, g_ffn_pre, g_ffn_post, w_in, w_out,
           da_lambda_q1, da_lambda_k1, da_lambda_q2, da_lambda_k2, da_subln, swa_sink, na_rpb,
           ret_gamma_fwd, ret_gamma_bwd, ffn_w_gate, ffn_w_up, ffn_w_down,
           moe_router, moe_w_gate, moe_w_up, moe_w_down):
    B, S, D = x.shape
    CTX = ctx.shape[1]
    T = S + CTX
    L = w_mod.shape[0]
    n_exp = moe_router.shape[-1]
    rows = S // GRID_W
    t_attn = 256

    c8 = jnp.zeros((8, D), F32).at[:B].set(c).at[B].set(c_ctx)
    mod = _modulation(c8, w_mod, b_mod).reshape(L, 8, 6, D)
    mod = jnp.pad(mod, ((0, 0), (0, 0), (0, 2), (0, 0)))
    modv = jnp.stack([mod[:, :B], jnp.broadcast_to(mod[:, B:B + 1], (L, B, 8, D))], axis=2)

    tabs = _rope_tables(S, CTX)
    perm = _swa_perm()
    swa0 = CB_SWA_Q * 256
    xa = jnp.concatenate([x, ctx], axis=1)

    for l in range(L):
        lambda_init = 0.8 - 0.6 * math.exp(-0.3 * l)
        last = l == L - 1
        w_in_l = w_in[l]
        w_in_b = jnp.concatenate([w_in_l[:, :swa0], w_in_l[:, swa0:swa0 + 256][:, perm], w_in_l[:, swa0 + 256:]],
                                 axis=1).astype(BF16)
        w_out_l = w_out[l]
        w_out_b = jnp.concatenate([w_out_l[:256], w_out_l[256:512][perm], w_out_l[512:]], axis=0).astype(BF16)

        P, QT, VT = _inproj(xa, modv[l], g_attn_pre[l], w_in_b, tabs, S, 256)

        lamp = jnp.zeros((8, LANES), F32)
        for r, v in enumerate((da_lambda_q1[l], da_lambda_k1[l], da_lambda_q2[l], da_lambda_k2[l])):
            lamp = lamp.at[r, :DA_QK].set(v)
        subln_full = jnp.tile(da_subln[l], N_HEADS).reshape(1, GROUP_W)
        tq_da = next(t for t in (512, 256) if S % t == 0)
        tk_da = next(t for t in (768, 512, 256) if T % t == 0)
        ya = _diff_attention(P, QT, VT, lamp, subln_full, lambda_init, 0, S // tq_da, tq_da, 0, T // tk_da, tk_da)
        if not last:
            ya_ctx = _diff_attention(P, QT, VT, lamp, subln_full, lambda_init, S // CTX, 1, CTX, S // CTX, 1, CTX)
            ya = jnp.concatenate([ya, ya_ctx], axis=1)
        yb = _window_attention(P, swa_sink[l].astype(F32), S, CTX)
        yn = _neighborhood_attention(P, _na_bias_tables(na_rpb[l], rows), S, CTX)
        lg = jnp.stack([jax.nn.log_sigmoid(ret_gamma_fwd[l].astype(F32)),
                        jax.nn.log_sigmoid(ret_gamma_bwd[l].astype(F32))])
        lgv = jnp.zeros((8, GROUP_W), F32).at[:2].set(jnp.repeat(lg, HEAD_DIM, axis=1))
        of, ob = _retention(P, lg.reshape(-1), lgv, S, CTX)

        R = S if last else T
        e = l // 2
        if l % 2 == 0:
            x1, h2 = _outproj(ya, yb, yn, of, ob, P, xa, modv[l], w_out_b, g_attn_post[l], g_ffn_pre[l],
                              R, S, 256)
            xa = _dense_ffn(h2, x1, modv[l], ffn_w_gate[e].astype(BF16), ffn_w_up[e].astype(BF16),
                            ffn_w_down[e].astype(BF16), g_ffn_post[l], S, 256)
        else:
            router = jnp.zeros((D, LANES), F32).at[:, :n_exp].set(moe_router[e])
            x1, h2, eid, gw = _outproj(ya, yb, yn, of, ob, P, xa, modv[l], w_out_b, g_attn_post[l],
                                       g_ffn_pre[l], R, S, 256, router=router, n_exp=n_exp)
            tr = 512
            src, pos, tile_expert, tile_valid, n_rows = _moe_routing(
                eid.reshape(B * R, LANES)[:, :TOP_K], n_exp, tr)
            xs = _gather_rows(h2.reshape(B * R, D), src, n_rows, 256)
            ys = _moe_ffn(xs, tile_expert, tile_valid, moe_w_gate[e].astype(BF16), moe_w_up[e].astype(BF16),
                          moe_w_down[e].astype(BF16), tr, 512)
            xa = _moe_combine(pos, ys, gw, x1, modv[l], g_ffn_post[l], S, 256)
    return xa[:, :S]
```

```python
import functools
import math

import jax
import jax.numpy as jnp
from jax import lax
from jax.experimental import pallas as pl
from jax.experimental.pallas import tpu as pltpu

F32 = jnp.float32
BF16 = jnp.bfloat16

GRID_W = 64
HEAD_DIM = 64
N_HEADS = 4
GROUP_W = N_HEADS * HEAD_DIM
DA_QK = 32
SWA_WINDOW = 128
NA_WIN_ROWS = 8
NA_WIN_COLS = 16
NA_TILE_ROWS = 4
RET_CHUNK = 128
ROPE_BASE = 10000.0
TOP_K = 2
EPS = 1e-6
NEG = -1e30
LOG2E = 1.4426950408889634
LANES = 128
ONES_ROWS = 16
ROUTER_ROWS = 16
DA_CHUNK = 64
VMEM_LIMIT = 56 * 1024 * 1024

CB_DA_Q, CB_DA_K, CB_DA_V = 0, 1, 2
CB_SWA_Q = 3
CB_SWA_K128, CB_SWA_V128 = 8, 9
CB_NA_Q, CB_NA_K, CB_NA_V = 5, 6, 7
CB_RET_Q, CB_RET_K, CB_RET_V, CB_RET_G = 8, 9, 10, 11
IN_WIDTH = 3072


def _params(sem):
    return pltpu.CompilerParams(dimension_semantics=sem, vmem_limit_bytes=VMEM_LIMIT)


def _silu(v):
    return v / (1.0 + jnp.exp(-v))


def _head_mask(shape, head_w, h, dim=1):
    return lax.broadcasted_iota(jnp.int32, shape, dim) // head_w == h


def _lane_pick(a, b):
    lane = lax.broadcasted_iota(jnp.int32, a.shape, 1)
    return jnp.where(lane < HEAD_DIM, a, b)


def _per_head_full(vals):
    return jnp.concatenate([_lane_pick(vals[0], vals[1]), _lane_pick(vals[2], vals[3])], axis=1)


def _group_mean_sq(a):
    sq = a * a
    r = lax.broadcasted_iota(jnp.int32, (GROUP_W, GROUP_W), 0) // HEAD_DIM
    c = lax.broadcasted_iota(jnp.int32, (GROUP_W, GROUP_W), 1) // HEAD_DIM
    g = jnp.where(r == c, 1.0, 0.0).astype(BF16)
    hi = sq.astype(BF16)
    lo = (sq - hi.astype(F32)).astype(BF16)
    tot = jnp.dot(hi, g, preferred_element_type=F32) + jnp.dot(lo, g, preferred_element_type=F32)
    return tot * (1.0 / HEAD_DIM)


def _row_mod(mod_ref, rows_are_ctx, j):
    return jnp.where(rows_are_ctx, mod_ref[0, 1, j:j + 1, :], mod_ref[0, 0, j:j + 1, :])


def _mod_kernel(c_ref, w_ref, b_ref, o_ref):
    s = _silu(c_ref[...])
    o_ref[0] = jnp.dot(s.astype(BF16), w_ref[0].astype(BF16), preferred_element_type=F32) + b_ref[0]


def _modulation(c8, w_mod, b_mod):
    L, D, W = w_mod.shape
    tn = 1536
    return pl.pallas_call(
        _mod_kernel,
        grid=(L, W // tn),
        in_specs=[pl.BlockSpec((8, D), lambda l, j: (0, 0)),
                  pl.BlockSpec((1, D, tn), lambda l, j: (l, 0, j)),
                  pl.BlockSpec((1, 1, tn), lambda l, j: (l, 0, j))],
        out_specs=pl.BlockSpec((1, 8, tn), lambda l, j: (l, 0, j)),
        out_shape=jax.ShapeDtypeStruct((L, 8, W), F32),
        compiler_params=_params(("arbitrary", "arbitrary")),
        name="modulation",
    )(c8, w_mod, b_mod.reshape(L, 1, W))


def _lane_block_ops():
    da_s = DA_QK ** -0.5 * LOG2E
    s64 = HEAD_DIM ** -0.5
    ops = []
    ops += [("r32", da_s)] * 2 + [("r32", 1.0)] * 2 + [(None, 1.0)] * 2
    ops += [("r64", s64 * LOG2E)] * 2 + [("r64", 1.0)] + [(None, 1.0)]
    ops += [(None, s64 * LOG2E)] * 2 + [(None, 1.0)] * 4
    ops += [("r64", 1.0)] * 2 + [("r64", s64)] * 2 + [(None, 1.0)] * 4
    return ops


def _rope(x, cos, sin_signed, w):
    lane = lax.broadcasted_iota(jnp.int32, x.shape, 1)
    first = (lane % (2 * w)) < w
    xr = jnp.where(first, pltpu.roll(x, LANES - w, 1), pltpu.roll(x, w, 1))
    return x * cos + xr * sin_signed


def _inproj_kernel(x_ref, mod_ref, g_ref, w_ref, c32_ref, s32_ref, c64_ref, s64_ref, o_ref, qt_ref, vt_ref,
                   *, tm, seq):
    i = pl.program_id(1)
    x = x_ref[0]
    rows = i * tm + lax.broadcasted_iota(jnp.int32, (tm, 1), 0)
    is_ctx = rows >= seq
    ms = jnp.mean(x * x, axis=-1, keepdims=True)
    h = x * lax.rsqrt(ms + EPS) * g_ref[...]
    h = h * (1.0 + _row_mod(mod_ref, is_ctx, 1)) + _row_mod(mod_ref, is_ctx, 0)
    hb = h.astype(BF16)
    ops = _lane_block_ops()
    for cb in range(IN_WIDTH // 256):
        p = jnp.dot(hb, w_ref[:, cb * 256:(cb + 1) * 256], preferred_element_type=F32)
        halves = []
        for hf in range(2):
            kind, scale = ops[cb * 2 + hf]
            ph = p[:, hf * LANES:(hf + 1) * LANES]
            if kind == "r32":
                ph = _rope(ph, c32_ref[...], s32_ref[...], DA_QK // 4)
            elif kind == "r64":
                ph = _rope(ph, c64_ref[...], s64_ref[...], HEAD_DIM // 4)
            if scale != 1.0:
                ph = ph * scale
            halves.append(ph)
        full = jnp.concatenate(halves, axis=1)
        o_ref[0, :, cb * 256:(cb + 1) * 256] = full.astype(BF16)
        if cb == CB_DA_Q:
            qt_ref[0] = full.T.astype(BF16)
        elif cb == CB_DA_V:
            vt_ref[0] = full.T.astype(BF16)


def _inproj(xa, modv, g, w_in_b, tabs, seq, tm):
    B, T, D = xa.shape
    kern = functools.partial(_inproj_kernel, tm=tm, seq=seq)
    tab_spec = pl.BlockSpec((tm, LANES), lambda b, i: (i, 0))
    return pl.pallas_call(
        kern,
        grid=(B, T // tm),
        in_specs=[pl.BlockSpec((1, tm, D), lambda b, i: (b, i, 0)),
                  pl.BlockSpec((1, 2, 8, D), lambda b, i: (b, 0, 0, 0)),
                  pl.BlockSpec((1, D), lambda b, i: (0, 0)),
                  pl.BlockSpec((D, IN_WIDTH), lambda b, i: (0, 0)),
                  tab_spec, tab_spec, tab_spec, tab_spec],
        out_specs=[pl.BlockSpec((1, tm, IN_WIDTH), lambda b, i: (b, i, 0)),
                   pl.BlockSpec((1, GROUP_W, tm), lambda b, i: (b, 0, i)),
                   pl.BlockSpec((1, GROUP_W, tm), lambda b, i: (b, 0, i))],
        out_shape=[jax.ShapeDtypeStruct((B, T, IN_WIDTH), BF16),
                   jax.ShapeDtypeStruct((B, GROUP_W, T), BF16),
                   jax.ShapeDtypeStruct((B, GROUP_W, T), BF16)],
        compiler_params=_params(("arbitrary", "arbitrary")),
        name="inproj",
    )(xa, modv, g.reshape(1, D), w_in_b, *tabs)


def _rope_tables(seq, ctx):
    t = jnp.arange(seq)
    row = (t // GRID_W).astype(F32)
    col = (t % GRID_W).astype(F32)
    out = []
    for dh in (DA_QK, HEAD_DIM):
        half = dh // 2
        quarter = half // 2
        lane = jnp.arange(LANES)
        d = lane % dh
        use_col = (d // half) == 1
        idx = (d % quarter).astype(F32)
        inv = ROPE_BASE ** (-idx * 2.0 / half)
        pos = jnp.where(use_col[None, :], col[:, None], row[:, None])
        ang = pos * inv[None, :]
        first = (d % half) < quarter
        cos = jnp.cos(ang)
        sin = jnp.where(first[None, :], -jnp.sin(ang), jnp.sin(ang))
        cos = jnp.concatenate([cos, jnp.ones((ctx, LANES), F32)], axis=0)
        sin = jnp.concatenate([sin, jnp.zeros((ctx, LANES), F32)], axis=0)
        out += [cos, sin]
    return out


def _rows_per_head(vals, tq):
    return jnp.concatenate([jnp.broadcast_to(v, (HEAD_DIM, tq)) for v in vals], axis=0)


def _da_kernel(lamp_ref, qt_ref, k_ref, vt_ref, g_ref, o_ref, qm_sc, m_sc, l_sc, acc_sc, st_sc, p_sc,
               *, tq, lambda_init):
    ki = pl.program_id(2)
    nk = pl.num_programs(2)

    @pl.when(ki == 0)
    def _():
        qt = qt_ref[0]
        for j in range(2 * N_HEADS):
            qm_sc[j] = jnp.where(_head_mask(qt.shape, DA_QK, j, 0), qt, jnp.zeros_like(qt))
        m_sc[...] = jnp.full(m_sc.shape, NEG, F32)
        l_sc[...] = jnp.zeros(l_sc.shape, F32)
        acc_sc[...] = jnp.zeros(acc_sc.shape, F32)

    k = k_ref[0]
    vt = vt_ref[0]
    ones = jnp.ones((ONES_ROWS, vt.shape[1]), BF16)

    tk = k.shape[0]

    def scores(j):
        st_sc[j % 2] = jnp.dot(k, qm_sc[j], preferred_element_type=F32)
        run = st_sc[j % 2, 0:DA_CHUNK, :]
        for c in range(1, tk // DA_CHUNK):
            run = jnp.maximum(run, st_sc[j % 2, c * DA_CHUNK:(c + 1) * DA_CHUNK, :])
        return jnp.max(run, axis=0, keepdims=True)

    mx = scores(0)
    for j in range(2 * N_HEADS):
        h, t = j // 2, j % 2
        mx_next = scores(j + 1) if j + 1 < 2 * N_HEADS else None
        rows = slice(h * HEAD_DIM, (h + 1) * HEAD_DIM)
        vte = jnp.concatenate([vt[rows, :], ones], axis=0)
        m_prev = m_sc[j]
        m_new = jnp.maximum(m_prev, mx)
        alpha = jnp.exp2(m_prev - m_new)
        for c in range(tk // DA_CHUNK):
            cr = slice(c * DA_CHUNK, (c + 1) * DA_CHUNK)
            p_sc[j % 2, cr, :] = jnp.exp2(st_sc[j % 2, cr, :] - m_new).astype(BF16)
        pv = jnp.dot(vte, p_sc[j % 2], preferred_element_type=F32)
        l_sc[j] = alpha * l_sc[j] + pv[HEAD_DIM:HEAD_DIM + 1, :]
        m_sc[j] = m_new
        acc_sc[t, rows, :] = alpha * acc_sc[t, rows, :] + pv[:HEAD_DIM, :]
        mx = mx_next

    @pl.when(ki == nk - 1)
    def _():
        lp = lamp_ref[...]
        lam = (jnp.exp(jnp.sum(lp[0:1] * lp[1:2], axis=1, keepdims=True))
               - jnp.exp(jnp.sum(lp[2:3] * lp[3:4], axis=1, keepdims=True))) + lambda_init
        o0 = acc_sc[0] / _rows_per_head([l_sc[2 * h] for h in range(N_HEADS)], tq)
        o1 = acc_sc[1] / _rows_per_head([l_sc[2 * h + 1] for h in range(N_HEADS)], tq)
        a = (o0 - lam * o1).T
        y = a * lax.rsqrt(_group_mean_sq(a) + EPS) * g_ref[...]
        o_ref[0] = (y * (1.0 - lambda_init)).astype(BF16)


def _diff_attention(P, QT, VT, lamp, subln_full, lambda_init, q0, nq, tq, k0, nk, tk):
    B = P.shape[0]
    kern = functools.partial(_da_kernel, tq=tq, lambda_init=lambda_init)
    return pl.pallas_call(
        kern,
        grid=(B, nq, nk),
        in_specs=[pl.BlockSpec((8, LANES), lambda b, qi, ki: (0, 0)),
                  pl.BlockSpec((1, GROUP_W, tq), lambda b, qi, ki: (b, 0, q0 + qi)),
                  pl.BlockSpec((1, tk, GROUP_W), lambda b, qi, ki: (b, k0 + ki, CB_DA_K)),
                  pl.BlockSpec((1, GROUP_W, tk), lambda b, qi, ki: (b, 0, k0 + ki)),
                  pl.BlockSpec((1, GROUP_W), lambda b, qi, ki: (0, 0))],
        out_specs=pl.BlockSpec((1, tq, GROUP_W), lambda b, qi, ki: (b, qi, 0)),
        out_shape=jax.ShapeDtypeStruct((B, nq * tq, GROUP_W), BF16),
        scratch_shapes=[pltpu.VMEM((2 * N_HEADS, GROUP_W, tq), BF16),
                        pltpu.VMEM((2 * N_HEADS, 1, tq), F32),
                        pltpu.VMEM((2 * N_HEADS, 1, tq), F32),
                        pltpu.VMEM((2, GROUP_W, tq), F32),
                        pltpu.VMEM((2, tk, tq), F32),
                        pltpu.VMEM((2, tk, tq), BF16)],
        compiler_params=_params(("arbitrary", "arbitrary", "arbitrary")),
        name="diff_attention",
    )(lamp, QT, P, VT, subln_full)


def _swa_kernel(sink_ref, q_ref, kp_ref, kc_ref, kn_ref, kx_ref, vp_ref, vc_ref, vn_ref, vx_ref, o_ref,
                *, seq, blk):
    i = pl.program_id(1)
    q = q_ref[0]
    kall = jnp.concatenate([kp_ref[0], kc_ref[0], kn_ref[0], kx_ref[0]], axis=0)
    vall = jnp.concatenate([vp_ref[0], vc_ref[0], vn_ref[0], vx_ref[0]], axis=0)
    nloc = 3 * blk
    nkeys = kall.shape[0]
    qs = []
    for g in range(2):
        qg = q[:, g * LANES:(g + 1) * LANES]
        for kvh in range(2):
            qs.append(jnp.where(_head_mask(qg.shape, HEAD_DIM, kvh), qg, jnp.zeros_like(qg)))
    s = lax.dot_general(jnp.concatenate(qs, axis=0), kall, (((1,), (1,)), ((), ())),
                        preferred_element_type=F32)
    qpos = i * blk + lax.broadcasted_iota(jnp.int32, (blk, nkeys), 0)
    c = lax.broadcasted_iota(jnp.int32, (blk, nkeys), 1)
    kpos = (i - 1) * blk + c
    valid = ((kpos >= 0) & (kpos < seq) & (jnp.abs(qpos - kpos) <= SWA_WINDOW) & (qpos < seq)) | (c >= nloc)
    vms = [jnp.where(_head_mask(vall.shape, HEAD_DIM, kvh), vall, jnp.zeros_like(vall)) for kvh in range(2)]
    vm = jnp.concatenate(vms, axis=0)
    for g in range(2):
        ps = []
        for kvh in range(2):
            j = 2 * g + kvh
            sink = sink_ref[2 * kvh + g] * LOG2E
            sj = jnp.where(valid, s[j * blk:(j + 1) * blk], NEG)
            m = jnp.maximum(jnp.max(sj, axis=1, keepdims=True), sink)
            e = jnp.exp2(sj - m)
            den = jnp.sum(e, axis=1, keepdims=True) + jnp.exp2(sink - m)
            ps.append((e / den).astype(BF16))
        o_ref[0, :, g * LANES:(g + 1) * LANES] = jnp.dot(
            jnp.concatenate(ps, axis=1), vm, preferred_element_type=F32).astype(BF16)


def _window_attention(P, sink, seq, ctx):
    B, T, _ = P.shape
    blk = SWA_WINDOW
    nt = T // blk
    kern = functools.partial(_swa_kernel, seq=seq, blk=blk)

    def nb(cb, off):
        return pl.BlockSpec((1, blk, LANES), lambda b, i: (b, jnp.clip(i + off, 0, nt - 1), cb))

    def cx(cb):
        return pl.BlockSpec((1, ctx, LANES), lambda b, i: (b, seq // ctx, cb))

    return pl.pallas_call(
        kern,
        grid=(B, nt),
        in_specs=[pl.BlockSpec(memory_space=pltpu.SMEM),
                  pl.BlockSpec((1, blk, GROUP_W), lambda b, i: (b, i, CB_SWA_Q)),
                  nb(CB_SWA_K128, -1), nb(CB_SWA_K128, 0), nb(CB_SWA_K128, 1), cx(CB_SWA_K128),
                  nb(CB_SWA_V128, -1), nb(CB_SWA_V128, 0), nb(CB_SWA_V128, 1), cx(CB_SWA_V128)],
        out_specs=pl.BlockSpec((1, blk, GROUP_W), lambda b, i: (b, i, 0)),
        out_shape=jax.ShapeDtypeStruct((B, T, GROUP_W), BF16),
        compiler_params=_params(("arbitrary", "arbitrary")),
        name="window_attention",
    )(sink, P, P, P, P, P, P, P, P, P)


def _na_kernel(q_ref, k0_ref, k1_ref, k2_ref, kx_ref, v0_ref, v1_ref, v2_ref, vx_ref, bias_ref, o_ref,
               *, tq, n_seq_tiles):
    ctx_tile = pl.program_id(1) >= n_seq_tiles
    q = q_ref[0]
    kall = jnp.concatenate([k0_ref[0], k1_ref[0], k2_ref[0], kx_ref[0]], axis=0)
    vall = jnp.concatenate([v0_ref[0], v1_ref[0], v2_ref[0], vx_ref[0]], axis=0)
    nloc = 3 * tq
    qs = jnp.concatenate(
        [jnp.where(_head_mask(q.shape, HEAD_DIM, h), q, jnp.zeros_like(q)) for h in range(N_HEADS)], axis=0)
    s = lax.dot_general(qs, kall, (((1,), (1,)), ((), ())), preferred_element_type=F32)
    ps = []
    for h in range(N_HEADS):
        sh = s[h * tq:(h + 1) * tq]
        s_loc = jnp.where(ctx_tile, NEG, sh[:, :nloc] + bias_ref[0, h])
        s_ctx = sh[:, nloc:]
        m = jnp.maximum(jnp.max(s_loc, axis=1, keepdims=True), jnp.max(s_ctx, axis=1, keepdims=True))
        e_loc = jnp.exp2(s_loc - m)
        e_ctx = jnp.exp2(s_ctx - m)
        den = jnp.sum(e_loc, axis=1, keepdims=True) + jnp.sum(e_ctx, axis=1, keepdims=True)
        ps.append(jnp.concatenate([e_loc / den, e_ctx / den], axis=1).astype(BF16))
    vm = jnp.concatenate(
        [jnp.where(_head_mask(vall.shape, HEAD_DIM, h), vall, jnp.zeros_like(vall)) for h in range(N_HEADS)],
        axis=0)
    o_ref[0] = jnp.dot(jnp.concatenate(ps, axis=1), vm, preferred_element_type=F32).astype(BF16)


def _na_bias_kernel(rt_ref, o_ref, sv_sc, *, rows):
    tr = NA_TILE_ROWS
    nrt = rows // tr
    nl = 3 * tr * GRID_W
    n_dcol = 2 * NA_WIN_COLS - 1
    rt = rt_ref[0] * LOG2E
    hi = rt.astype(BF16)
    r1 = rt - hi.astype(F32)
    mid = r1.astype(BF16)
    lo = (r1 - mid.astype(F32)).astype(BF16)
    lane = lax.broadcasted_iota(jnp.int32, (GRID_W, nl), 1)
    qc = lax.broadcasted_iota(jnp.int32, (GRID_W, nl), 0)
    kc = lane % GRID_W
    kr = lane // GRID_W
    dc = jnp.clip(kc - qc, 1 - NA_WIN_COLS, NA_WIN_COLS - 1) + (NA_WIN_COLS - 1)
    c0 = jnp.clip(qc - NA_WIN_COLS // 2, 0, GRID_W - NA_WIN_COLS)
    col_ok = (kc >= c0) & (kc < c0 + NA_WIN_COLS)
    irow = lax.broadcasted_iota(jnp.int32, (LANES, nl), 0)
    krl = lax.broadcasted_iota(jnp.int32, (LANES, nl), 1) // GRID_W
    for v, (t_idx, ws) in enumerate(((0, 0), (1, 0), (nrt - 1, nrt - 3))):
        for qr in range(tr):
            r = t_idx * tr + qr
            r0 = min(max(r - NA_WIN_ROWS // 2, 0), rows - NA_WIN_ROWS)
            d_row = jnp.clip(ws * tr + krl - r + (NA_WIN_ROWS - 1), 0, 2 * NA_WIN_ROWS - 2)
            onehot = jnp.where(irow == d_row, 1.0, 0.0).astype(BF16)
            sv_sc[...] = (jnp.dot(hi, onehot, preferred_element_type=F32)
                          + jnp.dot(mid, onehot, preferred_element_type=F32)
                          + jnp.dot(lo, onehot, preferred_element_type=F32))

            def pick(j, acc):
                return jnp.where(dc == j, sv_sc[pl.ds(j, 1), :], acc)

            acc = lax.fori_loop(0, n_dcol, pick, jnp.zeros((GRID_W, nl), F32))
            krow = ws * tr + kr
            ok = col_ok & (krow >= r0) & (krow < r0 + NA_WIN_ROWS)
            o_ref[v, 0, qr * GRID_W:(qr + 1) * GRID_W, :] = jnp.where(ok, acc, NEG)


def _na_bias_tables(rpb, rows):
    H, nr, ncol = rpb.shape
    rt = jnp.zeros((H, 32, LANES), F32).at[:, :ncol, :nr].set(jnp.swapaxes(rpb.astype(F32), 1, 2))
    tq = NA_TILE_ROWS * GRID_W
    return pl.pallas_call(
        functools.partial(_na_bias_kernel, rows=rows),
        grid=(H,),
        in_specs=[pl.BlockSpec((1, 32, LANES), lambda h: (h, 0, 0))],
        out_specs=pl.BlockSpec((3, 1, tq, 3 * tq), lambda h: (0, h, 0, 0)),
        out_shape=jax.ShapeDtypeStruct((3, H, tq, 3 * tq), F32),
        scratch_shapes=[pltpu.VMEM((32, 3 * tq), F32)],
        compiler_params=_params(("arbitrary",)),
        name="na_bias",
    )(rt)


def _neighborhood_attention(P, bias_tabs, seq, ctx):
    B, T, _ = P.shape
    tq = NA_TILE_ROWS * GRID_W
    nst = seq // tq
    nt = T // tq
    kern = functools.partial(_na_kernel, tq=tq, n_seq_tiles=nst)

    def win(i):
        return jnp.clip(i - 1, 0, nst - 3)

    def loc(cb, off):
        return pl.BlockSpec((1, tq, GROUP_W), lambda b, i: (b, win(i) + off, cb))

    def cx(cb):
        return pl.BlockSpec((1, ctx, GROUP_W), lambda b, i: (b, seq // ctx, cb))

    def variant(b, i):
        return (jnp.where(i >= nst, 0, i - win(i)), 0, 0, 0)

    return pl.pallas_call(
        kern,
        grid=(B, nt),
        in_specs=[pl.BlockSpec((1, tq, GROUP_W), lambda b, i: (b, i, CB_NA_Q)),
                  loc(CB_NA_K, 0), loc(CB_NA_K, 1), loc(CB_NA_K, 2), cx(CB_NA_K),
                  loc(CB_NA_V, 0), loc(CB_NA_V, 1), loc(CB_NA_V, 2), cx(CB_NA_V),
                  pl.BlockSpec((1, N_HEADS, tq, 3 * tq), variant)],
        out_specs=pl.BlockSpec((1, tq, GROUP_W), lambda b, i: (b, i, 0)),
        out_shape=jax.ShapeDtypeStruct((B, T, GROUP_W), BF16),
        compiler_params=_params(("arbitrary", "arbitrary")),
        name="neighborhood_attention",
    )(P, P, P, P, P, P, P, P, P, bias_tabs)


def _ret_direction(q, k, v, state_ref, lgs_ref, lgv, forward, base):
    C = q.shape[0]
    i = lax.broadcasted_iota(jnp.int32, (C, C), 0)
    j = lax.broadcasted_iota(jnp.int32, (C, C), 1)
    dist = (i - j if forward else j - i).astype(F32)
    keep = dist >= 0 if forward else dist > 0
    dist = jnp.where(keep, dist, 0.0)
    qs = jnp.concatenate(
        [jnp.where(_head_mask(q.shape, HEAD_DIM, h), q, jnp.zeros_like(q)) for h in range(N_HEADS)], axis=0)
    s = lax.dot_general(qs, k, (((1,), (1,)), ((), ())), preferred_element_type=F32)
    atts = []
    for h in range(N_HEADS):
        decay = jnp.where(keep, jnp.exp(dist * lgs_ref[base + h]), 0.0)
        atts.append((s[h * C:(h + 1) * C] * decay).astype(BF16))
    vm = jnp.concatenate(
        [jnp.where(_head_mask(v.shape, HEAD_DIM, h), v, jnp.zeros_like(v)) for h in range(N_HEADS)], axis=0)
    intra = jnp.dot(jnp.concatenate(atts, axis=1), vm, preferred_element_type=F32)
    r = lax.broadcasted_iota(jnp.int32, (C, 1), 0).astype(F32)
    xi = jnp.exp((r + 1.0 if forward else C - r) * lgv)
    zeta = jnp.exp((C - 1.0 - r if forward else r) * lgv)
    state = state_ref[...]
    cross = jnp.dot((q.astype(F32) * xi).astype(BF16), state.astype(BF16), preferred_element_type=F32)
    kz_t = (k.astype(F32) * zeta).T.astype(BF16)
    u = jnp.dot(kz_t, v, preferred_element_type=F32)
    rr = lax.broadcasted_iota(jnp.int32, u.shape, 0) // HEAD_DIM
    cc = lax.broadcasted_iota(jnp.int32, u.shape, 1) // HEAD_DIM
    state_ref[...] = jnp.where(rr == cc, jnp.exp(C * lgv) * state + u, 0.0)
    return intra + cross


def _ret_kernel(lgs_ref, lgv_ref, qf_ref, kf_ref, vf_ref, qb_ref, kb_ref, vb_ref, of_ref, ob_ref, sf_sc, sb_sc):
    @pl.when(pl.program_id(1) == 0)
    def _():
        sf_sc[...] = jnp.zeros(sf_sc.shape, F32)
        sb_sc[...] = jnp.zeros(sb_sc.shape, F32)

    of_ref[0] = _ret_direction(qf_ref[0], kf_ref[0], vf_ref[0], sf_sc, lgs_ref, lgv_ref[0:1], True, 0)
    ob_ref[0] = _ret_direction(qb_ref[0], kb_ref[0], vb_ref[0], sb_sc, lgs_ref, lgv_ref[1:2], False, N_HEADS)


def _retention(P, lgs, lgv, seq, ctx):
    B, T, _ = P.shape
    C = RET_CHUNK
    ns, nc = seq // C, ctx // C
    nt = ns + nc

    def fwd(n):
        return jnp.where(n < nc, ns + n, n - nc)

    def bwd(n):
        return jnp.where(n < nc, nt - 1 - n, nt - 1 - n)

    def spec(order, cb):
        return pl.BlockSpec((1, C, GROUP_W), lambda b, n: (b, order(n), cb))

    return pl.pallas_call(
        _ret_kernel,
        grid=(B, nt),
        in_specs=[pl.BlockSpec(memory_space=pltpu.SMEM),
                  pl.BlockSpec((8, GROUP_W), lambda b, n: (0, 0)),
                  spec(fwd, CB_RET_Q), spec(fwd, CB_RET_K), spec(fwd, CB_RET_V),
                  spec(bwd, CB_RET_Q), spec(bwd, CB_RET_K), spec(bwd, CB_RET_V)],
        out_specs=[pl.BlockSpec((1, C, GROUP_W), lambda b, n: (b, fwd(n), 0)),
                   pl.BlockSpec((1, C, GROUP_W), lambda b, n: (b, bwd(n), 0))],
        out_shape=[jax.ShapeDtypeStruct((B, T, GROUP_W), F32)] * 2,
        scratch_shapes=[pltpu.VMEM((GROUP_W, GROUP_W), F32)] * 2,
        compiler_params=_params(("arbitrary", "arbitrary")),
        name="retention",
    )(lgs, lgv, P, P, P, P, P, P)


def _outproj_kernel(*refs, tm, seq, n_exp):
    with_router = n_exp > 0
    if with_router:
        (ya_ref, yb_ref, yn_ref, of_ref, ob_ref, gt_ref, x_ref, mod_ref, w_ref, gpost_ref, gpre_ref, r_ref,
         x1_ref, h2_ref, eid_ref, gw_ref) = refs
    else:
        (ya_ref, yb_ref, yn_ref, of_ref, ob_ref, gt_ref, x_ref, mod_ref, w_ref, gpost_ref, gpre_ref,
         x1_ref, h2_ref) = refs
    i = pl.program_id(1)
    rows = i * tm + lax.broadcasted_iota(jnp.int32, (tm, 1), 0)
    is_ctx = rows >= seq
    o = of_ref[0] + ob_ref[0]
    yr = o * lax.rsqrt(_group_mean_sq(o) + EPS) * _silu(gt_ref[0].astype(F32))
    y = (jnp.dot(ya_ref[0], w_ref[0:256, :], preferred_element_type=F32)
         + jnp.dot(yb_ref[0], w_ref[256:512, :], preferred_element_type=F32)
         + jnp.dot(yn_ref[0], w_ref[512:768, :], preferred_element_type=F32)
         + jnp.dot(yr.astype(BF16), w_ref[768:1024, :], preferred_element_type=F32))
    yn = y * lax.rsqrt(jnp.mean(y * y, axis=-1, keepdims=True) + EPS) * gpost_ref[...]
    x1 = x_ref[0] + _row_mod(mod_ref, is_ctx, 2) * yn
    x1_ref[0] = x1
    h = x1 * lax.rsqrt(jnp.mean(x1 * x1, axis=-1, keepdims=True) + EPS) * gpre_ref[...]
    h = h * (1.0 + _row_mod(mod_ref, is_ctx, 4)) + _row_mod(mod_ref, is_ctx, 3)
    h2_ref[0] = h.astype(h2_ref.dtype)
    if with_router:
        r = r_ref[...]
        r_hi = r.astype(BF16)
        r_lo = (r - r_hi.astype(F32)).astype(BF16)
        h_hi = h.astype(BF16)
        h_lo = (h - h_hi.astype(F32)).astype(BF16)
        dn = (((1,), (1,)), ((), ()))
        lt = (lax.dot_general(r_hi, h_hi, dn, preferred_element_type=F32)
              + lax.dot_general(r_hi, h_lo, dn, preferred_element_type=F32)
              + lax.dot_general(r_lo, h_hi, dn, preferred_element_type=F32))
        e = lax.broadcasted_iota(jnp.int32, lt.shape, 0)
        lt = jnp.where(e < n_exp, lt, NEG)
        v1 = jnp.max(lt, axis=0, keepdims=True)
        i1 = jnp.min(jnp.where(lt == v1, e, ROUTER_ROWS), axis=0, keepdims=True)
        rest = jnp.where(e == i1, NEG, lt)
        v2 = jnp.max(rest, axis=0, keepdims=True)
        i2 = jnp.min(jnp.where(rest == v2, e, ROUTER_ROWS), axis=0, keepdims=True)
        e2 = jnp.exp(v2 - v1)
        w1 = 1.0 / (1.0 + e2)
        w2 = e2 / (1.0 + e2)
        eid_ref[0] = jnp.where(e == 0, i1, jnp.where(e == 1, i2, 0))[:8]
        wrow = jnp.where(e == 0, w1, jnp.where(e == 1, w2, 0.0))
        gw_ref[0] = jnp.concatenate([wrow, jnp.zeros((LANES - ROUTER_ROWS, tm), F32)], axis=0).T


def _outproj(ya, yb, yn, of, ob, P, xa, modv, w_out_b, g_post, g_pre2, rows, seq, tm, router=None, n_exp=0):
    B, _, D = xa.shape
    with_router = router is not None
    kern = functools.partial(_outproj_kernel, tm=tm, seq=seq, n_exp=n_exp)
    tok = lambda w: pl.BlockSpec((1, tm, w), lambda b, i: (b, i, 0))
    in_specs = [tok(GROUP_W), tok(GROUP_W), tok(GROUP_W), tok(GROUP_W), tok(GROUP_W),
                pl.BlockSpec((1, tm, GROUP_W), lambda b, i: (b, i, CB_RET_G)),
                tok(D),
                pl.BlockSpec((1, 2, 8, D), lambda b, i: (b, 0, 0, 0)),
                pl.BlockSpec((D, D), lambda b, i: (0, 0)),
                pl.BlockSpec((1, D), lambda b, i: (0, 0)),
                pl.BlockSpec((1, D), lambda b, i: (0, 0))]
    args = [ya, yb, yn, of, ob, P, xa, modv, w_out_b, g_post.reshape(1, D), g_pre2.reshape(1, D)]
    out_specs = [tok(D), tok(D)]
    out_shape = [jax.ShapeDtypeStruct((B, rows, D), F32),
                 jax.ShapeDtypeStruct((B, rows, D), F32 if with_router else BF16)]
    if with_router:
        in_specs.append(pl.BlockSpec((ROUTER_ROWS, D), lambda b, i: (0, 0)))
        args.append(router)
        out_specs += [pl.BlockSpec((1, 8, tm), lambda b, i: (b, 0, i)), tok(LANES)]
        out_shape += [jax.ShapeDtypeStruct((B, 8, rows), jnp.int32),
                      jax.ShapeDtypeStruct((B, rows, LANES), F32)]
    return pl.pallas_call(
        kern,
        grid=(B, rows // tm),
        in_specs=in_specs,
        out_specs=out_specs,
        out_shape=out_shape,
        compiler_params=_params(("arbitrary", "arbitrary")),
        name="outproj_router" if with_router else "outproj",
    )(*args)


def _ffn_kernel(h_ref, x_ref, mod_ref, wg_ref, wu_ref, wd_ref, gpost_ref, o_ref, *, tm, seq, fc):
    i = pl.program_id(1)
    rows = i * tm + lax.broadcasted_iota(jnp.int32, (tm, 1), 0)
    is_ctx = rows >= seq
    h = h_ref[0]
    acc = jnp.zeros((tm, o_ref.shape[-1]), F32)
    for c in range(wg_ref.shape[1] // fc):
        g = jnp.dot(h, wg_ref[:, c * fc:(c + 1) * fc], preferred_element_type=F32)
        u = jnp.dot(h, wu_ref[:, c * fc:(c + 1) * fc], preferred_element_type=F32)
        a = (_silu(g) * u).astype(BF16)
        acc = acc + jnp.dot(a, wd_ref[c * fc:(c + 1) * fc, :], preferred_element_type=F32)
    yn = acc * lax.rsqrt(jnp.mean(acc * acc, axis=-1, keepdims=True) + EPS) * gpost_ref[...]
    o_ref[0] = x_ref[0] + _row_mod(mod_ref, is_ctx, 5) * yn


def _dense_ffn(h2, x1, modv, wg, wu, wd, g_post, seq, tm):
    B, T, D = x1.shape
    F = wg.shape[1]
    kern = functools.partial(_ffn_kernel, tm=tm, seq=seq, fc=256)
    const = lambda shape: pl.BlockSpec(shape, lambda b, i: (0, 0), pipeline_mode=pl.Buffered(1))
    return pl.pallas_call(
        kern,
        grid=(B, T // tm),
        in_specs=[pl.BlockSpec((1, tm, D), lambda b, i: (b, i, 0)),
                  pl.BlockSpec((1, tm, D), lambda b, i: (b, i, 0)),
                  pl.BlockSpec((1, 2, 8, D), lambda b, i: (b, 0, 0, 0)),
                  const((D, F)), const((D, F)), const((F, D)),
                  pl.BlockSpec((1, D), lambda b, i: (0, 0))],
        out_specs=pl.BlockSpec((1, tm, D), lambda b, i: (b, i, 0)),
        out_shape=jax.ShapeDtypeStruct((B, T, D), F32),
        compiler_params=_params(("arbitrary", "arbitrary")),
        name="dense_ffn",
    )(h2, x1, modv, wg, wu, wd, g_post.reshape(1, D))


def _row_copy(src_hbm, row, dst_vmem, r, sem):
    return pltpu.make_async_copy(src_hbm.at[pl.ds(row, 1)], dst_vmem.at[pl.ds(r, 1)], sem)


def _moe_ffn_kernel(te_ref, tv_ref, src0_ref, srcn_ref, h_hbm, wg_ref, wu_ref, wd_ref, o_ref,
                    xg_sc, xb_sc, acc_sc, sem, *, tr):
    t = pl.program_id(0)
    f = pl.program_id(1)
    nt = pl.num_programs(0)
    nf = pl.num_programs(1)
    slot = t % 2

    def start_gather(src_ref, s):
        def body(r, c):
            _row_copy(h_hbm, src_ref[0, 0, r], xg_sc.at[s], r, sem.at[s]).start()
            return c

        lax.fori_loop(0, tr, body, 0, unroll=8)

    @pl.when((f == 0) & (t == 0) & (tv_ref[0] > 0))
    def _():
        start_gather(src0_ref, 0)

    @pl.when(f == 0)
    def _():
        acc_sc[...] = jnp.zeros(acc_sc.shape, F32)

    @pl.when((f == 0) & (tv_ref[t] > 0))
    def _():
        pltpu.make_async_copy(h_hbm.at[pl.ds(0, tr)], xg_sc.at[slot], sem.at[slot]).wait()
        xb_sc[...] = xg_sc[slot].astype(BF16)

    @pl.when((f == 0) & (t + 1 < nt) & (tv_ref[jnp.minimum(t + 1, nt - 1)] > 0))
    def _():
        start_gather(srcn_ref, 1 - slot)

    @pl.when(tv_ref[t] > 0)
    def _():
        x = xb_sc[...]
        g = jnp.dot(x, wg_ref[0], preferred_element_type=F32)
        u = jnp.dot(x, wu_ref[0], preferred_element_type=F32)
        a = (_silu(g) * u).astype(BF16)
        acc_sc[...] += jnp.dot(a, wd_ref[0], preferred_element_type=F32)

    @pl.when(f == nf - 1)
    def _():
        o_ref[...] = acc_sc[...]


def _moe_ffn(h_flat, src, tile_expert, tile_valid, wg, wu, wd, n_rows, tr, tf):
    D = h_flat.shape[1]
    F = wg.shape[2]
    nt = n_rows // tr
    return pl.pallas_call(
        functools.partial(_moe_ffn_kernel, tr=tr),
        grid_spec=pltpu.PrefetchScalarGridSpec(
            num_scalar_prefetch=2,
            grid=(nt, F // tf),
            in_specs=[pl.BlockSpec((1, 1, tr), lambda t, f, te, tv: (0, 0, 0), memory_space=pltpu.SMEM),
                      pl.BlockSpec((1, 1, tr), lambda t, f, te, tv: (jnp.minimum(t + 1, nt - 1), 0, 0),
                                   memory_space=pltpu.SMEM),
                      pl.BlockSpec(memory_space=pl.ANY),
                      pl.BlockSpec((1, D, tf), lambda t, f, te, tv: (te[t], 0, f)),
                      pl.BlockSpec((1, D, tf), lambda t, f, te, tv: (te[t], 0, f)),
                      pl.BlockSpec((1, tf, D), lambda t, f, te, tv: (te[t], f, 0))],
            out_specs=pl.BlockSpec((tr, D), lambda t, f, te, tv: (t, 0)),
            scratch_shapes=[pltpu.VMEM((2, tr, D), h_flat.dtype), pltpu.VMEM((tr, D), BF16),
                            pltpu.VMEM((tr, D), F32), pltpu.SemaphoreType.DMA((2,))]),
        out_shape=jax.ShapeDtypeStruct((n_rows, D), F32),
        compiler_params=_params(("arbitrary", "arbitrary")),
        name="moe_ffn",
    )(tile_expert, tile_valid, src.reshape(nt, 1, tr), src.reshape(nt, 1, tr), h_flat, wg, wu, wd)


def _combine_kernel(pos_ref, ys_hbm, gw_ref, x_ref, mod_ref, gpost_ref, o_ref, y1_sc, y2_sc, sem, *, tm, seq):
    def issue(r, c):
        _row_copy(ys_hbm, pos_ref[0, 0, r], y1_sc, r, sem.at[0]).start()
        _row_copy(ys_hbm, pos_ref[0, 1, r], y2_sc, r, sem.at[1]).start()
        return c

    lax.fori_loop(0, tm, issue, 0, unroll=8)
    pltpu.make_async_copy(ys_hbm.at[pl.ds(0, tm)], y1_sc, sem.at[0]).wait()
    pltpu.make_async_copy(ys_hbm.at[pl.ds(0, tm)], y2_sc, sem.at[1]).wait()
    rows = pl.program_id(1) * tm + lax.broadcasted_iota(jnp.int32, (tm, 1), 0)
    gw = gw_ref[0]
    y = gw[:, 0:1] * y1_sc[...] + gw[:, 1:2] * y2_sc[...]
    yn = y * lax.rsqrt(jnp.mean(y * y, axis=-1, keepdims=True) + EPS) * gpost_ref[...]
    o_ref[0] = x_ref[0] + _row_mod(mod_ref, rows >= seq, 5) * yn


def _moe_combine(pos, ys, gw, x1, modv, g_post, seq, tm):
    B, R, D = x1.shape
    ntile = R // tm
    kern = functools.partial(_combine_kernel, tm=tm, seq=seq)
    return pl.pallas_call(
        kern,
        grid=(B, ntile),
        in_specs=[pl.BlockSpec((1, TOP_K, tm), lambda b, i: (b * ntile + i, 0, 0), memory_space=pltpu.SMEM),
                  pl.BlockSpec(memory_space=pl.ANY),
                  pl.BlockSpec((1, tm, LANES), lambda b, i: (b, i, 0)),
                  pl.BlockSpec((1, tm, D), lambda b, i: (b, i, 0)),
                  pl.BlockSpec((1, 2, 8, D), lambda b, i: (b, 0, 0, 0)),
                  pl.BlockSpec((1, D), lambda b, i: (0, 0))],
        out_specs=pl.BlockSpec((1, tm, D), lambda b, i: (b, i, 0)),
        out_shape=jax.ShapeDtypeStruct((B, R, D), F32),
        scratch_shapes=[pltpu.VMEM((tm, D), F32), pltpu.VMEM((tm, D), F32), pltpu.SemaphoreType.DMA((2,))],
        compiler_params=_params(("arbitrary", "arbitrary")),
        name="moe_combine",
    )(pos.reshape(B * ntile, tm, TOP_K).transpose(0, 2, 1), ys, gw, x1, modv, g_post.reshape(1, D))


def _moe_routing(eid, n_exp, tr):
    N = eid.shape[0]
    e_flat = eid.reshape(-1)
    onehot = (e_flat[:, None] == jnp.arange(n_exp)[None, :]).astype(jnp.int32)
    rank = jnp.sum((jnp.cumsum(onehot, axis=0) - 1) * onehot, axis=1)
    counts = jnp.sum(onehot, axis=0)
    padded = ((counts + tr - 1) // tr) * tr
    ends = jnp.cumsum(padded)
    starts = ends - padded
    dest = starts[e_flat] + rank
    n_rows = (TOP_K * N // tr + n_exp) * tr
    src = jnp.zeros((n_rows,), jnp.int32).at[dest].set(jnp.arange(TOP_K * N, dtype=jnp.int32) // TOP_K)
    tile_start = jnp.arange(n_rows // tr, dtype=jnp.int32) * tr
    tile_expert = jnp.minimum(jnp.sum((tile_start[:, None] >= ends[None, :]).astype(jnp.int32), axis=1), n_exp - 1)
    tile_valid = (tile_start < ends[-1]).astype(jnp.int32)
    pos = dest.reshape(N, TOP_K).astype(jnp.int32)
    return src, pos, tile_expert, tile_valid, n_rows


def _swa_perm():
    idx = []
    for g in range(2):
        for kvh in range(2):
            idx += [(kvh * 2 + g) * HEAD_DIM + d for d in range(HEAD_DIM)]
    return jnp.array(idx, jnp.int32)


def kernel(x, c, ctx, c_ctx, w_mod, b_mod, g_attn_pre, g_attn_post, g_ffn_pre, g_ffn_post, w_in, w_out,
           da_lambda_q1, da_lambda_k1, da_lambda_q2, da_lambda_k2, da_subln, swa_sink, na_rpb,
           ret_gamma_fwd, ret_gamma_bwd, ffn_w_gate, ffn_w_up, ffn_w_down,
           moe_router, moe_w_gate, moe_w_up, moe_w_down):
    B, S, D = x.shape
    CTX = ctx.shape[1]
    T = S + CTX
    L = w_mod.shape[0]
    n_exp = moe_router.shape[-1]
    rows = S // GRID_W
    t_attn = 256

    c8 = jnp.zeros((8, D), F32).at[:B].set(c).at[B].set(c_ctx)
    mod = _modulation(c8, w_mod, b_mod).reshape(L, 8, 6, D)
    mod = jnp.pad(mod, ((0, 0), (0, 0), (0, 2), (0, 0)))
    modv = jnp.stack([mod[:, :B], jnp.broadcast_to(mod[:, B:B + 1], (L, B, 8, D))], axis=2)

    tabs = _rope_tables(S, CTX)
    perm = _swa_perm()
    swa0 = CB_SWA_Q * 256
    xa = jnp.concatenate([x, ctx], axis=1)

    for l in range(L):
        lambda_init = 0.8 - 0.6 * math.exp(-0.3 * l)
        last = l == L - 1
        w_in_l = w_in[l]
        w_in_b = jnp.concatenate([w_in_l[:, :swa0], w_in_l[:, swa0:swa0 + 256][:, perm], w_in_l[:, swa0 + 256:]],
                                 axis=1).astype(BF16)
        w_out_l = w_out[l]
        w_out_b = jnp.concatenate([w_out_l[:256], w_out_l[256:512][perm], w_out_l[512:]], axis=0).astype(BF16)

        P, QT, VT = _inproj(xa, modv[l], g_attn_pre[l], w_in_b, tabs, S, 256)

        lamp = jnp.zeros((8, LANES), F32)
        for r, v in enumerate((da_lambda_q1[l], da_lambda_k1[l], da_lambda_q2[l], da_lambda_k2[l])):
            lamp = lamp.at[r, :DA_QK].set(v)
        subln_full = jnp.tile(da_subln[l], N_HEADS).reshape(1, GROUP_W)
        tq_da = next(t for t in (512, 256) if S % t == 0)
        tk_da = next(t for t in (768, 512, 256) if T % t == 0)
        ya = _diff_attention(P, QT, VT, lamp, subln_full, lambda_init, 0, S // tq_da, tq_da, 0, T // tk_da, tk_da)
        if not last:
            ya_ctx = _diff_attention(P, QT, VT, lamp, subln_full, lambda_init, S // CTX, 1, CTX, S // CTX, 1, CTX)
            ya = jnp.concatenate([ya, ya_ctx], axis=1)
        yb = _window_attention(P, swa_sink[l].astype(F32), S, CTX)
        yn = _neighborhood_attention(P, _na_bias_tables(na_rpb[l], rows), S, CTX)
        lg = jnp.stack([jax.nn.log_sigmoid(ret_gamma_fwd[l].astype(F32)),
                        jax.nn.log_sigmoid(ret_gamma_bwd[l].astype(F32))])
        lgv = jnp.zeros((8, GROUP_W), F32).at[:2].set(jnp.repeat(lg, HEAD_DIM, axis=1))
        of, ob = _retention(P, lg.reshape(-1), lgv, S, CTX)

        R = S if last else T
        e = l // 2
        if l % 2 == 0:
            x1, h2 = _outproj(ya, yb, yn, of, ob, P, xa, modv[l], w_out_b, g_attn_post[l], g_ffn_pre[l],
                              R, S, 256)
            xa = _dense_ffn(h2, x1, modv[l], ffn_w_gate[e].astype(BF16), ffn_w_up[e].astype(BF16),
                            ffn_w_down[e].astype(BF16), g_ffn_post[l], S, 256)
        else:
            router = jnp.zeros((ROUTER_ROWS, D), F32).at[:n_exp].set(moe_router[e].T)
            x1, h2, eid, gw = _outproj(ya, yb, yn, of, ob, P, xa, modv[l], w_out_b, g_attn_post[l],
                                       g_ffn_pre[l], R, S, 256, router=router, n_exp=n_exp)
            tr = 512
            src, pos, tile_expert, tile_valid, n_rows = _moe_routing(
                jnp.swapaxes(eid[:, :TOP_K, :], 1, 2).reshape(B * R, TOP_K), n_exp, tr)
            ys = _moe_ffn(h2.reshape(B * R, D), src, tile_expert, tile_valid, moe_w_gate[e].astype(BF16),
                          moe_w_up[e].astype(BF16), moe_w_down[e].astype(BF16), n_rows, tr, 512)
            xa = _moe_combine(pos, ys, gw, x1, modv[l], g_ffn_post[l], S, 256)
    return xa[:, :S]
```

```python
import functools
import math

import jax
import jax.numpy as jnp
from jax import lax
from jax.experimental import pallas as pl
from jax.experimental.pallas import tpu as pltpu

F32 = jnp.float32
BF16 = jnp.bfloat16

GRID_W = 64
HEAD_DIM = 64
N_HEADS = 4
GROUP_W = N_HEADS * HEAD_DIM
DA_QK = 32
SWA_WINDOW = 128
NA_WIN_ROWS = 8
NA_WIN_COLS = 16
NA_TILE_ROWS = 4
RET_CHUNK = 128
ROPE_BASE = 10000.0
TOP_K = 2
EPS = 1e-6
NEG = -1e30
LOG2E = 1.4426950408889634
LANES = 128
ONES_ROWS = 16
ROUTER_ROWS = 16
TOKEN_TILES = (768, 512, 256)
EXPERT_F_TILES = (1792, 512, 256)
DA_CHUNK = 128
VMEM_LIMIT = 56 * 1024 * 1024

CB_DA_Q, CB_DA_K, CB_DA_V = 0, 1, 2
CB_SWA_Q = 3
CB_SWA_K128, CB_SWA_V128 = 8, 9
CB_NA_Q, CB_NA_K, CB_NA_V = 5, 6, 7
CB_RET_Q, CB_RET_K, CB_RET_V, CB_RET_G = 8, 9, 10, 11
IN_WIDTH = 3072


def _params(sem):
    return pltpu.CompilerParams(dimension_semantics=sem, vmem_limit_bytes=VMEM_LIMIT)


def _silu(v):
    return v / (1.0 + jnp.exp(-v))


def _head_mask(shape, head_w, h, dim=1):
    return lax.broadcasted_iota(jnp.int32, shape, dim) // head_w == h


def _lane_pick(a, b):
    lane = lax.broadcasted_iota(jnp.int32, a.shape, 1)
    return jnp.where(lane < HEAD_DIM, a, b)


def _per_head_full(vals):
    return jnp.concatenate([_lane_pick(vals[0], vals[1]), _lane_pick(vals[2], vals[3])], axis=1)


def _group_mean_sq(a):
    sq = a * a
    r = lax.broadcasted_iota(jnp.int32, (GROUP_W, GROUP_W), 0) // HEAD_DIM
    c = lax.broadcasted_iota(jnp.int32, (GROUP_W, GROUP_W), 1) // HEAD_DIM
    g = jnp.where(r == c, 1.0, 0.0).astype(BF16)
    hi = sq.astype(BF16)
    lo = (sq - hi.astype(F32)).astype(BF16)
    tot = jnp.dot(hi, g, preferred_element_type=F32) + jnp.dot(lo, g, preferred_element_type=F32)
    return tot * (1.0 / HEAD_DIM)


def _row_mod(mod_ref, rows_are_ctx, j):
    return jnp.where(rows_are_ctx, mod_ref[0, 1, j:j + 1, :], mod_ref[0, 0, j:j + 1, :])


def _mod_kernel(c_ref, w_ref, b_ref, o_ref):
    s = _silu(c_ref[...])
    o_ref[0] = jnp.dot(s.astype(BF16), w_ref[0].astype(BF16), preferred_element_type=F32) + b_ref[0]


def _modulation(c8, w_mod, b_mod):
    L, D, W = w_mod.shape
    tn = 1536
    return pl.pallas_call(
        _mod_kernel,
        grid=(L, W // tn),
        in_specs=[pl.BlockSpec((8, D), lambda l, j: (0, 0)),
                  pl.BlockSpec((1, D, tn), lambda l, j: (l, 0, j)),
                  pl.BlockSpec((1, 1, tn), lambda l, j: (l, 0, j))],
        out_specs=pl.BlockSpec((1, 8, tn), lambda l, j: (l, 0, j)),
        out_shape=jax.ShapeDtypeStruct((L, 8, W), F32),
        compiler_params=_params(("arbitrary", "arbitrary")),
        name="modulation",
    )(c8, w_mod, b_mod.reshape(L, 1, W))


def _lane_block_ops():
    da_s = DA_QK ** -0.5 * LOG2E
    s64 = HEAD_DIM ** -0.5
    ops = []
    ops += [("r32", da_s)] * 2 + [("r32", 1.0)] * 2 + [(None, 1.0)] * 2
    ops += [("r64", s64 * LOG2E)] * 2 + [("r64", 1.0)] + [(None, 1.0)]
    ops += [(None, s64 * LOG2E)] * 2 + [(None, 1.0)] * 4
    ops += [("r64", 1.0)] * 2 + [("r64", s64)] * 2 + [(None, 1.0)] * 4
    return ops


def _rope(x, cos, sin_signed, w):
    lane = lax.broadcasted_iota(jnp.int32, x.shape, 1)
    first = (lane % (2 * w)) < w
    xr = jnp.where(first, pltpu.roll(x, LANES - w, 1), pltpu.roll(x, w, 1))
    return x * cos + xr * sin_signed


def _inproj_kernel(x_ref, mod_ref, g_ref, w_ref, c32_ref, s32_ref, c64_ref, s64_ref, o_ref, qt_ref, vt_ref,
                   *, tm, seq):
    i = pl.program_id(1)
    x = x_ref[0]
    rows = i * tm + lax.broadcasted_iota(jnp.int32, (tm, 1), 0)
    is_ctx = rows >= seq
    ms = jnp.mean(x * x, axis=-1, keepdims=True)
    h = x * lax.rsqrt(ms + EPS) * g_ref[...]
    h = h * (1.0 + _row_mod(mod_ref, is_ctx, 1)) + _row_mod(mod_ref, is_ctx, 0)
    hb = h.astype(BF16)
    ops = _lane_block_ops()
    for cb in range(IN_WIDTH // 256):
        p = jnp.dot(hb, w_ref[:, cb * 256:(cb + 1) * 256], preferred_element_type=F32)
        halves = []
        for hf in range(2):
            kind, scale = ops[cb * 2 + hf]
            ph = p[:, hf * LANES:(hf + 1) * LANES]
            if kind == "r32":
                ph = _rope(ph, c32_ref[...], s32_ref[...], DA_QK // 4)
            elif kind == "r64":
                ph = _rope(ph, c64_ref[...], s64_ref[...], HEAD_DIM // 4)
            if scale != 1.0:
                ph = ph * scale
            halves.append(ph)
        full = jnp.concatenate(halves, axis=1)
        o_ref[0, :, cb * 256:(cb + 1) * 256] = full.astype(BF16)
        if cb == CB_DA_Q:
            qt_ref[0] = full.T.astype(BF16)
        elif cb == CB_DA_V:
            vt_ref[0] = full.T.astype(BF16)


def _inproj(xa, modv, g, w_in_b, tabs, seq, tm):
    B, T, D = xa.shape
    kern = functools.partial(_inproj_kernel, tm=tm, seq=seq)
    tab_spec = pl.BlockSpec((tm, LANES), lambda b, i: (i, 0))
    return pl.pallas_call(
        kern,
        grid=(B, T // tm),
        in_specs=[pl.BlockSpec((1, tm, D), lambda b, i: (b, i, 0)),
                  pl.BlockSpec((1, 2, 8, D), lambda b, i: (b, 0, 0, 0)),
                  pl.BlockSpec((1, D), lambda b, i: (0, 0)),
                  pl.BlockSpec((D, IN_WIDTH), lambda b, i: (0, 0)),
                  tab_spec, tab_spec, tab_spec, tab_spec],
        out_specs=[pl.BlockSpec((1, tm, IN_WIDTH), lambda b, i: (b, i, 0)),
                   pl.BlockSpec((1, GROUP_W, tm), lambda b, i: (b, 0, i)),
                   pl.BlockSpec((1, GROUP_W, tm), lambda b, i: (b, 0, i))],
        out_shape=[jax.ShapeDtypeStruct((B, T, IN_WIDTH), BF16),
                   jax.ShapeDtypeStruct((B, GROUP_W, T), BF16),
                   jax.ShapeDtypeStruct((B, GROUP_W, T), BF16)],
        compiler_params=_params(("arbitrary", "arbitrary")),
        name="inproj",
    )(xa, modv, g.reshape(1, D), w_in_b, *tabs)


def _rope_tables(seq, ctx):
    t = jnp.arange(seq)
    row = (t // GRID_W).astype(F32)
    col = (t % GRID_W).astype(F32)
    out = []
    for dh in (DA_QK, HEAD_DIM):
        half = dh // 2
        quarter = half // 2
        lane = jnp.arange(LANES)
        d = lane % dh
        use_col = (d // half) == 1
        idx = (d % quarter).astype(F32)
        inv = ROPE_BASE ** (-idx * 2.0 / half)
        pos = jnp.where(use_col[None, :], col[:, None], row[:, None])
        ang = pos * inv[None, :]
        first = (d % half) < quarter
        cos = jnp.cos(ang)
        sin = jnp.where(first[None, :], -jnp.sin(ang), jnp.sin(ang))
        cos = jnp.concatenate([cos, jnp.ones((ctx, LANES), F32)], axis=0)
        sin = jnp.concatenate([sin, jnp.zeros((ctx, LANES), F32)], axis=0)
        out += [cos, sin]
    return out


def _rows_per_head(vals, tq):
    return jnp.concatenate([jnp.broadcast_to(v, (HEAD_DIM, tq)) for v in vals], axis=0)


def _da_kernel(lamp_ref, qt_ref, k_ref, vt_ref, g_ref, o_ref, qm_sc, m_sc, l_sc, acc_sc, st_sc, p_sc,
               *, tq, lambda_init):
    ki = pl.program_id(2)
    nk = pl.num_programs(2)

    @pl.when(ki == 0)
    def _():
        qt = qt_ref[0]
        for j in range(2 * N_HEADS):
            qm_sc[j] = jnp.where(_head_mask(qt.shape, DA_QK, j, 0), qt, jnp.zeros_like(qt))
        m_sc[...] = jnp.full(m_sc.shape, NEG, F32)
        l_sc[...] = jnp.zeros(l_sc.shape, F32)
        acc_sc[...] = jnp.zeros(acc_sc.shape, F32)

    k = k_ref[0]
    vt = vt_ref[0]
    ones = jnp.ones((ONES_ROWS, vt.shape[1]), BF16)

    tk = k.shape[0]

    def scores(j):
        st_sc[j % 2] = jnp.dot(k, qm_sc[j], preferred_element_type=F32)
        run = st_sc[j % 2, 0:DA_CHUNK, :]
        for c in range(1, tk // DA_CHUNK):
            run = jnp.maximum(run, st_sc[j % 2, c * DA_CHUNK:(c + 1) * DA_CHUNK, :])
        return jnp.max(run, axis=0, keepdims=True)

    mx = scores(0)
    for j in range(2 * N_HEADS):
        h, t = j // 2, j % 2
        mx_next = scores(j + 1) if j + 1 < 2 * N_HEADS else None
        rows = slice(h * HEAD_DIM, (h + 1) * HEAD_DIM)
        vte = jnp.concatenate([vt[rows, :], ones], axis=0)
        m_prev = m_sc[j]
        m_new = jnp.maximum(m_prev, mx)
        alpha = jnp.exp2(m_prev - m_new)
        for c in range(tk // DA_CHUNK):
            cr = slice(c * DA_CHUNK, (c + 1) * DA_CHUNK)
            p_sc[j % 2, cr, :] = jnp.exp2(st_sc[j % 2, cr, :] - m_new).astype(BF16)
        pv = jnp.dot(vte, p_sc[j % 2], preferred_element_type=F32)
        l_sc[j] = alpha * l_sc[j] + pv[HEAD_DIM:HEAD_DIM + 1, :]
        m_sc[j] = m_new
        acc_sc[t, rows, :] = alpha * acc_sc[t, rows, :] + pv[:HEAD_DIM, :]
        mx = mx_next

    @pl.when(ki == nk - 1)
    def _():
        lp = lamp_ref[...]
        lam = (jnp.exp(jnp.sum(lp[0:1] * lp[1:2], axis=1, keepdims=True))
               - jnp.exp(jnp.sum(lp[2:3] * lp[3:4], axis=1, keepdims=True))) + lambda_init
        o0 = acc_sc[0] / _rows_per_head([l_sc[2 * h] for h in range(N_HEADS)], tq)
        o1 = acc_sc[1] / _rows_per_head([l_sc[2 * h + 1] for h in range(N_HEADS)], tq)
        a = (o0 - lam * o1).T
        y = a * lax.rsqrt(_group_mean_sq(a) + EPS) * g_ref[...]
        o_ref[0] = (y * (1.0 - lambda_init)).astype(BF16)


def _diff_attention(P, QT, VT, lamp, subln_full, lambda_init, q0, nq, tq, k0, nk, tk):
    B = P.shape[0]
    kern = functools.partial(_da_kernel, tq=tq, lambda_init=lambda_init)
    return pl.pallas_call(
        kern,
        grid=(B, nq, nk),
        in_specs=[pl.BlockSpec((8, LANES), lambda b, qi, ki: (0, 0)),
                  pl.BlockSpec((1, GROUP_W, tq), lambda b, qi, ki: (b, 0, q0 + qi)),
                  pl.BlockSpec((1, tk, GROUP_W), lambda b, qi, ki: (b, k0 + ki, CB_DA_K)),
                  pl.BlockSpec((1, GROUP_W, tk), lambda b, qi, ki: (b, 0, k0 + ki)),
                  pl.BlockSpec((1, GROUP_W), lambda b, qi, ki: (0, 0))],
        out_specs=pl.BlockSpec((1, tq, GROUP_W), lambda b, qi, ki: (b, qi, 0)),
        out_shape=jax.ShapeDtypeStruct((B, nq * tq, GROUP_W), BF16),
        scratch_shapes=[pltpu.VMEM((2 * N_HEADS, GROUP_W, tq), BF16),
                        pltpu.VMEM((2 * N_HEADS, 1, tq), F32),
                        pltpu.VMEM((2 * N_HEADS, 1, tq), F32),
                        pltpu.VMEM((2, GROUP_W, tq), F32),
                        pltpu.VMEM((2, tk, tq), F32),
                        pltpu.VMEM((2, tk, tq), BF16)],
        compiler_params=_params(("arbitrary", "arbitrary", "arbitrary")),
        name="diff_attention",
    )(lamp, QT, P, VT, subln_full)


def _swa_kernel(sink_ref, q_ref, kp_ref, kc_ref, kn_ref, kx_ref, vp_ref, vc_ref, vn_ref, vx_ref, o_ref,
                *, seq, blk):
    i = pl.program_id(0)
    nloc = 3 * blk
    nkeys = nloc + kx_ref.shape[1]
    qpos = i * blk + lax.broadcasted_iota(jnp.int32, (blk, nkeys), 0)
    c = lax.broadcasted_iota(jnp.int32, (blk, nkeys), 1)
    kpos = (i - 1) * blk + c
    valid = ((kpos >= 0) & (kpos < seq) & (jnp.abs(qpos - kpos) <= SWA_WINDOW) & (qpos < seq)) | (c >= nloc)
    for b in range(q_ref.shape[0]):
        q = q_ref[b]
        kall = jnp.concatenate([kp_ref[b], kc_ref[b], kn_ref[b], kx_ref[b]], axis=0)
        vall = jnp.concatenate([vp_ref[b], vc_ref[b], vn_ref[b], vx_ref[b]], axis=0)
        qs = []
        for g in range(2):
            qg = q[:, g * LANES:(g + 1) * LANES]
            for kvh in range(2):
                qs.append(jnp.where(_head_mask(qg.shape, HEAD_DIM, kvh), qg, jnp.zeros_like(qg)))
        s = lax.dot_general(jnp.concatenate(qs, axis=0), kall, (((1,), (1,)), ((), ())),
                            preferred_element_type=F32)
        vms = [jnp.where(_head_mask(vall.shape, HEAD_DIM, kvh), vall, jnp.zeros_like(vall)) for kvh in range(2)]
        vm = jnp.concatenate(vms, axis=0)
        for g in range(2):
            ps = []
            for kvh in range(2):
                j = 2 * g + kvh
                sink = sink_ref[2 * kvh + g] * LOG2E
                sj = jnp.where(valid, s[j * blk:(j + 1) * blk], NEG)
                m = jnp.maximum(jnp.max(sj, axis=1, keepdims=True), sink)
                e = jnp.exp2(sj - m)
                den = jnp.sum(e, axis=1, keepdims=True) + jnp.exp2(sink - m)
                ps.append((e / den).astype(BF16))
            o_ref[b, :, g * LANES:(g + 1) * LANES] = jnp.dot(
                jnp.concatenate(ps, axis=1), vm, preferred_element_type=F32).astype(BF16)


def _window_attention(P, sink, seq, ctx):
    B, T, _ = P.shape
    blk = SWA_WINDOW
    nt = T // blk
    kern = functools.partial(_swa_kernel, seq=seq, blk=blk)

    def nb(cb, off):
        return pl.BlockSpec((B, blk, LANES), lambda i: (0, jnp.clip(i + off, 0, nt - 1), cb))

    def cx(cb):
        return pl.BlockSpec((B, ctx, LANES), lambda i: (0, seq // ctx, cb))

    return pl.pallas_call(
        kern,
        grid=(nt,),
        in_specs=[pl.BlockSpec(memory_space=pltpu.SMEM),
                  pl.BlockSpec((B, blk, GROUP_W), lambda i: (0, i, CB_SWA_Q)),
                  nb(CB_SWA_K128, -1), nb(CB_SWA_K128, 0), nb(CB_SWA_K128, 1), cx(CB_SWA_K128),
                  nb(CB_SWA_V128, -1), nb(CB_SWA_V128, 0), nb(CB_SWA_V128, 1), cx(CB_SWA_V128)],
        out_specs=pl.BlockSpec((B, blk, GROUP_W), lambda i: (0, i, 0)),
        out_shape=jax.ShapeDtypeStruct((B, T, GROUP_W), BF16),
        compiler_params=_params(("arbitrary",)),
        name="window_attention",
    )(sink, P, P, P, P, P, P, P, P, P)


def _na_kernel(q_ref, k0_ref, k1_ref, k2_ref, kx_ref, v0_ref, v1_ref, v2_ref, vx_ref, bias_ref, o_ref,
               *, tq, n_seq_tiles):
    ctx_tile = pl.program_id(0) >= n_seq_tiles
    nloc = 3 * tq
    for b in range(q_ref.shape[0]):
        q = q_ref[b]
        kall = jnp.concatenate([k0_ref[b], k1_ref[b], k2_ref[b], kx_ref[b]], axis=0)
        vall = jnp.concatenate([v0_ref[b], v1_ref[b], v2_ref[b], vx_ref[b]], axis=0)
        qs = jnp.concatenate(
            [jnp.where(_head_mask(q.shape, HEAD_DIM, h), q, jnp.zeros_like(q)) for h in range(N_HEADS)], axis=0)
        s = lax.dot_general(qs, kall, (((1,), (1,)), ((), ())), preferred_element_type=F32)
        ps = []
        for h in range(N_HEADS):
            sh = s[h * tq:(h + 1) * tq]
            s_loc = jnp.where(ctx_tile, NEG, sh[:, :nloc] + bias_ref[0, h])
            s_ctx = sh[:, nloc:]
            m = jnp.maximum(jnp.max(s_loc, axis=1, keepdims=True), jnp.max(s_ctx, axis=1, keepdims=True))
            e_loc = jnp.exp2(s_loc - m)
            e_ctx = jnp.exp2(s_ctx - m)
            den = jnp.sum(e_loc, axis=1, keepdims=True) + jnp.sum(e_ctx, axis=1, keepdims=True)
            ps.append(jnp.concatenate([e_loc / den, e_ctx / den], axis=1).astype(BF16))
        vm = jnp.concatenate(
            [jnp.where(_head_mask(vall.shape, HEAD_DIM, h), vall, jnp.zeros_like(vall)) for h in range(N_HEADS)],
            axis=0)
        o_ref[b] = jnp.dot(jnp.concatenate(ps, axis=1), vm, preferred_element_type=F32).astype(BF16)


def _na_bias_kernel(rt_ref, o_ref, sv_sc, *, rows):
    tr = NA_TILE_ROWS
    nrt = rows // tr
    nl = 3 * tr * GRID_W
    n_dcol = 2 * NA_WIN_COLS - 1
    rt = rt_ref[0] * LOG2E
    hi = rt.astype(BF16)
    r1 = rt - hi.astype(F32)
    mid = r1.astype(BF16)
    lo = (r1 - mid.astype(F32)).astype(BF16)
    lane = lax.broadcasted_iota(jnp.int32, (GRID_W, nl), 1)
    qc = lax.broadcasted_iota(jnp.int32, (GRID_W, nl), 0)
    kc = lane % GRID_W
    kr = lane // GRID_W
    dc = jnp.clip(kc - qc, 1 - NA_WIN_COLS, NA_WIN_COLS - 1) + (NA_WIN_COLS - 1)
    c0 = jnp.clip(qc - NA_WIN_COLS // 2, 0, GRID_W - NA_WIN_COLS)
    col_ok = (kc >= c0) & (kc < c0 + NA_WIN_COLS)
    irow = lax.broadcasted_iota(jnp.int32, (LANES, nl), 0)
    krl = lax.broadcasted_iota(jnp.int32, (LANES, nl), 1) // GRID_W
    for v, (t_idx, ws) in enumerate(((0, 0), (1, 0), (nrt - 1, nrt - 3))):
        for qr in range(tr):
            r = t_idx * tr + qr
            r0 = min(max(r - NA_WIN_ROWS // 2, 0), rows - NA_WIN_ROWS)
            d_row = jnp.clip(ws * tr + krl - r + (NA_WIN_ROWS - 1), 0, 2 * NA_WIN_ROWS - 2)
            onehot = jnp.where(irow == d_row, 1.0, 0.0).astype(BF16)
            sv_sc[...] = (jnp.dot(hi, onehot, preferred_element_type=F32)
                          + jnp.dot(mid, onehot, preferred_element_type=F32)
                          + jnp.dot(lo, onehot, preferred_element_type=F32))

            def pick(j, acc):
                return jnp.where(dc == j, sv_sc[pl.ds(j, 1), :], acc)

            acc = lax.fori_loop(0, n_dcol, pick, jnp.zeros((GRID_W, nl), F32))
            krow = ws * tr + kr
            ok = col_ok & (krow >= r0) & (krow < r0 + NA_WIN_ROWS)
            o_ref[v, 0, qr * GRID_W:(qr + 1) * GRID_W, :] = jnp.where(ok, acc, NEG)


def _na_bias_tables(rpb, rows):
    H, nr, ncol = rpb.shape
    rt = jnp.zeros((H, 32, LANES), F32).at[:, :ncol, :nr].set(jnp.swapaxes(rpb.astype(F32), 1, 2))
    tq = NA_TILE_ROWS * GRID_W
    return pl.pallas_call(
        functools.partial(_na_bias_kernel, rows=rows),
        grid=(H,),
        in_specs=[pl.BlockSpec((1, 32, LANES), lambda h: (h, 0, 0))],
        out_specs=pl.BlockSpec((3, 1, tq, 3 * tq), lambda h: (0, h, 0, 0)),
        out_shape=jax.ShapeDtypeStruct((3, H, tq, 3 * tq), F32),
        scratch_shapes=[pltpu.VMEM((32, 3 * tq), F32)],
        compiler_params=_params(("arbitrary",)),
        name="na_bias",
    )(rt)


def _neighborhood_attention(P, bias_tabs, seq, ctx):
    B, T, _ = P.shape
    tq = NA_TILE_ROWS * GRID_W
    nst = seq // tq
    nt = T // tq
    kern = functools.partial(_na_kernel, tq=tq, n_seq_tiles=nst)

    def win(i):
        return jnp.clip(i - 1, 0, nst - 3)

    def loc(cb, off):
        return pl.BlockSpec((B, tq, GROUP_W), lambda i: (0, win(i) + off, cb))

    def cx(cb):
        return pl.BlockSpec((B, ctx, GROUP_W), lambda i: (0, seq // ctx, cb))

    def variant(i):
        return (jnp.where(i >= nst, 0, i - win(i)), 0, 0, 0)

    return pl.pallas_call(
        kern,
        grid=(nt,),
        in_specs=[pl.BlockSpec((B, tq, GROUP_W), lambda i: (0, i, CB_NA_Q)),
                  loc(CB_NA_K, 0), loc(CB_NA_K, 1), loc(CB_NA_K, 2), cx(CB_NA_K),
                  loc(CB_NA_V, 0), loc(CB_NA_V, 1), loc(CB_NA_V, 2), cx(CB_NA_V),
                  pl.BlockSpec((1, N_HEADS, tq, 3 * tq), variant)],
        out_specs=pl.BlockSpec((B, tq, GROUP_W), lambda i: (0, i, 0)),
        out_shape=jax.ShapeDtypeStruct((B, T, GROUP_W), BF16),
        compiler_params=_params(("arbitrary",)),
        name="neighborhood_attention",
    )(P, P, P, P, P, P, P, P, P, bias_tabs)


def _ret_direction(q, k, v, state_ref, lgs_ref, lgv, forward, base):
    C = q.shape[0]
    i = lax.broadcasted_iota(jnp.int32, (C, C), 0)
    j = lax.broadcasted_iota(jnp.int32, (C, C), 1)
    dist = (i - j if forward else j - i).astype(F32)
    keep = dist >= 0 if forward else dist > 0
    dist = jnp.where(keep, dist, 0.0)
    qs = jnp.concatenate(
        [jnp.where(_head_mask(q.shape, HEAD_DIM, h), q, jnp.zeros_like(q)) for h in range(N_HEADS)], axis=0)
    s = lax.dot_general(qs, k, (((1,), (1,)), ((), ())), preferred_element_type=F32)
    atts = []
    for h in range(N_HEADS):
        decay = jnp.where(keep, jnp.exp(dist * lgs_ref[base + h]), 0.0)
        atts.append((s[h * C:(h + 1) * C] * decay).astype(BF16))
    vm = jnp.concatenate(
        [jnp.where(_head_mask(v.shape, HEAD_DIM, h), v, jnp.zeros_like(v)) for h in range(N_HEADS)], axis=0)
    intra = jnp.dot(jnp.concatenate(atts, axis=1), vm, preferred_element_type=F32)
    r = lax.broadcasted_iota(jnp.int32, (C, 1), 0).astype(F32)
    xi = jnp.exp((r + 1.0 if forward else C - r) * lgv)
    zeta = jnp.exp((C - 1.0 - r if forward else r) * lgv)
    state = state_ref[...]
    cross = jnp.dot((q.astype(F32) * xi).astype(BF16), state.astype(BF16), preferred_element_type=F32)
    kz_t = (k.astype(F32) * zeta).T.astype(BF16)
    u = jnp.dot(kz_t, v, preferred_element_type=F32)
    rr = lax.broadcasted_iota(jnp.int32, u.shape, 0) // HEAD_DIM
    cc = lax.broadcasted_iota(jnp.int32, u.shape, 1) // HEAD_DIM
    state_ref[...] = jnp.where(rr == cc, jnp.exp(C * lgv) * state + u, 0.0)
    return intra + cross


def _ret_kernel(lgs_ref, lgv_ref, qf_ref, kf_ref, vf_ref, qb_ref, kb_ref, vb_ref, of_ref, ob_ref, sf_sc, sb_sc):
    @pl.when(pl.program_id(0) == 0)
    def _():
        sf_sc[...] = jnp.zeros(sf_sc.shape, F32)
        sb_sc[...] = jnp.zeros(sb_sc.shape, F32)

    for b in range(qf_ref.shape[0]):
        of_ref[b] = _ret_direction(qf_ref[b], kf_ref[b], vf_ref[b], sf_sc.at[b], lgs_ref, lgv_ref[0:1], True, 0)
        ob_ref[b] = _ret_direction(qb_ref[b], kb_ref[b], vb_ref[b], sb_sc.at[b], lgs_ref, lgv_ref[1:2], False,
                                   N_HEADS)


def _retention(P, lgs, lgv, seq, ctx):
    B, T, _ = P.shape
    C = RET_CHUNK
    ns, nc = seq // C, ctx // C
    nt = ns + nc

    def fwd(n):
        return jnp.where(n < nc, ns + n, n - nc)

    def bwd(n):
        return jnp.where(n < nc, nt - 1 - n, nt - 1 - n)

    def spec(order, cb):
        return pl.BlockSpec((B, C, GROUP_W), lambda n: (0, order(n), cb))

    return pl.pallas_call(
        _ret_kernel,
        grid=(nt,),
        in_specs=[pl.BlockSpec(memory_space=pltpu.SMEM),
                  pl.BlockSpec((8, GROUP_W), lambda n: (0, 0)),
                  spec(fwd, CB_RET_Q), spec(fwd, CB_RET_K), spec(fwd, CB_RET_V),
                  spec(bwd, CB_RET_Q), spec(bwd, CB_RET_K), spec(bwd, CB_RET_V)],
        out_specs=[pl.BlockSpec((B, C, GROUP_W), lambda n: (0, fwd(n), 0)),
                   pl.BlockSpec((B, C, GROUP_W), lambda n: (0, bwd(n), 0))],
        out_shape=[jax.ShapeDtypeStruct((B, T, GROUP_W), F32)] * 2,
        scratch_shapes=[pltpu.VMEM((B, GROUP_W, GROUP_W), F32)] * 2,
        compiler_params=_params(("arbitrary",)),
        name="retention",
    )(lgs, lgv, P, P, P, P, P, P)


def _outproj_kernel(*refs, tm, seq, n_exp):
    with_router = n_exp > 0
    if with_router:
        (ya_ref, yb_ref, yn_ref, of_ref, ob_ref, gt_ref, x_ref, mod_ref, w_ref, gpost_ref, gpre_ref, r_ref,
         x1_ref, h2_ref, eid_ref, gw_ref) = refs
    else:
        (ya_ref, yb_ref, yn_ref, of_ref, ob_ref, gt_ref, x_ref, mod_ref, w_ref, gpost_ref, gpre_ref,
         x1_ref, h2_ref) = refs
    i = pl.program_id(1)
    rows = i * tm + lax.broadcasted_iota(jnp.int32, (tm, 1), 0)
    is_ctx = rows >= seq
    o = of_ref[0] + ob_ref[0]
    yr = o * lax.rsqrt(_group_mean_sq(o) + EPS) * _silu(gt_ref[0].astype(F32))
    y = (jnp.dot(ya_ref[0], w_ref[0:256, :], preferred_element_type=F32)
         + jnp.dot(yb_ref[0], w_ref[256:512, :], preferred_element_type=F32)
         + jnp.dot(yn_ref[0], w_ref[512:768, :], preferred_element_type=F32)
         + jnp.dot(yr.astype(BF16), w_ref[768:1024, :], preferred_element_type=F32))
    yn = y * lax.rsqrt(jnp.mean(y * y, axis=-1, keepdims=True) + EPS) * gpost_ref[...]
    x1 = x_ref[0] + _row_mod(mod_ref, is_ctx, 2) * yn
    x1_ref[0] = x1
    h = x1 * lax.rsqrt(jnp.mean(x1 * x1, axis=-1, keepdims=True) + EPS) * gpre_ref[...]
    h = h * (1.0 + _row_mod(mod_ref, is_ctx, 4)) + _row_mod(mod_ref, is_ctx, 3)
    h2_ref[0] = h.astype(h2_ref.dtype)
    if with_router:
        r = r_ref[...]
        r_hi = r.astype(BF16)
        r_lo = (r - r_hi.astype(F32)).astype(BF16)
        h_hi = h.astype(BF16)
        h_lo = (h - h_hi.astype(F32)).astype(BF16)
        dn = (((1,), (1,)), ((), ()))
        lt = (lax.dot_general(r_hi, h_hi, dn, preferred_element_type=F32)
              + lax.dot_general(r_hi, h_lo, dn, preferred_element_type=F32)
              + lax.dot_general(r_lo, h_hi, dn, preferred_element_type=F32))
        e = lax.broadcasted_iota(jnp.int32, lt.shape, 0)
        lt = jnp.where(e < n_exp, lt, NEG)
        v1 = jnp.max(lt, axis=0, keepdims=True)
        i1 = jnp.min(jnp.where(lt == v1, e, ROUTER_ROWS), axis=0, keepdims=True)
        rest = jnp.where(e == i1, NEG, lt)
        v2 = jnp.max(rest, axis=0, keepdims=True)
        i2 = jnp.min(jnp.where(rest == v2, e, ROUTER_ROWS), axis=0, keepdims=True)
        e2 = jnp.exp(v2 - v1)
        w1 = 1.0 / (1.0 + e2)
        w2 = e2 / (1.0 + e2)
        eid_ref[0] = jnp.where(e == 0, i1, jnp.where(e == 1, i2, 0))[:8]
        wrow = jnp.where(e == 0, w1, jnp.where(e == 1, w2, 0.0))
        gw_ref[0] = jnp.concatenate([wrow, jnp.zeros((LANES - ROUTER_ROWS, tm), F32)], axis=0).T


def _outproj(ya, yb, yn, of, ob, P, xa, modv, w_out_b, g_post, g_pre2, rows, seq, tm, router=None, n_exp=0):
    B, _, D = xa.shape
    with_router = router is not None
    kern = functools.partial(_outproj_kernel, tm=tm, seq=seq, n_exp=n_exp)
    tok = lambda w: pl.BlockSpec((1, tm, w), lambda b, i: (b, i, 0))
    in_specs = [tok(GROUP_W), tok(GROUP_W), tok(GROUP_W), tok(GROUP_W), tok(GROUP_W),
                pl.BlockSpec((1, tm, GROUP_W), lambda b, i: (b, i, CB_RET_G)),
                tok(D),
                pl.BlockSpec((1, 2, 8, D), lambda b, i: (b, 0, 0, 0)),
                pl.BlockSpec((D, D), lambda b, i: (0, 0)),
                pl.BlockSpec((1, D), lambda b, i: (0, 0)),
                pl.BlockSpec((1, D), lambda b, i: (0, 0))]
    args = [ya, yb, yn, of, ob, P, xa, modv, w_out_b, g_post.reshape(1, D), g_pre2.reshape(1, D)]
    out_specs = [tok(D), tok(D)]
    out_shape = [jax.ShapeDtypeStruct((B, rows, D), F32),
                 jax.ShapeDtypeStruct((B, rows, D), F32 if with_router else BF16)]
    if with_router:
        in_specs.append(pl.BlockSpec((ROUTER_ROWS, D), lambda b, i: (0, 0)))
        args.append(router)
        out_specs += [pl.BlockSpec((1, 8, tm), lambda b, i: (b, 0, i)), tok(LANES)]
        out_shape += [jax.ShapeDtypeStruct((B, 8, rows), jnp.int32),
                      jax.ShapeDtypeStruct((B, rows, LANES), F32)]
    return pl.pallas_call(
        kern,
        grid=(B, rows // tm),
        in_specs=in_specs,
        out_specs=out_specs,
        out_shape=out_shape,
        compiler_params=_params(("arbitrary", "arbitrary")),
        name="outproj_router" if with_router else "outproj",
    )(*args)


def _ffn_kernel(h_ref, x_ref, mod_ref, wg_ref, wu_ref, wd_ref, gpost_ref, o_ref, *, tm, seq, fc):
    i = pl.program_id(1)
    rows = i * tm + lax.broadcasted_iota(jnp.int32, (tm, 1), 0)
    is_ctx = rows >= seq
    h = h_ref[0]
    acc = jnp.zeros((tm, o_ref.shape[-1]), F32)
    for c in range(wg_ref.shape[1] // fc):
        g = jnp.dot(h, wg_ref[:, c * fc:(c + 1) * fc], preferred_element_type=F32)
        u = jnp.dot(h, wu_ref[:, c * fc:(c + 1) * fc], preferred_element_type=F32)
        a = (_silu(g) * u).astype(BF16)
        acc = acc + jnp.dot(a, wd_ref[c * fc:(c + 1) * fc, :], preferred_element_type=F32)
    yn = acc * lax.rsqrt(jnp.mean(acc * acc, axis=-1, keepdims=True) + EPS) * gpost_ref[...]
    o_ref[0] = x_ref[0] + _row_mod(mod_ref, is_ctx, 5) * yn


def _dense_ffn(h2, x1, modv, wg, wu, wd, g_post, seq, tm):
    B, T, D = x1.shape
    F = wg.shape[1]
    kern = functools.partial(_ffn_kernel, tm=tm, seq=seq, fc=256)
    const = lambda shape: pl.BlockSpec(shape, lambda b, i: (0, 0), pipeline_mode=pl.Buffered(1))
    return pl.pallas_call(
        kern,
        grid=(B, T // tm),
        in_specs=[pl.BlockSpec((1, tm, D), lambda b, i: (b, i, 0)),
                  pl.BlockSpec((1, tm, D), lambda b, i: (b, i, 0)),
                  pl.BlockSpec((1, 2, 8, D), lambda b, i: (b, 0, 0, 0)),
                  const((D, F)), const((D, F)), const((F, D)),
                  pl.BlockSpec((1, D), lambda b, i: (0, 0))],
        out_specs=pl.BlockSpec((1, tm, D), lambda b, i: (b, i, 0)),
        out_shape=jax.ShapeDtypeStruct((B, T, D), F32),
        compiler_params=_params(("arbitrary", "arbitrary")),
        name="dense_ffn",
    )(h2, x1, modv, wg, wu, wd, g_post.reshape(1, D))


def _row_copy(src_hbm, row, dst_vmem, r, sem):
    return pltpu.make_async_copy(src_hbm.at[pl.ds(row, 1)], dst_vmem.at[pl.ds(r, 1)], sem)


def _moe_ffn_kernel(te_ref, tv_ref, src0_ref, srcn_ref, h_hbm, wg_ref, wu_ref, wd_ref, o_ref,
                    xg_sc, xb_sc, acc_sc, sem, *, tr):
    t = pl.program_id(0)
    f = pl.program_id(1)
    nt = pl.num_programs(0)
    nf = pl.num_programs(1)
    slot = t % 2

    def start_gather(src_ref, s):
        def body(r, c):
            _row_copy(h_hbm, src_ref[0, 0, r], xg_sc.at[s], r, sem.at[s]).start()
            return c

        lax.fori_loop(0, tr, body, 0, unroll=8)

    @pl.when((f == 0) & (t == 0) & (tv_ref[0] > 0))
    def _():
        start_gather(src0_ref, 0)

    @pl.when(f == 0)
    def _():
        acc_sc[...] = jnp.zeros(acc_sc.shape, F32)

    @pl.when((f == 0) & (tv_ref[t] > 0))
    def _():
        pltpu.make_async_copy(h_hbm.at[pl.ds(0, tr)], xg_sc.at[slot], sem.at[slot]).wait()
        xb_sc[...] = xg_sc[slot].astype(BF16)

    @pl.when((f == 0) & (t + 1 < nt) & (tv_ref[jnp.minimum(t + 1, nt - 1)] > 0))
    def _():
        start_gather(srcn_ref, 1 - slot)

    @pl.when(tv_ref[t] > 0)
    def _():
        x = xb_sc[...]
        g = jnp.dot(x, wg_ref[0], preferred_element_type=F32)
        u = jnp.dot(x, wu_ref[0], preferred_element_type=F32)
        a = (_silu(g) * u).astype(BF16)
        acc_sc[...] += jnp.dot(a, wd_ref[0], preferred_element_type=F32)

    @pl.when(f == nf - 1)
    def _():
        o_ref[...] = acc_sc[...]


def _moe_ffn(h_flat, src, tile_expert, tile_valid, wg, wu, wd, n_rows, tr, tf):
    D = h_flat.shape[1]
    F = wg.shape[2]
    nt = n_rows // tr
    return pl.pallas_call(
        functools.partial(_moe_ffn_kernel, tr=tr),
        grid_spec=pltpu.PrefetchScalarGridSpec(
            num_scalar_prefetch=2,
            grid=(nt, F // tf),
            in_specs=[pl.BlockSpec((1, 1, tr), lambda t, f, te, tv: (0, 0, 0), memory_space=pltpu.SMEM),
                      pl.BlockSpec((1, 1, tr), lambda t, f, te, tv: (jnp.minimum(t + 1, nt - 1), 0, 0),
                                   memory_space=pltpu.SMEM),
                      pl.BlockSpec(memory_space=pl.ANY),
                      pl.BlockSpec((1, D, tf), lambda t, f, te, tv: (te[t], 0, f)),
                      pl.BlockSpec((1, D, tf), lambda t, f, te, tv: (te[t], 0, f)),
                      pl.BlockSpec((1, tf, D), lambda t, f, te, tv: (te[t], f, 0))],
            out_specs=pl.BlockSpec((tr, D), lambda t, f, te, tv: (t, 0)),
            scratch_shapes=[pltpu.VMEM((2, tr, D), h_flat.dtype), pltpu.VMEM((tr, D), BF16),
                            pltpu.VMEM((tr, D), F32), pltpu.SemaphoreType.DMA((2,))]),
        out_shape=jax.ShapeDtypeStruct((n_rows, D), F32),
        compiler_params=_params(("arbitrary", "arbitrary")),
        name="moe_ffn",
    )(tile_expert, tile_valid, src.reshape(nt, 1, tr), src.reshape(nt, 1, tr), h_flat, wg, wu, wd)


def _combine_kernel(pos_ref, ys_hbm, gw_ref, x_ref, mod_ref, gpost_ref, o_ref, y1_sc, y2_sc, sem, *, tm, seq):
    def issue(r, c):
        _row_copy(ys_hbm, pos_ref[0, 0, r], y1_sc, r, sem.at[0]).start()
        _row_copy(ys_hbm, pos_ref[0, 1, r], y2_sc, r, sem.at[1]).start()
        return c

    lax.fori_loop(0, tm, issue, 0, unroll=8)
    pltpu.make_async_copy(ys_hbm.at[pl.ds(0, tm)], y1_sc, sem.at[0]).wait()
    pltpu.make_async_copy(ys_hbm.at[pl.ds(0, tm)], y2_sc, sem.at[1]).wait()
    rows = pl.program_id(1) * tm + lax.broadcasted_iota(jnp.int32, (tm, 1), 0)
    gw = gw_ref[0]
    y = gw[:, 0:1] * y1_sc[...] + gw[:, 1:2] * y2_sc[...]
    yn = y * lax.rsqrt(jnp.mean(y * y, axis=-1, keepdims=True) + EPS) * gpost_ref[...]
    o_ref[0] = x_ref[0] + _row_mod(mod_ref, rows >= seq, 5) * yn


def _moe_combine(pos, ys, gw, x1, modv, g_post, seq, tm):
    B, R, D = x1.shape
    ntile = R // tm
    kern = functools.partial(_combine_kernel, tm=tm, seq=seq)
    return pl.pallas_call(
        kern,
        grid=(B, ntile),
        in_specs=[pl.BlockSpec((1, TOP_K, tm), lambda b, i: (b * ntile + i, 0, 0), memory_space=pltpu.SMEM),
                  pl.BlockSpec(memory_space=pl.ANY),
                  pl.BlockSpec((1, tm, LANES), lambda b, i: (b, i, 0)),
                  pl.BlockSpec((1, tm, D), lambda b, i: (b, i, 0)),
                  pl.BlockSpec((1, 2, 8, D), lambda b, i: (b, 0, 0, 0)),
                  pl.BlockSpec((1, D), lambda b, i: (0, 0))],
        out_specs=pl.BlockSpec((1, tm, D), lambda b, i: (b, i, 0)),
        out_shape=jax.ShapeDtypeStruct((B, R, D), F32),
        scratch_shapes=[pltpu.VMEM((tm, D), F32), pltpu.VMEM((tm, D), F32), pltpu.SemaphoreType.DMA((2,))],
        compiler_params=_params(("arbitrary", "arbitrary")),
        name="moe_combine",
    )(pos.reshape(B * ntile, tm, TOP_K).transpose(0, 2, 1), ys, gw, x1, modv, g_post.reshape(1, D))


def _moe_routing(eid, n_exp, tr):
    N = eid.shape[0]
    e_flat = eid.reshape(-1)
    onehot = (e_flat[:, None] == jnp.arange(n_exp)[None, :]).astype(jnp.int32)
    rank = jnp.sum((jnp.cumsum(onehot, axis=0) - 1) * onehot, axis=1)
    counts = jnp.sum(onehot, axis=0)
    padded = ((counts + tr - 1) // tr) * tr
    ends = jnp.cumsum(padded)
    starts = ends - padded
    dest = starts[e_flat] + rank
    n_rows = (TOP_K * N // tr + n_exp) * tr
    src = jnp.zeros((n_rows,), jnp.int32).at[dest].set(jnp.arange(TOP_K * N, dtype=jnp.int32) // TOP_K)
    tile_start = jnp.arange(n_rows // tr, dtype=jnp.int32) * tr
    tile_expert = jnp.minimum(jnp.sum((tile_start[:, None] >= ends[None, :]).astype(jnp.int32), axis=1), n_exp - 1)
    tile_valid = (tile_start < ends[-1]).astype(jnp.int32)
    pos = dest.reshape(N, TOP_K).astype(jnp.int32)
    return src, pos, tile_expert, tile_valid, n_rows


def _swa_perm():
    idx = []
    for g in range(2):
        for kvh in range(2):
            idx += [(kvh * 2 + g) * HEAD_DIM + d for d in range(HEAD_DIM)]
    return jnp.array(idx, jnp.int32)


def kernel(x, c, ctx, c_ctx, w_mod, b_mod, g_attn_pre, g_attn_post, g_ffn_pre, g_ffn_post, w_in, w_out,
           da_lambda_q1, da_lambda_k1, da_lambda_q2, da_lambda_k2, da_subln, swa_sink, na_rpb,
           ret_gamma_fwd, ret_gamma_bwd, ffn_w_gate, ffn_w_up, ffn_w_down,
           moe_router, moe_w_gate, moe_w_up, moe_w_down):
    B, S, D = x.shape
    CTX = ctx.shape[1]
    T = S + CTX
    L = w_mod.shape[0]
    n_exp = moe_router.shape[-1]
    rows = S // GRID_W

    def token_tile(n_rows):
        return next(t for t in TOKEN_TILES if n_rows % t == 0)

    c8 = jnp.zeros((8, D), F32).at[:B].set(c).at[B].set(c_ctx)
    mod = _modulation(c8, w_mod, b_mod).reshape(L, 8, 6, D)
    mod = jnp.pad(mod, ((0, 0), (0, 0), (0, 2), (0, 0)))
    modv = jnp.stack([mod[:, :B], jnp.broadcast_to(mod[:, B:B + 1], (L, B, 8, D))], axis=2)

    tabs = _rope_tables(S, CTX)
    perm = _swa_perm()
    swa0 = CB_SWA_Q * 256
    xa = jnp.concatenate([x, ctx], axis=1)

    for l in range(L):
        lambda_init = 0.8 - 0.6 * math.exp(-0.3 * l)
        last = l == L - 1
        w_in_l = w_in[l]
        w_in_b = jnp.concatenate([w_in_l[:, :swa0], w_in_l[:, swa0:swa0 + 256][:, perm], w_in_l[:, swa0 + 256:]],
                                 axis=1).astype(BF16)
        w_out_l = w_out[l]
        w_out_b = jnp.concatenate([w_out_l[:256], w_out_l[256:512][perm], w_out_l[512:]], axis=0).astype(BF16)

        P, QT, VT = _inproj(xa, modv[l], g_attn_pre[l], w_in_b, tabs, S, token_tile(T))

        lamp = jnp.zeros((8, LANES), F32)
        for r, v in enumerate((da_lambda_q1[l], da_lambda_k1[l], da_lambda_q2[l], da_lambda_k2[l])):
            lamp = lamp.at[r, :DA_QK].set(v)
        subln_full = jnp.tile(da_subln[l], N_HEADS).reshape(1, GROUP_W)
        tq_da = next(t for t in (512, 256) if S % t == 0)
        tk_da = next(t for t in (1408, 768, 512, 256) if T % t == 0)
        ya = _diff_attention(P, QT, VT, lamp, subln_full, lambda_init, 0, S // tq_da, tq_da, 0, T // tk_da, tk_da)
        if not last:
            ya_ctx = _diff_attention(P, QT, VT, lamp, subln_full, lambda_init, S // CTX, 1, CTX, S // CTX, 1, CTX)
            ya = jnp.concatenate([ya, ya_ctx], axis=1)
        yb = _window_attention(P, swa_sink[l].astype(F32), S, CTX)
        yn = _neighborhood_attention(P, _na_bias_tables(na_rpb[l], rows), S, CTX)
        lg = jnp.stack([jax.nn.log_sigmoid(ret_gamma_fwd[l].astype(F32)),
                        jax.nn.log_sigmoid(ret_gamma_bwd[l].astype(F32))])
        lgv = jnp.zeros((8, GROUP_W), F32).at[:2].set(jnp.repeat(lg, HEAD_DIM, axis=1))
        of, ob = _retention(P, lg.reshape(-1), lgv, S, CTX)

        R = S if last else T
        e = l // 2
        if l % 2 == 0:
            x1, h2 = _outproj(ya, yb, yn, of, ob, P, xa, modv[l], w_out_b, g_attn_post[l], g_ffn_pre[l],
                              R, S, token_tile(R))
            xa = _dense_ffn(h2, x1, modv[l], ffn_w_gate[e].astype(BF16), ffn_w_up[e].astype(BF16),
                            ffn_w_down[e].astype(BF16), g_ffn_post[l], S, token_tile(R))
        else:
            router = jnp.zeros((ROUTER_ROWS, D), F32).at[:n_exp].set(moe_router[e].T)
            x1, h2, eid, gw = _outproj(ya, yb, yn, of, ob, P, xa, modv[l], w_out_b, g_attn_post[l],
                                       g_ffn_pre[l], R, S, token_tile(R), router=router, n_exp=n_exp)
            tr = 512
            src, pos, tile_expert, tile_valid, n_rows = _moe_routing(
                jnp.swapaxes(eid[:, :TOP_K, :], 1, 2).reshape(B * R, TOP_K), n_exp, tr)
            ys = _moe_ffn(h2.reshape(B * R, D), src, tile_expert, tile_valid, moe_w_gate[e].astype(BF16),
                          moe_w_up[e].astype(BF16), moe_w_down[e].astype(BF16), n_rows, tr,
                          next(t for t in EXPERT_F_TILES if moe_w_gate.shape[-1] % t == 0))
            xa = _moe_combine(pos, ys, gw, x1, modv[l], g_ffn_post[l], S, 256)
    return xa[:, :S]
```

```python
import functools
import math

import jax
import jax.numpy as jnp
from jax import lax
from jax.experimental import pallas as pl
from jax.experimental.pallas import tpu as pltpu

F32 = jnp.float32
BF16 = jnp.bfloat16

GRID_W = 64
HEAD_DIM = 64
N_HEADS = 4
GROUP_W = N_HEADS * HEAD_DIM
DA_QK = 32
SWA_WINDOW = 128
NA_WIN_ROWS = 8
NA_WIN_COLS = 16
NA_TILE_ROWS = 4
RET_CHUNK = 128
ROPE_BASE = 10000.0
TOP_K = 2
EPS = 1e-6
NEG = -1e30
LOG2E = 1.4426950408889634
LANES = 128
ONES_ROWS = 16
ROUTER_ROWS = 16
TOKEN_TILES = (768, 512, 256)
EXPERT_F_TILES = (1792, 512, 256)
DA_MAX_JUMP = 64.0
DA_CHUNK = 128
VMEM_LIMIT = 56 * 1024 * 1024

CB_DA_Q, CB_DA_K, CB_DA_V = 0, 1, 2
CB_SWA_Q = 3
CB_SWA_K128, CB_SWA_V128 = 8, 9
CB_NA_Q, CB_NA_K, CB_NA_V = 5, 6, 7
CB_RET_Q, CB_RET_K, CB_RET_V, CB_RET_G = 8, 9, 10, 11
IN_WIDTH = 3072


def _params(sem):
    return pltpu.CompilerParams(dimension_semantics=sem, vmem_limit_bytes=VMEM_LIMIT)


def _silu(v):
    return v / (1.0 + jnp.exp(-v))


def _head_mask(shape, head_w, h, dim=1):
    return lax.broadcasted_iota(jnp.int32, shape, dim) // head_w == h


def _lane_pick(a, b):
    lane = lax.broadcasted_iota(jnp.int32, a.shape, 1)
    return jnp.where(lane < HEAD_DIM, a, b)


def _per_head_full(vals):
    return jnp.concatenate([_lane_pick(vals[0], vals[1]), _lane_pick(vals[2], vals[3])], axis=1)


def _group_mean_sq(a):
    sq = a * a
    r = lax.broadcasted_iota(jnp.int32, (GROUP_W, GROUP_W), 0) // HEAD_DIM
    c = lax.broadcasted_iota(jnp.int32, (GROUP_W, GROUP_W), 1) // HEAD_DIM
    g = jnp.where(r == c, 1.0, 0.0).astype(BF16)
    hi = sq.astype(BF16)
    lo = (sq - hi.astype(F32)).astype(BF16)
    tot = jnp.dot(hi, g, preferred_element_type=F32) + jnp.dot(lo, g, preferred_element_type=F32)
    return tot * (1.0 / HEAD_DIM)


def _row_mod(mod_ref, rows_are_ctx, j):
    return jnp.where(rows_are_ctx, mod_ref[0, 1, j:j + 1, :], mod_ref[0, 0, j:j + 1, :])


def _mod_kernel(c_ref, w_ref, b_ref, o_ref):
    s = _silu(c_ref[...])
    o_ref[0] = jnp.dot(s.astype(BF16), w_ref[0].astype(BF16), preferred_element_type=F32) + b_ref[0]


def _modulation(c8, w_mod, b_mod):
    L, D, W = w_mod.shape
    tn = 1536
    return pl.pallas_call(
        _mod_kernel,
        grid=(L, W // tn),
        in_specs=[pl.BlockSpec((8, D), lambda l, j: (0, 0)),
                  pl.BlockSpec((1, D, tn), lambda l, j: (l, 0, j)),
                  pl.BlockSpec((1, 1, tn), lambda l, j: (l, 0, j))],
        out_specs=pl.BlockSpec((1, 8, tn), lambda l, j: (l, 0, j)),
        out_shape=jax.ShapeDtypeStruct((L, 8, W), F32),
        compiler_params=_params(("arbitrary", "arbitrary")),
        name="modulation",
    )(c8, w_mod, b_mod.reshape(L, 1, W))


def _lane_block_ops():
    da_s = DA_QK ** -0.5 * LOG2E
    s64 = HEAD_DIM ** -0.5
    ops = []
    ops += [("r32", da_s)] * 2 + [("r32", 1.0)] * 2 + [(None, 1.0)] * 2
    ops += [("r64", s64 * LOG2E)] * 2 + [("r64", 1.0)] + [(None, 1.0)]
    ops += [(None, s64 * LOG2E)] * 2 + [(None, 1.0)] * 4
    ops += [("r64", 1.0)] * 2 + [("r64", s64)] * 2 + [(None, 1.0)] * 4
    return ops


def _rope(x, cos, sin_signed, w):
    lane = lax.broadcasted_iota(jnp.int32, x.shape, 1)
    first = (lane % (2 * w)) < w
    xr = jnp.where(first, pltpu.roll(x, LANES - w, 1), pltpu.roll(x, w, 1))
    return x * cos + xr * sin_signed


def _inproj_kernel(x_ref, mod_ref, g_ref, w_ref, c32_ref, s32_ref, c64_ref, s64_ref, o_ref, qt_ref, vt_ref,
                   *, tm, seq):
    i = pl.program_id(1)
    x = x_ref[0]
    rows = i * tm + lax.broadcasted_iota(jnp.int32, (tm, 1), 0)
    is_ctx = rows >= seq
    ms = jnp.mean(x * x, axis=-1, keepdims=True)
    h = x * lax.rsqrt(ms + EPS) * g_ref[...]
    h = h * (1.0 + _row_mod(mod_ref, is_ctx, 1)) + _row_mod(mod_ref, is_ctx, 0)
    hb = h.astype(BF16)
    ops = _lane_block_ops()
    for cb in range(IN_WIDTH // 256):
        p = jnp.dot(hb, w_ref[:, cb * 256:(cb + 1) * 256], preferred_element_type=F32)
        halves = []
        for hf in range(2):
            kind, scale = ops[cb * 2 + hf]
            ph = p[:, hf * LANES:(hf + 1) * LANES]
            if kind == "r32":
                ph = _rope(ph, c32_ref[...], s32_ref[...], DA_QK // 4)
            elif kind == "r64":
                ph = _rope(ph, c64_ref[...], s64_ref[...], HEAD_DIM // 4)
            if scale != 1.0:
                ph = ph * scale
            halves.append(ph)
        full = jnp.concatenate(halves, axis=1)
        o_ref[0, :, cb * 256:(cb + 1) * 256] = full.astype(BF16)
        if cb == CB_DA_Q:
            qt_ref[0] = full.T.astype(BF16)
        elif cb == CB_DA_V:
            vt_ref[0] = full.T.astype(BF16)


def _inproj(xa, modv, g, w_in_b, tabs, seq, tm):
    B, T, D = xa.shape
    kern = functools.partial(_inproj_kernel, tm=tm, seq=seq)
    tab_spec = pl.BlockSpec((tm, LANES), lambda b, i: (i, 0))
    return pl.pallas_call(
        kern,
        grid=(B, T // tm),
        in_specs=[pl.BlockSpec((1, tm, D), lambda b, i: (b, i, 0)),
                  pl.BlockSpec((1, 2, 8, D), lambda b, i: (b, 0, 0, 0)),
                  pl.BlockSpec((1, D), lambda b, i: (0, 0)),
                  pl.BlockSpec((D, IN_WIDTH), lambda b, i: (0, 0)),
                  tab_spec, tab_spec, tab_spec, tab_spec],
        out_specs=[pl.BlockSpec((1, tm, IN_WIDTH), lambda b, i: (b, i, 0)),
                   pl.BlockSpec((1, GROUP_W, tm), lambda b, i: (b, 0, i)),
                   pl.BlockSpec((1, GROUP_W, tm), lambda b, i: (b, 0, i))],
        out_shape=[jax.ShapeDtypeStruct((B, T, IN_WIDTH), BF16),
                   jax.ShapeDtypeStruct((B, GROUP_W, T), BF16),
                   jax.ShapeDtypeStruct((B, GROUP_W, T), BF16)],
        compiler_params=_params(("arbitrary", "arbitrary")),
        name="inproj",
    )(xa, modv, g.reshape(1, D), w_in_b, *tabs)


def _rope_tables(seq, ctx):
    t = jnp.arange(seq)
    row = (t // GRID_W).astype(F32)
    col = (t % GRID_W).astype(F32)
    out = []
    for dh in (DA_QK, HEAD_DIM):
        half = dh // 2
        quarter = half // 2
        lane = jnp.arange(LANES)
        d = lane % dh
        use_col = (d // half) == 1
        idx = (d % quarter).astype(F32)
        inv = ROPE_BASE ** (-idx * 2.0 / half)
        pos = jnp.where(use_col[None, :], col[:, None], row[:, None])
        ang = pos * inv[None, :]
        first = (d % half) < quarter
        cos = jnp.cos(ang)
        sin = jnp.where(first[None, :], -jnp.sin(ang), jnp.sin(ang))
        cos = jnp.concatenate([cos, jnp.ones((ctx, LANES), F32)], axis=0)
        sin = jnp.concatenate([sin, jnp.zeros((ctx, LANES), F32)], axis=0)
        out += [cos, sin]
    return out


def _rows_per_head(vals, tq):
    return jnp.concatenate([jnp.broadcast_to(v, (HEAD_DIM, tq)) for v in vals], axis=0)


def _da_kernel(lamp_ref, qt_ref, k_ref, vt_ref, g_ref, o_ref, qm_sc, m_sc, l_sc, acc_sc, st_sc, p_sc, pv_sc, mx_sc,
               jump_sc, *, tq, lambda_init):
    ki = pl.program_id(2)
    nk = pl.num_programs(2)

    @pl.when(ki == 0)
    def _():
        qt = qt_ref[0]
        for j in range(2 * N_HEADS):
            qm_sc[j] = jnp.where(_head_mask(qt.shape, DA_QK, j, 0), qt, jnp.zeros_like(qt))
        m_sc[...] = jnp.full(m_sc.shape, NEG, F32)
        l_sc[...] = jnp.zeros(l_sc.shape, F32)
        acc_sc[...] = jnp.zeros(acc_sc.shape, F32)

    k = k_ref[0]
    vt = vt_ref[0]
    ones = jnp.ones((ONES_ROWS, vt.shape[1]), BF16)

    tk = k.shape[0]

    def vt_ext(h):
        return jnp.concatenate([vt[h * HEAD_DIM:(h + 1) * HEAD_DIM, :], ones], axis=0)

    @pl.when(ki == 0)
    def _():
        jump_sc[0] = 2.0 * DA_MAX_JUMP

    @pl.when(ki > 0)
    def _():
        jump = None
        for j in range(2 * N_HEADS):
            st = jnp.dot(k, qm_sc[j], preferred_element_type=F32)
            m_prev = m_sc[j]
            mx = jnp.max(st, axis=0, keepdims=True)
            p = jnp.exp2(st - m_prev).astype(BF16)
            pv_sc[j] = jnp.dot(vt_ext(j // 2), p, preferred_element_type=F32)
            mx_sc[j] = mx
            worst = jnp.max(mx - m_prev)
            jump = worst if jump is None else jnp.maximum(jump, worst)
        jump_sc[0] = jump

    redo = jump_sc[0] > DA_MAX_JUMP

    @pl.when(jnp.logical_not(redo))
    def _():
        for j in range(2 * N_HEADS):
            h, t = j // 2, j % 2
            rows = slice(h * HEAD_DIM, (h + 1) * HEAD_DIM)
            m_prev = m_sc[j]
            m_new = jnp.maximum(m_prev, mx_sc[j])
            alpha = jnp.exp2(m_prev - m_new)
            l_sc[j] = alpha * (l_sc[j] + pv_sc[j, HEAD_DIM:HEAD_DIM + 1, :])
            acc_sc[t, rows, :] = alpha * (acc_sc[t, rows, :] + pv_sc[j, :HEAD_DIM, :])
            m_sc[j] = m_new

    @pl.when(redo)
    def _():
        for j in range(2 * N_HEADS):
            h, t = j // 2, j % 2
            rows = slice(h * HEAD_DIM, (h + 1) * HEAD_DIM)
            st_sc[...] = jnp.dot(k, qm_sc[j], preferred_element_type=F32)
            run = st_sc[0:DA_CHUNK, :]
            for c in range(1, tk // DA_CHUNK):
                run = jnp.maximum(run, st_sc[c * DA_CHUNK:(c + 1) * DA_CHUNK, :])
            m_prev = m_sc[j]
            m_new = jnp.maximum(m_prev, jnp.max(run, axis=0, keepdims=True))
            alpha = jnp.exp2(m_prev - m_new)
            for c in range(tk // DA_CHUNK):
                cr = slice(c * DA_CHUNK, (c + 1) * DA_CHUNK)
                p_sc[cr, :] = jnp.exp2(st_sc[cr, :] - m_new).astype(BF16)
            pv = jnp.dot(vt_ext(h), p_sc[...], preferred_element_type=F32)
            l_sc[j] = alpha * l_sc[j] + pv[HEAD_DIM:HEAD_DIM + 1, :]
            m_sc[j] = m_new
            acc_sc[t, rows, :] = alpha * acc_sc[t, rows, :] + pv[:HEAD_DIM, :]

    @pl.when(ki == nk - 1)
    def _():
        lp = lamp_ref[...]
        lam = (jnp.exp(jnp.sum(lp[0:1] * lp[1:2], axis=1, keepdims=True))
               - jnp.exp(jnp.sum(lp[2:3] * lp[3:4], axis=1, keepdims=True))) + lambda_init
        o0 = acc_sc[0] / _rows_per_head([l_sc[2 * h] for h in range(N_HEADS)], tq)
        o1 = acc_sc[1] / _rows_per_head([l_sc[2 * h + 1] for h in range(N_HEADS)], tq)
        a = (o0 - lam * o1).T
        y = a * lax.rsqrt(_group_mean_sq(a) + EPS) * g_ref[...]
        o_ref[0] = (y * (1.0 - lambda_init)).astype(BF16)


def _diff_attention(P, QT, VT, lamp, subln_full, lambda_init, q0, nq, tq, k0, nk, tk):
    B = P.shape[0]
    kern = functools.partial(_da_kernel, tq=tq, lambda_init=lambda_init)
    return pl.pallas_call(
        kern,
        grid=(B, nq, nk),
        in_specs=[pl.BlockSpec((8, LANES), lambda b, qi, ki: (0, 0)),
                  pl.BlockSpec((1, GROUP_W, tq), lambda b, qi, ki: (b, 0, q0 + qi)),
                  pl.BlockSpec((1, tk, GROUP_W), lambda b, qi, ki: (b, k0 + ki, CB_DA_K)),
                  pl.BlockSpec((1, GROUP_W, tk), lambda b, qi, ki: (b, 0, k0 + ki)),
                  pl.BlockSpec((1, GROUP_W), lambda b, qi, ki: (0, 0))],
        out_specs=pl.BlockSpec((1, tq, GROUP_W), lambda b, qi, ki: (b, qi, 0)),
        out_shape=jax.ShapeDtypeStruct((B, nq * tq, GROUP_W), BF16),
        scratch_shapes=[pltpu.VMEM((2 * N_HEADS, GROUP_W, tq), BF16),
                        pltpu.VMEM((2 * N_HEADS, 1, tq), F32),
                        pltpu.VMEM((2 * N_HEADS, 1, tq), F32),
                        pltpu.VMEM((2, GROUP_W, tq), F32),
                        pltpu.VMEM((tk, tq), F32),
                        pltpu.VMEM((tk, tq), BF16),
                        pltpu.VMEM((2 * N_HEADS, HEAD_DIM + ONES_ROWS, tq), F32),
                        pltpu.VMEM((2 * N_HEADS, 1, tq), F32),
                        pltpu.SMEM((1,), F32)],
        compiler_params=_params(("arbitrary", "arbitrary", "arbitrary")),
        name="diff_attention",
    )(lamp, QT, P, VT, subln_full)


def _swa_kernel(sink_ref, q_ref, kp_ref, kc_ref, kn_ref, kx_ref, vp_ref, vc_ref, vn_ref, vx_ref, o_ref,
                *, seq, blk):
    i = pl.program_id(0)
    nloc = 3 * blk
    nkeys = nloc + kx_ref.shape[1]
    qpos = i * blk + lax.broadcasted_iota(jnp.int32, (blk, nkeys), 0)
    c = lax.broadcasted_iota(jnp.int32, (blk, nkeys), 1)
    kpos = (i - 1) * blk + c
    valid = ((kpos >= 0) & (kpos < seq) & (jnp.abs(qpos - kpos) <= SWA_WINDOW) & (qpos < seq)) | (c >= nloc)
    for b in range(q_ref.shape[0]):
        q = q_ref[b]
        kall = jnp.concatenate([kp_ref[b], kc_ref[b], kn_ref[b], kx_ref[b]], axis=0)
        vall = jnp.concatenate([vp_ref[b], vc_ref[b], vn_ref[b], vx_ref[b]], axis=0)
        qs = []
        for g in range(2):
            qg = q[:, g * LANES:(g + 1) * LANES]
            for kvh in range(2):
                qs.append(jnp.where(_head_mask(qg.shape, HEAD_DIM, kvh), qg, jnp.zeros_like(qg)))
        s = lax.dot_general(jnp.concatenate(qs, axis=0), kall, (((1,), (1,)), ((), ())),
                            preferred_element_type=F32)
        vms = [jnp.where(_head_mask(vall.shape, HEAD_DIM, kvh), vall, jnp.zeros_like(vall)) for kvh in range(2)]
        vm = jnp.concatenate(vms, axis=0)
        for g in range(2):
            ps = []
            for kvh in range(2):
                j = 2 * g + kvh
                sink = sink_ref[2 * kvh + g] * LOG2E
                sj = jnp.where(valid, s[j * blk:(j + 1) * blk], NEG)
                m = jnp.maximum(jnp.max(sj, axis=1, keepdims=True), sink)
                e = jnp.exp2(sj - m)
                den = jnp.sum(e, axis=1, keepdims=True) + jnp.exp2(sink - m)
                ps.append((e / den).astype(BF16))
            o_ref[b, :, g * LANES:(g + 1) * LANES] = jnp.dot(
                jnp.concatenate(ps, axis=1), vm, preferred_element_type=F32).astype(BF16)


def _window_attention(P, sink, seq, ctx):
    B, T, _ = P.shape
    blk = SWA_WINDOW
    nt = T // blk
    kern = functools.partial(_swa_kernel, seq=seq, blk=blk)

    def nb(cb, off):
        return pl.BlockSpec((B, blk, LANES), lambda i: (0, jnp.clip(i + off, 0, nt - 1), cb))

    def cx(cb):
        return pl.BlockSpec((B, ctx, LANES), lambda i: (0, seq // ctx, cb))

    return pl.pallas_call(
        kern,
        grid=(nt,),
        in_specs=[pl.BlockSpec(memory_space=pltpu.SMEM),
                  pl.BlockSpec((B, blk, GROUP_W), lambda i: (0, i, CB_SWA_Q)),
                  nb(CB_SWA_K128, -1), nb(CB_SWA_K128, 0), nb(CB_SWA_K128, 1), cx(CB_SWA_K128),
                  nb(CB_SWA_V128, -1), nb(CB_SWA_V128, 0), nb(CB_SWA_V128, 1), cx(CB_SWA_V128)],
        out_specs=pl.BlockSpec((B, blk, GROUP_W), lambda i: (0, i, 0)),
        out_shape=jax.ShapeDtypeStruct((B, T, GROUP_W), BF16),
        compiler_params=_params(("arbitrary",)),
        name="window_attention",
    )(sink, P, P, P, P, P, P, P, P, P)


def _na_kernel(q_ref, k0_ref, k1_ref, k2_ref, kx_ref, v0_ref, v1_ref, v2_ref, vx_ref, bias_ref, o_ref,
               *, tq, n_seq_tiles):
    ctx_tile = pl.program_id(0) >= n_seq_tiles
    nloc = 3 * tq
    for b in range(q_ref.shape[0]):
        q = q_ref[b]
        kall = jnp.concatenate([k0_ref[b], k1_ref[b], k2_ref[b], kx_ref[b]], axis=0)
        vall = jnp.concatenate([v0_ref[b], v1_ref[b], v2_ref[b], vx_ref[b]], axis=0)
        qs = jnp.concatenate(
            [jnp.where(_head_mask(q.shape, HEAD_DIM, h), q, jnp.zeros_like(q)) for h in range(N_HEADS)], axis=0)
        s = lax.dot_general(qs, kall, (((1,), (1,)), ((), ())), preferred_element_type=F32)
        ps = []
        for h in range(N_HEADS):
            sh = s[h * tq:(h + 1) * tq]
            s_loc = jnp.where(ctx_tile, NEG, sh[:, :nloc] + bias_ref[0, h])
            s_ctx = sh[:, nloc:]
            m = jnp.maximum(jnp.max(s_loc, axis=1, keepdims=True), jnp.max(s_ctx, axis=1, keepdims=True))
            e_loc = jnp.exp2(s_loc - m)
            e_ctx = jnp.exp2(s_ctx - m)
            den = jnp.sum(e_loc, axis=1, keepdims=True) + jnp.sum(e_ctx, axis=1, keepdims=True)
            ps.append(jnp.concatenate([e_loc / den, e_ctx / den], axis=1).astype(BF16))
        vm = jnp.concatenate(
            [jnp.where(_head_mask(vall.shape, HEAD_DIM, h), vall, jnp.zeros_like(vall)) for h in range(N_HEADS)],
            axis=0)
        o_ref[b] = jnp.dot(jnp.concatenate(ps, axis=1), vm, preferred_element_type=F32).astype(BF16)


def _na_bias_kernel(rt_ref, o_ref, sv_sc, *, rows):
    tr = NA_TILE_ROWS
    nrt = rows // tr
    nl = 3 * tr * GRID_W
    n_dcol = 2 * NA_WIN_COLS - 1
    rt = rt_ref[0] * LOG2E
    hi = rt.astype(BF16)
    r1 = rt - hi.astype(F32)
    mid = r1.astype(BF16)
    lo = (r1 - mid.astype(F32)).astype(BF16)
    lane = lax.broadcasted_iota(jnp.int32, (GRID_W, nl), 1)
    qc = lax.broadcasted_iota(jnp.int32, (GRID_W, nl), 0)
    kc = lane % GRID_W
    kr = lane // GRID_W
    dc = jnp.clip(kc - qc, 1 - NA_WIN_COLS, NA_WIN_COLS - 1) + (NA_WIN_COLS - 1)
    c0 = jnp.clip(qc - NA_WIN_COLS // 2, 0, GRID_W - NA_WIN_COLS)
    col_ok = (kc >= c0) & (kc < c0 + NA_WIN_COLS)
    irow = lax.broadcasted_iota(jnp.int32, (LANES, nl), 0)
    krl = lax.broadcasted_iota(jnp.int32, (LANES, nl), 1) // GRID_W
    for v, (t_idx, ws) in enumerate(((0, 0), (1, 0), (nrt - 1, nrt - 3))):
        for qr in range(tr):
            r = t_idx * tr + qr
            r0 = min(max(r - NA_WIN_ROWS // 2, 0), rows - NA_WIN_ROWS)
            d_row = jnp.clip(ws * tr + krl - r + (NA_WIN_ROWS - 1), 0, 2 * NA_WIN_ROWS - 2)
            onehot = jnp.where(irow == d_row, 1.0, 0.0).astype(BF16)
            sv_sc[...] = (jnp.dot(hi, onehot, preferred_element_type=F32)
                          + jnp.dot(mid, onehot, preferred_element_type=F32)
                          + jnp.dot(lo, onehot, preferred_element_type=F32))

            def pick(j, acc):
                return jnp.where(dc == j, sv_sc[pl.ds(j, 1), :], acc)

            acc = lax.fori_loop(0, n_dcol, pick, jnp.zeros((GRID_W, nl), F32))
            krow = ws * tr + kr
            ok = col_ok & (krow >= r0) & (krow < r0 + NA_WIN_ROWS)
            o_ref[v, 0, qr * GRID_W:(qr + 1) * GRID_W, :] = jnp.where(ok, acc, NEG)


def _na_bias_tables(rpb, rows):
    H, nr, ncol = rpb.shape
    rt = jnp.zeros((H, 32, LANES), F32).at[:, :ncol, :nr].set(jnp.swapaxes(rpb.astype(F32), 1, 2))
    tq = NA_TILE_ROWS * GRID_W
    return pl.pallas_call(
        functools.partial(_na_bias_kernel, rows=rows),
        grid=(H,),
        in_specs=[pl.BlockSpec((1, 32, LANES), lambda h: (h, 0, 0))],
        out_specs=pl.BlockSpec((3, 1, tq, 3 * tq), lambda h: (0, h, 0, 0)),
        out_shape=jax.ShapeDtypeStruct((3, H, tq, 3 * tq), F32),
        scratch_shapes=[pltpu.VMEM((32, 3 * tq), F32)],
        compiler_params=_params(("arbitrary",)),
        name="na_bias",
    )(rt)


def _neighborhood_attention(P, bias_tabs, seq, ctx):
    B, T, _ = P.shape
    tq = NA_TILE_ROWS * GRID_W
    nst = seq // tq
    nt = T // tq
    kern = functools.partial(_na_kernel, tq=tq, n_seq_tiles=nst)

    def win(i):
        return jnp.clip(i - 1, 0, nst - 3)

    def loc(cb, off):
        return pl.BlockSpec((B, tq, GROUP_W), lambda i: (0, win(i) + off, cb))

    def cx(cb):
        return pl.BlockSpec((B, ctx, GROUP_W), lambda i: (0, seq // ctx, cb))

    def variant(i):
        return (jnp.where(i >= nst, 0, i - win(i)), 0, 0, 0)

    return pl.pallas_call(
        kern,
        grid=(nt,),
        in_specs=[pl.BlockSpec((B, tq, GROUP_W), lambda i: (0, i, CB_NA_Q)),
                  loc(CB_NA_K, 0), loc(CB_NA_K, 1), loc(CB_NA_K, 2), cx(CB_NA_K),
                  loc(CB_NA_V, 0), loc(CB_NA_V, 1), loc(CB_NA_V, 2), cx(CB_NA_V),
                  pl.BlockSpec((1, N_HEADS, tq, 3 * tq), variant)],
        out_specs=pl.BlockSpec((B, tq, GROUP_W), lambda i: (0, i, 0)),
        out_shape=jax.ShapeDtypeStruct((B, T, GROUP_W), BF16),
        compiler_params=_params(("arbitrary",)),
        name="neighborhood_attention",
    )(P, P, P, P, P, P, P, P, P, bias_tabs)


def _ret_direction(q, k, v, state_ref, lgs_ref, lgv, forward, base):
    C = q.shape[0]
    i = lax.broadcasted_iota(jnp.int32, (C, C), 0)
    j = lax.broadcasted_iota(jnp.int32, (C, C), 1)
    dist = (i - j if forward else j - i).astype(F32)
    keep = dist >= 0 if forward else dist > 0
    dist = jnp.where(keep, dist, 0.0)
    qs = jnp.concatenate(
        [jnp.where(_head_mask(q.shape, HEAD_DIM, h), q, jnp.zeros_like(q)) for h in range(N_HEADS)], axis=0)
    s = lax.dot_general(qs, k, (((1,), (1,)), ((), ())), preferred_element_type=F32)
    atts = []
    for h in range(N_HEADS):
        decay = jnp.where(keep, jnp.exp(dist * lgs_ref[base + h]), 0.0)
        atts.append((s[h * C:(h + 1) * C] * decay).astype(BF16))
    vm = jnp.concatenate(
        [jnp.where(_head_mask(v.shape, HEAD_DIM, h), v, jnp.zeros_like(v)) for h in range(N_HEADS)], axis=0)
    intra = jnp.dot(jnp.concatenate(atts, axis=1), vm, preferred_element_type=F32)
    r = lax.broadcasted_iota(jnp.int32, (C, 1), 0).astype(F32)
    xi = jnp.exp((r + 1.0 if forward else C - r) * lgv)
    zeta = jnp.exp((C - 1.0 - r if forward else r) * lgv)
    state = state_ref[...]
    cross = jnp.dot((q.astype(F32) * xi).astype(BF16), state.astype(BF16), preferred_element_type=F32)
    kz_t = (k.astype(F32) * zeta).T.astype(BF16)
    u = jnp.dot(kz_t, v, preferred_element_type=F32)
    rr = lax.broadcasted_iota(jnp.int32, u.shape, 0) // HEAD_DIM
    cc = lax.broadcasted_iota(jnp.int32, u.shape, 1) // HEAD_DIM
    state_ref[...] = jnp.where(rr == cc, jnp.exp(C * lgv) * state + u, 0.0)
    return intra + cross


def _ret_kernel(lgs_ref, lgv_ref, qf_ref, kf_ref, vf_ref, qb_ref, kb_ref, vb_ref, of_ref, ob_ref, sf_sc, sb_sc):
    @pl.when(pl.program_id(0) == 0)
    def _():
        sf_sc[...] = jnp.zeros(sf_sc.shape, F32)
        sb_sc[...] = jnp.zeros(sb_sc.shape, F32)

    for b in range(qf_ref.shape[0]):
        of_ref[b] = _ret_direction(qf_ref[b], kf_ref[b], vf_ref[b], sf_sc.at[b], lgs_ref, lgv_ref[0:1], True, 0)
        ob_ref[b] = _ret_direction(qb_ref[b], kb_ref[b], vb_ref[b], sb_sc.at[b], lgs_ref, lgv_ref[1:2], False,
                                   N_HEADS)


def _retention(P, lgs, lgv, seq, ctx):
    B, T, _ = P.shape
    C = RET_CHUNK
    ns, nc = seq // C, ctx // C
    nt = ns + nc

    def fwd(n):
        return jnp.where(n < nc, ns + n, n - nc)

    def bwd(n):
        return jnp.where(n < nc, nt - 1 - n, nt - 1 - n)

    def spec(order, cb):
        return pl.BlockSpec((B, C, GROUP_W), lambda n: (0, order(n), cb))

    return pl.pallas_call(
        _ret_kernel,
        grid=(nt,),
        in_specs=[pl.BlockSpec(memory_space=pltpu.SMEM),
                  pl.BlockSpec((8, GROUP_W), lambda n: (0, 0)),
                  spec(fwd, CB_RET_Q), spec(fwd, CB_RET_K), spec(fwd, CB_RET_V),
                  spec(bwd, CB_RET_Q), spec(bwd, CB_RET_K), spec(bwd, CB_RET_V)],
        out_specs=[pl.BlockSpec((B, C, GROUP_W), lambda n: (0, fwd(n), 0)),
                   pl.BlockSpec((B, C, GROUP_W), lambda n: (0, bwd(n), 0))],
        out_shape=[jax.ShapeDtypeStruct((B, T, GROUP_W), F32)] * 2,
        scratch_shapes=[pltpu.VMEM((B, GROUP_W, GROUP_W), F32)] * 2,
        compiler_params=_params(("arbitrary",)),
        name="retention",
    )(lgs, lgv, P, P, P, P, P, P)


def _outproj_kernel(*refs, tm, seq, n_exp):
    with_router = n_exp > 0
    if with_router:
        (ya_ref, yb_ref, yn_ref, of_ref, ob_ref, gt_ref, x_ref, mod_ref, w_ref, gpost_ref, gpre_ref, r_ref,
         x1_ref, h2_ref, eid_ref, gw_ref) = refs
    else:
        (ya_ref, yb_ref, yn_ref, of_ref, ob_ref, gt_ref, x_ref, mod_ref, w_ref, gpost_ref, gpre_ref,
         x1_ref, h2_ref) = refs
    i = pl.program_id(1)
    rows = i * tm + lax.broadcasted_iota(jnp.int32, (tm, 1), 0)
    is_ctx = rows >= seq
    o = of_ref[0] + ob_ref[0]
    yr = o * lax.rsqrt(_group_mean_sq(o) + EPS) * _silu(gt_ref[0].astype(F32))
    y = (jnp.dot(ya_ref[0], w_ref[0:256, :], preferred_element_type=F32)
         + jnp.dot(yb_ref[0], w_ref[256:512, :], preferred_element_type=F32)
         + jnp.dot(yn_ref[0], w_ref[512:768, :], preferred_element_type=F32)
         + jnp.dot(yr.astype(BF16), w_ref[768:1024, :], preferred_element_type=F32))
    yn = y * lax.rsqrt(jnp.mean(y * y, axis=-1, keepdims=True) + EPS) * gpost_ref[...]
    x1 = x_ref[0] + _row_mod(mod_ref, is_ctx, 2) * yn
    x1_ref[0] = x1
    h = x1 * lax.rsqrt(jnp.mean(x1 * x1, axis=-1, keepdims=True) + EPS) * gpre_ref[...]
    h = h * (1.0 + _row_mod(mod_ref, is_ctx, 4)) + _row_mod(mod_ref, is_ctx, 3)
    h2_ref[0] = h.astype(h2_ref.dtype)
    if with_router:
        r = r_ref[...]
        r_hi = r.astype(BF16)
        r_lo = (r - r_hi.astype(F32)).astype(BF16)
        h_hi = h.astype(BF16)
        h_lo = (h - h_hi.astype(F32)).astype(BF16)
        dn = (((1,), (1,)), ((), ()))
        lt = (lax.dot_general(r_hi, h_hi, dn, preferred_element_type=F32)
              + lax.dot_general(r_hi, h_lo, dn, preferred_element_type=F32)
              + lax.dot_general(r_lo, h_hi, dn, preferred_element_type=F32))
        e = lax.broadcasted_iota(jnp.int32, lt.shape, 0)
        lt = jnp.where(e < n_exp, lt, NEG)
        v1 = jnp.max(lt, axis=0, keepdims=True)
        i1 = jnp.min(jnp.where(lt == v1, e, ROUTER_ROWS), axis=0, keepdims=True)
        rest = jnp.where(e == i1, NEG, lt)
        v2 = jnp.max(rest, axis=0, keepdims=True)
        i2 = jnp.min(jnp.where(rest == v2, e, ROUTER_ROWS), axis=0, keepdims=True)
        e2 = jnp.exp(v2 - v1)
        w1 = 1.0 / (1.0 + e2)
        w2 = e2 / (1.0 + e2)
        eid_ref[0] = jnp.where(e == 0, i1, jnp.where(e == 1, i2, 0))[:8]
        wrow = jnp.where(e == 0, w1, jnp.where(e == 1, w2, 0.0))
        gw_ref[0] = jnp.concatenate([wrow, jnp.zeros((LANES - ROUTER_ROWS, tm), F32)], axis=0).T


def _outproj(ya, yb, yn, of, ob, P, xa, modv, w_out_b, g_post, g_pre2, rows, seq, tm, router=None, n_exp=0):
    B, _, D = xa.shape
    with_router = router is not None
    kern = functools.partial(_outproj_kernel, tm=tm, seq=seq, n_exp=n_exp)
    tok = lambda w: pl.BlockSpec((1, tm, w), lambda b, i: (b, i, 0))
    in_specs = [tok(GROUP_W), tok(GROUP_W), tok(GROUP_W), tok(GROUP_W), tok(GROUP_W),
                pl.BlockSpec((1, tm, GROUP_W), lambda b, i: (b, i, CB_RET_G)),
                tok(D),
                pl.BlockSpec((1, 2, 8, D), lambda b, i: (b, 0, 0, 0)),
                pl.BlockSpec((D, D), lambda b, i: (0, 0)),
                pl.BlockSpec((1, D), lambda b, i: (0, 0)),
                pl.BlockSpec((1, D), lambda b, i: (0, 0))]
    args = [ya, yb, yn, of, ob, P, xa, modv, w_out_b, g_post.reshape(1, D), g_pre2.reshape(1, D)]
    out_specs = [tok(D), tok(D)]
    out_shape = [jax.ShapeDtypeStruct((B, rows, D), F32),
                 jax.ShapeDtypeStruct((B, rows, D), F32 if with_router else BF16)]
    if with_router:
        in_specs.append(pl.BlockSpec((ROUTER_ROWS, D), lambda b, i: (0, 0)))
        args.append(router)
        out_specs += [pl.BlockSpec((1, 8, tm), lambda b, i: (b, 0, i)), tok(LANES)]
        out_shape += [jax.ShapeDtypeStruct((B, 8, rows), jnp.int32),
                      jax.ShapeDtypeStruct((B, rows, LANES), F32)]
    return pl.pallas_call(
        kern,
        grid=(B, rows // tm),
        in_specs=in_specs,
        out_specs=out_specs,
        out_shape=out_shape,
        compiler_params=_params(("arbitrary", "arbitrary")),
        name="outproj_router" if with_router else "outproj",
    )(*args)


def _ffn_kernel(h_ref, x_ref, mod_ref, wg_ref, wu_ref, wd_ref, gpost_ref, o_ref, *, tm, seq, fc):
    i = pl.program_id(1)
    rows = i * tm + lax.broadcasted_iota(jnp.int32, (tm, 1), 0)
    is_ctx = rows >= seq
    h = h_ref[0]
    acc = jnp.zeros((tm, o_ref.shape[-1]), F32)
    for c in range(wg_ref.shape[1] // fc):
        g = jnp.dot(h, wg_ref[:, c * fc:(c + 1) * fc], preferred_element_type=F32)
        u = jnp.dot(h, wu_ref[:, c * fc:(c + 1) * fc], preferred_element_type=F32)
        a = (_silu(g) * u).astype(BF16)
        acc = acc + jnp.dot(a, wd_ref[c * fc:(c + 1) * fc, :], preferred_element_type=F32)
    yn = acc * lax.rsqrt(jnp.mean(acc * acc, axis=-1, keepdims=True) + EPS) * gpost_ref[...]
    o_ref[0] = x_ref[0] + _row_mod(mod_ref, is_ctx, 5) * yn


def _dense_ffn(h2, x1, modv, wg, wu, wd, g_post, seq, tm):
    B, T, D = x1.shape
    F = wg.shape[1]
    kern = functools.partial(_ffn_kernel, tm=tm, seq=seq, fc=256)
    const = lambda shape: pl.BlockSpec(shape, lambda b, i: (0, 0), pipeline_mode=pl.Buffered(1))
    return pl.pallas_call(
        kern,
        grid=(B, T // tm),
        in_specs=[pl.BlockSpec((1, tm, D), lambda b, i: (b, i, 0)),
                  pl.BlockSpec((1, tm, D), lambda b, i: (b, i, 0)),
                  pl.BlockSpec((1, 2, 8, D), lambda b, i: (b, 0, 0, 0)),
                  const((D, F)), const((D, F)), const((F, D)),
                  pl.BlockSpec((1, D), lambda b, i: (0, 0))],
        out_specs=pl.BlockSpec((1, tm, D), lambda b, i: (b, i, 0)),
        out_shape=jax.ShapeDtypeStruct((B, T, D), F32),
        compiler_params=_params(("arbitrary", "arbitrary")),
        name="dense_ffn",
    )(h2, x1, modv, wg, wu, wd, g_post.reshape(1, D))


def _row_copy(src_hbm, row, dst_vmem, r, sem):
    return pltpu.make_async_copy(src_hbm.at[pl.ds(row, 1)], dst_vmem.at[pl.ds(r, 1)], sem)


def _moe_ffn_kernel(te_ref, tv_ref, src0_ref, srcn_ref, h_hbm, wg_ref, wu_ref, wd_ref, o_ref,
                    xg_sc, xb_sc, acc_sc, sem, *, tr):
    t = pl.program_id(0)
    f = pl.program_id(1)
    nt = pl.num_programs(0)
    nf = pl.num_programs(1)
    slot = t % 2

    def start_gather(src_ref, s):
        def body(r, c):
            _row_copy(h_hbm, src_ref[0, 0, r], xg_sc.at[s], r, sem.at[s]).start()
            return c

        lax.fori_loop(0, tr, body, 0, unroll=8)

    @pl.when((f == 0) & (t == 0) & (tv_ref[0] > 0))
    def _():
        start_gather(src0_ref, 0)

    @pl.when(f == 0)
    def _():
        acc_sc[...] = jnp.zeros(acc_sc.shape, F32)

    @pl.when((f == 0) & (tv_ref[t] > 0))
    def _():
        pltpu.make_async_copy(h_hbm.at[pl.ds(0, tr)], xg_sc.at[slot], sem.at[slot]).wait()
        xb_sc[...] = xg_sc[slot].astype(BF16)

    @pl.when((f == 0) & (t + 1 < nt) & (tv_ref[jnp.minimum(t + 1, nt - 1)] > 0))
    def _():
        start_gather(srcn_ref, 1 - slot)

    @pl.when(tv_ref[t] > 0)
    def _():
        x = xb_sc[...]
        g = jnp.dot(x, wg_ref[0], preferred_element_type=F32)
        u = jnp.dot(x, wu_ref[0], preferred_element_type=F32)
        a = (_silu(g) * u).astype(BF16)
        acc_sc[...] += jnp.dot(a, wd_ref[0], preferred_element_type=F32)

    @pl.when(f == nf - 1)
    def _():
        o_ref[...] = acc_sc[...]


def _moe_ffn(h_flat, src, tile_expert, tile_valid, wg, wu, wd, n_rows, tr, tf):
    D = h_flat.shape[1]
    F = wg.shape[2]
    nt = n_rows // tr
    return pl.pallas_call(
        functools.partial(_moe_ffn_kernel, tr=tr),
        grid_spec=pltpu.PrefetchScalarGridSpec(
            num_scalar_prefetch=2,
            grid=(nt, F // tf),
            in_specs=[pl.BlockSpec((1, 1, tr), lambda t, f, te, tv: (0, 0, 0), memory_space=pltpu.SMEM),
                      pl.BlockSpec((1, 1, tr), lambda t, f, te, tv: (jnp.minimum(t + 1, nt - 1), 0, 0),
                                   memory_space=pltpu.SMEM),
                      pl.BlockSpec(memory_space=pl.ANY),
                      pl.BlockSpec((1, D, tf), lambda t, f, te, tv: (te[t], 0, f)),
                      pl.BlockSpec((1, D, tf), lambda t, f, te, tv: (te[t], 0, f)),
                      pl.BlockSpec((1, tf, D), lambda t, f, te, tv: (te[t], f, 0))],
            out_specs=pl.BlockSpec((tr, D), lambda t, f, te, tv: (t, 0)),
            scratch_shapes=[pltpu.VMEM((2, tr, D), h_flat.dtype), pltpu.VMEM((tr, D), BF16),
                            pltpu.VMEM((tr, D), F32), pltpu.SemaphoreType.DMA((2,))]),
        out_shape=jax.ShapeDtypeStruct((n_rows, D), F32),
        compiler_params=_params(("arbitrary", "arbitrary")),
        name="moe_ffn",
    )(tile_expert, tile_valid, src.reshape(nt, 1, tr), src.reshape(nt, 1, tr), h_flat, wg, wu, wd)


def _combine_kernel(pos_ref, ys_hbm, gw_ref, x_ref, mod_ref, gpost_ref, o_ref, y1_sc, y2_sc, sem, *, tm, seq):
    def issue(r, c):
        _row_copy(ys_hbm, pos_ref[0, 0, r], y1_sc, r, sem.at[0]).start()
        _row_copy(ys_hbm, pos_ref[0, 1, r], y2_sc, r, sem.at[1]).start()
        return c

    lax.fori_loop(0, tm, issue, 0, unroll=8)
    pltpu.make_async_copy(ys_hbm.at[pl.ds(0, tm)], y1_sc, sem.at[0]).wait()
    pltpu.make_async_copy(ys_hbm.at[pl.ds(0, tm)], y2_sc, sem.at[1]).wait()
    rows = pl.program_id(1) * tm + lax.broadcasted_iota(jnp.int32, (tm, 1), 0)
    gw = gw_ref[0]
    y = gw[:, 0:1] * y1_sc[...] + gw[:, 1:2] * y2_sc[...]
    yn = y * lax.rsqrt(jnp.mean(y * y, axis=-1, keepdims=True) + EPS) * gpost_ref[...]
    o_ref[0] = x_ref[0] + _row_mod(mod_ref, rows >= seq, 5) * yn


def _moe_combine(pos, ys, gw, x1, modv, g_post, seq, tm):
    B, R, D = x1.shape
    ntile = R // tm
    kern = functools.partial(_combine_kernel, tm=tm, seq=seq)
    return pl.pallas_call(
        kern,
        grid=(B, ntile),
        in_specs=[pl.BlockSpec((1, TOP_K, tm), lambda b, i: (b * ntile + i, 0, 0), memory_space=pltpu.SMEM),
                  pl.BlockSpec(memory_space=pl.ANY),
                  pl.BlockSpec((1, tm, LANES), lambda b, i: (b, i, 0)),
                  pl.BlockSpec((1, tm, D), lambda b, i: (b, i, 0)),
                  pl.BlockSpec((1, 2, 8, D), lambda b, i: (b, 0, 0, 0)),
                  pl.BlockSpec((1, D), lambda b, i: (0, 0))],
        out_specs=pl.BlockSpec((1, tm, D), lambda b, i: (b, i, 0)),
        out_shape=jax.ShapeDtypeStruct((B, R, D), F32),
        scratch_shapes=[pltpu.VMEM((tm, D), F32), pltpu.VMEM((tm, D), F32), pltpu.SemaphoreType.DMA((2,))],
        compiler_params=_params(("arbitrary", "arbitrary")),
        name="moe_combine",
    )(pos.reshape(B * ntile, tm, TOP_K).transpose(0, 2, 1), ys, gw, x1, modv, g_post.reshape(1, D))


def _moe_routing(eid, n_exp, tr):
    N = eid.shape[0]
    e_flat = eid.reshape(-1)
    onehot = (e_flat[:, None] == jnp.arange(n_exp)[None, :]).astype(jnp.int32)
    rank = jnp.sum((jnp.cumsum(onehot, axis=0) - 1) * onehot, axis=1)
    counts = jnp.sum(onehot, axis=0)
    padded = ((counts + tr - 1) // tr) * tr
    ends = jnp.cumsum(padded)
    starts = ends - padded
    dest = starts[e_flat] + rank
    n_rows = (TOP_K * N // tr + n_exp) * tr
    src = jnp.zeros((n_rows,), jnp.int32).at[dest].set(jnp.arange(TOP_K * N, dtype=jnp.int32) // TOP_K)
    tile_start = jnp.arange(n_rows // tr, dtype=jnp.int32) * tr
    tile_expert = jnp.minimum(jnp.sum((tile_start[:, None] >= ends[None, :]).astype(jnp.int32), axis=1), n_exp - 1)
    tile_valid = (tile_start < ends[-1]).astype(jnp.int32)
    pos = dest.reshape(N, TOP_K).astype(jnp.int32)
    return src, pos, tile_expert, tile_valid, n_rows


def _swa_perm():
    idx = []
    for g in range(2):
        for kvh in range(2):
            idx += [(kvh * 2 + g) * HEAD_DIM + d for d in range(HEAD_DIM)]
    return jnp.array(idx, jnp.int32)


def kernel(x, c, ctx, c_ctx, w_mod, b_mod, g_attn_pre, g_attn_post, g_ffn_pre, g_ffn_post, w_in, w_out,
           da_lambda_q1, da_lambda_k1, da_lambda_q2, da_lambda_k2, da_subln, swa_sink, na_rpb,
           ret_gamma_fwd, ret_gamma_bwd, ffn_w_gate, ffn_w_up, ffn_w_down,
           moe_router, moe_w_gate, moe_w_up, moe_w_down):
    B, S, D = x.shape
    CTX = ctx.shape[1]
    T = S + CTX
    L = w_mod.shape[0]
    n_exp = moe_router.shape[-1]
    rows = S // GRID_W

    def token_tile(n_rows):
        return next(t for t in TOKEN_TILES if n_rows % t == 0)

    c8 = jnp.zeros((8, D), F32).at[:B].set(c).at[B].set(c_ctx)
    mod = _modulation(c8, w_mod, b_mod).reshape(L, 8, 6, D)
    mod = jnp.pad(mod, ((0, 0), (0, 0), (0, 2), (0, 0)))
    modv = jnp.stack([mod[:, :B], jnp.broadcast_to(mod[:, B:B + 1], (L, B, 8, D))], axis=2)

    tabs = _rope_tables(S, CTX)
    perm = _swa_perm()
    swa0 = CB_SWA_Q * 256
    xa = jnp.concatenate([x, ctx], axis=1)

    for l in range(L):
        lambda_init = 0.8 - 0.6 * math.exp(-0.3 * l)
        last = l == L - 1
        w_in_l = w_in[l]
        w_in_b = jnp.concatenate([w_in_l[:, :swa0], w_in_l[:, swa0:swa0 + 256][:, perm], w_in_l[:, swa0 + 256:]],
                                 axis=1).astype(BF16)
        w_out_l = w_out[l]
        w_out_b = jnp.concatenate([w_out_l[:256], w_out_l[256:512][perm], w_out_l[512:]], axis=0).astype(BF16)

        P, QT, VT = _inproj(xa, modv[l], g_attn_pre[l], w_in_b, tabs, S, token_tile(T))

        lamp = jnp.zeros((8, LANES), F32)
        for r, v in enumerate((da_lambda_q1[l], da_lambda_k1[l], da_lambda_q2[l], da_lambda_k2[l])):
            lamp = lamp.at[r, :DA_QK].set(v)
        subln_full = jnp.tile(da_subln[l], N_HEADS).reshape(1, GROUP_W)
        tq_da = next(t for t in (512, 256) if S % t == 0)
        tk_da = next(t for t in (768, 512, 256) if T % t == 0)
        ya = _diff_attention(P, QT, VT, lamp, subln_full, lambda_init, 0, S // tq_da, tq_da, 0, T // tk_da, tk_da)
        if not last:
            ya_ctx = _diff_attention(P, QT, VT, lamp, subln_full, lambda_init, S // CTX, 1, CTX, S // CTX, 1, CTX)
            ya = jnp.concatenate([ya, ya_ctx], axis=1)
        yb = _window_attention(P, swa_sink[l].astype(F32), S, CTX)
        yn = _neighborhood_attention(P, _na_bias_tables(na_rpb[l], rows), S, CTX)
        lg = jnp.stack([jax.nn.log_sigmoid(ret_gamma_fwd[l].astype(F32)),
                        jax.nn.log_sigmoid(ret_gamma_bwd[l].astype(F32))])
        lgv = jnp.zeros((8, GROUP_W), F32).at[:2].set(jnp.repeat(lg, HEAD_DIM, axis=1))
        of, ob = _retention(P, lg.reshape(-1), lgv, S, CTX)

        R = S if last else T
        e = l // 2
        if l % 2 == 0:
            x1, h2 = _outproj(ya, yb, yn, of, ob, P, xa, modv[l], w_out_b, g_attn_post[l], g_ffn_pre[l],
                              R, S, token_tile(R))
            xa = _dense_ffn(h2, x1, modv[l], ffn_w_gate[e].astype(BF16), ffn_w_up[e].astype(BF16),
                            ffn_w_down[e].astype(BF16), g_ffn_post[l], S, token_tile(R))
        else:
            router = jnp.zeros((ROUTER_ROWS, D), F32).at[:n_exp].set(moe_router[e].T)
            x1, h2, eid, gw = _outproj(ya, yb, yn, of, ob, P, xa, modv[l], w_out_b, g_attn_post[l],
                                       g_ffn_pre[l], R, S, token_tile(R), router=router, n_exp=n_exp)
            tr = 512
            src, pos, tile_expert, tile_valid, n_rows = _moe_routing(
                jnp.swapaxes(eid[:, :TOP_K, :], 1, 2).reshape(B * R, TOP_K), n_exp, tr)
            ys = _moe_ffn(h2.reshape(B * R, D), src, tile_expert, tile_valid, moe_w_gate[e].astype(BF16),
                          moe_w_up[e].astype(BF16), moe_w_down[e].astype(BF16), n_rows, tr,
                          next(t for t in EXPERT_F_TILES if moe_w_gate.shape[-1] % t == 0))
            xa = _moe_combine(pos, ys, gw, x1, modv[l], g_ffn_post[l], S, 256)
    return xa[:, :S]
```

```python
import functools
import math

import jax
import jax.numpy as jnp
from jax import lax
from jax.experimental import pallas as pl
from jax.experimental.pallas import tpu as pltpu

F32 = jnp.float32
BF16 = jnp.bfloat16

GRID_W = 64
HEAD_DIM = 64
N_HEADS = 4
GROUP_W = N_HEADS * HEAD_DIM
DA_QK = 32
SWA_WINDOW = 128
NA_WIN_ROWS = 8
NA_WIN_COLS = 16
NA_TILE_ROWS = 4
RET_CHUNK = 128
ROPE_BASE = 10000.0
TOP_K = 2
EPS = 1e-6
NEG = -1e30
LOG2E = 1.4426950408889634
LANES = 128
ONES_ROWS = 16
ROUTER_ROWS = 16
TOKEN_TILES = (768, 512, 256)
EXPERT_F_TILES = (1792, 512, 256)
DA_MAX_JUMP = 64.0
DA_CHUNK = 128
VMEM_LIMIT = 56 * 1024 * 1024

CB_DA_Q, CB_DA_K, CB_DA_V = 0, 1, 2
CB_SWA_Q = 3
CB_SWA_K128, CB_SWA_V128 = 8, 9
CB_NA_Q, CB_NA_K, CB_NA_V = 5, 6, 7
CB_RET_Q, CB_RET_K, CB_RET_V, CB_RET_G = 8, 9, 10, 11
IN_WIDTH = 3072


def _params(sem):
    return pltpu.CompilerParams(dimension_semantics=sem, vmem_limit_bytes=VMEM_LIMIT)


def _silu(v):
    return v / (1.0 + jnp.exp(-v))


def _head_mask(shape, head_w, h, dim=1):
    return lax.broadcasted_iota(jnp.int32, shape, dim) // head_w == h


def _lane_pick(a, b):
    lane = lax.broadcasted_iota(jnp.int32, a.shape, 1)
    return jnp.where(lane < HEAD_DIM, a, b)


def _per_head_full(vals):
    return jnp.concatenate([_lane_pick(vals[0], vals[1]), _lane_pick(vals[2], vals[3])], axis=1)


def _group_mean_sq(a):
    sq = a * a
    r = lax.broadcasted_iota(jnp.int32, (GROUP_W, GROUP_W), 0) // HEAD_DIM
    c = lax.broadcasted_iota(jnp.int32, (GROUP_W, GROUP_W), 1) // HEAD_DIM
    g = jnp.where(r == c, 1.0, 0.0).astype(BF16)
    hi = sq.astype(BF16)
    lo = (sq - hi.astype(F32)).astype(BF16)
    tot = jnp.dot(hi, g, preferred_element_type=F32) + jnp.dot(lo, g, preferred_element_type=F32)
    return tot * (1.0 / HEAD_DIM)


def _row_mod(mod_ref, rows_are_ctx, j):
    return jnp.where(rows_are_ctx, mod_ref[0, 1, j:j + 1, :], mod_ref[0, 0, j:j + 1, :])


def _mod_kernel(c_ref, w_ref, b_ref, o_ref):
    s = _silu(c_ref[...])
    o_ref[0] = jnp.dot(s.astype(BF16), w_ref[0].astype(BF16), preferred_element_type=F32) + b_ref[0]


def _modulation(c8, w_mod, b_mod):
    L, D, W = w_mod.shape
    tn = 1536
    return pl.pallas_call(
        _mod_kernel,
        grid=(L, W // tn),
        in_specs=[pl.BlockSpec((8, D), lambda l, j: (0, 0)),
                  pl.BlockSpec((1, D, tn), lambda l, j: (l, 0, j)),
                  pl.BlockSpec((1, 1, tn), lambda l, j: (l, 0, j))],
        out_specs=pl.BlockSpec((1, 8, tn), lambda l, j: (l, 0, j)),
        out_shape=jax.ShapeDtypeStruct((L, 8, W), F32),
        compiler_params=_params(("arbitrary", "arbitrary")),
        name="modulation",
    )(c8, w_mod, b_mod.reshape(L, 1, W))


def _lane_block_ops():
    da_s = DA_QK ** -0.5 * LOG2E
    s64 = HEAD_DIM ** -0.5
    ops = []
    ops += [("r32", da_s)] * 2 + [("r32", 1.0)] * 2 + [(None, 1.0)] * 2
    ops += [("r64", s64 * LOG2E)] * 2 + [("r64", 1.0)] + [(None, 1.0)]
    ops += [(None, s64 * LOG2E)] * 2 + [(None, 1.0)] * 4
    ops += [("r64", 1.0)] * 2 + [("r64", s64)] * 2 + [(None, 1.0)] * 4
    return ops


def _rope(x, cos, sin_signed, w):
    lane = lax.broadcasted_iota(jnp.int32, x.shape, 1)
    first = (lane % (2 * w)) < w
    xr = jnp.where(first, pltpu.roll(x, LANES - w, 1), pltpu.roll(x, w, 1))
    return x * cos + xr * sin_signed


def _inproj_kernel(x_ref, mod_ref, g_ref, w_ref, c32_ref, s32_ref, c64_ref, s64_ref, o_ref, qt_ref, vt_ref,
                   *, tm, seq):
    i = pl.program_id(1)
    x = x_ref[0]
    rows = i * tm + lax.broadcasted_iota(jnp.int32, (tm, 1), 0)
    is_ctx = rows >= seq
    ms = jnp.mean(x * x, axis=-1, keepdims=True)
    h = x * lax.rsqrt(ms + EPS) * g_ref[...]
    h = h * (1.0 + _row_mod(mod_ref, is_ctx, 1)) + _row_mod(mod_ref, is_ctx, 0)
    hb = h.astype(BF16)
    ops = _lane_block_ops()
    for cb in range(IN_WIDTH // 256):
        p = jnp.dot(hb, w_ref[:, cb * 256:(cb + 1) * 256], preferred_element_type=F32)
        halves = []
        for hf in range(2):
            kind, scale = ops[cb * 2 + hf]
            ph = p[:, hf * LANES:(hf + 1) * LANES]
            if kind == "r32":
                ph = _rope(ph, c32_ref[...], s32_ref[...], DA_QK // 4)
            elif kind == "r64":
                ph = _rope(ph, c64_ref[...], s64_ref[...], HEAD_DIM // 4)
            if scale != 1.0:
                ph = ph * scale
            halves.append(ph)
        full = jnp.concatenate(halves, axis=1)
        o_ref[0, :, cb * 256:(cb + 1) * 256] = full.astype(BF16)
        if cb == CB_DA_Q:
            qt_ref[0] = full.T.astype(BF16)
        elif cb == CB_DA_V:
            vt_ref[0] = full.T.astype(BF16)


def _inproj(xa, modv, g, w_in_b, tabs, seq, tm):
    B, T, D = xa.shape
    kern = functools.partial(_inproj_kernel, tm=tm, seq=seq)
    tab_spec = pl.BlockSpec((tm, LANES), lambda b, i: (i, 0))
    return pl.pallas_call(
        kern,
        grid=(B, T // tm),
        in_specs=[pl.BlockSpec((1, tm, D), lambda b, i: (b, i, 0)),
                  pl.BlockSpec((1, 2, 8, D), lambda b, i: (b, 0, 0, 0)),
                  pl.BlockSpec((1, D), lambda b, i: (0, 0)),
                  pl.BlockSpec((D, IN_WIDTH), lambda b, i: (0, 0)),
                  tab_spec, tab_spec, tab_spec, tab_spec],
        out_specs=[pl.BlockSpec((1, tm, IN_WIDTH), lambda b, i: (b, i, 0)),
                   pl.BlockSpec((1, GROUP_W, tm), lambda b, i: (b, 0, i)),
                   pl.BlockSpec((1, GROUP_W, tm), lambda b, i: (b, 0, i))],
        out_shape=[jax.ShapeDtypeStruct((B, T, IN_WIDTH), BF16),
                   jax.ShapeDtypeStruct((B, GROUP_W, T), BF16),
                   jax.ShapeDtypeStruct((B, GROUP_W, T), BF16)],
        compiler_params=_params(("arbitrary", "arbitrary")),
        name="inproj",
    )(xa, modv, g.reshape(1, D), w_in_b, *tabs)


def _rope_tables(seq, ctx):
    t = jnp.arange(seq)
    row = (t // GRID_W).astype(F32)
    col = (t % GRID_W).astype(F32)
    out = []
    for dh in (DA_QK, HEAD_DIM):
        half = dh // 2
        quarter = half // 2
        lane = jnp.arange(LANES)
        d = lane % dh
        use_col = (d // half) == 1
        idx = (d % quarter).astype(F32)
        inv = ROPE_BASE ** (-idx * 2.0 / half)
        pos = jnp.where(use_col[None, :], col[:, None], row[:, None])
        ang = pos * inv[None, :]
        first = (d % half) < quarter
        cos = jnp.cos(ang)
        sin = jnp.where(first[None, :], -jnp.sin(ang), jnp.sin(ang))
        cos = jnp.concatenate([cos, jnp.ones((ctx, LANES), F32)], axis=0)
        sin = jnp.concatenate([sin, jnp.zeros((ctx, LANES), F32)], axis=0)
        out += [cos, sin]
    return out


def _rows_per_head(vals, tq):
    return jnp.concatenate([jnp.broadcast_to(v, (HEAD_DIM, tq)) for v in vals], axis=0)


def _da_kernel(lamp_ref, qt_ref, k_ref, vt_ref, g_ref, o_ref, qm_sc, m_sc, l_sc, acc_sc, st_sc, p_sc,
               jump_sc, *, tq, lambda_init):
    ki = pl.program_id(2)
    nk = pl.num_programs(2)

    @pl.when(ki == 0)
    def _():
        qt = qt_ref[0]
        for j in range(2 * N_HEADS):
            qm_sc[j] = jnp.where(_head_mask(qt.shape, DA_QK, j, 0), qt, jnp.zeros_like(qt))
        m_sc[...] = jnp.full(m_sc.shape, NEG, F32)
        l_sc[...] = jnp.zeros(l_sc.shape, F32)
        acc_sc[...] = jnp.zeros(acc_sc.shape, F32)

    k = k_ref[0]
    vt = vt_ref[0]
    ones = jnp.ones((ONES_ROWS, vt.shape[1]), BF16)

    tk = k.shape[0]

    def vt_ext(h):
        return jnp.concatenate([vt[h * HEAD_DIM:(h + 1) * HEAD_DIM, :], ones], axis=0)

    @pl.when(ki == 0)
    def _():
        jump_sc[0] = 2.0 * DA_MAX_JUMP

    @pl.when(ki > 0)
    def _():
        jump = None
        pvs, mxs = [], []
        for j in range(2 * N_HEADS):
            st = jnp.dot(k, qm_sc[j], preferred_element_type=F32)
            m_prev = m_sc[j]
            mx = jnp.max(st, axis=0, keepdims=True)
            p = jnp.exp2(st - m_prev).astype(BF16)
            pvs.append(jnp.dot(vt_ext(j // 2), p, preferred_element_type=F32))
            mxs.append(mx)
            worst = jnp.max(mx - m_prev)
            jump = worst if jump is None else jnp.maximum(jump, worst)
        jump_sc[0] = jump
        keep = jump <= DA_MAX_JUMP
        for j in range(2 * N_HEADS):
            h, t = j // 2, j % 2
            rows = slice(h * HEAD_DIM, (h + 1) * HEAD_DIM)
            m_prev = m_sc[j]
            m_new = jnp.maximum(m_prev, mxs[j])
            alpha = jnp.exp2(m_prev - m_new)
            l_old = l_sc[j]
            acc_old = acc_sc[t, rows, :]
            l_sc[j] = jnp.where(keep, alpha * (l_old + pvs[j][HEAD_DIM:HEAD_DIM + 1, :]), l_old)
            acc_sc[t, rows, :] = jnp.where(keep, alpha * (acc_old + pvs[j][:HEAD_DIM, :]), acc_old)
            m_sc[j] = jnp.where(keep, m_new, m_prev)

    redo = jump_sc[0] > DA_MAX_JUMP

    @pl.when(redo)
    def _():
        for j in range(2 * N_HEADS):
            h, t = j // 2, j % 2
            rows = slice(h * HEAD_DIM, (h + 1) * HEAD_DIM)
            st_sc[...] = jnp.dot(k, qm_sc[j], preferred_element_type=F32)
            run = st_sc[0:DA_CHUNK, :]
            for c in range(1, tk // DA_CHUNK):
                run = jnp.maximum(run, st_sc[c * DA_CHUNK:(c + 1) * DA_CHUNK, :])
            m_prev = m_sc[j]
            m_new = jnp.maximum(m_prev, jnp.max(run, axis=0, keepdims=True))
            alpha = jnp.exp2(m_prev - m_new)
            for c in range(tk // DA_CHUNK):
                cr = slice(c * DA_CHUNK, (c + 1) * DA_CHUNK)
                p_sc[cr, :] = jnp.exp2(st_sc[cr, :] - m_new).astype(BF16)
            pv = jnp.dot(vt_ext(h), p_sc[...], preferred_element_type=F32)
            l_sc[j] = alpha * l_sc[j] + pv[HEAD_DIM:HEAD_DIM + 1, :]
            m_sc[j] = m_new
            acc_sc[t, rows, :] = alpha * acc_sc[t, rows, :] + pv[:HEAD_DIM, :]

    @pl.when(ki == nk - 1)
    def _():
        lp = lamp_ref[...]
        lam = (jnp.exp(jnp.sum(lp[0:1] * lp[1:2], axis=1, keepdims=True))
               - jnp.exp(jnp.sum(lp[2:3] * lp[3:4], axis=1, keepdims=True))) + lambda_init
        o0 = acc_sc[0] / _rows_per_head([l_sc[2 * h] for h in range(N_HEADS)], tq)
        o1 = acc_sc[1] / _rows_per_head([l_sc[2 * h + 1] for h in range(N_HEADS)], tq)
        a = (o0 - lam * o1).T
        y = a * lax.rsqrt(_group_mean_sq(a) + EPS) * g_ref[...]
        o_ref[0] = (y * (1.0 - lambda_init)).astype(BF16)


def _diff_attention(P, QT, VT, lamp, subln_full, lambda_init, q0, nq, tq, k0, nk, tk):
    B = P.shape[0]
    kern = functools.partial(_da_kernel, tq=tq, lambda_init=lambda_init)
    return pl.pallas_call(
        kern,
        grid=(B, nq, nk),
        in_specs=[pl.BlockSpec((8, LANES), lambda b, qi, ki: (0, 0)),
                  pl.BlockSpec((1, GROUP_W, tq), lambda b, qi, ki: (b, 0, q0 + qi)),
                  pl.BlockSpec((1, tk, GROUP_W), lambda b, qi, ki: (b, k0 + ki, CB_DA_K)),
                  pl.BlockSpec((1, GROUP_W, tk), lambda b, qi, ki: (b, 0, k0 + ki)),
                  pl.BlockSpec((1, GROUP_W), lambda b, qi, ki: (0, 0))],
        out_specs=pl.BlockSpec((1, tq, GROUP_W), lambda b, qi, ki: (b, qi, 0)),
        out_shape=jax.ShapeDtypeStruct((B, nq * tq, GROUP_W), BF16),
        scratch_shapes=[pltpu.VMEM((2 * N_HEADS, GROUP_W, tq), BF16),
                        pltpu.VMEM((2 * N_HEADS, 1, tq), F32),
                        pltpu.VMEM((2 * N_HEADS, 1, tq), F32),
                        pltpu.VMEM((2, GROUP_W, tq), F32),
                        pltpu.VMEM((tk, tq), F32),
                        pltpu.VMEM((tk, tq), BF16),
                        pltpu.SMEM((1,), F32)],
        compiler_params=_params(("arbitrary", "arbitrary", "arbitrary")),
        name="diff_attention",
    )(lamp, QT, P, VT, subln_full)


def _swa_kernel(sink_ref, q_ref, kp_ref, kc_ref, kn_ref, kx_ref, vp_ref, vc_ref, vn_ref, vx_ref, o_ref,
                *, seq, blk):
    i = pl.program_id(0)
    nloc = 3 * blk
    nkeys = nloc + kx_ref.shape[1]
    qpos = i * blk + lax.broadcasted_iota(jnp.int32, (blk, nkeys), 0)
    c = lax.broadcasted_iota(jnp.int32, (blk, nkeys), 1)
    kpos = (i - 1) * blk + c
    valid = ((kpos >= 0) & (kpos < seq) & (jnp.abs(qpos - kpos) <= SWA_WINDOW) & (qpos < seq)) | (c >= nloc)
    for b in range(q_ref.shape[0]):
        q = q_ref[b]
        kall = jnp.concatenate([kp_ref[b], kc_ref[b], kn_ref[b], kx_ref[b]], axis=0)
        vall = jnp.concatenate([vp_ref[b], vc_ref[b], vn_ref[b], vx_ref[b]], axis=0)
        qs = []
        for g in range(2):
            qg = q[:, g * LANES:(g + 1) * LANES]
            for kvh in range(2):
                qs.append(jnp.where(_head_mask(qg.shape, HEAD_DIM, kvh), qg, jnp.zeros_like(qg)))
        s = lax.dot_general(jnp.concatenate(qs, axis=0), kall, (((1,), (1,)), ((), ())),
                            preferred_element_type=F32)
        vms = [jnp.where(_head_mask(vall.shape, HEAD_DIM, kvh), vall, jnp.zeros_like(vall)) for kvh in range(2)]
        vm = jnp.concatenate(vms, axis=0)
        for g in range(2):
            ps = []
            for kvh in range(2):
                j = 2 * g + kvh
                sink = sink_ref[2 * kvh + g] * LOG2E
                sj = jnp.where(valid, s[j * blk:(j + 1) * blk], NEG)
                m = jnp.maximum(jnp.max(sj, axis=1, keepdims=True), sink)
                e = jnp.exp2(sj - m)
                den = jnp.sum(e, axis=1, keepdims=True) + jnp.exp2(sink - m)
                ps.append((e / den).astype(BF16))
            o_ref[b, :, g * LANES:(g + 1) * LANES] = jnp.dot(
                jnp.concatenate(ps, axis=1), vm, preferred_element_type=F32).astype(BF16)


def _window_attention(P, sink, seq, ctx):
    B, T, _ = P.shape
    blk = SWA_WINDOW
    nt = T // blk
    kern = functools.partial(_swa_kernel, seq=seq, blk=blk)

    def nb(cb, off):
        return pl.BlockSpec((B, blk, LANES), lambda i: (0, jnp.clip(i + off, 0, nt - 1), cb))

    def cx(cb):
        return pl.BlockSpec((B, ctx, LANES), lambda i: (0, seq // ctx, cb))

    return pl.pallas_call(
        kern,
        grid=(nt,),
        in_specs=[pl.BlockSpec(memory_space=pltpu.SMEM),
                  pl.BlockSpec((B, blk, GROUP_W), lambda i: (0, i, CB_SWA_Q)),
                  nb(CB_SWA_K128, -1), nb(CB_SWA_K128, 0), nb(CB_SWA_K128, 1), cx(CB_SWA_K128),
                  nb(CB_SWA_V128, -1), nb(CB_SWA_V128, 0), nb(CB_SWA_V128, 1), cx(CB_SWA_V128)],
        out_specs=pl.BlockSpec((B, blk, GROUP_W), lambda i: (0, i, 0)),
        out_shape=jax.ShapeDtypeStruct((B, T, GROUP_W), BF16),
        compiler_params=_params(("arbitrary",)),
        name="window_attention",
    )(sink, P, P, P, P, P, P, P, P, P)


def _na_kernel(q_ref, k0_ref, k1_ref, k2_ref, kx_ref, v0_ref, v1_ref, v2_ref, vx_ref, bias_ref, o_ref,
               *, tq, n_seq_tiles):
    ctx_tile = pl.program_id(0) >= n_seq_tiles
    nloc = 3 * tq
    for b in range(q_ref.shape[0]):
        q = q_ref[b]
        kall = jnp.concatenate([k0_ref[b], k1_ref[b], k2_ref[b], kx_ref[b]], axis=0)
        vall = jnp.concatenate([v0_ref[b], v1_ref[b], v2_ref[b], vx_ref[b]], axis=0)
        qs = jnp.concatenate(
            [jnp.where(_head_mask(q.shape, HEAD_DIM, h), q, jnp.zeros_like(q)) for h in range(N_HEADS)], axis=0)
        s = lax.dot_general(qs, kall, (((1,), (1,)), ((), ())), preferred_element_type=F32)
        ps = []
        for h in range(N_HEADS):
            sh = s[h * tq:(h + 1) * tq]
            s_loc = jnp.where(ctx_tile, NEG, sh[:, :nloc] + bias_ref[0, h])
            s_ctx = sh[:, nloc:]
            m = jnp.maximum(jnp.max(s_loc, axis=1, keepdims=True), jnp.max(s_ctx, axis=1, keepdims=True))
            e_loc = jnp.exp2(s_loc - m)
            e_ctx = jnp.exp2(s_ctx - m)
            den = jnp.sum(e_loc, axis=1, keepdims=True) + jnp.sum(e_ctx, axis=1, keepdims=True)
            ps.append(jnp.concatenate([e_loc / den, e_ctx / den], axis=1).astype(BF16))
        vm = jnp.concatenate(
            [jnp.where(_head_mask(vall.shape, HEAD_DIM, h), vall, jnp.zeros_like(vall)) for h in range(N_HEADS)],
            axis=0)
        o_ref[b] = jnp.dot(jnp.concatenate(ps, axis=1), vm, preferred_element_type=F32).astype(BF16)


def _na_bias_kernel(rt_ref, o_ref, sv_sc, *, rows):
    tr = NA_TILE_ROWS
    nrt = rows // tr
    nl = 3 * tr * GRID_W
    n_dcol = 2 * NA_WIN_COLS - 1
    rt = rt_ref[0] * LOG2E
    hi = rt.astype(BF16)
    r1 = rt - hi.astype(F32)
    mid = r1.astype(BF16)
    lo = (r1 - mid.astype(F32)).astype(BF16)
    lane = lax.broadcasted_iota(jnp.int32, (GRID_W, nl), 1)
    qc = lax.broadcasted_iota(jnp.int32, (GRID_W, nl), 0)
    kc = lane % GRID_W
    kr = lane // GRID_W
    dc = jnp.clip(kc - qc, 1 - NA_WIN_COLS, NA_WIN_COLS - 1) + (NA_WIN_COLS - 1)
    c0 = jnp.clip(qc - NA_WIN_COLS // 2, 0, GRID_W - NA_WIN_COLS)
    col_ok = (kc >= c0) & (kc < c0 + NA_WIN_COLS)
    irow = lax.broadcasted_iota(jnp.int32, (LANES, nl), 0)
    krl = lax.broadcasted_iota(jnp.int32, (LANES, nl), 1) // GRID_W
    for v, (t_idx, ws) in enumerate(((0, 0), (1, 0), (nrt - 1, nrt - 3))):
        for qr in range(tr):
            r = t_idx * tr + qr
            r0 = min(max(r - NA_WIN_ROWS // 2, 0), rows - NA_WIN_ROWS)
            d_row = jnp.clip(ws * tr + krl - r + (NA_WIN_ROWS - 1), 0, 2 * NA_WIN_ROWS - 2)
            onehot = jnp.where(irow == d_row, 1.0, 0.0).astype(BF16)
            sv_sc[...] = (jnp.dot(hi, onehot, preferred_element_type=F32)
                          + jnp.dot(mid, onehot, preferred_element_type=F32)
                          + jnp.dot(lo, onehot, preferred_element_type=F32))

            def pick(j, acc):
                return jnp.where(dc == j, sv_sc[pl.ds(j, 1), :], acc)

            acc = lax.fori_loop(0, n_dcol, pick, jnp.zeros((GRID_W, nl), F32), unroll=True)
            krow = ws * tr + kr
            ok = col_ok & (krow >= r0) & (krow < r0 + NA_WIN_ROWS)
            o_ref[v, 0, qr * GRID_W:(qr + 1) * GRID_W, :] = jnp.where(ok, acc, NEG)


def _na_bias_tables(rpb, rows):
    H, nr, ncol = rpb.shape
    rt = jnp.zeros((H, 32, LANES), F32).at[:, :ncol, :nr].set(jnp.swapaxes(rpb.astype(F32), 1, 2))
    tq = NA_TILE_ROWS * GRID_W
    return pl.pallas_call(
        functools.partial(_na_bias_kernel, rows=rows),
        grid=(H,),
        in_specs=[pl.BlockSpec((1, 32, LANES), lambda h: (h, 0, 0))],
        out_specs=pl.BlockSpec((3, 1, tq, 3 * tq), lambda h: (0, h, 0, 0)),
        out_shape=jax.ShapeDtypeStruct((3, H, tq, 3 * tq), F32),
        scratch_shapes=[pltpu.VMEM((32, 3 * tq), F32)],
        compiler_params=_params(("arbitrary",)),
        name="na_bias",
    )(rt)


def _neighborhood_attention(P, bias_tabs, seq, ctx):
    B, T, _ = P.shape
    tq = NA_TILE_ROWS * GRID_W
    nst = seq // tq
    nt = T // tq
    kern = functools.partial(_na_kernel, tq=tq, n_seq_tiles=nst)

    def win(i):
        return jnp.clip(i - 1, 0, nst - 3)

    def loc(cb, off):
        return pl.BlockSpec((B, tq, GROUP_W), lambda i: (0, win(i) + off, cb))

    def cx(cb):
        return pl.BlockSpec((B, ctx, GROUP_W), lambda i: (0, seq // ctx, cb))

    def variant(i):
        return (jnp.where(i >= nst, 0, i - win(i)), 0, 0, 0)

    return pl.pallas_call(
        kern,
        grid=(nt,),
        in_specs=[pl.BlockSpec((B, tq, GROUP_W), lambda i: (0, i, CB_NA_Q)),
                  loc(CB_NA_K, 0), loc(CB_NA_K, 1), loc(CB_NA_K, 2), cx(CB_NA_K),
                  loc(CB_NA_V, 0), loc(CB_NA_V, 1), loc(CB_NA_V, 2), cx(CB_NA_V),
                  pl.BlockSpec((1, N_HEADS, tq, 3 * tq), variant)],
        out_specs=pl.BlockSpec((B, tq, GROUP_W), lambda i: (0, i, 0)),
        out_shape=jax.ShapeDtypeStruct((B, T, GROUP_W), BF16),
        compiler_params=_params(("arbitrary",)),
        name="neighborhood_attention",
    )(P, P, P, P, P, P, P, P, P, bias_tabs)


def _ret_direction(q, k, v, state_ref, lgs_ref, lgv, forward, base):
    C = q.shape[0]
    i = lax.broadcasted_iota(jnp.int32, (C, C), 0)
    j = lax.broadcasted_iota(jnp.int32, (C, C), 1)
    dist = (i - j if forward else j - i).astype(F32)
    keep = dist >= 0 if forward else dist > 0
    dist = jnp.where(keep, dist, 0.0)
    qs = jnp.concatenate(
        [jnp.where(_head_mask(q.shape, HEAD_DIM, h), q, jnp.zeros_like(q)) for h in range(N_HEADS)], axis=0)
    s = lax.dot_general(qs, k, (((1,), (1,)), ((), ())), preferred_element_type=F32)
    atts = []
    for h in range(N_HEADS):
        decay = jnp.where(keep, jnp.exp(dist * lgs_ref[base + h]), 0.0)
        atts.append((s[h * C:(h + 1) * C] * decay).astype(BF16))
    vm = jnp.concatenate(
        [jnp.where(_head_mask(v.shape, HEAD_DIM, h), v, jnp.zeros_like(v)) for h in range(N_HEADS)], axis=0)
    intra = jnp.dot(jnp.concatenate(atts, axis=1), vm, preferred_element_type=F32)
    r = lax.broadcasted_iota(jnp.int32, (C, 1), 0).astype(F32)
    xi = jnp.exp((r + 1.0 if forward else C - r) * lgv)
    zeta = jnp.exp((C - 1.0 - r if forward else r) * lgv)
    state = state_ref[...]
    cross = jnp.dot((q.astype(F32) * xi).astype(BF16), state.astype(BF16), preferred_element_type=F32)
    kz_t = (k.astype(F32) * zeta).T.astype(BF16)
    u = jnp.dot(kz_t, v, preferred_element_type=F32)
    rr = lax.broadcasted_iota(jnp.int32, u.shape, 0) // HEAD_DIM
    cc = lax.broadcasted_iota(jnp.int32, u.shape, 1) // HEAD_DIM
    state_ref[...] = jnp.where(rr == cc, jnp.exp(C * lgv) * state + u, 0.0)
    return intra + cross


def _ret_kernel(lgs_ref, lgv_ref, qf_ref, kf_ref, vf_ref, qb_ref, kb_ref, vb_ref, of_ref, ob_ref, sf_sc, sb_sc):
    @pl.when(pl.program_id(0) == 0)
    def _():
        sf_sc[...] = jnp.zeros(sf_sc.shape, F32)
        sb_sc[...] = jnp.zeros(sb_sc.shape, F32)

    for b in range(qf_ref.shape[0]):
        of_ref[b] = _ret_direction(qf_ref[b], kf_ref[b], vf_ref[b], sf_sc.at[b], lgs_ref, lgv_ref[0:1], True, 0)
        ob_ref[b] = _ret_direction(qb_ref[b], kb_ref[b], vb_ref[b], sb_sc.at[b], lgs_ref, lgv_ref[1:2], False,
                                   N_HEADS)


def _retention(P, lgs, lgv, seq, ctx):
    B, T, _ = P.shape
    C = RET_CHUNK
    ns, nc = seq // C, ctx // C
    nt = ns + nc

    def fwd(n):
        return jnp.where(n < nc, ns + n, n - nc)

    def bwd(n):
        return jnp.where(n < nc, nt - 1 - n, nt - 1 - n)

    def spec(order, cb):
        return pl.BlockSpec((B, C, GROUP_W), lambda n: (0, order(n), cb))

    return pl.pallas_call(
        _ret_kernel,
        grid=(nt,),
        in_specs=[pl.BlockSpec(memory_space=pltpu.SMEM),
                  pl.BlockSpec((8, GROUP_W), lambda n: (0, 0)),
                  spec(fwd, CB_RET_Q), spec(fwd, CB_RET_K), spec(fwd, CB_RET_V),
                  spec(bwd, CB_RET_Q), spec(bwd, CB_RET_K), spec(bwd, CB_RET_V)],
        out_specs=[pl.BlockSpec((B, C, GROUP_W), lambda n: (0, fwd(n), 0)),
                   pl.BlockSpec((B, C, GROUP_W), lambda n: (0, bwd(n), 0))],
        out_shape=[jax.ShapeDtypeStruct((B, T, GROUP_W), F32)] * 2,
        scratch_shapes=[pltpu.VMEM((B, GROUP_W, GROUP_W), F32)] * 2,
        compiler_params=_params(("arbitrary",)),
        name="retention",
    )(lgs, lgv, P, P, P, P, P, P)


def _outproj_kernel(*refs, tm, seq, n_exp):
    with_router = n_exp > 0
    if with_router:
        (ya_ref, yb_ref, yn_ref, of_ref, ob_ref, gt_ref, x_ref, mod_ref, w_ref, gpost_ref, gpre_ref, r_ref,
         x1_ref, h2_ref, eid_ref, gw_ref) = refs
    else:
        (ya_ref, yb_ref, yn_ref, of_ref, ob_ref, gt_ref, x_ref, mod_ref, w_ref, gpost_ref, gpre_ref,
         x1_ref, h2_ref) = refs
    i = pl.program_id(1)
    rows = i * tm + lax.broadcasted_iota(jnp.int32, (tm, 1), 0)
    is_ctx = rows >= seq
    o = of_ref[0] + ob_ref[0]
    yr = o * lax.rsqrt(_group_mean_sq(o) + EPS) * _silu(gt_ref[0].astype(F32))
    y = (jnp.dot(ya_ref[0], w_ref[0:256, :], preferred_element_type=F32)
         + jnp.dot(yb_ref[0], w_ref[256:512, :], preferred_element_type=F32)
         + jnp.dot(yn_ref[0], w_ref[512:768, :], preferred_element_type=F32)
         + jnp.dot(yr.astype(BF16), w_ref[768:1024, :], preferred_element_type=F32))
    yn = y * lax.rsqrt(jnp.mean(y * y, axis=-1, keepdims=True) + EPS) * gpost_ref[...]
    x1 = x_ref[0] + _row_mod(mod_ref, is_ctx, 2) * yn
    x1_ref[0] = x1
    h = x1 * lax.rsqrt(jnp.mean(x1 * x1, axis=-1, keepdims=True) + EPS) * gpre_ref[...]
    h = h * (1.0 + _row_mod(mod_ref, is_ctx, 4)) + _row_mod(mod_ref, is_ctx, 3)
    h2_ref[0] = h.astype(h2_ref.dtype)
    if with_router:
        r = r_ref[...]
        r_hi = r.astype(BF16)
        r_lo = (r - r_hi.astype(F32)).astype(BF16)
        h_hi = h.astype(BF16)
        h_lo = (h - h_hi.astype(F32)).astype(BF16)
        dn = (((1,), (1,)), ((), ()))
        lt = (lax.dot_general(r_hi, h_hi, dn, preferred_element_type=F32)
              + lax.dot_general(r_hi, h_lo, dn, preferred_element_type=F32)
              + lax.dot_general(r_lo, h_hi, dn, preferred_element_type=F32))
        e = lax.broadcasted_iota(jnp.int32, lt.shape, 0)
        lt = jnp.where(e < n_exp, lt, NEG)
        v1 = jnp.max(lt, axis=0, keepdims=True)
        i1 = jnp.min(jnp.where(lt == v1, e, ROUTER_ROWS), axis=0, keepdims=True)
        rest = jnp.where(e == i1, NEG, lt)
        v2 = jnp.max(rest, axis=0, keepdims=True)
        i2 = jnp.min(jnp.where(rest == v2, e, ROUTER_ROWS), axis=0, keepdims=True)
        e2 = jnp.exp(v2 - v1)
        w1 = 1.0 / (1.0 + e2)
        w2 = e2 / (1.0 + e2)
        eid_ref[0] = jnp.where(e == 0, i1, jnp.where(e == 1, i2, 0))[:8]
        wrow = jnp.where(e == 0, w1, jnp.where(e == 1, w2, 0.0))
        gw_ref[0] = jnp.concatenate([wrow, jnp.zeros((LANES - ROUTER_ROWS, tm), F32)], axis=0).T


def _outproj(ya, yb, yn, of, ob, P, xa, modv, w_out_b, g_post, g_pre2, rows, seq, tm, router=None, n_exp=0):
    B, _, D = xa.shape
    with_router = router is not None
    kern = functools.partial(_outproj_kernel, tm=tm, seq=seq, n_exp=n_exp)
    tok = lambda w: pl.BlockSpec((1, tm, w), lambda b, i: (b, i, 0))
    in_specs = [tok(GROUP_W), tok(GROUP_W), tok(GROUP_W), tok(GROUP_W), tok(GROUP_W),
                pl.BlockSpec((1, tm, GROUP_W), lambda b, i: (b, i, CB_RET_G)),
                tok(D),
                pl.BlockSpec((1, 2, 8, D), lambda b, i: (b, 0, 0, 0)),
                pl.BlockSpec((D, D), lambda b, i: (0, 0)),
                pl.BlockSpec((1, D), lambda b, i: (0, 0)),
                pl.BlockSpec((1, D), lambda b, i: (0, 0))]
    args = [ya, yb, yn, of, ob, P, xa, modv, w_out_b, g_post.reshape(1, D), g_pre2.reshape(1, D)]
    out_specs = [tok(D), tok(D)]
    out_shape = [jax.ShapeDtypeStruct((B, rows, D), F32),
                 jax.ShapeDtypeStruct((B, rows, D), F32 if with_router else BF16)]
    if with_router:
        in_specs.append(pl.BlockSpec((ROUTER_ROWS, D), lambda b, i: (0, 0)))
        args.append(router)
        out_specs += [pl.BlockSpec((1, 8, tm), lambda b, i: (b, 0, i)), tok(LANES)]
        out_shape += [jax.ShapeDtypeStruct((B, 8, rows), jnp.int32),
                      jax.ShapeDtypeStruct((B, rows, LANES), F32)]
    return pl.pallas_call(
        kern,
        grid=(B, rows // tm),
        in_specs=in_specs,
        out_specs=out_specs,
        out_shape=out_shape,
        compiler_params=_params(("arbitrary", "arbitrary")),
        name="outproj_router" if with_router else "outproj",
    )(*args)


def _ffn_kernel(h_ref, x_ref, mod_ref, wg_ref, wu_ref, wd_ref, gpost_ref, o_ref, *, tm, seq, fc):
    i = pl.program_id(1)
    rows = i * tm + lax.broadcasted_iota(jnp.int32, (tm, 1), 0)
    is_ctx = rows >= seq
    h = h_ref[0]
    acc = jnp.zeros((tm, o_ref.shape[-1]), F32)
    for c in range(wg_ref.shape[1] // fc):
        g = jnp.dot(h, wg_ref[:, c * fc:(c + 1) * fc], preferred_element_type=F32)
        u = jnp.dot(h, wu_ref[:, c * fc:(c + 1) * fc], preferred_element_type=F32)
        a = (_silu(g) * u).astype(BF16)
        acc = acc + jnp.dot(a, wd_ref[c * fc:(c + 1) * fc, :], preferred_element_type=F32)
    yn = acc * lax.rsqrt(jnp.mean(acc * acc, axis=-1, keepdims=True) + EPS) * gpost_ref[...]
    o_ref[0] = x_ref[0] + _row_mod(mod_ref, is_ctx, 5) * yn


def _dense_ffn(h2, x1, modv, wg, wu, wd, g_post, seq, tm):
    B, T, D = x1.shape
    F = wg.shape[1]
    kern = functools.partial(_ffn_kernel, tm=tm, seq=seq, fc=256)
    const = lambda shape: pl.BlockSpec(shape, lambda b, i: (0, 0), pipeline_mode=pl.Buffered(1))
    return pl.pallas_call(
        kern,
        grid=(B, T // tm),
        in_specs=[pl.BlockSpec((1, tm, D), lambda b, i: (b, i, 0)),
                  pl.BlockSpec((1, tm, D), lambda b, i: (b, i, 0)),
                  pl.BlockSpec((1, 2, 8, D), lambda b, i: (b, 0, 0, 0)),
                  const((D, F)), const((D, F)), const((F, D)),
                  pl.BlockSpec((1, D), lambda b, i: (0, 0))],
        out_specs=pl.BlockSpec((1, tm, D), lambda b, i: (b, i, 0)),
        out_shape=jax.ShapeDtypeStruct((B, T, D), F32),
        compiler_params=_params(("arbitrary", "arbitrary")),
        name="dense_ffn",
    )(h2, x1, modv, wg, wu, wd, g_post.reshape(1, D))


def _row_copy(src_hbm, row, dst_vmem, r, sem):
    return pltpu.make_async_copy(src_hbm.at[pl.ds(row, 1)], dst_vmem.at[pl.ds(r, 1)], sem)


def _moe_ffn_kernel(te_ref, tv_ref, src0_ref, srcn_ref, h_hbm, wg_ref, wu_ref, wd_ref, o_ref,
                    xg_sc, xb_sc, acc_sc, sem, *, tr):
    t = pl.program_id(0)
    f = pl.program_id(1)
    nt = pl.num_programs(0)
    nf = pl.num_programs(1)
    slot = t % 2

    def start_gather(src_ref, s):
        def body(r, c):
            _row_copy(h_hbm, src_ref[0, 0, r], xg_sc.at[s], r, sem.at[s]).start()
            return c

        lax.fori_loop(0, tr, body, 0, unroll=8)

    @pl.when((f == 0) & (t == 0) & (tv_ref[0] > 0))
    def _():
        start_gather(src0_ref, 0)

    @pl.when(f == 0)
    def _():
        acc_sc[...] = jnp.zeros(acc_sc.shape, F32)

    @pl.when((f == 0) & (tv_ref[t] > 0))
    def _():
        pltpu.make_async_copy(h_hbm.at[pl.ds(0, tr)], xg_sc.at[slot], sem.at[slot]).wait()
        xb_sc[...] = xg_sc[slot].astype(BF16)

    @pl.when((f == 0) & (t + 1 < nt) & (tv_ref[jnp.minimum(t + 1, nt - 1)] > 0))
    def _():
        start_gather(srcn_ref, 1 - slot)

    @pl.when(tv_ref[t] > 0)
    def _():
        x = xb_sc[...]
        g = jnp.dot(x, wg_ref[0], preferred_element_type=F32)
        u = jnp.dot(x, wu_ref[0], preferred_element_type=F32)
        a = (_silu(g) * u).astype(BF16)
        acc_sc[...] += jnp.dot(a, wd_ref[0], preferred_element_type=F32)

    @pl.when(f == nf - 1)
    def _():
        o_ref[...] = acc_sc[...]


def _moe_ffn(h_flat, src, tile_expert, tile_valid, wg, wu, wd, n_rows, tr, tf):
    D = h_flat.shape[1]
    F = wg.shape[2]
    nt = n_rows // tr
    return pl.pallas_call(
        functools.partial(_moe_ffn_kernel, tr=tr),
        grid_spec=pltpu.PrefetchScalarGridSpec(
            num_scalar_prefetch=2,
            grid=(nt, F // tf),
            in_specs=[pl.BlockSpec((1, 1, tr), lambda t, f, te, tv: (0, 0, 0), memory_space=pltpu.SMEM),
                      pl.BlockSpec((1, 1, tr), lambda t, f, te, tv: (jnp.minimum(t + 1, nt - 1), 0, 0),
                                   memory_space=pltpu.SMEM),
                      pl.BlockSpec(memory_space=pl.ANY),
                      pl.BlockSpec((1, D, tf), lambda t, f, te, tv: (te[t], 0, f)),
                      pl.BlockSpec((1, D, tf), lambda t, f, te, tv: (te[t], 0, f)),
                      pl.BlockSpec((1, tf, D), lambda t, f, te, tv: (te[t], f, 0))],
            out_specs=pl.BlockSpec((tr, D), lambda t, f, te, tv: (t, 0)),
            scratch_shapes=[pltpu.VMEM((2, tr, D), h_flat.dtype), pltpu.VMEM((tr, D), BF16),
                            pltpu.VMEM((tr, D), F32), pltpu.SemaphoreType.DMA((2,))]),
        out_shape=jax.ShapeDtypeStruct((n_rows, D), F32),
        compiler_params=_params(("arbitrary", "arbitrary")),
        name="moe_ffn",
    )(tile_expert, tile_valid, src.reshape(nt, 1, tr), src.reshape(nt, 1, tr), h_flat, wg, wu, wd)


def _combine_kernel(pos_ref, ys_hbm, gw_ref, x_ref, mod_ref, gpost_ref, o_ref, y1_sc, y2_sc, sem, *, tm, seq):
    def issue(r, c):
        _row_copy(ys_hbm, pos_ref[0, 0, r], y1_sc, r, sem.at[0]).start()
        _row_copy(ys_hbm, pos_ref[0, 1, r], y2_sc, r, sem.at[1]).start()
        return c

    lax.fori_loop(0, tm, issue, 0, unroll=8)
    pltpu.make_async_copy(ys_hbm.at[pl.ds(0, tm)], y1_sc, sem.at[0]).wait()
    pltpu.make_async_copy(ys_hbm.at[pl.ds(0, tm)], y2_sc, sem.at[1]).wait()
    rows = pl.program_id(1) * tm + lax.broadcasted_iota(jnp.int32, (tm, 1), 0)
    gw = gw_ref[0]
    y = gw[:, 0:1] * y1_sc[...] + gw[:, 1:2] * y2_sc[...]
    yn = y * lax.rsqrt(jnp.mean(y * y, axis=-1, keepdims=True) + EPS) * gpost_ref[...]
    o_ref[0] = x_ref[0] + _row_mod(mod_ref, rows >= seq, 5) * yn


def _moe_combine(pos, ys, gw, x1, modv, g_post, seq, tm):
    B, R, D = x1.shape
    ntile = R // tm
    kern = functools.partial(_combine_kernel, tm=tm, seq=seq)
    return pl.pallas_call(
        kern,
        grid=(B, ntile),
        in_specs=[pl.BlockSpec((1, TOP_K, tm), lambda b, i: (b * ntile + i, 0, 0), memory_space=pltpu.SMEM),
                  pl.BlockSpec(memory_space=pl.ANY),
                  pl.BlockSpec((1, tm, LANES), lambda b, i: (b, i, 0)),
                  pl.BlockSpec((1, tm, D), lambda b, i: (b, i, 0)),
                  pl.BlockSpec((1, 2, 8, D), lambda b, i: (b, 0, 0, 0)),
                  pl.BlockSpec((1, D), lambda b, i: (0, 0))],
        out_specs=pl.BlockSpec((1, tm, D), lambda b, i: (b, i, 0)),
        out_shape=jax.ShapeDtypeStruct((B, R, D), F32),
        scratch_shapes=[pltpu.VMEM((tm, D), F32), pltpu.VMEM((tm, D), F32), pltpu.SemaphoreType.DMA((2,))],
        compiler_params=_params(("arbitrary", "arbitrary")),
        name="moe_combine",
    )(pos.reshape(B * ntile, tm, TOP_K).transpose(0, 2, 1), ys, gw, x1, modv, g_post.reshape(1, D))


def _moe_routing(eid, n_exp, tr):
    N = eid.shape[0]
    e_flat = eid.reshape(-1)
    onehot = (e_flat[:, None] == jnp.arange(n_exp)[None, :]).astype(jnp.int32)
    rank = jnp.sum((jnp.cumsum(onehot, axis=0) - 1) * onehot, axis=1)
    counts = jnp.sum(onehot, axis=0)
    padded = ((counts + tr - 1) // tr) * tr
    ends = jnp.cumsum(padded)
    starts = ends - padded
    dest = starts[e_flat] + rank
    n_rows = (TOP_K * N // tr + n_exp) * tr
    src = jnp.zeros((n_rows,), jnp.int32).at[dest].set(jnp.arange(TOP_K * N, dtype=jnp.int32) // TOP_K)
    tile_start = jnp.arange(n_rows // tr, dtype=jnp.int32) * tr
    tile_expert = jnp.minimum(jnp.sum((tile_start[:, None] >= ends[None, :]).astype(jnp.int32), axis=1), n_exp - 1)
    tile_valid = (tile_start < ends[-1]).astype(jnp.int32)
    pos = dest.reshape(N, TOP_K).astype(jnp.int32)
    return src, pos, tile_expert, tile_valid, n_rows


def _swa_perm():
    idx = []
    for g in range(2):
        for kvh in range(2):
            idx += [(kvh * 2 + g) * HEAD_DIM + d for d in range(HEAD_DIM)]
    return jnp.array(idx, jnp.int32)


def kernel(x, c, ctx, c_ctx, w_mod, b_mod, g_attn_pre, g_attn_post, g_ffn_pre, g_ffn_post, w_in, w_out,
           da_lambda_q1, da_lambda_k1, da_lambda_q2, da_lambda_k2, da_subln, swa_sink, na_rpb,
           ret_gamma_fwd, ret_gamma_bwd, ffn_w_gate, ffn_w_up, ffn_w_down,
           moe_router, moe_w_gate, moe_w_up, moe_w_down):
    B, S, D = x.shape
    CTX = ctx.shape[1]
    T = S + CTX
    L = w_mod.shape[0]
    n_exp = moe_router.shape[-1]
    rows = S // GRID_W

    def token_tile(n_rows):
        return next(t for t in TOKEN_TILES if n_rows % t == 0)

    c8 = jnp.zeros((8, D), F32).at[:B].set(c).at[B].set(c_ctx)
    mod = _modulation(c8, w_mod, b_mod).reshape(L, 8, 6, D)
    mod = jnp.pad(mod, ((0, 0), (0, 0), (0, 2), (0, 0)))
    modv = jnp.stack([mod[:, :B], jnp.broadcast_to(mod[:, B:B + 1], (L, B, 8, D))], axis=2)

    tabs = _rope_tables(S, CTX)
    perm = _swa_perm()
    swa0 = CB_SWA_Q * 256
    xa = jnp.concatenate([x, ctx], axis=1)

    for l in range(L):
        lambda_init = 0.8 - 0.6 * math.exp(-0.3 * l)
        last = l == L - 1
        w_in_l = w_in[l]
        w_in_b = jnp.concatenate([w_in_l[:, :swa0], w_in_l[:, swa0:swa0 + 256][:, perm], w_in_l[:, swa0 + 256:]],
                                 axis=1).astype(BF16)
        w_out_l = w_out[l]
        w_out_b = jnp.concatenate([w_out_l[:256], w_out_l[256:512][perm], w_out_l[512:]], axis=0).astype(BF16)

        P, QT, VT = _inproj(xa, modv[l], g_attn_pre[l], w_in_b, tabs, S, token_tile(T))

        lamp = jnp.zeros((8, LANES), F32)
        for r, v in enumerate((da_lambda_q1[l], da_lambda_k1[l], da_lambda_q2[l], da_lambda_k2[l])):
            lamp = lamp.at[r, :DA_QK].set(v)
        subln_full = jnp.tile(da_subln[l], N_HEADS).reshape(1, GROUP_W)
        tq_da = next(t for t in (1024, 512, 256) if S % t == 0)
        tk_da = next(t for t in (768, 512, 256) if T % t == 0)
        ya = _diff_attention(P, QT, VT, lamp, subln_full, lambda_init, 0, S // tq_da, tq_da, 0, T // tk_da, tk_da)
        if not last:
            ya_ctx = _diff_attention(P, QT, VT, lamp, subln_full, lambda_init, S // CTX, 1, CTX, S // CTX, 1, CTX)
            ya = jnp.concatenate([ya, ya_ctx], axis=1)
        yb = _window_attention(P, swa_sink[l].astype(F32), S, CTX)
        yn = _neighborhood_attention(P, _na_bias_tables(na_rpb[l], rows), S, CTX)
        lg = jnp.stack([jax.nn.log_sigmoid(ret_gamma_fwd[l].astype(F32)),
                        jax.nn.log_sigmoid(ret_gamma_bwd[l].astype(F32))])
        lgv = jnp.zeros((8, GROUP_W), F32).at[:2].set(jnp.repeat(lg, HEAD_DIM, axis=1))
        of, ob = _retention(P, lg.reshape(-1), lgv, S, CTX)

        R = S if last else T
        e = l // 2
        if l % 2 == 0:
            x1, h2 = _outproj(ya, yb, yn, of, ob, P, xa, modv[l], w_out_b, g_attn_post[l], g_ffn_pre[l],
                              R, S, token_tile(R))
            xa = _dense_ffn(h2, x1, modv[l], ffn_w_gate[e].astype(BF16), ffn_w_up[e].astype(BF16),
                            ffn_w_down[e].astype(BF16), g_ffn_post[l], S, token_tile(R))
        else:
            router = jnp.zeros((ROUTER_ROWS, D), F32).at[:n_exp].set(moe_router[e].T)
            x1, h2, eid, gw = _outproj(ya, yb, yn, of, ob, P, xa, modv[l], w_out_b, g_attn_post[l],
                                       g_ffn_pre[l], R, S, token_tile(R), router=router, n_exp=n_exp)
            tr = 512
            src, pos, tile_expert, tile_valid, n_rows = _moe_routing(
                jnp.swapaxes(eid[:, :TOP_K, :], 1, 2).reshape(B * R, TOP_K), n_exp, tr)
            ys = _moe_ffn(h2.reshape(B * R, D), src, tile_expert, tile_valid, moe_w_gate[e].astype(BF16),
                          moe_w_up[e].astype(BF16), moe_w_down[e].astype(BF16), n_rows, tr,
                          next(t for t in EXPERT_F_TILES if moe_w_gate.shape[-1] % t == 0))
            xa = _moe_combine(pos, ys, gw, x1, modv[l], g_ffn_post[l], S, 256)
    return xa[:, :S]
```

```python
import functools
import math

import jax
import jax.numpy as jnp
from jax import lax
from jax.experimental import pallas as pl
from jax.experimental.pallas import tpu as pltpu

F32 = jnp.float32
BF16 = jnp.bfloat16

GRID_W = 64
HEAD_DIM = 64
N_HEADS = 4
GROUP_W = N_HEADS * HEAD_DIM
DA_QK = 32
SWA_WINDOW = 128
NA_WIN_ROWS = 8
NA_WIN_COLS = 16
NA_TILE_ROWS = 4
RET_CHUNK = 128
ROPE_BASE = 10000.0
TOP_K = 2
EPS = 1e-6
NEG = -1e30
LOG2E = 1.4426950408889634
LANES = 128
ONES_ROWS = 16
ROUTER_ROWS = 16
TOKEN_TILES = (768, 512, 256)
EXPERT_F_TILES = (1792, 512, 256)
DA_MAX_JUMP = 64.0
DA_CHUNK = 128
VMEM_LIMIT = 56 * 1024 * 1024

CB_DA_Q, CB_DA_K, CB_DA_V = 0, 1, 2
CB_SWA_Q = 3
CB_SWA_K128, CB_SWA_V128 = 8, 9
CB_NA_Q, CB_NA_K, CB_NA_V = 5, 6, 7
CB_RET_Q, CB_RET_K, CB_RET_V, CB_RET_G = 8, 9, 10, 11
IN_WIDTH = 3072


def _params(sem):
    return pltpu.CompilerParams(dimension_semantics=sem, vmem_limit_bytes=VMEM_LIMIT)


def _silu(v):
    return v / (1.0 + jnp.exp(-v))


def _head_mask(shape, head_w, h, dim=1):
    return lax.broadcasted_iota(jnp.int32, shape, dim) // head_w == h


def _lane_pick(a, b):
    lane = lax.broadcasted_iota(jnp.int32, a.shape, 1)
    return jnp.where(lane < HEAD_DIM, a, b)


def _per_head_full(vals):
    return jnp.concatenate([_lane_pick(vals[0], vals[1]), _lane_pick(vals[2], vals[3])], axis=1)


def _group_mean_sq(a):
    sq = a * a
    r = lax.broadcasted_iota(jnp.int32, (GROUP_W, GROUP_W), 0) // HEAD_DIM
    c = lax.broadcasted_iota(jnp.int32, (GROUP_W, GROUP_W), 1) // HEAD_DIM
    g = jnp.where(r == c, 1.0, 0.0).astype(BF16)
    hi = sq.astype(BF16)
    lo = (sq - hi.astype(F32)).astype(BF16)
    tot = jnp.dot(hi, g, preferred_element_type=F32) + jnp.dot(lo, g, preferred_element_type=F32)
    return tot * (1.0 / HEAD_DIM)


def _row_mod(mod_ref, rows_are_ctx, j):
    return jnp.where(rows_are_ctx, mod_ref[0, 1, j:j + 1, :], mod_ref[0, 0, j:j + 1, :])


def _mod_kernel(c_ref, w_ref, b_ref, o_ref):
    s = _silu(c_ref[...])
    o_ref[0] = jnp.dot(s.astype(BF16), w_ref[0].astype(BF16), preferred_element_type=F32) + b_ref[0]


def _modulation(c8, w_mod, b_mod):
    L, D, W = w_mod.shape
    tn = 1536
    return pl.pallas_call(
        _mod_kernel,
        grid=(L, W // tn),
        in_specs=[pl.BlockSpec((8, D), lambda l, j: (0, 0)),
                  pl.BlockSpec((1, D, tn), lambda l, j: (l, 0, j)),
                  pl.BlockSpec((1, 1, tn), lambda l, j: (l, 0, j))],
        out_specs=pl.BlockSpec((1, 8, tn), lambda l, j: (l, 0, j)),
        out_shape=jax.ShapeDtypeStruct((L, 8, W), F32),
        compiler_params=_params(("arbitrary", "arbitrary")),
        name="modulation",
    )(c8, w_mod, b_mod.reshape(L, 1, W))


def _lane_block_ops():
    da_s = DA_QK ** -0.5 * LOG2E
    s64 = HEAD_DIM ** -0.5
    ops = []
    ops += [("r32", da_s)] * 2 + [("r32", 1.0)] * 2 + [(None, 1.0)] * 2
    ops += [("r64", s64 * LOG2E)] * 2 + [("r64", 1.0)] + [(None, 1.0)]
    ops += [(None, s64 * LOG2E)] * 2 + [(None, 1.0)] * 4
    ops += [("r64", 1.0)] * 2 + [("r64", s64)] * 2 + [(None, 1.0)] * 4
    return ops


def _rope(x, cos, sin_signed, w):
    lane = lax.broadcasted_iota(jnp.int32, x.shape, 1)
    first = (lane % (2 * w)) < w
    xr = jnp.where(first, pltpu.roll(x, LANES - w, 1), pltpu.roll(x, w, 1))
    return x * cos + xr * sin_signed


def _inproj_kernel(x_ref, mod_ref, g_ref, w_ref, c32_ref, s32_ref, c64_ref, s64_ref, o_ref, qt_ref, vt_ref,
                   *, tm, seq):
    i = pl.program_id(1)
    x = x_ref[0]
    rows = i * tm + lax.broadcasted_iota(jnp.int32, (tm, 1), 0)
    is_ctx = rows >= seq
    ms = jnp.mean(x * x, axis=-1, keepdims=True)
    h = x * lax.rsqrt(ms + EPS) * g_ref[...]
    h = h * (1.0 + _row_mod(mod_ref, is_ctx, 1)) + _row_mod(mod_ref, is_ctx, 0)
    hb = h.astype(BF16)
    ops = _lane_block_ops()
    for cb in range(IN_WIDTH // 256):
        p = jnp.dot(hb, w_ref[:, cb * 256:(cb + 1) * 256], preferred_element_type=F32)
        halves = []
        for hf in range(2):
            kind, scale = ops[cb * 2 + hf]
            ph = p[:, hf * LANES:(hf + 1) * LANES]
            if kind == "r32":
                ph = _rope(ph, c32_ref[...], s32_ref[...], DA_QK // 4)
            elif kind == "r64":
                ph = _rope(ph, c64_ref[...], s64_ref[...], HEAD_DIM // 4)
            if scale != 1.0:
                ph = ph * scale
            halves.append(ph)
        full = jnp.concatenate(halves, axis=1)
        o_ref[0, :, cb * 256:(cb + 1) * 256] = full.astype(BF16)
        if cb == CB_DA_Q:
            qt_ref[0] = full.T.astype(BF16)
        elif cb == CB_DA_V:
            vt_ref[0] = full.T.astype(BF16)


def _inproj(xa, modv, g, w_in_b, tabs, seq, tm):
    B, T, D = xa.shape
    kern = functools.partial(_inproj_kernel, tm=tm, seq=seq)
    tab_spec = pl.BlockSpec((tm, LANES), lambda b, i: (i, 0))
    return pl.pallas_call(
        kern,
        grid=(B, T // tm),
        in_specs=[pl.BlockSpec((1, tm, D), lambda b, i: (b, i, 0)),
                  pl.BlockSpec((1, 2, 8, D), lambda b, i: (b, 0, 0, 0)),
                  pl.BlockSpec((1, D), lambda b, i: (0, 0)),
                  pl.BlockSpec((D, IN_WIDTH), lambda b, i: (0, 0)),
                  tab_spec, tab_spec, tab_spec, tab_spec],
        out_specs=[pl.BlockSpec((1, tm, IN_WIDTH), lambda b, i: (b, i, 0)),
                   pl.BlockSpec((1, GROUP_W, tm), lambda b, i: (b, 0, i)),
                   pl.BlockSpec((1, GROUP_W, tm), lambda b, i: (b, 0, i))],
        out_shape=[jax.ShapeDtypeStruct((B, T, IN_WIDTH), BF16),
                   jax.ShapeDtypeStruct((B, GROUP_W, T), BF16),
                   jax.ShapeDtypeStruct((B, GROUP_W, T), BF16)],
        compiler_params=_params(("arbitrary", "arbitrary")),
        name="inproj",
    )(xa, modv, g.reshape(1, D), w_in_b, *tabs)


def _rope_tables(seq, ctx):
    t = jnp.arange(seq)
    row = (t // GRID_W).astype(F32)
    col = (t % GRID_W).astype(F32)
    out = []
    for dh in (DA_QK, HEAD_DIM):
        half = dh // 2
        quarter = half // 2
        lane = jnp.arange(LANES)
        d = lane % dh
        use_col = (d // half) == 1
        idx = (d % quarter).astype(F32)
        inv = ROPE_BASE ** (-idx * 2.0 / half)
        pos = jnp.where(use_col[None, :], col[:, None], row[:, None])
        ang = pos * inv[None, :]
        first = (d % half) < quarter
        cos = jnp.cos(ang)
        sin = jnp.where(first[None, :], -jnp.sin(ang), jnp.sin(ang))
        cos = jnp.concatenate([cos, jnp.ones((ctx, LANES), F32)], axis=0)
        sin = jnp.concatenate([sin, jnp.zeros((ctx, LANES), F32)], axis=0)
        out += [cos, sin]
    return out


def _rows_per_head(vals, tq):
    return jnp.concatenate([jnp.broadcast_to(v, (HEAD_DIM, tq)) for v in vals], axis=0)


def _da_kernel(lamp_ref, qt_ref, k_ref, vt_ref, g_ref, o_ref, qm_sc, m_sc, l_sc, acc_sc, st_sc, p_sc,
               *, tq, lambda_init):
    ki = pl.program_id(2)
    nk = pl.num_programs(2)

    @pl.when(ki == 0)
    def _():
        qt = qt_ref[0]
        for j in range(2 * N_HEADS):
            qm = jnp.where(_head_mask(qt.shape, DA_QK, j, 0), qt, jnp.zeros_like(qt))
            qm_sc[j] = qm
            m_sc[j] = jnp.max(jnp.dot(k_ref[0, 0:DA_CHUNK, :], qm, preferred_element_type=F32), axis=0,
                              keepdims=True)
        l_sc[...] = jnp.zeros(l_sc.shape, F32)
        acc_sc[...] = jnp.zeros(acc_sc.shape, F32)

    k = k_ref[0]
    vt = vt_ref[0]
    ones = jnp.ones((ONES_ROWS, vt.shape[1]), BF16)

    tk = k.shape[0]

    def vt_ext(h):
        return jnp.concatenate([vt[h * HEAD_DIM:(h + 1) * HEAD_DIM, :], ones], axis=0)

    jump = None
    pvs, mxs = [], []
    for j in range(2 * N_HEADS):
        st = jnp.dot(k, qm_sc[j], preferred_element_type=F32)
        m_prev = m_sc[j]
        mx = jnp.max(st, axis=0, keepdims=True)
        p = jnp.exp2(st - m_prev).astype(BF16)
        pvs.append(jnp.dot(vt_ext(j // 2), p, preferred_element_type=F32))
        mxs.append(mx)
        worst = jnp.max(mx - m_prev)
        jump = worst if jump is None else jnp.maximum(jump, worst)
    keep = jump <= DA_MAX_JUMP
    for j in range(2 * N_HEADS):
        h, t = j // 2, j % 2
        rows = slice(h * HEAD_DIM, (h + 1) * HEAD_DIM)
        m_prev = m_sc[j]
        m_new = jnp.maximum(m_prev, mxs[j])
        alpha = jnp.exp2(m_prev - m_new)
        l_old = l_sc[j]
        acc_old = acc_sc[t, rows, :]
        l_sc[j] = jnp.where(keep, alpha * (l_old + pvs[j][HEAD_DIM:HEAD_DIM + 1, :]), l_old)
        acc_sc[t, rows, :] = jnp.where(keep, alpha * (acc_old + pvs[j][:HEAD_DIM, :]), acc_old)
        m_sc[j] = jnp.where(keep, m_new, m_prev)

    @pl.when(jnp.logical_not(keep))
    def _():
        for j in range(2 * N_HEADS):
            h, t = j // 2, j % 2
            rows = slice(h * HEAD_DIM, (h + 1) * HEAD_DIM)
            st_sc[...] = jnp.dot(k, qm_sc[j], preferred_element_type=F32)
            run = st_sc[0:DA_CHUNK, :]
            for c in range(1, tk // DA_CHUNK):
                run = jnp.maximum(run, st_sc[c * DA_CHUNK:(c + 1) * DA_CHUNK, :])
            m_prev = m_sc[j]
            m_new = jnp.maximum(m_prev, jnp.max(run, axis=0, keepdims=True))
            alpha = jnp.exp2(m_prev - m_new)
            for c in range(tk // DA_CHUNK):
                cr = slice(c * DA_CHUNK, (c + 1) * DA_CHUNK)
                p_sc[cr, :] = jnp.exp2(st_sc[cr, :] - m_new).astype(BF16)
            pv = jnp.dot(vt_ext(h), p_sc[...], preferred_element_type=F32)
            l_sc[j] = alpha * l_sc[j] + pv[HEAD_DIM:HEAD_DIM + 1, :]
            m_sc[j] = m_new
            acc_sc[t, rows, :] = alpha * acc_sc[t, rows, :] + pv[:HEAD_DIM, :]

    @pl.when(ki == nk - 1)
    def _():
        lp = lamp_ref[...]
        lam = (jnp.exp(jnp.sum(lp[0:1] * lp[1:2], axis=1, keepdims=True))
               - jnp.exp(jnp.sum(lp[2:3] * lp[3:4], axis=1, keepdims=True))) + lambda_init
        o0 = acc_sc[0] / _rows_per_head([l_sc[2 * h] for h in range(N_HEADS)], tq)
        o1 = acc_sc[1] / _rows_per_head([l_sc[2 * h + 1] for h in range(N_HEADS)], tq)
        a = (o0 - lam * o1).T
        y = a * lax.rsqrt(_group_mean_sq(a) + EPS) * g_ref[...]
        o_ref[0] = (y * (1.0 - lambda_init)).astype(BF16)


def _diff_attention(P, QT, VT, lamp, subln_full, lambda_init, q0, nq, tq, k0, nk, tk):
    B = P.shape[0]
    kern = functools.partial(_da_kernel, tq=tq, lambda_init=lambda_init)
    return pl.pallas_call(
        kern,
        grid=(B, nq, nk),
        in_specs=[pl.BlockSpec((8, LANES), lambda b, qi, ki: (0, 0)),
                  pl.BlockSpec((1, GROUP_W, tq), lambda b, qi, ki: (b, 0, q0 + qi)),
                  pl.BlockSpec((1, tk, GROUP_W), lambda b, qi, ki: (b, k0 + ki, CB_DA_K)),
                  pl.BlockSpec((1, GROUP_W, tk), lambda b, qi, ki: (b, 0, k0 + ki)),
                  pl.BlockSpec((1, GROUP_W), lambda b, qi, ki: (0, 0))],
        out_specs=pl.BlockSpec((1, tq, GROUP_W), lambda b, qi, ki: (b, qi, 0)),
        out_shape=jax.ShapeDtypeStruct((B, nq * tq, GROUP_W), BF16),
        scratch_shapes=[pltpu.VMEM((2 * N_HEADS, GROUP_W, tq), BF16),
                        pltpu.VMEM((2 * N_HEADS, 1, tq), F32),
                        pltpu.VMEM((2 * N_HEADS, 1, tq), F32),
                        pltpu.VMEM((2, GROUP_W, tq), F32),
                        pltpu.VMEM((tk, tq), F32),
                        pltpu.VMEM((tk, tq), BF16)],
        compiler_params=_params(("arbitrary", "arbitrary", "arbitrary")),
        name="diff_attention",
    )(lamp, QT, P, VT, subln_full)


def _swa_kernel(sink_ref, q_ref, kp_ref, kc_ref, kn_ref, kx_ref, vp_ref, vc_ref, vn_ref, vx_ref, o_ref,
                *, seq, blk):
    i = pl.program_id(0)
    nloc = 3 * blk
    nkeys = nloc + kx_ref.shape[1]
    qpos = i * blk + lax.broadcasted_iota(jnp.int32, (blk, nkeys), 0)
    c = lax.broadcasted_iota(jnp.int32, (blk, nkeys), 1)
    kpos = (i - 1) * blk + c
    valid = ((kpos >= 0) & (kpos < seq) & (jnp.abs(qpos - kpos) <= SWA_WINDOW) & (qpos < seq)) | (c >= nloc)
    for b in range(q_ref.shape[0]):
        q = q_ref[b]
        kall = jnp.concatenate([kp_ref[b], kc_ref[b], kn_ref[b], kx_ref[b]], axis=0)
        vall = jnp.concatenate([vp_ref[b], vc_ref[b], vn_ref[b], vx_ref[b]], axis=0)
        qs = []
        for g in range(2):
            qg = q[:, g * LANES:(g + 1) * LANES]
            for kvh in range(2):
                qs.append(jnp.where(_head_mask(qg.shape, HEAD_DIM, kvh), qg, jnp.zeros_like(qg)))
        s = lax.dot_general(jnp.concatenate(qs, axis=0), kall, (((1,), (1,)), ((), ())),
                            preferred_element_type=F32)
        vms = [jnp.where(_head_mask(vall.shape, HEAD_DIM, kvh), vall, jnp.zeros_like(vall)) for kvh in range(2)]
        vm = jnp.concatenate(vms, axis=0)
        for g in range(2):
            ps = []
            for kvh in range(2):
                j = 2 * g + kvh
                sink = sink_ref[2 * kvh + g] * LOG2E
                sj = jnp.where(valid, s[j * blk:(j + 1) * blk], NEG)
                m = jnp.maximum(jnp.max(sj, axis=1, keepdims=True), sink)
                e = jnp.exp2(sj - m)
                den = jnp.sum(e, axis=1, keepdims=True) + jnp.exp2(sink - m)
                ps.append((e / den).astype(BF16))
            o_ref[b, :, g * LANES:(g + 1) * LANES] = jnp.dot(
                jnp.concatenate(ps, axis=1), vm, preferred_element_type=F32).astype(BF16)


def _window_attention(P, sink, seq, ctx):
    B, T, _ = P.shape
    blk = SWA_WINDOW
    nt = T // blk
    kern = functools.partial(_swa_kernel, seq=seq, blk=blk)

    def nb(cb, off):
        return pl.BlockSpec((B, blk, LANES), lambda i: (0, jnp.clip(i + off, 0, nt - 1), cb))

    def cx(cb):
        return pl.BlockSpec((B, ctx, LANES), lambda i: (0, seq // ctx, cb))

    return pl.pallas_call(
        kern,
        grid=(nt,),
        in_specs=[pl.BlockSpec(memory_space=pltpu.SMEM),
                  pl.BlockSpec((B, blk, GROUP_W), lambda i: (0, i, CB_SWA_Q)),
                  nb(CB_SWA_K128, -1), nb(CB_SWA_K128, 0), nb(CB_SWA_K128, 1), cx(CB_SWA_K128),
                  nb(CB_SWA_V128, -1), nb(CB_SWA_V128, 0), nb(CB_SWA_V128, 1), cx(CB_SWA_V128)],
        out_specs=pl.BlockSpec((B, blk, GROUP_W), lambda i: (0, i, 0)),
        out_shape=jax.ShapeDtypeStruct((B, T, GROUP_W), BF16),
        compiler_params=_params(("arbitrary",)),
        name="window_attention",
    )(sink, P, P, P, P, P, P, P, P, P)


def _na_kernel(q_ref, k0_ref, k1_ref, k2_ref, kx_ref, v0_ref, v1_ref, v2_ref, vx_ref, bias_ref, o_ref,
               *, tq, n_seq_tiles):
    ctx_tile = pl.program_id(0) >= n_seq_tiles
    nloc = 3 * tq
    for b in range(q_ref.shape[0]):
        q = q_ref[b]
        kall = jnp.concatenate([k0_ref[b], k1_ref[b], k2_ref[b], kx_ref[b]], axis=0)
        vall = jnp.concatenate([v0_ref[b], v1_ref[b], v2_ref[b], vx_ref[b]], axis=0)
        qs = jnp.concatenate(
            [jnp.where(_head_mask(q.shape, HEAD_DIM, h), q, jnp.zeros_like(q)) for h in range(N_HEADS)], axis=0)
        s = lax.dot_general(qs, kall, (((1,), (1,)), ((), ())), preferred_element_type=F32)
        ps = []
        for h in range(N_HEADS):
            sh = s[h * tq:(h + 1) * tq]
            s_loc = jnp.where(ctx_tile, NEG, sh[:, :nloc] + bias_ref[0, h])
            s_ctx = sh[:, nloc:]
            m = jnp.maximum(jnp.max(s_loc, axis=1, keepdims=True), jnp.max(s_ctx, axis=1, keepdims=True))
            e_loc = jnp.exp2(s_loc - m)
            e_ctx = jnp.exp2(s_ctx - m)
            den = jnp.sum(e_loc, axis=1, keepdims=True) + jnp.sum(e_ctx, axis=1, keepdims=True)
            ps.append(jnp.concatenate([e_loc / den, e_ctx / den], axis=1).astype(BF16))
        vm = jnp.concatenate(
            [jnp.where(_head_mask(vall.shape, HEAD_DIM, h), vall, jnp.zeros_like(vall)) for h in range(N_HEADS)],
            axis=0)
        o_ref[b] = jnp.dot(jnp.concatenate(ps, axis=1), vm, preferred_element_type=F32).astype(BF16)


def _na_bias_kernel(rt_ref, o_ref, sv_sc, *, rows):
    tr = NA_TILE_ROWS
    nrt = rows // tr
    nl = 3 * tr * GRID_W
    n_dcol = 2 * NA_WIN_COLS - 1
    rt = rt_ref[0] * LOG2E
    hi = rt.astype(BF16)
    r1 = rt - hi.astype(F32)
    mid = r1.astype(BF16)
    lo = (r1 - mid.astype(F32)).astype(BF16)
    lane = lax.broadcasted_iota(jnp.int32, (GRID_W, nl), 1)
    qc = lax.broadcasted_iota(jnp.int32, (GRID_W, nl), 0)
    kc = lane % GRID_W
    kr = lane // GRID_W
    dc = jnp.clip(kc - qc, 1 - NA_WIN_COLS, NA_WIN_COLS - 1) + (NA_WIN_COLS - 1)
    c0 = jnp.clip(qc - NA_WIN_COLS // 2, 0, GRID_W - NA_WIN_COLS)
    col_ok = (kc >= c0) & (kc < c0 + NA_WIN_COLS)
    irow = lax.broadcasted_iota(jnp.int32, (LANES, nl), 0)
    krl = lax.broadcasted_iota(jnp.int32, (LANES, nl), 1) // GRID_W
    for v, (t_idx, ws) in enumerate(((0, 0), (1, 0), (nrt - 1, nrt - 3))):
        for qr in range(tr):
            r = t_idx * tr + qr
            r0 = min(max(r - NA_WIN_ROWS // 2, 0), rows - NA_WIN_ROWS)
            d_row = jnp.clip(ws * tr + krl - r + (NA_WIN_ROWS - 1), 0, 2 * NA_WIN_ROWS - 2)
            onehot = jnp.where(irow == d_row, 1.0, 0.0).astype(BF16)
            sv_sc[...] = (jnp.dot(hi, onehot, preferred_element_type=F32)
                          + jnp.dot(mid, onehot, preferred_element_type=F32)
                          + jnp.dot(lo, onehot, preferred_element_type=F32))

            def pick(j, acc):
                return jnp.where(dc == j, sv_sc[pl.ds(j, 1), :], acc)

            acc = lax.fori_loop(0, n_dcol, pick, jnp.zeros((GRID_W, nl), F32), unroll=True)
            krow = ws * tr + kr
            ok = col_ok & (krow >= r0) & (krow < r0 + NA_WIN_ROWS)
            o_ref[v, 0, qr * GRID_W:(qr + 1) * GRID_W, :] = jnp.where(ok, acc, NEG)


def _na_bias_tables(rpb, rows):
    H, nr, ncol = rpb.shape
    rt = jnp.zeros((H, 32, LANES), F32).at[:, :ncol, :nr].set(jnp.swapaxes(rpb.astype(F32), 1, 2))
    tq = NA_TILE_ROWS * GRID_W
    return pl.pallas_call(
        functools.partial(_na_bias_kernel, rows=rows),
        grid=(H,),
        in_specs=[pl.BlockSpec((1, 32, LANES), lambda h: (h, 0, 0))],
        out_specs=pl.BlockSpec((3, 1, tq, 3 * tq), lambda h: (0, h, 0, 0)),
        out_shape=jax.ShapeDtypeStruct((3, H, tq, 3 * tq), F32),
        scratch_shapes=[pltpu.VMEM((32, 3 * tq), F32)],
        compiler_params=_params(("arbitrary",)),
        name="na_bias",
    )(rt)


def _neighborhood_attention(P, bias_tabs, seq, ctx):
    B, T, _ = P.shape
    tq = NA_TILE_ROWS * GRID_W
    nst = seq // tq
    nt = T // tq
    kern = functools.partial(_na_kernel, tq=tq, n_seq_tiles=nst)

    def win(i):
        return jnp.clip(i - 1, 0, nst - 3)

    def loc(cb, off):
        return pl.BlockSpec((B, tq, GROUP_W), lambda i: (0, win(i) + off, cb))

    def cx(cb):
        return pl.BlockSpec((B, ctx, GROUP_W), lambda i: (0, seq // ctx, cb))

    def variant(i):
        return (jnp.where(i >= nst, 0, i - win(i)), 0, 0, 0)

    return pl.pallas_call(
        kern,
        grid=(nt,),
        in_specs=[pl.BlockSpec((B, tq, GROUP_W), lambda i: (0, i, CB_NA_Q)),
                  loc(CB_NA_K, 0), loc(CB_NA_K, 1), loc(CB_NA_K, 2), cx(CB_NA_K),
                  loc(CB_NA_V, 0), loc(CB_NA_V, 1), loc(CB_NA_V, 2), cx(CB_NA_V),
                  pl.BlockSpec((1, N_HEADS, tq, 3 * tq), variant)],
        out_specs=pl.BlockSpec((B, tq, GROUP_W), lambda i: (0, i, 0)),
        out_shape=jax.ShapeDtypeStruct((B, T, GROUP_W), BF16),
        compiler_params=_params(("arbitrary",)),
        name="neighborhood_attention",
    )(P, P, P, P, P, P, P, P, P, bias_tabs)


def _ret_direction(q, k, v, state_ref, lgs_ref, lgv, forward, base):
    C = q.shape[0]
    i = lax.broadcasted_iota(jnp.int32, (C, C), 0)
    j = lax.broadcasted_iota(jnp.int32, (C, C), 1)
    dist = (i - j if forward else j - i).astype(F32)
    keep = dist >= 0 if forward else dist > 0
    dist = jnp.where(keep, dist, 0.0)
    qs = jnp.concatenate(
        [jnp.where(_head_mask(q.shape, HEAD_DIM, h), q, jnp.zeros_like(q)) for h in range(N_HEADS)], axis=0)
    s = lax.dot_general(qs, k, (((1,), (1,)), ((), ())), preferred_element_type=F32)
    atts = []
    for h in range(N_HEADS):
        decay = jnp.where(keep, jnp.exp(dist * lgs_ref[base + h]), 0.0)
        atts.append((s[h * C:(h + 1) * C] * decay).astype(BF16))
    vm = jnp.concatenate(
        [jnp.where(_head_mask(v.shape, HEAD_DIM, h), v, jnp.zeros_like(v)) for h in range(N_HEADS)], axis=0)
    intra = jnp.dot(jnp.concatenate(atts, axis=1), vm, preferred_element_type=F32)
    r = lax.broadcasted_iota(jnp.int32, (C, 1), 0).astype(F32)
    xi = jnp.exp((r + 1.0 if forward else C - r) * lgv)
    zeta = jnp.exp((C - 1.0 - r if forward else r) * lgv)
    state = state_ref[...]
    cross = jnp.dot((q.astype(F32) * xi).astype(BF16), state.astype(BF16), preferred_element_type=F32)
    kz_t = (k.astype(F32) * zeta).T.astype(BF16)
    u = jnp.dot(kz_t, v, preferred_element_type=F32)
    rr = lax.broadcasted_iota(jnp.int32, u.shape, 0) // HEAD_DIM
    cc = lax.broadcasted_iota(jnp.int32, u.shape, 1) // HEAD_DIM
    state_ref[...] = jnp.where(rr == cc, jnp.exp(C * lgv) * state + u, 0.0)
    return intra + cross


def _ret_kernel(lgs_ref, lgv_ref, qf_ref, kf_ref, vf_ref, qb_ref, kb_ref, vb_ref, of_ref, ob_ref, sf_sc, sb_sc):
    @pl.when(pl.program_id(0) == 0)
    def _():
        sf_sc[...] = jnp.zeros(sf_sc.shape, F32)
        sb_sc[...] = jnp.zeros(sb_sc.shape, F32)

    for b in range(qf_ref.shape[0]):
        of_ref[b] = _ret_direction(qf_ref[b], kf_ref[b], vf_ref[b], sf_sc.at[b], lgs_ref, lgv_ref[0:1], True, 0)
        ob_ref[b] = _ret_direction(qb_ref[b], kb_ref[b], vb_ref[b], sb_sc.at[b], lgs_ref, lgv_ref[1:2], False,
                                   N_HEADS)


def _retention(P, lgs, lgv, seq, ctx):
    B, T, _ = P.shape
    C = RET_CHUNK
    ns, nc = seq // C, ctx // C
    nt = ns + nc

    def fwd(n):
        return jnp.where(n < nc, ns + n, n - nc)

    def bwd(n):
        return jnp.where(n < nc, nt - 1 - n, nt - 1 - n)

    def spec(order, cb):
        return pl.BlockSpec((B, C, GROUP_W), lambda n: (0, order(n), cb))

    return pl.pallas_call(
        _ret_kernel,
        grid=(nt,),
        in_specs=[pl.BlockSpec(memory_space=pltpu.SMEM),
                  pl.BlockSpec((8, GROUP_W), lambda n: (0, 0)),
                  spec(fwd, CB_RET_Q), spec(fwd, CB_RET_K), spec(fwd, CB_RET_V),
                  spec(bwd, CB_RET_Q), spec(bwd, CB_RET_K), spec(bwd, CB_RET_V)],
        out_specs=[pl.BlockSpec((B, C, GROUP_W), lambda n: (0, fwd(n), 0)),
                   pl.BlockSpec((B, C, GROUP_W), lambda n: (0, bwd(n), 0))],
        out_shape=[jax.ShapeDtypeStruct((B, T, GROUP_W), F32)] * 2,
        scratch_shapes=[pltpu.VMEM((B, GROUP_W, GROUP_W), F32)] * 2,
        compiler_params=_params(("arbitrary",)),
        name="retention",
    )(lgs, lgv, P, P, P, P, P, P)


def _outproj_kernel(*refs, tm, seq, n_exp):
    with_router = n_exp > 0
    if with_router:
        (ya_ref, yb_ref, yn_ref, of_ref, ob_ref, gt_ref, x_ref, mod_ref, w_ref, gpost_ref, gpre_ref, r_ref,
         x1_ref, h2_ref, eid_ref, gw_ref) = refs
    else:
        (ya_ref, yb_ref, yn_ref, of_ref, ob_ref, gt_ref, x_ref, mod_ref, w_ref, gpost_ref, gpre_ref,
         x1_ref, h2_ref) = refs
    i = pl.program_id(1)
    rows = i * tm + lax.broadcasted_iota(jnp.int32, (tm, 1), 0)
    is_ctx = rows >= seq
    o = of_ref[0] + ob_ref[0]
    yr = o * lax.rsqrt(_group_mean_sq(o) + EPS) * _silu(gt_ref[0].astype(F32))
    y = (jnp.dot(ya_ref[0], w_ref[0:256, :], preferred_element_type=F32)
         + jnp.dot(yb_ref[0], w_ref[256:512, :], preferred_element_type=F32)
         + jnp.dot(yn_ref[0], w_ref[512:768, :], preferred_element_type=F32)
         + jnp.dot(yr.astype(BF16), w_ref[768:1024, :], preferred_element_type=F32))
    yn = y * lax.rsqrt(jnp.mean(y * y, axis=-1, keepdims=True) + EPS) * gpost_ref[...]
    x1 = x_ref[0] + _row_mod(mod_ref, is_ctx, 2) * yn
    x1_ref[0] = x1
    h = x1 * lax.rsqrt(jnp.mean(x1 * x1, axis=-1, keepdims=True) + EPS) * gpre_ref[...]
    h = h * (1.0 + _row_mod(mod_ref, is_ctx, 4)) + _row_mod(mod_ref, is_ctx, 3)
    h2_ref[0] = h.astype(h2_ref.dtype)
    if with_router:
        r = r_ref[...]
        r_hi = r.astype(BF16)
        r_lo = (r - r_hi.astype(F32)).astype(BF16)
        h_hi = h.astype(BF16)
        h_lo = (h - h_hi.astype(F32)).astype(BF16)
        dn = (((1,), (1,)), ((), ()))
        lt = (lax.dot_general(r_hi, h_hi, dn, preferred_element_type=F32)
              + lax.dot_general(r_hi, h_lo, dn, preferred_element_type=F32)
              + lax.dot_general(r_lo, h_hi, dn, preferred_element_type=F32))
        e = lax.broadcasted_iota(jnp.int32, lt.shape, 0)
        lt = jnp.where(e < n_exp, lt, NEG)
        v1 = jnp.max(lt, axis=0, keepdims=True)
        i1 = jnp.min(jnp.where(lt == v1, e, ROUTER_ROWS), axis=0, keepdims=True)
        rest = jnp.where(e == i1, NEG, lt)
        v2 = jnp.max(rest, axis=0, keepdims=True)
        i2 = jnp.min(jnp.where(rest == v2, e, ROUTER_ROWS), axis=0, keepdims=True)
        e2 = jnp.exp(v2 - v1)
        w1 = 1.0 / (1.0 + e2)
        w2 = e2 / (1.0 + e2)
        eid_ref[0] = jnp.where(e == 0, i1, jnp.where(e == 1, i2, 0))[:8]
        wrow = jnp.where(e == 0, w1, jnp.where(e == 1, w2, 0.0))
        gw_ref[0] = jnp.concatenate([wrow, jnp.zeros((LANES - ROUTER_ROWS, tm), F32)], axis=0).T


def _outproj(ya, yb, yn, of, ob, P, xa, modv, w_out_b, g_post, g_pre2, rows, seq, tm, router=None, n_exp=0):
    B, _, D = xa.shape
    with_router = router is not None
    kern = functools.partial(_outproj_kernel, tm=tm, seq=seq, n_exp=n_exp)
    tok = lambda w: pl.BlockSpec((1, tm, w), lambda b, i: (b, i, 0))
    in_specs = [tok(GROUP_W), tok(GROUP_W), tok(GROUP_W), tok(GROUP_W), tok(GROUP_W),
                pl.BlockSpec((1, tm, GROUP_W), lambda b, i: (b, i, CB_RET_G)),
                tok(D),
                pl.BlockSpec((1, 2, 8, D), lambda b, i: (b, 0, 0, 0)),
                pl.BlockSpec((D, D), lambda b, i: (0, 0)),
                pl.BlockSpec((1, D), lambda b, i: (0, 0)),
                pl.BlockSpec((1, D), lambda b, i: (0, 0))]
    args = [ya, yb, yn, of, ob, P, xa, modv, w_out_b, g_post.reshape(1, D), g_pre2.reshape(1, D)]
    out_specs = [tok(D), tok(D)]
    out_shape = [jax.ShapeDtypeStruct((B, rows, D), F32),
                 jax.ShapeDtypeStruct((B, rows, D), F32 if with_router else BF16)]
    if with_router:
        in_specs.append(pl.BlockSpec((ROUTER_ROWS, D), lambda b, i: (0, 0)))
        args.append(router)
        out_specs += [pl.BlockSpec((1, 8, tm), lambda b, i: (b, 0, i)), tok(LANES)]
        out_shape += [jax.ShapeDtypeStruct((B, 8, rows), jnp.int32),
                      jax.ShapeDtypeStruct((B, rows, LANES), F32)]
    return pl.pallas_call(
        kern,
        grid=(B, rows // tm),
        in_specs=in_specs,
        out_specs=out_specs,
        out_shape=out_shape,
        compiler_params=_params(("arbitrary", "arbitrary")),
        name="outproj_router" if with_router else "outproj",
    )(*args)


def _ffn_kernel(h_ref, x_ref, mod_ref, wg_ref, wu_ref, wd_ref, gpost_ref, o_ref, *, tm, seq, fc):
    i = pl.program_id(1)
    rows = i * tm + lax.broadcasted_iota(jnp.int32, (tm, 1), 0)
    is_ctx = rows >= seq
    h = h_ref[0]
    acc = jnp.zeros((tm, o_ref.shape[-1]), F32)
    for c in range(wg_ref.shape[1] // fc):
        g = jnp.dot(h, wg_ref[:, c * fc:(c + 1) * fc], preferred_element_type=F32)
        u = jnp.dot(h, wu_ref[:, c * fc:(c + 1) * fc], preferred_element_type=F32)
        a = (_silu(g) * u).astype(BF16)
        acc = acc + jnp.dot(a, wd_ref[c * fc:(c + 1) * fc, :], preferred_element_type=F32)
    yn = acc * lax.rsqrt(jnp.mean(acc * acc, axis=-1, keepdims=True) + EPS) * gpost_ref[...]
    o_ref[0] = x_ref[0] + _row_mod(mod_ref, is_ctx, 5) * yn


def _dense_ffn(h2, x1, modv, wg, wu, wd, g_post, seq, tm):
    B, T, D = x1.shape
    F = wg.shape[1]
    kern = functools.partial(_ffn_kernel, tm=tm, seq=seq, fc=256)
    const = lambda shape: pl.BlockSpec(shape, lambda b, i: (0, 0), pipeline_mode=pl.Buffered(1))
    return pl.pallas_call(
        kern,
        grid=(B, T // tm),
        in_specs=[pl.BlockSpec((1, tm, D), lambda b, i: (b, i, 0)),
                  pl.BlockSpec((1, tm, D), lambda b, i: (b, i, 0)),
                  pl.BlockSpec((1, 2, 8, D), lambda b, i: (b, 0, 0, 0)),
                  const((D, F)), const((D, F)), const((F, D)),
                  pl.BlockSpec((1, D), lambda b, i: (0, 0))],
        out_specs=pl.BlockSpec((1, tm, D), lambda b, i: (b, i, 0)),
        out_shape=jax.ShapeDtypeStruct((B, T, D), F32),
        compiler_params=_params(("arbitrary", "arbitrary")),
        name="dense_ffn",
    )(h2, x1, modv, wg, wu, wd, g_post.reshape(1, D))


def _row_copy(src_hbm, row, dst_vmem, r, sem):
    return pltpu.make_async_copy(src_hbm.at[pl.ds(row, 1)], dst_vmem.at[pl.ds(r, 1)], sem)


def _moe_ffn_kernel(te_ref, tv_ref, src0_ref, srcn_ref, h_hbm, wg_ref, wu_ref, wd_ref, o_ref,
                    xg_sc, xb_sc, acc_sc, sem, *, tr):
    t = pl.program_id(0)
    f = pl.program_id(1)
    nt = pl.num_programs(0)
    nf = pl.num_programs(1)
    slot = t % 2

    def start_gather(src_ref, s):
        def body(r, c):
            _row_copy(h_hbm, src_ref[0, 0, r], xg_sc.at[s], r, sem.at[s]).start()
            return c

        lax.fori_loop(0, tr, body, 0, unroll=8)

    @pl.when((f == 0) & (t == 0) & (tv_ref[0] > 0))
    def _():
        start_gather(src0_ref, 0)

    @pl.when(f == 0)
    def _():
        acc_sc[...] = jnp.zeros(acc_sc.shape, F32)

    @pl.when((f == 0) & (tv_ref[t] > 0))
    def _():
        pltpu.make_async_copy(h_hbm.at[pl.ds(0, tr)], xg_sc.at[slot], sem.at[slot]).wait()
        xb_sc[...] = xg_sc[slot].astype(BF16)

    @pl.when((f == 0) & (t + 1 < nt) & (tv_ref[jnp.minimum(t + 1, nt - 1)] > 0))
    def _():
        start_gather(srcn_ref, 1 - slot)

    @pl.when(tv_ref[t] > 0)
    def _():
        x = xb_sc[...]
        g = jnp.dot(x, wg_ref[0], preferred_element_type=F32)
        u = jnp.dot(x, wu_ref[0], preferred_element_type=F32)
        a = (_silu(g) * u).astype(BF16)
        acc_sc[...] += jnp.dot(a, wd_ref[0], preferred_element_type=F32)

    @pl.when(f == nf - 1)
    def _():
        o_ref[...] = acc_sc[...]


def _moe_ffn(h_flat, src, tile_expert, tile_valid, wg, wu, wd, n_rows, tr, tf):
    D = h_flat.shape[1]
    F = wg.shape[2]
    nt = n_rows // tr
    return pl.pallas_call(
        functools.partial(_moe_ffn_kernel, tr=tr),
        grid_spec=pltpu.PrefetchScalarGridSpec(
            num_scalar_prefetch=2,
            grid=(nt, F // tf),
            in_specs=[pl.BlockSpec((1, 1, tr), lambda t, f, te, tv: (0, 0, 0), memory_space=pltpu.SMEM),
                      pl.BlockSpec((1, 1, tr), lambda t, f, te, tv: (jnp.minimum(t + 1, nt - 1), 0, 0),
                                   memory_space=pltpu.SMEM),
                      pl.BlockSpec(memory_space=pl.ANY),
                      pl.BlockSpec((1, D, tf), lambda t, f, te, tv: (te[t], 0, f)),
                      pl.BlockSpec((1, D, tf), lambda t, f, te, tv: (te[t], 0, f)),
                      pl.BlockSpec((1, tf, D), lambda t, f, te, tv: (te[t], f, 0))],
            out_specs=pl.BlockSpec((tr, D), lambda t, f, te, tv: (t, 0)),
            scratch_shapes=[pltpu.VMEM((2, tr, D), h_flat.dtype), pltpu.VMEM((tr, D), BF16),
                            pltpu.VMEM((tr, D), F32), pltpu.SemaphoreType.DMA((2,))]),
        out_shape=jax.ShapeDtypeStruct((n_rows, D), F32),
        compiler_params=_params(("arbitrary", "arbitrary")),
        name="moe_ffn",
    )(tile_expert, tile_valid, src.reshape(nt, 1, tr), src.reshape(nt, 1, tr), h_flat, wg, wu, wd)


def _combine_kernel(pos_ref, ys_hbm, gw_ref, x_ref, mod_ref, gpost_ref, o_ref, y1_sc, y2_sc, sem, *, tm, seq):
    def issue(r, c):
        _row_copy(ys_hbm, pos_ref[0, 0, r], y1_sc, r, sem.at[0]).start(priority=0)
        _row_copy(ys_hbm, pos_ref[0, 1, r], y2_sc, r, sem.at[1]).start(priority=1)
        return c

    lax.fori_loop(0, tm, issue, 0, unroll=8)
    pltpu.make_async_copy(ys_hbm.at[pl.ds(0, tm)], y1_sc, sem.at[0]).wait()
    pltpu.make_async_copy(ys_hbm.at[pl.ds(0, tm)], y2_sc, sem.at[1]).wait()
    rows = pl.program_id(1) * tm + lax.broadcasted_iota(jnp.int32, (tm, 1), 0)
    gw = gw_ref[0]
    y = gw[:, 0:1] * y1_sc[...] + gw[:, 1:2] * y2_sc[...]
    yn = y * lax.rsqrt(jnp.mean(y * y, axis=-1, keepdims=True) + EPS) * gpost_ref[...]
    o_ref[0] = x_ref[0] + _row_mod(mod_ref, rows >= seq, 5) * yn


def _moe_combine(pos, ys, gw, x1, modv, g_post, seq, tm):
    B, R, D = x1.shape
    ntile = R // tm
    kern = functools.partial(_combine_kernel, tm=tm, seq=seq)
    return pl.pallas_call(
        kern,
        grid=(B, ntile),
        in_specs=[pl.BlockSpec((1, TOP_K, tm), lambda b, i: (b * ntile + i, 0, 0), memory_space=pltpu.SMEM),
                  pl.BlockSpec(memory_space=pl.ANY),
                  pl.BlockSpec((1, tm, LANES), lambda b, i: (b, i, 0)),
                  pl.BlockSpec((1, tm, D), lambda b, i: (b, i, 0)),
                  pl.BlockSpec((1, 2, 8, D), lambda b, i: (b, 0, 0, 0)),
                  pl.BlockSpec((1, D), lambda b, i: (0, 0))],
        out_specs=pl.BlockSpec((1, tm, D), lambda b, i: (b, i, 0)),
        out_shape=jax.ShapeDtypeStruct((B, R, D), F32),
        scratch_shapes=[pltpu.VMEM((tm, D), F32), pltpu.VMEM((tm, D), F32), pltpu.SemaphoreType.DMA((2,))],
        compiler_params=_params(("arbitrary", "arbitrary")),
        name="moe_combine",
    )(pos.reshape(B * ntile, tm, TOP_K).transpose(0, 2, 1), ys, gw, x1, modv, g_post.reshape(1, D))


def _moe_routing(eid, n_exp, tr):
    N = eid.shape[0]
    e_flat = eid.reshape(-1)
    onehot = (e_flat[:, None] == jnp.arange(n_exp)[None, :]).astype(jnp.int32)
    rank = jnp.sum((jnp.cumsum(onehot, axis=0) - 1) * onehot, axis=1)
    counts = jnp.sum(onehot, axis=0)
    padded = ((counts + tr - 1) // tr) * tr
    ends = jnp.cumsum(padded)
    starts = ends - padded
    dest = starts[e_flat] + rank
    n_rows = (TOP_K * N // tr + n_exp) * tr
    src = jnp.zeros((n_rows,), jnp.int32).at[dest].set(jnp.arange(TOP_K * N, dtype=jnp.int32) // TOP_K)
    tile_start = jnp.arange(n_rows // tr, dtype=jnp.int32) * tr
    tile_expert = jnp.minimum(jnp.sum((tile_start[:, None] >= ends[None, :]).astype(jnp.int32), axis=1), n_exp - 1)
    tile_valid = (tile_start < ends[-1]).astype(jnp.int32)
    pos = dest.reshape(N, TOP_K).astype(jnp.int32)
    return src, pos, tile_expert, tile_valid, n_rows


def _swa_perm():
    idx = []
    for g in range(2):
        for kvh in range(2):
            idx += [(kvh * 2 + g) * HEAD_DIM + d for d in range(HEAD_DIM)]
    return jnp.array(idx, jnp.int32)


def kernel(x, c, ctx, c_ctx, w_mod, b_mod, g_attn_pre, g_attn_post, g_ffn_pre, g_ffn_post, w_in, w_out,
           da_lambda_q1, da_lambda_k1, da_lambda_q2, da_lambda_k2, da_subln, swa_sink, na_rpb,
           ret_gamma_fwd, ret_gamma_bwd, ffn_w_gate, ffn_w_up, ffn_w_down,
           moe_router, moe_w_gate, moe_w_up, moe_w_down):
    B, S, D = x.shape
    CTX = ctx.shape[1]
    T = S + CTX
    L = w_mod.shape[0]
    n_exp = moe_router.shape[-1]
    rows = S // GRID_W

    def token_tile(n_rows):
        return next(t for t in TOKEN_TILES if n_rows % t == 0)

    c8 = jnp.zeros((8, D), F32).at[:B].set(c).at[B].set(c_ctx)
    mod = _modulation(c8, w_mod, b_mod).reshape(L, 8, 6, D)
    mod = jnp.pad(mod, ((0, 0), (0, 0), (0, 2), (0, 0)))
    modv = jnp.stack([mod[:, :B], jnp.broadcast_to(mod[:, B:B + 1], (L, B, 8, D))], axis=2)

    tabs = _rope_tables(S, CTX)
    perm = _swa_perm()
    swa0 = CB_SWA_Q * 256
    xa = jnp.concatenate([x, ctx], axis=1)

    for l in range(L):
        lambda_init = 0.8 - 0.6 * math.exp(-0.3 * l)
        last = l == L - 1
        w_in_l = w_in[l]
        w_in_b = jnp.concatenate([w_in_l[:, :swa0], w_in_l[:, swa0:swa0 + 256][:, perm], w_in_l[:, swa0 + 256:]],
                                 axis=1).astype(BF16)
        w_out_l = w_out[l]
        w_out_b = jnp.concatenate([w_out_l[:256], w_out_l[256:512][perm], w_out_l[512:]], axis=0).astype(BF16)

        P, QT, VT = _inproj(xa, modv[l], g_attn_pre[l], w_in_b, tabs, S, token_tile(T))

        lamp = jnp.zeros((8, LANES), F32)
        for r, v in enumerate((da_lambda_q1[l], da_lambda_k1[l], da_lambda_q2[l], da_lambda_k2[l])):
            lamp = lamp.at[r, :DA_QK].set(v)
        subln_full = jnp.tile(da_subln[l], N_HEADS).reshape(1, GROUP_W)
        tq_da = next(t for t in (1024, 512, 256) if S % t == 0)
        tk_da = next(t for t in (768, 512, 256) if T % t == 0)
        ya = _diff_attention(P, QT, VT, lamp, subln_full, lambda_init, 0, S // tq_da, tq_da, 0, T // tk_da, tk_da)
        if not last:
            ya_ctx = _diff_attention(P, QT, VT, lamp, subln_full, lambda_init, S // CTX, 1, CTX, S // CTX, 1, CTX)
            ya = jnp.concatenate([ya, ya_ctx], axis=1)
        yb = _window_attention(P, swa_sink[l].astype(F32), S, CTX)
        yn = _neighborhood_attention(P, _na_bias_tables(na_rpb[l], rows), S, CTX)
        lg = jnp.stack([jax.nn.log_sigmoid(ret_gamma_fwd[l].astype(F32)),
                        jax.nn.log_sigmoid(ret_gamma_bwd[l].astype(F32))])
        lgv = jnp.zeros((8, GROUP_W), F32).at[:2].set(jnp.repeat(lg, HEAD_DIM, axis=1))
        of, ob = _retention(P, lg.reshape(-1), lgv, S, CTX)

        R = S if last else T
        e = l // 2
        if l % 2 == 0:
            x1, h2 = _outproj(ya, yb, yn, of, ob, P, xa, modv[l], w_out_b, g_attn_post[l], g_ffn_pre[l],
                              R, S, token_tile(R))
            xa = _dense_ffn(h2, x1, modv[l], ffn_w_gate[e].astype(BF16), ffn_w_up[e].astype(BF16),
                            ffn_w_down[e].astype(BF16), g_ffn_post[l], S, token_tile(R))
        else:
            router = jnp.zeros((ROUTER_ROWS, D), F32).at[:n_exp].set(moe_router[e].T)
            x1, h2, eid, gw = _outproj(ya, yb, yn, of, ob, P, xa, modv[l], w_out_b, g_attn_post[l],
                                       g_ffn_pre[l], R, S, token_tile(R), router=router, n_exp=n_exp)
            tr = 512
            src, pos, tile_expert, tile_valid, n_rows = _moe_routing(
                jnp.swapaxes(eid[:, :TOP_K, :], 1, 2).reshape(B * R, TOP_K), n_exp, tr)
            ys = _moe_ffn(h2.reshape(B * R, D), src, tile_expert, tile_valid, moe_w_gate[e].astype(BF16),
                          moe_w_up[e].astype(BF16), moe_w_down[e].astype(BF16), n_rows, tr,
                          next(t for t in EXPERT_F_TILES if moe_w_gate.shape[-1] % t == 0))
            xa = _moe_combine(pos, ys, gw, x1, modv[l], g_ffn_post[l], S,
                              next(t for t in (512, 256) if R % t == 0))
    return xa[:, :S]
```

```python
import functools
import math

import jax
import jax.numpy as jnp
from jax import lax
from jax.experimental import pallas as pl
from jax.experimental.pallas import tpu as pltpu

F32 = jnp.float32
BF16 = jnp.bfloat16

GRID_W = 64
HEAD_DIM = 64
N_HEADS = 4
GROUP_W = N_HEADS * HEAD_DIM
DA_QK = 32
SWA_WINDOW = 128
NA_WIN_ROWS = 8
NA_WIN_COLS = 16
NA_TILE_ROWS = 4
LOCAL_TILE = NA_TILE_ROWS * GRID_W
ROPE_BASE = 10000.0
TOP_K = 2
EPS = 1e-6
NEG = -1e30
LOG2E = 1.4426950408889634
LANES = 128
ONES_ROWS = 16
ROUTER_ROWS = 16
TOKEN_TILES = (768, 512, 256)
EXPERT_F_TILES = (1792, 512, 256)
DA_MAX_JUMP = 64.0
DA_CHUNK = 128
VMEM_LIMIT = 56 * 1024 * 1024

CB_DA_Q, CB_DA_K, CB_DA_V = 0, 1, 2
CB_SWA_Q = 3
CB_SWA_K128, CB_SWA_V128 = 8, 9
CB_NA_Q, CB_NA_K, CB_NA_V = 5, 6, 7
CB_RET_Q, CB_RET_K, CB_RET_V, CB_RET_G = 8, 9, 10, 11
IN_WIDTH = 3072


def _params(sem):
    return pltpu.CompilerParams(dimension_semantics=sem, vmem_limit_bytes=VMEM_LIMIT)


def _silu(v):
    return v / (1.0 + jnp.exp(-v))


def _head_mask(shape, head_w, h, dim=1):
    return lax.broadcasted_iota(jnp.int32, shape, dim) // head_w == h


def _lane_pick(a, b):
    lane = lax.broadcasted_iota(jnp.int32, a.shape, 1)
    return jnp.where(lane < HEAD_DIM, a, b)


def _per_head_full(vals):
    return jnp.concatenate([_lane_pick(vals[0], vals[1]), _lane_pick(vals[2], vals[3])], axis=1)


def _group_mean_sq(a):
    sq = a * a
    r = lax.broadcasted_iota(jnp.int32, (GROUP_W, GROUP_W), 0) // HEAD_DIM
    c = lax.broadcasted_iota(jnp.int32, (GROUP_W, GROUP_W), 1) // HEAD_DIM
    g = jnp.where(r == c, 1.0, 0.0).astype(BF16)
    hi = sq.astype(BF16)
    lo = (sq - hi.astype(F32)).astype(BF16)
    tot = jnp.dot(hi, g, preferred_element_type=F32) + jnp.dot(lo, g, preferred_element_type=F32)
    return tot * (1.0 / HEAD_DIM)


def _row_mod(mod_ref, rows_are_ctx, j):
    return jnp.where(rows_are_ctx, mod_ref[0, 1, j:j + 1, :], mod_ref[0, 0, j:j + 1, :])


def _mod_kernel(c_ref, w_ref, b_ref, o_ref):
    s = _silu(c_ref[...])
    o_ref[0] = jnp.dot(s.astype(BF16), w_ref[0].astype(BF16), preferred_element_type=F32) + b_ref[0]


def _modulation(c8, w_mod, b_mod):
    L, D, W = w_mod.shape
    tn = 1536
    return pl.pallas_call(
        _mod_kernel,
        grid=(L, W // tn),
        in_specs=[pl.BlockSpec((8, D), lambda l, j: (0, 0)),
                  pl.BlockSpec((1, D, tn), lambda l, j: (l, 0, j)),
                  pl.BlockSpec((1, 1, tn), lambda l, j: (l, 0, j))],
        out_specs=pl.BlockSpec((1, 8, tn), lambda l, j: (l, 0, j)),
        out_shape=jax.ShapeDtypeStruct((L, 8, W), F32),
        compiler_params=_params(("arbitrary", "arbitrary")),
        name="modulation",
    )(c8, w_mod, b_mod.reshape(L, 1, W))


def _lane_block_ops():
    da_s = DA_QK ** -0.5 * LOG2E
    s64 = HEAD_DIM ** -0.5
    ops = []
    ops += [("r32", da_s)] * 2 + [("r32", 1.0)] * 2 + [(None, 1.0)] * 2
    ops += [("r64", s64 * LOG2E)] * 2 + [("r64", 1.0)] + [(None, 1.0)]
    ops += [(None, s64 * LOG2E)] * 2 + [(None, 1.0)] * 4
    ops += [("r64", 1.0)] * 2 + [("r64", s64)] * 2 + [(None, 1.0)] * 4
    return ops


def _rope(x, cos, sin_signed, w):
    lane = lax.broadcasted_iota(jnp.int32, x.shape, 1)
    first = (lane % (2 * w)) < w
    xr = jnp.where(first, pltpu.roll(x, LANES - w, 1), pltpu.roll(x, w, 1))
    return x * cos + xr * sin_signed


def _inproj_kernel(x_ref, mod_ref, g_ref, w_ref, c32_ref, s32_ref, c64_ref, s64_ref, o_ref, qt_ref, vt_ref,
                   *, tm, seq):
    i = pl.program_id(1)
    x = x_ref[0]
    rows = i * tm + lax.broadcasted_iota(jnp.int32, (tm, 1), 0)
    is_ctx = rows >= seq
    ms = jnp.mean(x * x, axis=-1, keepdims=True)
    h = x * lax.rsqrt(ms + EPS) * g_ref[...]
    h = h * (1.0 + _row_mod(mod_ref, is_ctx, 1)) + _row_mod(mod_ref, is_ctx, 0)
    hb = h.astype(BF16)
    ops = _lane_block_ops()
    for cb in range(IN_WIDTH // 256):
        p = jnp.dot(hb, w_ref[:, cb * 256:(cb + 1) * 256], preferred_element_type=F32)
        halves = []
        for hf in range(2):
            kind, scale = ops[cb * 2 + hf]
            ph = p[:, hf * LANES:(hf + 1) * LANES]
            if kind == "r32":
                ph = _rope(ph, c32_ref[...], s32_ref[...], DA_QK // 4)
            elif kind == "r64":
                ph = _rope(ph, c64_ref[...], s64_ref[...], HEAD_DIM // 4)
            if scale != 1.0:
                ph = ph * scale
            halves.append(ph)
        full = jnp.concatenate(halves, axis=1)
        o_ref[0, :, cb * 256:(cb + 1) * 256] = full.astype(BF16)
        if cb == CB_DA_Q:
            qt_ref[0] = full.T.astype(BF16)
        elif cb == CB_DA_V:
            vt_ref[0] = full.T.astype(BF16)


def _inproj(xa, modv, g, w_in_b, tabs, seq, tm):
    B, T, D = xa.shape
    kern = functools.partial(_inproj_kernel, tm=tm, seq=seq)
    tab_spec = pl.BlockSpec((tm, LANES), lambda b, i: (i, 0))
    return pl.pallas_call(
        kern,
        grid=(B, T // tm),
        in_specs=[pl.BlockSpec((1, tm, D), lambda b, i: (b, i, 0)),
                  pl.BlockSpec((1, 2, 8, D), lambda b, i: (b, 0, 0, 0)),
                  pl.BlockSpec((1, D), lambda b, i: (0, 0)),
                  pl.BlockSpec((D, IN_WIDTH), lambda b, i: (0, 0)),
                  tab_spec, tab_spec, tab_spec, tab_spec],
        out_specs=[pl.BlockSpec((1, tm, IN_WIDTH), lambda b, i: (b, i, 0)),
                   pl.BlockSpec((1, GROUP_W, tm), lambda b, i: (b, 0, i)),
                   pl.BlockSpec((1, GROUP_W, tm), lambda b, i: (b, 0, i))],
        out_shape=[jax.ShapeDtypeStruct((B, T, IN_WIDTH), BF16),
                   jax.ShapeDtypeStruct((B, GROUP_W, T), BF16),
                   jax.ShapeDtypeStruct((B, GROUP_W, T), BF16)],
        compiler_params=_params(("arbitrary", "arbitrary")),
        name="inproj",
    )(xa, modv, g.reshape(1, D), w_in_b, *tabs)


def _rope_tables(seq, ctx):
    t = jnp.arange(seq)
    row = (t // GRID_W).astype(F32)
    col = (t % GRID_W).astype(F32)
    out = []
    for dh in (DA_QK, HEAD_DIM):
        half = dh // 2
        quarter = half // 2
        lane = jnp.arange(LANES)
        d = lane % dh
        use_col = (d // half) == 1
        idx = (d % quarter).astype(F32)
        inv = ROPE_BASE ** (-idx * 2.0 / half)
        pos = jnp.where(use_col[None, :], col[:, None], row[:, None])
        ang = pos * inv[None, :]
        first = (d % half) < quarter
        cos = jnp.cos(ang)
        sin = jnp.where(first[None, :], -jnp.sin(ang), jnp.sin(ang))
        cos = jnp.concatenate([cos, jnp.ones((ctx, LANES), F32)], axis=0)
        sin = jnp.concatenate([sin, jnp.zeros((ctx, LANES), F32)], axis=0)
        out += [cos, sin]
    return out


def _rows_per_head(vals, tq):
    return jnp.concatenate([jnp.broadcast_to(v, (HEAD_DIM, tq)) for v in vals], axis=0)


def _da_kernel(lamp_ref, qt_ref, k_ref, vt_ref, g_ref, o_ref, qm_sc, m_sc, l_sc, acc_sc, st_sc, p_sc,
               *, tq, lambda_init):
    ki = pl.program_id(2)
    nk = pl.num_programs(2)

    @pl.when(ki == 0)
    def _():
        qt = qt_ref[0]
        for j in range(2 * N_HEADS):
            qm = jnp.where(_head_mask(qt.shape, DA_QK, j, 0), qt, jnp.zeros_like(qt))
            qm_sc[j] = qm
            m_sc[j] = jnp.max(jnp.dot(k_ref[0, 0:DA_CHUNK, :], qm, preferred_element_type=F32), axis=0,
                              keepdims=True)
        l_sc[...] = jnp.zeros(l_sc.shape, F32)
        acc_sc[...] = jnp.zeros(acc_sc.shape, F32)

    k = k_ref[0]
    vt = vt_ref[0]
    ones = jnp.ones((ONES_ROWS, vt.shape[1]), BF16)

    tk = k.shape[0]

    def vt_ext(h):
        return jnp.concatenate([vt[h * HEAD_DIM:(h + 1) * HEAD_DIM, :], ones], axis=0)

    jump = None
    pvs, mxs = [], []
    for j in range(2 * N_HEADS):
        st = jnp.dot(k, qm_sc[j], preferred_element_type=F32)
        m_prev = m_sc[j]
        mx = jnp.max(st, axis=0, keepdims=True)
        p = jnp.exp2(st - m_prev).astype(BF16)
        pvs.append(jnp.dot(vt_ext(j // 2), p, preferred_element_type=F32))
        mxs.append(mx)
        worst = jnp.max(mx - m_prev)
        jump = worst if jump is None else jnp.maximum(jump, worst)
    keep = jump <= DA_MAX_JUMP
    for j in range(2 * N_HEADS):
        h, t = j // 2, j % 2
        rows = slice(h * HEAD_DIM, (h + 1) * HEAD_DIM)
        m_prev = m_sc[j]
        m_new = jnp.maximum(m_prev, mxs[j])
        alpha = jnp.exp2(m_prev - m_new)
        l_old = l_sc[j]
        acc_old = acc_sc[t, rows, :]
        l_sc[j] = jnp.where(keep, alpha * (l_old + pvs[j][HEAD_DIM:HEAD_DIM + 1, :]), l_old)
        acc_sc[t, rows, :] = jnp.where(keep, alpha * (acc_old + pvs[j][:HEAD_DIM, :]), acc_old)
        m_sc[j] = jnp.where(keep, m_new, m_prev)

    @pl.when(jnp.logical_not(keep))
    def _():
        for j in range(2 * N_HEADS):
            h, t = j // 2, j % 2
            rows = slice(h * HEAD_DIM, (h + 1) * HEAD_DIM)
            st_sc[...] = jnp.dot(k, qm_sc[j], preferred_element_type=F32)
            run = st_sc[0:DA_CHUNK, :]
            for c in range(1, tk // DA_CHUNK):
                run = jnp.maximum(run, st_sc[c * DA_CHUNK:(c + 1) * DA_CHUNK, :])
            m_prev = m_sc[j]
            m_new = jnp.maximum(m_prev, jnp.max(run, axis=0, keepdims=True))
            alpha = jnp.exp2(m_prev - m_new)
            for c in range(tk // DA_CHUNK):
                cr = slice(c * DA_CHUNK, (c + 1) * DA_CHUNK)
                p_sc[cr, :] = jnp.exp2(st_sc[cr, :] - m_new).astype(BF16)
            pv = jnp.dot(vt_ext(h), p_sc[...], preferred_element_type=F32)
            l_sc[j] = alpha * l_sc[j] + pv[HEAD_DIM:HEAD_DIM + 1, :]
            m_sc[j] = m_new
            acc_sc[t, rows, :] = alpha * acc_sc[t, rows, :] + pv[:HEAD_DIM, :]

    @pl.when(ki == nk - 1)
    def _():
        lp = lamp_ref[...]
        lam = (jnp.exp(jnp.sum(lp[0:1] * lp[1:2], axis=1, keepdims=True))
               - jnp.exp(jnp.sum(lp[2:3] * lp[3:4], axis=1, keepdims=True))) + lambda_init
        o0 = acc_sc[0] / _rows_per_head([l_sc[2 * h] for h in range(N_HEADS)], tq)
        o1 = acc_sc[1] / _rows_per_head([l_sc[2 * h + 1] for h in range(N_HEADS)], tq)
        a = (o0 - lam * o1).T
        y = a * lax.rsqrt(_group_mean_sq(a) + EPS) * g_ref[...]
        o_ref[0] = (y * (1.0 - lambda_init)).astype(BF16)


def _diff_attention(P, QT, VT, lamp, subln_full, lambda_init, q0, nq, tq, k0, nk, tk):
    B = P.shape[0]
    kern = functools.partial(_da_kernel, tq=tq, lambda_init=lambda_init)
    return pl.pallas_call(
        kern,
        grid=(B, nq, nk),
        in_specs=[pl.BlockSpec((8, LANES), lambda b, qi, ki: (0, 0)),
                  pl.BlockSpec((1, GROUP_W, tq), lambda b, qi, ki: (b, 0, q0 + qi)),
                  pl.BlockSpec((1, tk, GROUP_W), lambda b, qi, ki: (b, k0 + ki, CB_DA_K)),
                  pl.BlockSpec((1, GROUP_W, tk), lambda b, qi, ki: (b, 0, k0 + ki)),
                  pl.BlockSpec((1, GROUP_W), lambda b, qi, ki: (0, 0))],
        out_specs=pl.BlockSpec((1, tq, GROUP_W), lambda b, qi, ki: (b, qi, 0)),
        out_shape=jax.ShapeDtypeStruct((B, nq * tq, GROUP_W), BF16),
        scratch_shapes=[pltpu.VMEM((2 * N_HEADS, GROUP_W, tq), BF16),
                        pltpu.VMEM((2 * N_HEADS, 1, tq), F32),
                        pltpu.VMEM((2 * N_HEADS, 1, tq), F32),
                        pltpu.VMEM((2, GROUP_W, tq), F32),
                        pltpu.VMEM((tk, tq), F32),
                        pltpu.VMEM((tk, tq), BF16)],
        compiler_params=_params(("arbitrary", "arbitrary", "arbitrary")),
        name="diff_attention",
    )(lamp, QT, P, VT, subln_full)


def _swa_kernel(sink_ref, q_ref, kp_ref, kc_ref, kn_ref, kx_ref, vp_ref, vc_ref, vn_ref, vx_ref, o_ref,
                *, seq, blk):
    i = pl.program_id(0)
    nloc = blk + 2 * SWA_WINDOW
    nkeys = nloc + kx_ref.shape[1]
    qpos = i * blk + lax.broadcasted_iota(jnp.int32, (blk, nkeys), 0)
    c = lax.broadcasted_iota(jnp.int32, (blk, nkeys), 1)
    kpos = i * blk - SWA_WINDOW + c
    valid = ((kpos >= 0) & (kpos < seq) & (jnp.abs(qpos - kpos) <= SWA_WINDOW) & (qpos < seq)) | (c >= nloc)
    for b in range(q_ref.shape[0]):
        q = q_ref[b]
        kall = jnp.concatenate([kp_ref[b], kc_ref[b], kn_ref[b], kx_ref[b]], axis=0)
        vall = jnp.concatenate([vp_ref[b], vc_ref[b], vn_ref[b], vx_ref[b]], axis=0)
        qs = []
        for g in range(2):
            qg = q[:, g * LANES:(g + 1) * LANES]
            for kvh in range(2):
                qs.append(jnp.where(_head_mask(qg.shape, HEAD_DIM, kvh), qg, jnp.zeros_like(qg)))
        s = lax.dot_general(jnp.concatenate(qs, axis=0), kall, (((1,), (1,)), ((), ())),
                            preferred_element_type=F32)
        vms = [jnp.where(_head_mask(vall.shape, HEAD_DIM, kvh), vall, jnp.zeros_like(vall)) for kvh in range(2)]
        vm = jnp.concatenate(vms, axis=0)
        for g in range(2):
            ps = []
            for kvh in range(2):
                j = 2 * g + kvh
                sink = sink_ref[2 * kvh + g] * LOG2E
                sj = jnp.where(valid, s[j * blk:(j + 1) * blk], NEG)
                m = jnp.maximum(jnp.max(sj, axis=1, keepdims=True), sink)
                e = jnp.exp2(sj - m)
                den = jnp.sum(e, axis=1, keepdims=True) + jnp.exp2(sink - m)
                ps.append((e / den).astype(BF16))
            o_ref[b, :, g * LANES:(g + 1) * LANES] = jnp.dot(
                jnp.concatenate(ps, axis=1), vm, preferred_element_type=F32).astype(BF16)


def _na_kernel(q_ref, k0_ref, k1_ref, k2_ref, kx_ref, v0_ref, v1_ref, v2_ref, vx_ref, bias_ref, o_ref,
               *, tq, n_seq_tiles):
    ctx_tile = pl.program_id(0) >= n_seq_tiles
    nloc = 3 * tq
    for b in range(q_ref.shape[0]):
        q = q_ref[b]
        kall = jnp.concatenate([k0_ref[b], k1_ref[b], k2_ref[b], kx_ref[b]], axis=0)
        vall = jnp.concatenate([v0_ref[b], v1_ref[b], v2_ref[b], vx_ref[b]], axis=0)
        qs = jnp.concatenate(
            [jnp.where(_head_mask(q.shape, HEAD_DIM, h), q, jnp.zeros_like(q)) for h in range(N_HEADS)], axis=0)
        s = lax.dot_general(qs, kall, (((1,), (1,)), ((), ())), preferred_element_type=F32)
        ps = []
        for h in range(N_HEADS):
            sh = s[h * tq:(h + 1) * tq]
            s_loc = jnp.where(ctx_tile, NEG, sh[:, :nloc] + bias_ref[0, h])
            s_ctx = sh[:, nloc:]
            m = jnp.maximum(jnp.max(s_loc, axis=1, keepdims=True), jnp.max(s_ctx, axis=1, keepdims=True))
            e_loc = jnp.exp2(s_loc - m)
            e_ctx = jnp.exp2(s_ctx - m)
            den = jnp.sum(e_loc, axis=1, keepdims=True) + jnp.sum(e_ctx, axis=1, keepdims=True)
            ps.append(jnp.concatenate([e_loc / den, e_ctx / den], axis=1).astype(BF16))
        vm = jnp.concatenate(
            [jnp.where(_head_mask(vall.shape, HEAD_DIM, h), vall, jnp.zeros_like(vall)) for h in range(N_HEADS)],
            axis=0)
        o_ref[b] = jnp.dot(jnp.concatenate(ps, axis=1), vm, preferred_element_type=F32).astype(BF16)


def _na_bias_kernel(rt_ref, o_ref, sv_sc, *, rows):
    tr = NA_TILE_ROWS
    nrt = rows // tr
    nl = 3 * tr * GRID_W
    n_dcol = 2 * NA_WIN_COLS - 1
    rt = rt_ref[0] * LOG2E
    hi = rt.astype(BF16)
    r1 = rt - hi.astype(F32)
    mid = r1.astype(BF16)
    lo = (r1 - mid.astype(F32)).astype(BF16)
    lane = lax.broadcasted_iota(jnp.int32, (GRID_W, nl), 1)
    qc = lax.broadcasted_iota(jnp.int32, (GRID_W, nl), 0)
    kc = lane % GRID_W
    kr = lane // GRID_W
    dc = jnp.clip(kc - qc, 1 - NA_WIN_COLS, NA_WIN_COLS - 1) + (NA_WIN_COLS - 1)
    c0 = jnp.clip(qc - NA_WIN_COLS // 2, 0, GRID_W - NA_WIN_COLS)
    col_ok = (kc >= c0) & (kc < c0 + NA_WIN_COLS)
    irow = lax.broadcasted_iota(jnp.int32, (LANES, nl), 0)
    krl = lax.broadcasted_iota(jnp.int32, (LANES, nl), 1) // GRID_W
    for v, (t_idx, ws) in enumerate(((0, 0), (1, 0), (nrt - 1, nrt - 3))):
        for qr in range(tr):
            r = t_idx * tr + qr
            r0 = min(max(r - NA_WIN_ROWS // 2, 0), rows - NA_WIN_ROWS)
            d_row = jnp.clip(ws * tr + krl - r + (NA_WIN_ROWS - 1), 0, 2 * NA_WIN_ROWS - 2)
            onehot = jnp.where(irow == d_row, 1.0, 0.0).astype(BF16)
            sv_sc[...] = (jnp.dot(hi, onehot, preferred_element_type=F32)
                          + jnp.dot(mid, onehot, preferred_element_type=F32)
                          + jnp.dot(lo, onehot, preferred_element_type=F32))

            def pick(j, acc):
                return jnp.where(dc == j, sv_sc[pl.ds(j, 1), :], acc)

            acc = lax.fori_loop(0, n_dcol, pick, jnp.zeros((GRID_W, nl), F32), unroll=True)
            krow = ws * tr + kr
            ok = col_ok & (krow >= r0) & (krow < r0 + NA_WIN_ROWS)
            o_ref[v, 0, qr * GRID_W:(qr + 1) * GRID_W, :] = jnp.where(ok, acc, NEG)


def _na_bias_tables(rpb, rows):
    H, nr, ncol = rpb.shape
    rt = jnp.zeros((H, 32, LANES), F32).at[:, :ncol, :nr].set(jnp.swapaxes(rpb.astype(F32), 1, 2))
    tq = NA_TILE_ROWS * GRID_W
    return pl.pallas_call(
        functools.partial(_na_bias_kernel, rows=rows),
        grid=(H,),
        in_specs=[pl.BlockSpec((1, 32, LANES), lambda h: (h, 0, 0))],
        out_specs=pl.BlockSpec((3, 1, tq, 3 * tq), lambda h: (0, h, 0, 0)),
        out_shape=jax.ShapeDtypeStruct((3, H, tq, 3 * tq), F32),
        scratch_shapes=[pltpu.VMEM((32, 3 * tq), F32)],
        compiler_params=_params(("arbitrary",)),
        name="na_bias",
    )(rt)


def _ret_direction(q, k, v, state_ref, lgs_ref, lgv, forward, base):
    C = q.shape[0]
    i = lax.broadcasted_iota(jnp.int32, (C, C), 0)
    j = lax.broadcasted_iota(jnp.int32, (C, C), 1)
    dist = (i - j if forward else j - i).astype(F32)
    keep = dist >= 0 if forward else dist > 0
    dist = jnp.where(keep, dist, 0.0)
    qs = jnp.concatenate(
        [jnp.where(_head_mask(q.shape, HEAD_DIM, h), q, jnp.zeros_like(q)) for h in range(N_HEADS)], axis=0)
    s = lax.dot_general(qs, k, (((1,), (1,)), ((), ())), preferred_element_type=F32)
    atts = []
    for h in range(N_HEADS):
        decay = jnp.where(keep, jnp.exp(dist * lgs_ref[base + h]), 0.0)
        atts.append((s[h * C:(h + 1) * C] * decay).astype(BF16))
    vm = jnp.concatenate(
        [jnp.where(_head_mask(v.shape, HEAD_DIM, h), v, jnp.zeros_like(v)) for h in range(N_HEADS)], axis=0)
    intra = jnp.dot(jnp.concatenate(atts, axis=1), vm, preferred_element_type=F32)
    r = lax.broadcasted_iota(jnp.int32, (C, 1), 0).astype(F32)
    xi = jnp.exp((r + 1.0 if forward else C - r) * lgv)
    zeta = jnp.exp((C - 1.0 - r if forward else r) * lgv)
    state = state_ref[...]
    cross = jnp.dot((q.astype(F32) * xi).astype(BF16), state.astype(BF16), preferred_element_type=F32)
    kz_t = (k.astype(F32) * zeta).T.astype(BF16)
    u = jnp.dot(kz_t, v, preferred_element_type=F32)
    rr = lax.broadcasted_iota(jnp.int32, u.shape, 0) // HEAD_DIM
    cc = lax.broadcasted_iota(jnp.int32, u.shape, 1) // HEAD_DIM
    state_ref[...] = jnp.where(rr == cc, jnp.exp(C * lgv) * state + u, 0.0)
    return intra + cross


def _ret_kernel(lgs_ref, lgv_ref, qf_ref, kf_ref, vf_ref, qb_ref, kb_ref, vb_ref, of_ref, ob_ref, sf_sc, sb_sc):
    @pl.when(pl.program_id(0) == 0)
    def _():
        sf_sc[...] = jnp.zeros(sf_sc.shape, F32)
        sb_sc[...] = jnp.zeros(sb_sc.shape, F32)

    for b in range(qf_ref.shape[0]):
        of_ref[b] = _ret_direction(qf_ref[b], kf_ref[b], vf_ref[b], sf_sc.at[b], lgs_ref, lgv_ref[0:1], True, 0)
        ob_ref[b] = _ret_direction(qb_ref[b], kb_ref[b], vb_ref[b], sb_sc.at[b], lgs_ref, lgv_ref[1:2], False,
                                   N_HEADS)


def _local_kernel(sink_ref, lgs_ref, lgv_ref, *refs, seq, tile, n_seq_tiles):
    swa_in, na_in, ret_in = refs[0:9], refs[9:19], refs[19:25]
    yb_ref, yn_ref, of_ref, ob_ref, sf_sc, sb_sc = refs[25:]
    _swa_kernel(sink_ref, *swa_in, yb_ref, seq=seq, blk=tile)
    _na_kernel(*na_in, yn_ref, tq=tile, n_seq_tiles=n_seq_tiles)
    _ret_kernel(lgs_ref, lgv_ref, *ret_in, of_ref, ob_ref, sf_sc, sb_sc)


def _local_mixers(P, sink, bias_tabs, lgs, lgv, seq, ctx):
    B, T, _ = P.shape
    tile = LOCAL_TILE
    W = SWA_WINDOW
    nt, nst, nw = T // tile, seq // tile, T // W
    nc = ctx // tile
    kern = functools.partial(_local_kernel, seq=seq, tile=tile, n_seq_tiles=nst)

    def rows(cb, width=GROUP_W, index=lambda i: i, size=tile):
        return pl.BlockSpec((B, size, width), lambda i: (0, index(i), cb))

    def ctx_rows(cb, width):
        return pl.BlockSpec((B, ctx, width), lambda i: (0, seq // ctx, cb))

    before = lambda i: jnp.clip(i * (tile // W) - 1, 0, nw - 1)
    after = lambda i: jnp.clip((i + 1) * (tile // W), 0, nw - 1)
    swa = [rows(CB_SWA_Q)]
    for cb in (CB_SWA_K128, CB_SWA_V128):
        swa += [rows(cb, LANES, before, W), rows(cb, LANES), rows(cb, LANES, after, W), ctx_rows(cb, LANES)]

    win = lambda i: jnp.clip(i - 1, 0, nst - 3)
    na = [rows(CB_NA_Q)]
    for cb in (CB_NA_K, CB_NA_V):
        na += [rows(cb, index=lambda i, o=o: win(i) + o) for o in range(3)] + [ctx_rows(cb, GROUP_W)]
    na.append(pl.BlockSpec((1, N_HEADS, tile, 3 * tile), lambda i: (jnp.where(i >= nst, 0, i - win(i)), 0, 0, 0)))

    fwd = lambda n: jnp.where(n < nc, nst + n, n - nc)
    bwd = lambda n: nt - 1 - n
    ret = [rows(cb, index=order) for order in (fwd, bwd) for cb in (CB_RET_Q, CB_RET_K, CB_RET_V)]

    tok = lambda order: pl.BlockSpec((B, tile, GROUP_W), lambda i: (0, order(i), 0))
    ident = lambda i: i
    return pl.pallas_call(
        kern,
        grid=(nt,),
        in_specs=[pl.BlockSpec(memory_space=pltpu.SMEM), pl.BlockSpec(memory_space=pltpu.SMEM),
                  pl.BlockSpec((8, GROUP_W), lambda i: (0, 0))] + swa + na + ret,
        out_specs=[tok(ident), tok(ident), tok(fwd), tok(bwd)],
        out_shape=[jax.ShapeDtypeStruct((B, T, GROUP_W), BF16)] * 2 + [jax.ShapeDtypeStruct((B, T, GROUP_W), F32)] * 2,
        scratch_shapes=[pltpu.VMEM((B, GROUP_W, GROUP_W), F32)] * 2,
        compiler_params=_params(("arbitrary",)),
        name="local_mixers",
    )(sink, lgs, lgv, *([P] * 9), *([P] * 9), bias_tabs, *([P] * 6))


def _outproj_kernel(*refs, tm, seq, n_exp):
    with_router = n_exp > 0
    if with_router:
        (ya_ref, yb_ref, yn_ref, of_ref, ob_ref, gt_ref, x_ref, mod_ref, w_ref, gpost_ref, gpre_ref, r_ref,
         x1_ref, h2_ref, eid_ref, gw_ref) = refs
    else:
        (ya_ref, yb_ref, yn_ref, of_ref, ob_ref, gt_ref, x_ref, mod_ref, w_ref, gpost_ref, gpre_ref,
         x1_ref, h2_ref) = refs
    i = pl.program_id(1)
    rows = i * tm + lax.broadcasted_iota(jnp.int32, (tm, 1), 0)
    is_ctx = rows >= seq
    o = of_ref[0] + ob_ref[0]
    yr = o * lax.rsqrt(_group_mean_sq(o) + EPS) * _silu(gt_ref[0].astype(F32))
    y = (jnp.dot(ya_ref[0], w_ref[0:256, :], preferred_element_type=F32)
         + jnp.dot(yb_ref[0], w_ref[256:512, :], preferred_element_type=F32)
         + jnp.dot(yn_ref[0], w_ref[512:768, :], preferred_element_type=F32)
         + jnp.dot(yr.astype(BF16), w_ref[768:1024, :], preferred_element_type=F32))
    yn = y * lax.rsqrt(jnp.mean(y * y, axis=-1, keepdims=True) + EPS) * gpost_ref[...]
    x1 = x_ref[0] + _row_mod(mod_ref, is_ctx, 2) * yn
    x1_ref[0] = x1
    h = x1 * lax.rsqrt(jnp.mean(x1 * x1, axis=-1, keepdims=True) + EPS) * gpre_ref[...]
    h = h * (1.0 + _row_mod(mod_ref, is_ctx, 4)) + _row_mod(mod_ref, is_ctx, 3)
    h2_ref[0] = h.astype(h2_ref.dtype)
    if with_router:
        r = r_ref[...]
        r_hi = r.astype(BF16)
        r_lo = (r - r_hi.astype(F32)).astype(BF16)
        h_hi = h.astype(BF16)
        h_lo = (h - h_hi.astype(F32)).astype(BF16)
        dn = (((1,), (1,)), ((), ()))
        lt = (lax.dot_general(r_hi, h_hi, dn, preferred_element_type=F32)
              + lax.dot_general(r_hi, h_lo, dn, preferred_element_type=F32)
              + lax.dot_general(r_lo, h_hi, dn, preferred_element_type=F32))
        e = lax.broadcasted_iota(jnp.int32, lt.shape, 0)
        lt = jnp.where(e < n_exp, lt, NEG)
        v1 = jnp.max(lt, axis=0, keepdims=True)
        i1 = jnp.min(jnp.where(lt == v1, e, ROUTER_ROWS), axis=0, keepdims=True)
        rest = jnp.where(e == i1, NEG, lt)
        v2 = jnp.max(rest, axis=0, keepdims=True)
        i2 = jnp.min(jnp.where(rest == v2, e, ROUTER_ROWS), axis=0, keepdims=True)
        e2 = jnp.exp(v2 - v1)
        w1 = 1.0 / (1.0 + e2)
        w2 = e2 / (1.0 + e2)
        eid_ref[0] = jnp.where(e == 0, i1, jnp.where(e == 1, i2, 0))[:8]
        wrow = jnp.where(e == 0, w1, jnp.where(e == 1, w2, 0.0))
        gw_ref[0] = jnp.concatenate([wrow, jnp.zeros((LANES - ROUTER_ROWS, tm), F32)], axis=0).T


def _outproj(ya, yb, yn, of, ob, P, xa, modv, w_out_b, g_post, g_pre2, rows, seq, tm, router=None, n_exp=0):
    B, _, D = xa.shape
    with_router = router is not None
    kern = functools.partial(_outproj_kernel, tm=tm, seq=seq, n_exp=n_exp)
    tok = lambda w: pl.BlockSpec((1, tm, w), lambda b, i: (b, i, 0))
    in_specs = [tok(GROUP_W), tok(GROUP_W), tok(GROUP_W), tok(GROUP_W), tok(GROUP_W),
                pl.BlockSpec((1, tm, GROUP_W), lambda b, i: (b, i, CB_RET_G)),
                tok(D),
                pl.BlockSpec((1, 2, 8, D), lambda b, i: (b, 0, 0, 0)),
                pl.BlockSpec((D, D), lambda b, i: (0, 0)),
                pl.BlockSpec((1, D), lambda b, i: (0, 0)),
                pl.BlockSpec((1, D), lambda b, i: (0, 0))]
    args = [ya, yb, yn, of, ob, P, xa, modv, w_out_b, g_post.reshape(1, D), g_pre2.reshape(1, D)]
    out_specs = [tok(D), tok(D)]
    out_shape = [jax.ShapeDtypeStruct((B, rows, D), F32),
                 jax.ShapeDtypeStruct((B, rows, D), F32 if with_router else BF16)]
    if with_router:
        in_specs.append(pl.BlockSpec((ROUTER_ROWS, D), lambda b, i: (0, 0)))
        args.append(router)
        out_specs += [pl.BlockSpec((1, 8, tm), lambda b, i: (b, 0, i)), tok(LANES)]
        out_shape += [jax.ShapeDtypeStruct((B, 8, rows), jnp.int32),
                      jax.ShapeDtypeStruct((B, rows, LANES), F32)]
    return pl.pallas_call(
        kern,
        grid=(B, rows // tm),
        in_specs=in_specs,
        out_specs=out_specs,
        out_shape=out_shape,
        compiler_params=_params(("arbitrary", "arbitrary")),
        name="outproj_router" if with_router else "outproj",
    )(*args)


def _ffn_kernel(h_ref, x_ref, mod_ref, wg_ref, wu_ref, wd_ref, gpost_ref, o_ref, *, tm, seq, fc):
    i = pl.program_id(1)
    rows = i * tm + lax.broadcasted_iota(jnp.int32, (tm, 1), 0)
    is_ctx = rows >= seq
    h = h_ref[0]
    acc = jnp.zeros((tm, o_ref.shape[-1]), F32)
    for c in range(wg_ref.shape[1] // fc):
        g = jnp.dot(h, wg_ref[:, c * fc:(c + 1) * fc], preferred_element_type=F32)
        u = jnp.dot(h, wu_ref[:, c * fc:(c + 1) * fc], preferred_element_type=F32)
        a = (_silu(g) * u).astype(BF16)
        acc = acc + jnp.dot(a, wd_ref[c * fc:(c + 1) * fc, :], preferred_element_type=F32)
    yn = acc * lax.rsqrt(jnp.mean(acc * acc, axis=-1, keepdims=True) + EPS) * gpost_ref[...]
    o_ref[0] = x_ref[0] + _row_mod(mod_ref, is_ctx, 5) * yn


def _dense_ffn(h2, x1, modv, wg, wu, wd, g_post, seq, tm):
    B, T, D = x1.shape
    F = wg.shape[1]
    kern = functools.partial(_ffn_kernel, tm=tm, seq=seq, fc=256)
    const = lambda shape: pl.BlockSpec(shape, lambda b, i: (0, 0), pipeline_mode=pl.Buffered(1))
    return pl.pallas_call(
        kern,
        grid=(B, T // tm),
        in_specs=[pl.BlockSpec((1, tm, D), lambda b, i: (b, i, 0)),
                  pl.BlockSpec((1, tm, D), lambda b, i: (b, i, 0)),
                  pl.BlockSpec((1, 2, 8, D), lambda b, i: (b, 0, 0, 0)),
                  const((D, F)), const((D, F)), const((F, D)),
                  pl.BlockSpec((1, D), lambda b, i: (0, 0))],
        out_specs=pl.BlockSpec((1, tm, D), lambda b, i: (b, i, 0)),
        out_shape=jax.ShapeDtypeStruct((B, T, D), F32),
        compiler_params=_params(("arbitrary", "arbitrary")),
        name="dense_ffn",
    )(h2, x1, modv, wg, wu, wd, g_post.reshape(1, D))


def _row_copy(src_hbm, row, dst_vmem, r, sem):
    return pltpu.make_async_copy(src_hbm.at[pl.ds(row, 1)], dst_vmem.at[pl.ds(r, 1)], sem)


def _moe_ffn_kernel(te_ref, tv_ref, src0_ref, srcn_ref, h_hbm, wg_ref, wu_ref, wd_ref, o_ref,
                    xg_sc, xb_sc, acc_sc, sem, *, tr):
    t = pl.program_id(0)
    f = pl.program_id(1)
    nt = pl.num_programs(0)
    nf = pl.num_programs(1)
    slot = t % 2

    def start_gather(src_ref, s):
        def body(r, c):
            _row_copy(h_hbm, src_ref[0, 0, r], xg_sc.at[s], r, sem.at[s]).start()
            return c

        lax.fori_loop(0, tr, body, 0, unroll=8)

    @pl.when((f == 0) & (t == 0) & (tv_ref[0] > 0))
    def _():
        start_gather(src0_ref, 0)

    @pl.when(f == 0)
    def _():
        acc_sc[...] = jnp.zeros(acc_sc.shape, F32)

    @pl.when((f == 0) & (tv_ref[t] > 0))
    def _():
        pltpu.make_async_copy(h_hbm.at[pl.ds(0, tr)], xg_sc.at[slot], sem.at[slot]).wait()
        xb_sc[...] = xg_sc[slot].astype(BF16)

    @pl.when((f == 0) & (t + 1 < nt) & (tv_ref[jnp.minimum(t + 1, nt - 1)] > 0))
    def _():
        start_gather(srcn_ref, 1 - slot)

    @pl.when(tv_ref[t] > 0)
    def _():
        x = xb_sc[...]
        g = jnp.dot(x, wg_ref[0], preferred_element_type=F32)
        u = jnp.dot(x, wu_ref[0], preferred_element_type=F32)
        a = (_silu(g) * u).astype(BF16)
        acc_sc[...] += jnp.dot(a, wd_ref[0], preferred_element_type=F32)

    @pl.when(f == nf - 1)
    def _():
        o_ref[...] = acc_sc[...]


def _moe_ffn(h_flat, src, tile_expert, tile_valid, wg, wu, wd, n_rows, tr, tf):
    D = h_flat.shape[1]
    F = wg.shape[2]
    nt = n_rows // tr
    return pl.pallas_call(
        functools.partial(_moe_ffn_kernel, tr=tr),
        grid_spec=pltpu.PrefetchScalarGridSpec(
            num_scalar_prefetch=2,
            grid=(nt, F // tf),
            in_specs=[pl.BlockSpec((1, 1, tr), lambda t, f, te, tv: (0, 0, 0), memory_space=pltpu.SMEM),
                      pl.BlockSpec((1, 1, tr), lambda t, f, te, tv: (jnp.minimum(t + 1, nt - 1), 0, 0),
                                   memory_space=pltpu.SMEM),
                      pl.BlockSpec(memory_space=pl.ANY),
                      pl.BlockSpec((1, D, tf), lambda t, f, te, tv: (te[t], 0, f)),
                      pl.BlockSpec((1, D, tf), lambda t, f, te, tv: (te[t], 0, f)),
                      pl.BlockSpec((1, tf, D), lambda t, f, te, tv: (te[t], f, 0))],
            out_specs=pl.BlockSpec((tr, D), lambda t, f, te, tv: (t, 0)),
            scratch_shapes=[pltpu.VMEM((2, tr, D), h_flat.dtype), pltpu.VMEM((tr, D), BF16),
                            pltpu.VMEM((tr, D), F32), pltpu.SemaphoreType.DMA((2,))]),
        out_shape=jax.ShapeDtypeStruct((n_rows, D), F32),
        compiler_params=_params(("arbitrary", "arbitrary")),
        name="moe_ffn",
    )(tile_expert, tile_valid, src.reshape(nt, 1, tr), src.reshape(nt, 1, tr), h_flat, wg, wu, wd)


def _combine_kernel(pos_ref, ys_hbm, gw_ref, x_ref, mod_ref, gpost_ref, o_ref, y1_sc, y2_sc, sem, *, tm, seq):
    def issue(r, c):
        _row_copy(ys_hbm, pos_ref[0, 0, r], y1_sc, r, sem.at[0]).start(priority=0)
        _row_copy(ys_hbm, pos_ref[0, 1, r], y2_sc, r, sem.at[1]).start(priority=1)
        return c

    lax.fori_loop(0, tm, issue, 0, unroll=8)
    pltpu.make_async_copy(ys_hbm.at[pl.ds(0, tm)], y1_sc, sem.at[0]).wait()
    pltpu.make_async_copy(ys_hbm.at[pl.ds(0, tm)], y2_sc, sem.at[1]).wait()
    rows = pl.program_id(1) * tm + lax.broadcasted_iota(jnp.int32, (tm, 1), 0)
    gw = gw_ref[0]
    y = gw[:, 0:1] * y1_sc[...] + gw[:, 1:2] * y2_sc[...]
    yn = y * lax.rsqrt(jnp.mean(y * y, axis=-1, keepdims=True) + EPS) * gpost_ref[...]
    o_ref[0] = x_ref[0] + _row_mod(mod_ref, rows >= seq, 5) * yn


def _moe_combine(pos, ys, gw, x1, modv, g_post, seq, tm):
    B, R, D = x1.shape
    ntile = R // tm
    kern = functools.partial(_combine_kernel, tm=tm, seq=seq)
    return pl.pallas_call(
        kern,
        grid=(B, ntile),
        in_specs=[pl.BlockSpec((1, TOP_K, tm), lambda b, i: (b * ntile + i, 0, 0), memory_space=pltpu.SMEM),
                  pl.BlockSpec(memory_space=pl.ANY),
                  pl.BlockSpec((1, tm, LANES), lambda b, i: (b, i, 0)),
                  pl.BlockSpec((1, tm, D), lambda b, i: (b, i, 0)),
                  pl.BlockSpec((1, 2, 8, D), lambda b, i: (b, 0, 0, 0)),
                  pl.BlockSpec((1, D), lambda b, i: (0, 0))],
        out_specs=pl.BlockSpec((1, tm, D), lambda b, i: (b, i, 0)),
        out_shape=jax.ShapeDtypeStruct((B, R, D), F32),
        scratch_shapes=[pltpu.VMEM((tm, D), F32), pltpu.VMEM((tm, D), F32), pltpu.SemaphoreType.DMA((2,))],
        compiler_params=_params(("arbitrary", "arbitrary")),
        name="moe_combine",
    )(pos.reshape(B * ntile, tm, TOP_K).transpose(0, 2, 1), ys, gw, x1, modv, g_post.reshape(1, D))


def _moe_routing(eid, n_exp, tr):
    N = eid.shape[0]
    e_flat = eid.reshape(-1)
    onehot = (e_flat[:, None] == jnp.arange(n_exp)[None, :]).astype(jnp.int32)
    rank = jnp.sum((jnp.cumsum(onehot, axis=0) - 1) * onehot, axis=1)
    counts = jnp.sum(onehot, axis=0)
    padded = ((counts + tr - 1) // tr) * tr
    ends = jnp.cumsum(padded)
    starts = ends - padded
    dest = starts[e_flat] + rank
    n_rows = (TOP_K * N // tr + n_exp) * tr
    src = jnp.zeros((n_rows,), jnp.int32).at[dest].set(jnp.arange(TOP_K * N, dtype=jnp.int32) // TOP_K)
    tile_start = jnp.arange(n_rows // tr, dtype=jnp.int32) * tr
    tile_expert = jnp.minimum(jnp.sum((tile_start[:, None] >= ends[None, :]).astype(jnp.int32), axis=1), n_exp - 1)
    tile_valid = (tile_start < ends[-1]).astype(jnp.int32)
    pos = dest.reshape(N, TOP_K).astype(jnp.int32)
    return src, pos, tile_expert, tile_valid, n_rows


def _swa_perm():
    idx = []
    for g in range(2):
        for kvh in range(2):
            idx += [(kvh * 2 + g) * HEAD_DIM + d for d in range(HEAD_DIM)]
    return jnp.array(idx, jnp.int32)


def kernel(x, c, ctx, c_ctx, w_mod, b_mod, g_attn_pre, g_attn_post, g_ffn_pre, g_ffn_post, w_in, w_out,
           da_lambda_q1, da_lambda_k1, da_lambda_q2, da_lambda_k2, da_subln, swa_sink, na_rpb,
           ret_gamma_fwd, ret_gamma_bwd, ffn_w_gate, ffn_w_up, ffn_w_down,
           moe_router, moe_w_gate, moe_w_up, moe_w_down):
    B, S, D = x.shape
    CTX = ctx.shape[1]
    T = S + CTX
    L = w_mod.shape[0]
    n_exp = moe_router.shape[-1]
    rows = S // GRID_W

    def token_tile(n_rows):
        return next(t for t in TOKEN_TILES if n_rows % t == 0)

    c8 = jnp.zeros((8, D), F32).at[:B].set(c).at[B].set(c_ctx)
    mod = _modulation(c8, w_mod, b_mod).reshape(L, 8, 6, D)
    mod = jnp.pad(mod, ((0, 0), (0, 0), (0, 2), (0, 0)))
    modv = jnp.stack([mod[:, :B], jnp.broadcast_to(mod[:, B:B + 1], (L, B, 8, D))], axis=2)

    tabs = _rope_tables(S, CTX)
    perm = _swa_perm()
    swa0 = CB_SWA_Q * 256
    xa = jnp.concatenate([x, ctx], axis=1)

    for l in range(L):
        lambda_init = 0.8 - 0.6 * math.exp(-0.3 * l)
        last = l == L - 1
        w_in_l = w_in[l]
        w_in_b = jnp.concatenate([w_in_l[:, :swa0], w_in_l[:, swa0:swa0 + 256][:, perm], w_in_l[:, swa0 + 256:]],
                                 axis=1).astype(BF16)
        w_out_l = w_out[l]
        w_out_b = jnp.concatenate([w_out_l[:256], w_out_l[256:512][perm], w_out_l[512:]], axis=0).astype(BF16)

        P, QT, VT = _inproj(xa, modv[l], g_attn_pre[l], w_in_b, tabs, S, token_tile(T))

        lamp = jnp.zeros((8, LANES), F32)
        for r, v in enumerate((da_lambda_q1[l], da_lambda_k1[l], da_lambda_q2[l], da_lambda_k2[l])):
            lamp = lamp.at[r, :DA_QK].set(v)
        subln_full = jnp.tile(da_subln[l], N_HEADS).reshape(1, GROUP_W)
        tq_da = next(t for t in (1024, 512, 256) if S % t == 0)
        tk_da = next(t for t in (768, 512, 256) if T % t == 0)
        ya = _diff_attention(P, QT, VT, lamp, subln_full, lambda_init, 0, S // tq_da, tq_da, 0, T // tk_da, tk_da)
        if not last:
            ya_ctx = _diff_attention(P, QT, VT, lamp, subln_full, lambda_init, S // CTX, 1, CTX, S // CTX, 1, CTX)
            ya = jnp.concatenate([ya, ya_ctx], axis=1)
        lg = jnp.stack([jax.nn.log_sigmoid(ret_gamma_fwd[l].astype(F32)),
                        jax.nn.log_sigmoid(ret_gamma_bwd[l].astype(F32))])
        lgv = jnp.zeros((8, GROUP_W), F32).at[:2].set(jnp.repeat(lg, HEAD_DIM, axis=1))
        yb, yn, of, ob = _local_mixers(P, swa_sink[l].astype(F32), _na_bias_tables(na_rpb[l], rows),
                                       lg.reshape(-1), lgv, S, CTX)

        R = S if last else T
        e = l // 2
        if l % 2 == 0:
            x1, h2 = _outproj(ya, yb, yn, of, ob, P, xa, modv[l], w_out_b, g_attn_post[l], g_ffn_pre[l],
                              R, S, token_tile(R))
            xa = _dense_ffn(h2, x1, modv[l], ffn_w_gate[e].astype(BF16), ffn_w_up[e].astype(BF16),
                            ffn_w_down[e].astype(BF16), g_ffn_post[l], S, token_tile(R))
        else:
            router = jnp.zeros((ROUTER_ROWS, D), F32).at[:n_exp].set(moe_router[e].T)
            x1, h2, eid, gw = _outproj(ya, yb, yn, of, ob, P, xa, modv[l], w_out_b, g_attn_post[l],
                                       g_ffn_pre[l], R, S, token_tile(R), router=router, n_exp=n_exp)
            tr = 512
            src, pos, tile_expert, tile_valid, n_rows = _moe_routing(
                jnp.swapaxes(eid[:, :TOP_K, :], 1, 2).reshape(B * R, TOP_K), n_exp, tr)
            ys = _moe_ffn(h2.reshape(B * R, D), src, tile_expert, tile_valid, moe_w_gate[e].astype(BF16),
                          moe_w_up[e].astype(BF16), moe_w_down[e].astype(BF16), n_rows, tr,
                          next(t for t in EXPERT_F_TILES if moe_w_gate.shape[-1] % t == 0))
            xa = _moe_combine(pos, ys, gw, x1, modv[l], g_ffn_post[l], S,
                              next(t for t in (512, 256) if R % t == 0))
    return xa[:, :S]
```

```python
import functools
import math

import jax
import jax.numpy as jnp
from jax import lax
from jax.experimental import pallas as pl
from jax.experimental.pallas import tpu as pltpu

F32 = jnp.float32
BF16 = jnp.bfloat16

GRID_W = 64
HEAD_DIM = 64
N_HEADS = 4
GROUP_W = N_HEADS * HEAD_DIM
DA_QK = 32
SWA_WINDOW = 128
NA_WIN_ROWS = 8
NA_WIN_COLS = 16
NA_TILE_ROWS = 4
LOCAL_TILE = NA_TILE_ROWS * GRID_W
ROPE_BASE = 10000.0
TOP_K = 2
EPS = 1e-6
NEG = -1e30
LOG2E = 1.4426950408889634
LANES = 128
ONES_ROWS = 16
ROUTER_ROWS = 16
TOKEN_TILES = (768, 512, 256)
EXPERT_F_TILES = (1792, 512, 256)
DA_MAX_JUMP = 64.0
DA_CHUNK = 128
VMEM_LIMIT = 56 * 1024 * 1024

CB_DA_Q, CB_DA_K, CB_DA_V = 0, 1, 2
CB_SWA_Q = 3
CB_SWA_K128, CB_SWA_V128 = 8, 9
CB_NA_Q, CB_NA_K, CB_NA_V = 5, 6, 7
CB_RET_Q, CB_RET_K, CB_RET_V, CB_RET_G = 8, 9, 10, 11
IN_WIDTH = 3072


def _params(sem):
    return pltpu.CompilerParams(dimension_semantics=sem, vmem_limit_bytes=VMEM_LIMIT)


def _silu(v):
    return v / (1.0 + jnp.exp(-v))


def _head_mask(shape, head_w, h, dim=1):
    return lax.broadcasted_iota(jnp.int32, shape, dim) // head_w == h


def _lane_pick(a, b):
    lane = lax.broadcasted_iota(jnp.int32, a.shape, 1)
    return jnp.where(lane < HEAD_DIM, a, b)


def _per_head_full(vals):
    return jnp.concatenate([_lane_pick(vals[0], vals[1]), _lane_pick(vals[2], vals[3])], axis=1)


def _group_mean_sq(a):
    sq = a * a
    r = lax.broadcasted_iota(jnp.int32, (GROUP_W, GROUP_W), 0) // HEAD_DIM
    c = lax.broadcasted_iota(jnp.int32, (GROUP_W, GROUP_W), 1) // HEAD_DIM
    g = jnp.where(r == c, 1.0, 0.0).astype(BF16)
    hi = sq.astype(BF16)
    lo = (sq - hi.astype(F32)).astype(BF16)
    tot = jnp.dot(hi, g, preferred_element_type=F32) + jnp.dot(lo, g, preferred_element_type=F32)
    return tot * (1.0 / HEAD_DIM)


def _row_mod(mod_ref, rows_are_ctx, j):
    return jnp.where(rows_are_ctx, mod_ref[0, 1, j:j + 1, :], mod_ref[0, 0, j:j + 1, :])


def _mod_kernel(c_ref, w_ref, b_ref, o_ref):
    s = _silu(c_ref[...])
    o_ref[0] = jnp.dot(s.astype(BF16), w_ref[0].astype(BF16), preferred_element_type=F32) + b_ref[0]


def _modulation(c8, w_mod, b_mod):
    L, D, W = w_mod.shape
    tn = 1536
    return pl.pallas_call(
        _mod_kernel,
        grid=(L, W // tn),
        in_specs=[pl.BlockSpec((8, D), lambda l, j: (0, 0)),
                  pl.BlockSpec((1, D, tn), lambda l, j: (l, 0, j)),
                  pl.BlockSpec((1, 1, tn), lambda l, j: (l, 0, j))],
        out_specs=pl.BlockSpec((1, 8, tn), lambda l, j: (l, 0, j)),
        out_shape=jax.ShapeDtypeStruct((L, 8, W), F32),
        compiler_params=_params(("arbitrary", "arbitrary")),
        name="modulation",
    )(c8, w_mod, b_mod.reshape(L, 1, W))


def _lane_block_ops():
    da_s = DA_QK ** -0.5 * LOG2E
    s64 = HEAD_DIM ** -0.5
    ops = []
    ops += [("r32", da_s)] * 2 + [("r32", 1.0)] * 2 + [(None, 1.0)] * 2
    ops += [("r64", s64 * LOG2E)] * 2 + [("r64", 1.0)] + [(None, 1.0)]
    ops += [(None, s64 * LOG2E)] * 2 + [(None, 1.0)] * 4
    ops += [("r64", 1.0)] * 2 + [("r64", s64)] * 2 + [(None, 1.0)] * 4
    return ops


def _rope(x, cos, sin_signed, w):
    lane = lax.broadcasted_iota(jnp.int32, x.shape, 1)
    first = (lane % (2 * w)) < w
    xr = jnp.where(first, pltpu.roll(x, LANES - w, 1), pltpu.roll(x, w, 1))
    return x * cos + xr * sin_signed


def _inproj_kernel(x_ref, mod_ref, g_ref, w_ref, c32_ref, s32_ref, c64_ref, s64_ref, o_ref, qt_ref, vt_ref,
                   *, tm, seq):
    i = pl.program_id(1)
    x = x_ref[0]
    rows = i * tm + lax.broadcasted_iota(jnp.int32, (tm, 1), 0)
    is_ctx = rows >= seq
    ms = jnp.mean(x * x, axis=-1, keepdims=True)
    h = x * lax.rsqrt(ms + EPS) * g_ref[...]
    h = h * (1.0 + _row_mod(mod_ref, is_ctx, 1)) + _row_mod(mod_ref, is_ctx, 0)
    hb = h.astype(BF16)
    ops = _lane_block_ops()
    for cb in range(IN_WIDTH // 256):
        p = jnp.dot(hb, w_ref[:, cb * 256:(cb + 1) * 256], preferred_element_type=F32)
        halves = []
        for hf in range(2):
            kind, scale = ops[cb * 2 + hf]
            ph = p[:, hf * LANES:(hf + 1) * LANES]
            if kind == "r32":
                ph = _rope(ph, c32_ref[...], s32_ref[...], DA_QK // 4)
            elif kind == "r64":
                ph = _rope(ph, c64_ref[...], s64_ref[...], HEAD_DIM // 4)
            if scale != 1.0:
                ph = ph * scale
            halves.append(ph)
        full = jnp.concatenate(halves, axis=1)
        o_ref[0, :, cb * 256:(cb + 1) * 256] = full.astype(BF16)
        if cb == CB_DA_Q:
            qt_ref[0] = full.T.astype(BF16)
        elif cb == CB_DA_V:
            vt_ref[0] = full.T.astype(BF16)


def _inproj(xa, modv, g, w_in_b, tabs, seq, tm):
    B, T, D = xa.shape
    kern = functools.partial(_inproj_kernel, tm=tm, seq=seq)
    tab_spec = pl.BlockSpec((tm, LANES), lambda b, i: (i, 0))
    return pl.pallas_call(
        kern,
        grid=(B, T // tm),
        in_specs=[pl.BlockSpec((1, tm, D), lambda b, i: (b, i, 0)),
                  pl.BlockSpec((1, 2, 8, D), lambda b, i: (b, 0, 0, 0)),
                  pl.BlockSpec((1, D), lambda b, i: (0, 0)),
                  pl.BlockSpec((D, IN_WIDTH), lambda b, i: (0, 0)),
                  tab_spec, tab_spec, tab_spec, tab_spec],
        out_specs=[pl.BlockSpec((1, tm, IN_WIDTH), lambda b, i: (b, i, 0)),
                   pl.BlockSpec((1, GROUP_W, tm), lambda b, i: (b, 0, i)),
                   pl.BlockSpec((1, GROUP_W, tm), lambda b, i: (b, 0, i))],
        out_shape=[jax.ShapeDtypeStruct((B, T, IN_WIDTH), BF16),
                   jax.ShapeDtypeStruct((B, GROUP_W, T), BF16),
                   jax.ShapeDtypeStruct((B, GROUP_W, T), BF16)],
        compiler_params=_params(("arbitrary", "arbitrary")),
        name="inproj",
    )(xa, modv, g.reshape(1, D), w_in_b, *tabs)


def _rope_tables(seq, ctx):
    t = jnp.arange(seq)
    row = (t // GRID_W).astype(F32)
    col = (t % GRID_W).astype(F32)
    out = []
    for dh in (DA_QK, HEAD_DIM):
        half = dh // 2
        quarter = half // 2
        lane = jnp.arange(LANES)
        d = lane % dh
        use_col = (d // half) == 1
        idx = (d % quarter).astype(F32)
        inv = ROPE_BASE ** (-idx * 2.0 / half)
        pos = jnp.where(use_col[None, :], col[:, None], row[:, None])
        ang = pos * inv[None, :]
        first = (d % half) < quarter
        cos = jnp.cos(ang)
        sin = jnp.where(first[None, :], -jnp.sin(ang), jnp.sin(ang))
        cos = jnp.concatenate([cos, jnp.ones((ctx, LANES), F32)], axis=0)
        sin = jnp.concatenate([sin, jnp.zeros((ctx, LANES), F32)], axis=0)
        out += [cos, sin]
    return out


def _rows_per_head(vals, tq):
    return jnp.concatenate([jnp.broadcast_to(v, (HEAD_DIM, tq)) for v in vals], axis=0)


def _da_kernel(lamp_ref, qt_ref, k_ref, vt_ref, g_ref, o_ref, qm_sc, m_sc, l_sc, acc_sc, st_sc, p_sc,
               *, tq, lambda_init):
    ki = pl.program_id(2)
    nk = pl.num_programs(2)

    @pl.when(ki == 0)
    def _():
        qt = qt_ref[0]
        for j in range(2 * N_HEADS):
            qm = jnp.where(_head_mask(qt.shape, DA_QK, j, 0), qt, jnp.zeros_like(qt))
            qm_sc[j] = qm
            m_sc[j] = jnp.max(jnp.dot(k_ref[0, 0:DA_CHUNK, :], qm, preferred_element_type=F32), axis=0,
                              keepdims=True)
        l_sc[...] = jnp.zeros(l_sc.shape, F32)
        acc_sc[...] = jnp.zeros(acc_sc.shape, F32)

    k = k_ref[0]
    vt = vt_ref[0]
    ones = jnp.ones((ONES_ROWS, vt.shape[1]), BF16)

    tk = k.shape[0]

    def vt_ext(h):
        return jnp.concatenate([vt[h * HEAD_DIM:(h + 1) * HEAD_DIM, :], ones], axis=0)

    jump = None
    pvs, mxs = [], []
    for j in range(2 * N_HEADS):
        st = jnp.dot(k, qm_sc[j], preferred_element_type=F32)
        m_prev = m_sc[j]
        mx = jnp.max(st, axis=0, keepdims=True)
        p = jnp.exp2(st - m_prev).astype(BF16)
        pvs.append(jnp.dot(vt_ext(j // 2), p, preferred_element_type=F32))
        mxs.append(mx)
        jump = mx - m_prev if jump is None else jnp.maximum(jump, mx - m_prev)
    keep = jnp.max(jump) <= DA_MAX_JUMP
    for j in range(2 * N_HEADS):
        h, t = j // 2, j % 2
        rows = slice(h * HEAD_DIM, (h + 1) * HEAD_DIM)
        m_prev = m_sc[j]
        m_new = jnp.maximum(m_prev, mxs[j])
        alpha = jnp.exp2(m_prev - m_new)
        l_old = l_sc[j]
        acc_old = acc_sc[t, rows, :]
        l_sc[j] = jnp.where(keep, alpha * (l_old + pvs[j][HEAD_DIM:HEAD_DIM + 1, :]), l_old)
        acc_sc[t, rows, :] = jnp.where(keep, alpha * (acc_old + pvs[j][:HEAD_DIM, :]), acc_old)
        m_sc[j] = jnp.where(keep, m_new, m_prev)

    @pl.when(jnp.logical_not(keep))
    def _():
        for j in range(2 * N_HEADS):
            h, t = j // 2, j % 2
            rows = slice(h * HEAD_DIM, (h + 1) * HEAD_DIM)
            st_sc[...] = jnp.dot(k, qm_sc[j], preferred_element_type=F32)
            run = st_sc[0:DA_CHUNK, :]
            for c in range(1, tk // DA_CHUNK):
                run = jnp.maximum(run, st_sc[c * DA_CHUNK:(c + 1) * DA_CHUNK, :])
            m_prev = m_sc[j]
            m_new = jnp.maximum(m_prev, jnp.max(run, axis=0, keepdims=True))
            alpha = jnp.exp2(m_prev - m_new)
            for c in range(tk // DA_CHUNK):
                cr = slice(c * DA_CHUNK, (c + 1) * DA_CHUNK)
                p_sc[cr, :] = jnp.exp2(st_sc[cr, :] - m_new).astype(BF16)
            pv = jnp.dot(vt_ext(h), p_sc[...], preferred_element_type=F32)
            l_sc[j] = alpha * l_sc[j] + pv[HEAD_DIM:HEAD_DIM + 1, :]
            m_sc[j] = m_new
            acc_sc[t, rows, :] = alpha * acc_sc[t, rows, :] + pv[:HEAD_DIM, :]

    @pl.when(ki == nk - 1)
    def _():
        lp = lamp_ref[...]
        lam = (jnp.exp(jnp.sum(lp[0:1] * lp[1:2], axis=1, keepdims=True))
               - jnp.exp(jnp.sum(lp[2:3] * lp[3:4], axis=1, keepdims=True))) + lambda_init
        o0 = acc_sc[0] / _rows_per_head([l_sc[2 * h] for h in range(N_HEADS)], tq)
        o1 = acc_sc[1] / _rows_per_head([l_sc[2 * h + 1] for h in range(N_HEADS)], tq)
        a = (o0 - lam * o1).T
        y = a * lax.rsqrt(_group_mean_sq(a) + EPS) * g_ref[...]
        o_ref[0] = (y * (1.0 - lambda_init)).astype(BF16)


def _diff_attention(P, QT, VT, lamp, subln_full, lambda_init, q0, nq, tq, k0, nk, tk):
    B = P.shape[0]
    kern = functools.partial(_da_kernel, tq=tq, lambda_init=lambda_init)
    return pl.pallas_call(
        kern,
        grid=(B, nq, nk),
        in_specs=[pl.BlockSpec((8, LANES), lambda b, qi, ki: (0, 0)),
                  pl.BlockSpec((1, GROUP_W, tq), lambda b, qi, ki: (b, 0, q0 + qi)),
                  pl.BlockSpec((1, tk, GROUP_W), lambda b, qi, ki: (b, k0 + ki, CB_DA_K)),
                  pl.BlockSpec((1, GROUP_W, tk), lambda b, qi, ki: (b, 0, k0 + ki)),
                  pl.BlockSpec((1, GROUP_W), lambda b, qi, ki: (0, 0))],
        out_specs=pl.BlockSpec((1, tq, GROUP_W), lambda b, qi, ki: (b, qi, 0)),
        out_shape=jax.ShapeDtypeStruct((B, nq * tq, GROUP_W), BF16),
        scratch_shapes=[pltpu.VMEM((2 * N_HEADS, GROUP_W, tq), BF16),
                        pltpu.VMEM((2 * N_HEADS, 1, tq), F32),
                        pltpu.VMEM((2 * N_HEADS, 1, tq), F32),
                        pltpu.VMEM((2, GROUP_W, tq), F32),
                        pltpu.VMEM((tk, tq), F32),
                        pltpu.VMEM((tk, tq), BF16)],
        compiler_params=_params(("arbitrary", "arbitrary", "arbitrary")),
        name="diff_attention",
    )(lamp, QT, P, VT, subln_full)


def _swa_kernel(sink_ref, q_ref, kp_ref, kc_ref, kn_ref, kx_ref, vp_ref, vc_ref, vn_ref, vx_ref, o_ref,
                *, seq, blk):
    i = pl.program_id(0)
    nloc = blk + 2 * SWA_WINDOW
    nkeys = nloc + kx_ref.shape[1]
    qpos = i * blk + lax.broadcasted_iota(jnp.int32, (blk, nkeys), 0)
    c = lax.broadcasted_iota(jnp.int32, (blk, nkeys), 1)
    kpos = i * blk - SWA_WINDOW + c
    valid = ((kpos >= 0) & (kpos < seq) & (jnp.abs(qpos - kpos) <= SWA_WINDOW) & (qpos < seq)) | (c >= nloc)
    for b in range(q_ref.shape[0]):
        q = q_ref[b]
        kall = jnp.concatenate([kp_ref[b], kc_ref[b], kn_ref[b], kx_ref[b]], axis=0)
        vall = jnp.concatenate([vp_ref[b], vc_ref[b], vn_ref[b], vx_ref[b]], axis=0)
        qs = []
        for g in range(2):
            qg = q[:, g * LANES:(g + 1) * LANES]
            for kvh in range(2):
                qs.append(jnp.where(_head_mask(qg.shape, HEAD_DIM, kvh), qg, jnp.zeros_like(qg)))
        s = lax.dot_general(jnp.concatenate(qs, axis=0), kall, (((1,), (1,)), ((), ())),
                            preferred_element_type=F32)
        vms = [jnp.where(_head_mask(vall.shape, HEAD_DIM, kvh), vall, jnp.zeros_like(vall)) for kvh in range(2)]
        vm = jnp.concatenate(vms, axis=0)
        for g in range(2):
            ps, invs = [], []
            for kvh in range(2):
                j = 2 * g + kvh
                sink = sink_ref[2 * kvh + g] * LOG2E
                sj = jnp.where(valid, s[j * blk:(j + 1) * blk], NEG)
                m = jnp.maximum(jnp.max(sj, axis=1, keepdims=True), sink)
                e = jnp.exp2(sj - m)
                den = jnp.sum(e, axis=1, keepdims=True) + jnp.exp2(sink - m)
                ps.append(e.astype(BF16))
                invs.append(jnp.broadcast_to(1.0 / den, (blk, LANES)))
            out = jnp.dot(jnp.concatenate(ps, axis=1), vm, preferred_element_type=F32)
            o_ref[b, :, g * LANES:(g + 1) * LANES] = (out * _lane_pick(invs[0], invs[1])).astype(BF16)


def _na_kernel(q_ref, k0_ref, k1_ref, k2_ref, kx_ref, v0_ref, v1_ref, v2_ref, vx_ref, bias_ref, o_ref, *, tq):
    nloc = 3 * tq
    for b in range(q_ref.shape[0]):
        q = q_ref[b]
        kall = jnp.concatenate([k0_ref[b], k1_ref[b], k2_ref[b], kx_ref[b]], axis=0)
        vall = jnp.concatenate([v0_ref[b], v1_ref[b], v2_ref[b], vx_ref[b]], axis=0)
        qs = jnp.concatenate(
            [jnp.where(_head_mask(q.shape, HEAD_DIM, h), q, jnp.zeros_like(q)) for h in range(N_HEADS)], axis=0)
        s = lax.dot_general(qs, kall, (((1,), (1,)), ((), ())), preferred_element_type=F32)
        ps, invs = [], []
        for h in range(N_HEADS):
            sh = s[h * tq:(h + 1) * tq]
            s_loc = sh[:, :nloc] + bias_ref[0, h]
            s_ctx = sh[:, nloc:]
            m = jnp.maximum(jnp.max(s_loc, axis=1, keepdims=True), jnp.max(s_ctx, axis=1, keepdims=True))
            e_loc = jnp.exp2(s_loc - m)
            e_ctx = jnp.exp2(s_ctx - m)
            den = jnp.sum(e_loc, axis=1, keepdims=True) + jnp.sum(e_ctx, axis=1, keepdims=True)
            ps.append(jnp.concatenate([e_loc, e_ctx], axis=1).astype(BF16))
            invs.append(jnp.broadcast_to(1.0 / den, (tq, LANES)))
        vm = jnp.concatenate(
            [jnp.where(_head_mask(vall.shape, HEAD_DIM, h), vall, jnp.zeros_like(vall)) for h in range(N_HEADS)],
            axis=0)
        out = jnp.dot(jnp.concatenate(ps, axis=1), vm, preferred_element_type=F32)
        o_ref[b] = (out * _per_head_full(invs)).astype(BF16)


def _na_bias_kernel(rt_ref, o_ref, sv_sc, *, rows):
    tr = NA_TILE_ROWS
    nrt = rows // tr
    nl = 3 * tr * GRID_W
    n_dcol = 2 * NA_WIN_COLS - 1
    rt = rt_ref[0] * LOG2E
    hi = rt.astype(BF16)
    r1 = rt - hi.astype(F32)
    mid = r1.astype(BF16)
    lo = (r1 - mid.astype(F32)).astype(BF16)
    lane = lax.broadcasted_iota(jnp.int32, (GRID_W, nl), 1)
    qc = lax.broadcasted_iota(jnp.int32, (GRID_W, nl), 0)
    kc = lane % GRID_W
    kr = lane // GRID_W
    dc = jnp.clip(kc - qc, 1 - NA_WIN_COLS, NA_WIN_COLS - 1) + (NA_WIN_COLS - 1)
    c0 = jnp.clip(qc - NA_WIN_COLS // 2, 0, GRID_W - NA_WIN_COLS)
    col_ok = (kc >= c0) & (kc < c0 + NA_WIN_COLS)
    irow = lax.broadcasted_iota(jnp.int32, (LANES, nl), 0)
    krl = lax.broadcasted_iota(jnp.int32, (LANES, nl), 1) // GRID_W
    for v, (t_idx, ws) in enumerate(((0, 0), (1, 0), (nrt - 1, nrt - 3))):
        for qr in range(tr):
            r = t_idx * tr + qr
            r0 = min(max(r - NA_WIN_ROWS // 2, 0), rows - NA_WIN_ROWS)
            d_row = jnp.clip(ws * tr + krl - r + (NA_WIN_ROWS - 1), 0, 2 * NA_WIN_ROWS - 2)
            onehot = jnp.where(irow == d_row, 1.0, 0.0).astype(BF16)
            sv_sc[...] = (jnp.dot(hi, onehot, preferred_element_type=F32)
                          + jnp.dot(mid, onehot, preferred_element_type=F32)
                          + jnp.dot(lo, onehot, preferred_element_type=F32))

            def pick(j, acc):
                return jnp.where(dc == j, sv_sc[pl.ds(j, 1), :], acc)

            acc = lax.fori_loop(0, n_dcol, pick, jnp.zeros((GRID_W, nl), F32), unroll=True)
            krow = ws * tr + kr
            ok = col_ok & (krow >= r0) & (krow < r0 + NA_WIN_ROWS)
            o_ref[v, 0, qr * GRID_W:(qr + 1) * GRID_W, :] = jnp.where(ok, acc, NEG)
    o_ref[3, 0] = jnp.full((tr * GRID_W, nl), NEG, F32)


def _na_bias_tables(rpb, rows):
    H, nr, ncol = rpb.shape
    rt = jnp.zeros((H, 32, LANES), F32).at[:, :ncol, :nr].set(jnp.swapaxes(rpb.astype(F32), 1, 2))
    tq = NA_TILE_ROWS * GRID_W
    return pl.pallas_call(
        functools.partial(_na_bias_kernel, rows=rows),
        grid=(H,),
        in_specs=[pl.BlockSpec((1, 32, LANES), lambda h: (h, 0, 0))],
        out_specs=pl.BlockSpec((4, 1, tq, 3 * tq), lambda h: (0, h, 0, 0)),
        out_shape=jax.ShapeDtypeStruct((4, H, tq, 3 * tq), F32),
        scratch_shapes=[pltpu.VMEM((32, 3 * tq), F32)],
        compiler_params=_params(("arbitrary",)),
        name="na_bias",
    )(rt)


def _ret_direction(q, k, v, state_ref, lgs_ref, lgv, forward, base):
    C = q.shape[0]
    i = lax.broadcasted_iota(jnp.int32, (C, C), 0)
    j = lax.broadcasted_iota(jnp.int32, (C, C), 1)
    dist = (i - j if forward else j - i).astype(F32)
    keep = dist >= 0 if forward else dist > 0
    dist = jnp.where(keep, dist, 0.0)
    qs = jnp.concatenate(
        [jnp.where(_head_mask(q.shape, HEAD_DIM, h), q, jnp.zeros_like(q)) for h in range(N_HEADS)], axis=0)
    s = lax.dot_general(qs, k, (((1,), (1,)), ((), ())), preferred_element_type=F32)
    atts = []
    for h in range(N_HEADS):
        decay = jnp.where(keep, jnp.exp(dist * lgs_ref[base + h]), 0.0)
        atts.append((s[h * C:(h + 1) * C] * decay).astype(BF16))
    vm = jnp.concatenate(
        [jnp.where(_head_mask(v.shape, HEAD_DIM, h), v, jnp.zeros_like(v)) for h in range(N_HEADS)], axis=0)
    intra = jnp.dot(jnp.concatenate(atts, axis=1), vm, preferred_element_type=F32)
    r = lax.broadcasted_iota(jnp.int32, (C, 1), 0).astype(F32)
    xi = jnp.exp((r + 1.0 if forward else C - r) * lgv)
    zeta = jnp.exp((C - 1.0 - r if forward else r) * lgv)
    state = state_ref[...]
    cross = jnp.dot((q.astype(F32) * xi).astype(BF16), state.astype(BF16), preferred_element_type=F32)
    kz_t = (k.astype(F32) * zeta).T.astype(BF16)
    u = jnp.dot(kz_t, v, preferred_element_type=F32)
    rr = lax.broadcasted_iota(jnp.int32, u.shape, 0) // HEAD_DIM
    cc = lax.broadcasted_iota(jnp.int32, u.shape, 1) // HEAD_DIM
    state_ref[...] = jnp.where(rr == cc, jnp.exp(C * lgv) * state + u, 0.0)
    return intra + cross


def _ret_kernel(lgs_ref, lgv_ref, qf_ref, kf_ref, vf_ref, qb_ref, kb_ref, vb_ref, of_ref, ob_ref, sf_sc, sb_sc):
    @pl.when(pl.program_id(0) == 0)
    def _():
        sf_sc[...] = jnp.zeros(sf_sc.shape, F32)
        sb_sc[...] = jnp.zeros(sb_sc.shape, F32)

    for b in range(qf_ref.shape[0]):
        of_ref[b] = _ret_direction(qf_ref[b], kf_ref[b], vf_ref[b], sf_sc.at[b], lgs_ref, lgv_ref[0:1], True, 0)
        ob_ref[b] = _ret_direction(qb_ref[b], kb_ref[b], vb_ref[b], sb_sc.at[b], lgs_ref, lgv_ref[1:2], False,
                                   N_HEADS)


def _local_kernel(sink_ref, lgs_ref, lgv_ref, *refs, seq, tile):
    swa_in, na_in, ret_in = refs[0:9], refs[9:19], refs[19:25]
    yb_ref, yn_ref, of_ref, ob_ref, sf_sc, sb_sc = refs[25:]
    _swa_kernel(sink_ref, *swa_in, yb_ref, seq=seq, blk=tile)
    _na_kernel(*na_in, yn_ref, tq=tile)
    _ret_kernel(lgs_ref, lgv_ref, *ret_in, of_ref, ob_ref, sf_sc, sb_sc)


def _local_mixers(P, sink, bias_tabs, lgs, lgv, seq, ctx):
    B, T, _ = P.shape
    tile = LOCAL_TILE
    W = SWA_WINDOW
    nt, nst, nw = T // tile, seq // tile, T // W
    nc = ctx // tile
    kern = functools.partial(_local_kernel, seq=seq, tile=tile)

    def rows(cb, width=GROUP_W, index=lambda i: i, size=tile):
        return pl.BlockSpec((B, size, width), lambda i: (0, index(i), cb))

    def ctx_rows(cb, width):
        return pl.BlockSpec((B, ctx, width), lambda i: (0, seq // ctx, cb))

    before = lambda i: jnp.clip(i * (tile // W) - 1, 0, nw - 1)
    after = lambda i: jnp.clip((i + 1) * (tile // W), 0, nw - 1)
    swa = [rows(CB_SWA_Q)]
    for cb in (CB_SWA_K128, CB_SWA_V128):
        swa += [rows(cb, LANES, before, W), rows(cb, LANES), rows(cb, LANES, after, W), ctx_rows(cb, LANES)]

    win = lambda i: jnp.clip(i - 1, 0, nst - 3)
    na = [rows(CB_NA_Q)]
    for cb in (CB_NA_K, CB_NA_V):
        na += [rows(cb, index=lambda i, o=o: win(i) + o) for o in range(3)] + [ctx_rows(cb, GROUP_W)]
    na.append(pl.BlockSpec((1, N_HEADS, tile, 3 * tile), lambda i: (jnp.where(i >= nst, 3, i - win(i)), 0, 0, 0)))

    fwd = lambda n: jnp.where(n < nc, nst + n, n - nc)
    bwd = lambda n: nt - 1 - n
    ret = [rows(cb, index=order) for order in (fwd, bwd) for cb in (CB_RET_Q, CB_RET_K, CB_RET_V)]

    tok = lambda order: pl.BlockSpec((B, tile, GROUP_W), lambda i: (0, order(i), 0))
    ident = lambda i: i
    return pl.pallas_call(
        kern,
        grid=(nt,),
        in_specs=[pl.BlockSpec(memory_space=pltpu.SMEM), pl.BlockSpec(memory_space=pltpu.SMEM),
                  pl.BlockSpec((8, GROUP_W), lambda i: (0, 0))] + swa + na + ret,
        out_specs=[tok(ident), tok(ident), tok(fwd), tok(bwd)],
        out_shape=[jax.ShapeDtypeStruct((B, T, GROUP_W), BF16)] * 2 + [jax.ShapeDtypeStruct((B, T, GROUP_W), F32)] * 2,
        scratch_shapes=[pltpu.VMEM((B, GROUP_W, GROUP_W), F32)] * 2,
        compiler_params=_params(("arbitrary",)),
        name="local_mixers",
    )(sink, lgs, lgv, *([P] * 9), *([P] * 9), bias_tabs, *([P] * 6))


def _outproj_kernel(*refs, tm, seq, n_exp):
    with_router = n_exp > 0
    if with_router:
        (ya_ref, yb_ref, yn_ref, of_ref, ob_ref, gt_ref, x_ref, mod_ref, w_ref, gpost_ref, gpre_ref, r_ref,
         x1_ref, h2_ref, eid_ref, gw_ref) = refs
    else:
        (ya_ref, yb_ref, yn_ref, of_ref, ob_ref, gt_ref, x_ref, mod_ref, w_ref, gpost_ref, gpre_ref,
         x1_ref, h2_ref) = refs
    i = pl.program_id(1)
    rows = i * tm + lax.broadcasted_iota(jnp.int32, (tm, 1), 0)
    is_ctx = rows >= seq
    o = of_ref[0] + ob_ref[0]
    yr = o * lax.rsqrt(_group_mean_sq(o) + EPS) * _silu(gt_ref[0].astype(F32))
    y = (jnp.dot(ya_ref[0], w_ref[0:256, :], preferred_element_type=F32)
         + jnp.dot(yb_ref[0], w_ref[256:512, :], preferred_element_type=F32)
         + jnp.dot(yn_ref[0], w_ref[512:768, :], preferred_element_type=F32)
         + jnp.dot(yr.astype(BF16), w_ref[768:1024, :], preferred_element_type=F32))
    yn = y * lax.rsqrt(jnp.mean(y * y, axis=-1, keepdims=True) + EPS) * gpost_ref[...]
    x1 = x_ref[0] + _row_mod(mod_ref, is_ctx, 2) * yn
    x1_ref[0] = x1
    h = x1 * lax.rsqrt(jnp.mean(x1 * x1, axis=-1, keepdims=True) + EPS) * gpre_ref[...]
    h = h * (1.0 + _row_mod(mod_ref, is_ctx, 4)) + _row_mod(mod_ref, is_ctx, 3)
    h2_ref[0] = h.astype(h2_ref.dtype)
    if with_router:
        r = r_ref[...]
        r_hi = r.astype(BF16)
        r_lo = (r - r_hi.astype(F32)).astype(BF16)
        h_hi = h.astype(BF16)
        h_lo = (h - h_hi.astype(F32)).astype(BF16)
        dn = (((1,), (1,)), ((), ()))
        lt = (lax.dot_general(r_hi, h_hi, dn, preferred_element_type=F32)
              + lax.dot_general(r_hi, h_lo, dn, preferred_element_type=F32)
              + lax.dot_general(r_lo, h_hi, dn, preferred_element_type=F32))
        e = lax.broadcasted_iota(jnp.int32, lt.shape, 0)
        lt = jnp.where(e < n_exp, lt, NEG)
        v1 = jnp.max(lt, axis=0, keepdims=True)
        i1 = jnp.min(jnp.where(lt == v1, e, ROUTER_ROWS), axis=0, keepdims=True)
        rest = jnp.where(e == i1, NEG, lt)
        v2 = jnp.max(rest, axis=0, keepdims=True)
        i2 = jnp.min(jnp.where(rest == v2, e, ROUTER_ROWS), axis=0, keepdims=True)
        e2 = jnp.exp(v2 - v1)
        w1 = 1.0 / (1.0 + e2)
        w2 = e2 / (1.0 + e2)
        eid_ref[0] = jnp.where(e == 0, i1, jnp.where(e == 1, i2, 0))[:8]
        wrow = jnp.where(e == 0, w1, jnp.where(e == 1, w2, 0.0))
        gw_ref[0] = jnp.concatenate([wrow, jnp.zeros((LANES - ROUTER_ROWS, tm), F32)], axis=0).T


def _outproj(ya, yb, yn, of, ob, P, xa, modv, w_out_b, g_post, g_pre2, rows, seq, tm, router=None, n_exp=0):
    B, _, D = xa.shape
    with_router = router is not None
    kern = functools.partial(_outproj_kernel, tm=tm, seq=seq, n_exp=n_exp)
    tok = lambda w: pl.BlockSpec((1, tm, w), lambda b, i: (b, i, 0))
    in_specs = [tok(GROUP_W), tok(GROUP_W), tok(GROUP_W), tok(GROUP_W), tok(GROUP_W),
                pl.BlockSpec((1, tm, GROUP_W), lambda b, i: (b, i, CB_RET_G)),
                tok(D),
                pl.BlockSpec((1, 2, 8, D), lambda b, i: (b, 0, 0, 0)),
                pl.BlockSpec((D, D), lambda b, i: (0, 0)),
                pl.BlockSpec((1, D), lambda b, i: (0, 0)),
                pl.BlockSpec((1, D), lambda b, i: (0, 0))]
    args = [ya, yb, yn, of, ob, P, xa, modv, w_out_b, g_post.reshape(1, D), g_pre2.reshape(1, D)]
    out_specs = [tok(D), tok(D)]
    out_shape = [jax.ShapeDtypeStruct((B, rows, D), F32),
                 jax.ShapeDtypeStruct((B, rows, D), F32 if with_router else BF16)]
    if with_router:
        in_specs.append(pl.BlockSpec((ROUTER_ROWS, D), lambda b, i: (0, 0)))
        args.append(router)
        out_specs += [pl.BlockSpec((1, 8, tm), lambda b, i: (b, 0, i)), tok(LANES)]
        out_shape += [jax.ShapeDtypeStruct((B, 8, rows), jnp.int32),
                      jax.ShapeDtypeStruct((B, rows, LANES), F32)]
    return pl.pallas_call(
        kern,
        grid=(B, rows // tm),
        in_specs=in_specs,
        out_specs=out_specs,
        out_shape=out_shape,
        compiler_params=_params(("arbitrary", "arbitrary")),
        name="outproj_router" if with_router else "outproj",
    )(*args)


def _ffn_kernel(h_ref, x_ref, mod_ref, wg_ref, wu_ref, wd_ref, gpost_ref, o_ref, *, tm, seq, fc):
    i = pl.program_id(1)
    rows = i * tm + lax.broadcasted_iota(jnp.int32, (tm, 1), 0)
    is_ctx = rows >= seq
    h = h_ref[0]
    acc = jnp.zeros((tm, o_ref.shape[-1]), F32)
    for c in range(wg_ref.shape[1] // fc):
        g = jnp.dot(h, wg_ref[:, c * fc:(c + 1) * fc], preferred_element_type=F32)
        u = jnp.dot(h, wu_ref[:, c * fc:(c + 1) * fc], preferred_element_type=F32)
        a = (_silu(g) * u).astype(BF16)
        acc = acc + jnp.dot(a, wd_ref[c * fc:(c + 1) * fc, :], preferred_element_type=F32)
    yn = acc * lax.rsqrt(jnp.mean(acc * acc, axis=-1, keepdims=True) + EPS) * gpost_ref[...]
    o_ref[0] = x_ref[0] + _row_mod(mod_ref, is_ctx, 5) * yn


def _dense_ffn(h2, x1, modv, wg, wu, wd, g_post, seq, tm):
    B, T, D = x1.shape
    F = wg.shape[1]
    kern = functools.partial(_ffn_kernel, tm=tm, seq=seq, fc=256)
    const = lambda shape: pl.BlockSpec(shape, lambda b, i: (0, 0), pipeline_mode=pl.Buffered(1))
    return pl.pallas_call(
        kern,
        grid=(B, T // tm),
        in_specs=[pl.BlockSpec((1, tm, D), lambda b, i: (b, i, 0)),
                  pl.BlockSpec((1, tm, D), lambda b, i: (b, i, 0)),
                  pl.BlockSpec((1, 2, 8, D), lambda b, i: (b, 0, 0, 0)),
                  const((D, F)), const((D, F)), const((F, D)),
                  pl.BlockSpec((1, D), lambda b, i: (0, 0))],
        out_specs=pl.BlockSpec((1, tm, D), lambda b, i: (b, i, 0)),
        out_shape=jax.ShapeDtypeStruct((B, T, D), F32),
        compiler_params=_params(("arbitrary", "arbitrary")),
        name="dense_ffn",
    )(h2, x1, modv, wg, wu, wd, g_post.reshape(1, D))


def _row_copy(src_hbm, row, dst_vmem, r, sem):
    return pltpu.make_async_copy(src_hbm.at[pl.ds(row, 1)], dst_vmem.at[pl.ds(r, 1)], sem)


def _moe_ffn_kernel(te_ref, tv_ref, src0_ref, srcn_ref, h_hbm, wg_ref, wu_ref, wd_ref, o_ref,
                    xg_sc, xb_sc, acc_sc, sem, *, tr):
    t = pl.program_id(0)
    f = pl.program_id(1)
    nt = pl.num_programs(0)
    nf = pl.num_programs(1)
    slot = t % 2

    def start_gather(src_ref, s):
        def body(r, c):
            _row_copy(h_hbm, src_ref[0, 0, r], xg_sc.at[s], r, sem.at[s]).start()
            return c

        lax.fori_loop(0, tr, body, 0, unroll=8)

    @pl.when((f == 0) & (t == 0) & (tv_ref[0] > 0))
    def _():
        start_gather(src0_ref, 0)

    @pl.when(f == 0)
    def _():
        acc_sc[...] = jnp.zeros(acc_sc.shape, F32)

    @pl.when((f == 0) & (tv_ref[t] > 0))
    def _():
        pltpu.make_async_copy(h_hbm.at[pl.ds(0, tr)], xg_sc.at[slot], sem.at[slot]).wait()
        xb_sc[...] = xg_sc[slot].astype(BF16)

    @pl.when((f == 0) & (t + 1 < nt) & (tv_ref[jnp.minimum(t + 1, nt - 1)] > 0))
    def _():
        start_gather(srcn_ref, 1 - slot)

    @pl.when(tv_ref[t] > 0)
    def _():
        x = xb_sc[...]
        g = jnp.dot(x, wg_ref[0], preferred_element_type=F32)
        u = jnp.dot(x, wu_ref[0], preferred_element_type=F32)
        a = (_silu(g) * u).astype(BF16)
        acc_sc[...] += jnp.dot(a, wd_ref[0], preferred_element_type=F32)

    @pl.when(f == nf - 1)
    def _():
        o_ref[...] = acc_sc[...]


def _moe_ffn(h_flat, src, tile_expert, tile_valid, wg, wu, wd, n_rows, tr, tf):
    D = h_flat.shape[1]
    F = wg.shape[2]
    nt = n_rows // tr
    return pl.pallas_call(
        functools.partial(_moe_ffn_kernel, tr=tr),
        grid_spec=pltpu.PrefetchScalarGridSpec(
            num_scalar_prefetch=2,
            grid=(nt, F // tf),
            in_specs=[pl.BlockSpec((1, 1, tr), lambda t, f, te, tv: (0, 0, 0), memory_space=pltpu.SMEM),
                      pl.BlockSpec((1, 1, tr), lambda t, f, te, tv: (jnp.minimum(t + 1, nt - 1), 0, 0),
                                   memory_space=pltpu.SMEM),
                      pl.BlockSpec(memory_space=pl.ANY),
                      pl.BlockSpec((1, D, tf), lambda t, f, te, tv: (te[t], 0, f)),
                      pl.BlockSpec((1, D, tf), lambda t, f, te, tv: (te[t], 0, f)),
                      pl.BlockSpec((1, tf, D), lambda t, f, te, tv: (te[t], f, 0))],
            out_specs=pl.BlockSpec((tr, D), lambda t, f, te, tv: (t, 0)),
            scratch_shapes=[pltpu.VMEM((2, tr, D), h_flat.dtype), pltpu.VMEM((tr, D), BF16),
                            pltpu.VMEM((tr, D), F32), pltpu.SemaphoreType.DMA((2,))]),
        out_shape=jax.ShapeDtypeStruct((n_rows, D), F32),
        compiler_params=_params(("arbitrary", "arbitrary")),
        name="moe_ffn",
    )(tile_expert, tile_valid, src.reshape(nt, 1, tr), src.reshape(nt, 1, tr), h_flat, wg, wu, wd)


def _combine_kernel(pos_ref, ys_hbm, gw_ref, x_ref, mod_ref, gpost_ref, o_ref, y1_sc, y2_sc, sem, *, tm, seq):
    def issue(r, c):
        _row_copy(ys_hbm, pos_ref[0, 0, r], y1_sc, r, sem.at[0]).start(priority=0)
        _row_copy(ys_hbm, pos_ref[0, 1, r], y2_sc, r, sem.at[1]).start(priority=1)
        return c

    lax.fori_loop(0, tm, issue, 0, unroll=8)
    pltpu.make_async_copy(ys_hbm.at[pl.ds(0, tm)], y1_sc, sem.at[0]).wait()
    pltpu.make_async_copy(ys_hbm.at[pl.ds(0, tm)], y2_sc, sem.at[1]).wait()
    rows = pl.program_id(1) * tm + lax.broadcasted_iota(jnp.int32, (tm, 1), 0)
    gw = gw_ref[0]
    y = gw[:, 0:1] * y1_sc[...] + gw[:, 1:2] * y2_sc[...]
    yn = y * lax.rsqrt(jnp.mean(y * y, axis=-1, keepdims=True) + EPS) * gpost_ref[...]
    o_ref[0] = x_ref[0] + _row_mod(mod_ref, rows >= seq, 5) * yn


def _moe_combine(pos, ys, gw, x1, modv, g_post, seq, tm):
    B, R, D = x1.shape
    ntile = R // tm
    kern = functools.partial(_combine_kernel, tm=tm, seq=seq)
    return pl.pallas_call(
        kern,
        grid=(B, ntile),
        in_specs=[pl.BlockSpec((1, TOP_K, tm), lambda b, i: (b * ntile + i, 0, 0), memory_space=pltpu.SMEM),
                  pl.BlockSpec(memory_space=pl.ANY),
                  pl.BlockSpec((1, tm, LANES), lambda b, i: (b, i, 0)),
                  pl.BlockSpec((1, tm, D), lambda b, i: (b, i, 0)),
                  pl.BlockSpec((1, 2, 8, D), lambda b, i: (b, 0, 0, 0)),
                  pl.BlockSpec((1, D), lambda b, i: (0, 0))],
        out_specs=pl.BlockSpec((1, tm, D), lambda b, i: (b, i, 0)),
        out_shape=jax.ShapeDtypeStruct((B, R, D), F32),
        scratch_shapes=[pltpu.VMEM((tm, D), F32), pltpu.VMEM((tm, D), F32), pltpu.SemaphoreType.DMA((2,))],
        compiler_params=_params(("arbitrary", "arbitrary")),
        name="moe_combine",
    )(pos.reshape(B * ntile, tm, TOP_K).transpose(0, 2, 1), ys, gw, x1, modv, g_post.reshape(1, D))


def _moe_routing(eid, n_exp, tr):
    N = eid.shape[0]
    e_flat = eid.reshape(-1)
    onehot = (e_flat[:, None] == jnp.arange(n_exp)[None, :]).astype(jnp.int32)
    rank = jnp.sum((jnp.cumsum(onehot, axis=0) - 1) * onehot, axis=1)
    counts = jnp.sum(onehot, axis=0)
    padded = ((counts + tr - 1) // tr) * tr
    ends = jnp.cumsum(padded)
    starts = ends - padded
    dest = starts[e_flat] + rank
    n_rows = (TOP_K * N // tr + n_exp) * tr
    src = jnp.zeros((n_rows,), jnp.int32).at[dest].set(jnp.arange(TOP_K * N, dtype=jnp.int32) // TOP_K)
    tile_start = jnp.arange(n_rows // tr, dtype=jnp.int32) * tr
    tile_expert = jnp.minimum(jnp.sum((tile_start[:, None] >= ends[None, :]).astype(jnp.int32), axis=1), n_exp - 1)
    tile_valid = (tile_start < ends[-1]).astype(jnp.int32)
    pos = dest.reshape(N, TOP_K).astype(jnp.int32)
    return src, pos, tile_expert, tile_valid, n_rows


def _swa_perm():
    idx = []
    for g in range(2):
        for kvh in range(2):
            idx += [(kvh * 2 + g) * HEAD_DIM + d for d in range(HEAD_DIM)]
    return jnp.array(idx, jnp.int32)


def kernel(x, c, ctx, c_ctx, w_mod, b_mod, g_attn_pre, g_attn_post, g_ffn_pre, g_ffn_post, w_in, w_out,
           da_lambda_q1, da_lambda_k1, da_lambda_q2, da_lambda_k2, da_subln, swa_sink, na_rpb,
           ret_gamma_fwd, ret_gamma_bwd, ffn_w_gate, ffn_w_up, ffn_w_down,
           moe_router, moe_w_gate, moe_w_up, moe_w_down):
    B, S, D = x.shape
    CTX = ctx.shape[1]
    T = S + CTX
    L = w_mod.shape[0]
    n_exp = moe_router.shape[-1]
    rows = S // GRID_W

    def token_tile(n_rows):
        return next(t for t in TOKEN_TILES if n_rows % t == 0)

    c8 = jnp.zeros((8, D), F32).at[:B].set(c).at[B].set(c_ctx)
    mod = _modulation(c8, w_mod, b_mod).reshape(L, 8, 6, D)
    mod = jnp.pad(mod, ((0, 0), (0, 0), (0, 2), (0, 0)))
    modv = jnp.stack([mod[:, :B], jnp.broadcast_to(mod[:, B:B + 1], (L, B, 8, D))], axis=2)

    tabs = _rope_tables(S, CTX)
    perm = _swa_perm()
    swa0 = CB_SWA_Q * 256
    xa = jnp.concatenate([x, ctx], axis=1)

    for l in range(L):
        lambda_init = 0.8 - 0.6 * math.exp(-0.3 * l)
        last = l == L - 1
        w_in_l = w_in[l]
        w_in_b = jnp.concatenate([w_in_l[:, :swa0], w_in_l[:, swa0:swa0 + 256][:, perm], w_in_l[:, swa0 + 256:]],
                                 axis=1).astype(BF16)
        w_out_l = w_out[l]
        w_out_b = jnp.concatenate([w_out_l[:256], w_out_l[256:512][perm], w_out_l[512:]], axis=0).astype(BF16)

        P, QT, VT = _inproj(xa, modv[l], g_attn_pre[l], w_in_b, tabs, S, token_tile(T))

        lamp = jnp.zeros((8, LANES), F32)
        for r, v in enumerate((da_lambda_q1[l], da_lambda_k1[l], da_lambda_q2[l], da_lambda_k2[l])):
            lamp = lamp.at[r, :DA_QK].set(v)
        subln_full = jnp.tile(da_subln[l], N_HEADS).reshape(1, GROUP_W)
        tq_da = next(t for t in (1024, 512, 256) if S % t == 0)
        tk_da = next(t for t in (768, 512, 256) if T % t == 0)
        ya = _diff_attention(P, QT, VT, lamp, subln_full, lambda_init, 0, S // tq_da, tq_da, 0, T // tk_da, tk_da)
        if not last:
            ya_ctx = _diff_attention(P, QT, VT, lamp, subln_full, lambda_init, S // CTX, 1, CTX, S // CTX, 1, CTX)
            ya = jnp.concatenate([ya, ya_ctx], axis=1)
        lg = jnp.stack([jax.nn.log_sigmoid(ret_gamma_fwd[l].astype(F32)),
                        jax.nn.log_sigmoid(ret_gamma_bwd[l].astype(F32))])
        lgv = jnp.zeros((8, GROUP_W), F32).at[:2].set(jnp.repeat(lg, HEAD_DIM, axis=1))
        yb, yn, of, ob = _local_mixers(P, swa_sink[l].astype(F32), _na_bias_tables(na_rpb[l], rows),
                                       lg.reshape(-1), lgv, S, CTX)

        R = S if last else T
        e = l // 2
        if l % 2 == 0:
            x1, h2 = _outproj(ya, yb, yn, of, ob, P, xa, modv[l], w_out_b, g_attn_post[l], g_ffn_pre[l],
                              R, S, token_tile(R))
            xa = _dense_ffn(h2, x1, modv[l], ffn_w_gate[e].astype(BF16), ffn_w_up[e].astype(BF16),
                            ffn_w_down[e].astype(BF16), g_ffn_post[l], S, token_tile(R))
        else:
            router = jnp.zeros((ROUTER_ROWS, D), F32).at[:n_exp].set(moe_router[e].T)
            x1, h2, eid, gw = _outproj(ya, yb, yn, of, ob, P, xa, modv[l], w_out_b, g_attn_post[l],
                                       g_ffn_pre[l], R, S, token_tile(R), router=router, n_exp=n_exp)
            tr = 512
            src, pos, tile_expert, tile_valid, n_rows = _moe_routing(
                jnp.swapaxes(eid[:, :TOP_K, :], 1, 2).reshape(B * R, TOP_K), n_exp, tr)
            ys = _moe_ffn(h2.reshape(B * R, D), src, tile_expert, tile_valid, moe_w_gate[e].astype(BF16),
                          moe_w_up[e].astype(BF16), moe_w_down[e].astype(BF16), n_rows, tr,
                          next(t for t in EXPERT_F_TILES if moe_w_gate.shape[-1] % t == 0))
            xa = _moe_combine(pos, ys, gw, x1, modv[l], g_ffn_post[l], S,
                              next(t for t in (512, 256) if R % t == 0))
    return xa[:, :S]
```

```python
import functools
import math

import jax
import jax.numpy as jnp
from jax import lax
from jax.experimental import pallas as pl
from jax.experimental.pallas import tpu as pltpu

F32 = jnp.float32
BF16 = jnp.bfloat16

GRID_W = 64
HEAD_DIM = 64
N_HEADS = 4
GROUP_W = N_HEADS * HEAD_DIM
DA_QK = 32
SWA_WINDOW = 128
NA_WIN_ROWS = 8
NA_WIN_COLS = 16
NA_TILE_ROWS = 4
LOCAL_TILE = NA_TILE_ROWS * GRID_W
ROPE_BASE = 10000.0
TOP_K = 2
EPS = 1e-6
NEG = -1e30
LOG2E = 1.4426950408889634
LANES = 128
ONES_ROWS = 16
ROUTER_ROWS = 16
TOKEN_TILES = (768, 512, 256)
EXPERT_F_TILES = (1792, 512, 256)
EXPERT_ROW_TILE = 512
DA_MAX_JUMP = 64.0
DA_CHUNK = 128
V7X_VMEM_BYTES = 64 * 1024 * 1024
VMEM_LIMIT = V7X_VMEM_BYTES * 7 // 8

CB_DA_Q, CB_DA_K, CB_DA_V = 0, 1, 2
CB_SWA_Q = 3
CB_SWA_K128, CB_SWA_V128 = 8, 9
CB_NA_Q, CB_NA_K, CB_NA_V = 5, 6, 7
CB_RET_Q, CB_RET_K, CB_RET_V, CB_RET_G = 8, 9, 10, 11
IN_WIDTH = 3072


def _params(sem):
    return pltpu.CompilerParams(dimension_semantics=sem, vmem_limit_bytes=VMEM_LIMIT)


def _silu(v):
    return v / (1.0 + jnp.exp(-v))


def _head_mask(shape, head_w, h, dim=1):
    return lax.broadcasted_iota(jnp.int32, shape, dim) // head_w == h


def _lane_pick(a, b):
    lane = lax.broadcasted_iota(jnp.int32, a.shape, 1)
    return jnp.where(lane < HEAD_DIM, a, b)


def _per_head_full(vals):
    return jnp.concatenate([_lane_pick(vals[0], vals[1]), _lane_pick(vals[2], vals[3])], axis=1)


def _group_mean_sq(a):
    sq = a * a
    r = lax.broadcasted_iota(jnp.int32, (GROUP_W, GROUP_W), 0) // HEAD_DIM
    c = lax.broadcasted_iota(jnp.int32, (GROUP_W, GROUP_W), 1) // HEAD_DIM
    g = jnp.where(r == c, 1.0, 0.0).astype(BF16)
    hi = sq.astype(BF16)
    lo = (sq - hi.astype(F32)).astype(BF16)
    tot = jnp.dot(hi, g, preferred_element_type=F32) + jnp.dot(lo, g, preferred_element_type=F32)
    return tot * (1.0 / HEAD_DIM)


def _row_mod(mod_ref, rows_are_ctx, j):
    return jnp.where(rows_are_ctx, mod_ref[0, 1, j:j + 1, :], mod_ref[0, 0, j:j + 1, :])


def _mod_kernel(c_ref, w_ref, b_ref, o_ref):
    s = _silu(c_ref[...])
    o_ref[0] = jnp.dot(s.astype(BF16), w_ref[0].astype(BF16), preferred_element_type=F32) + b_ref[0]


def _modulation(c8, w_mod, b_mod):
    L, D, W = w_mod.shape
    tn = 1536
    return pl.pallas_call(
        _mod_kernel,
        grid=(L, W // tn),
        in_specs=[pl.BlockSpec((8, D), lambda l, j: (0, 0)),
                  pl.BlockSpec((1, D, tn), lambda l, j: (l, 0, j)),
                  pl.BlockSpec((1, 1, tn), lambda l, j: (l, 0, j))],
        out_specs=pl.BlockSpec((1, 8, tn), lambda l, j: (l, 0, j)),
        out_shape=jax.ShapeDtypeStruct((L, 8, W), F32),
        compiler_params=_params(("arbitrary", "arbitrary")),
        name="modulation",
    )(c8, w_mod, b_mod.reshape(L, 1, W))


def _lane_block_ops():
    da_s = DA_QK ** -0.5 * LOG2E
    s64 = HEAD_DIM ** -0.5
    ops = []
    ops += [("r32", da_s)] * 2 + [("r32", 1.0)] * 2 + [(None, 1.0)] * 2
    ops += [("r64", s64 * LOG2E)] * 2 + [("r64", 1.0)] + [(None, 1.0)]
    ops += [(None, s64 * LOG2E)] * 2 + [(None, 1.0)] * 4
    ops += [("r64", 1.0)] * 2 + [("r64", s64)] * 2 + [(None, 1.0)] * 4
    return ops


def _rope(x, cos, sin_signed, w):
    lane = lax.broadcasted_iota(jnp.int32, x.shape, 1)
    first = (lane % (2 * w)) < w
    xr = jnp.where(first, pltpu.roll(x, LANES - w, 1), pltpu.roll(x, w, 1))
    return x * cos + xr * sin_signed


def _inproj_kernel(x_ref, mod_ref, g_ref, w_ref, c32_ref, s32_ref, c64_ref, s64_ref, o_ref, qt_ref, vt_ref,
                   *, tm, seq):
    i = pl.program_id(1)
    x = x_ref[0]
    rows = i * tm + lax.broadcasted_iota(jnp.int32, (tm, 1), 0)
    is_ctx = rows >= seq
    ms = jnp.mean(x * x, axis=-1, keepdims=True)
    h = x * lax.rsqrt(ms + EPS) * g_ref[...]
    h = h * (1.0 + _row_mod(mod_ref, is_ctx, 1)) + _row_mod(mod_ref, is_ctx, 0)
    hb = h.astype(BF16)
    ops = _lane_block_ops()
    for cb in range(IN_WIDTH // 256):
        p = jnp.dot(hb, w_ref[:, cb * 256:(cb + 1) * 256], preferred_element_type=F32)
        halves = []
        for hf in range(2):
            kind, scale = ops[cb * 2 + hf]
            ph = p[:, hf * LANES:(hf + 1) * LANES]
            if kind == "r32":
                ph = _rope(ph, c32_ref[...], s32_ref[...], DA_QK // 4)
            elif kind == "r64":
                ph = _rope(ph, c64_ref[...], s64_ref[...], HEAD_DIM // 4)
            if scale != 1.0:
                ph = ph * scale
            halves.append(ph)
        full = jnp.concatenate(halves, axis=1)
        o_ref[0, :, cb * 256:(cb + 1) * 256] = full.astype(BF16)
        if cb == CB_DA_Q:
            qt_ref[0] = full.T.astype(BF16)
        elif cb == CB_DA_V:
            vt_ref[0] = full.T.astype(BF16)


def _inproj(xa, modv, g, w_in_b, tabs, seq, tm):
    B, T, D = xa.shape
    kern = functools.partial(_inproj_kernel, tm=tm, seq=seq)
    tab_spec = pl.BlockSpec((tm, LANES), lambda b, i: (i, 0))
    return pl.pallas_call(
        kern,
        grid=(B, T // tm),
        in_specs=[pl.BlockSpec((1, tm, D), lambda b, i: (b, i, 0)),
                  pl.BlockSpec((1, 2, 8, D), lambda b, i: (b, 0, 0, 0)),
                  pl.BlockSpec((1, D), lambda b, i: (0, 0)),
                  pl.BlockSpec((D, IN_WIDTH), lambda b, i: (0, 0)),
                  tab_spec, tab_spec, tab_spec, tab_spec],
        out_specs=[pl.BlockSpec((1, tm, IN_WIDTH), lambda b, i: (b, i, 0)),
                   pl.BlockSpec((1, GROUP_W, tm), lambda b, i: (b, 0, i)),
                   pl.BlockSpec((1, GROUP_W, tm), lambda b, i: (b, 0, i))],
        out_shape=[jax.ShapeDtypeStruct((B, T, IN_WIDTH), BF16),
                   jax.ShapeDtypeStruct((B, GROUP_W, T), BF16),
                   jax.ShapeDtypeStruct((B, GROUP_W, T), BF16)],
        compiler_params=_params(("arbitrary", "arbitrary")),
        name="inproj",
    )(xa, modv, g.reshape(1, D), w_in_b, *tabs)


def _rope_tables(seq, ctx):
    t = jnp.arange(seq)
    row = (t // GRID_W).astype(F32)
    col = (t % GRID_W).astype(F32)
    out = []
    for dh in (DA_QK, HEAD_DIM):
        half = dh // 2
        quarter = half // 2
        lane = jnp.arange(LANES)
        d = lane % dh
        use_col = (d // half) == 1
        idx = (d % quarter).astype(F32)
        inv = ROPE_BASE ** (-idx * 2.0 / half)
        pos = jnp.where(use_col[None, :], col[:, None], row[:, None])
        ang = pos * inv[None, :]
        first = (d % half) < quarter
        cos = jnp.cos(ang)
        sin = jnp.where(first[None, :], -jnp.sin(ang), jnp.sin(ang))
        cos = jnp.concatenate([cos, jnp.ones((ctx, LANES), F32)], axis=0)
        sin = jnp.concatenate([sin, jnp.zeros((ctx, LANES), F32)], axis=0)
        out += [cos, sin]
    return out


def _rows_per_head(vals, tq):
    return jnp.concatenate([jnp.broadcast_to(v, (HEAD_DIM, tq)) for v in vals], axis=0)


def _da_kernel(lamp_ref, qt_ref, k_ref, vt_ref, g_ref, o_ref, qm_sc, m_sc, l_sc, acc_sc, st_sc, p_sc,
               *, tq, lambda_init):
    ki = pl.program_id(2)
    nk = pl.num_programs(2)

    @pl.when(ki == 0)
    def _():
        qt = qt_ref[0]
        for j in range(2 * N_HEADS):
            qm = jnp.where(_head_mask(qt.shape, DA_QK, j, 0), qt, jnp.zeros_like(qt))
            qm_sc[j] = qm
            m_sc[j] = jnp.max(jnp.dot(k_ref[0, 0:DA_CHUNK, :], qm, preferred_element_type=F32), axis=0,
                              keepdims=True)
        l_sc[...] = jnp.zeros(l_sc.shape, F32)
        acc_sc[...] = jnp.zeros(acc_sc.shape, F32)

    k = k_ref[0]
    vt = vt_ref[0]
    ones = jnp.ones((ONES_ROWS, vt.shape[1]), BF16)

    tk = k.shape[0]

    def vt_ext(h):
        return jnp.concatenate([vt[h * HEAD_DIM:(h + 1) * HEAD_DIM, :], ones], axis=0)

    jump = None
    pvs, mxs = [], []
    for j in range(2 * N_HEADS):
        st = jnp.dot(k, qm_sc[j], preferred_element_type=F32)
        m_prev = m_sc[j]
        mx = jnp.max(st, axis=0, keepdims=True)
        p = jnp.exp2(st - m_prev).astype(BF16)
        pvs.append(jnp.dot(vt_ext(j // 2), p, preferred_element_type=F32))
        mxs.append(mx)
        jump = mx - m_prev if jump is None else jnp.maximum(jump, mx - m_prev)
    keep = jnp.max(jump) <= DA_MAX_JUMP
    for j in range(2 * N_HEADS):
        h, t = j // 2, j % 2
        rows = slice(h * HEAD_DIM, (h + 1) * HEAD_DIM)
        m_prev = m_sc[j]
        m_new = jnp.maximum(m_prev, mxs[j])
        alpha = jnp.exp2(m_prev - m_new)
        l_old = l_sc[j]
        acc_old = acc_sc[t, rows, :]
        l_sc[j] = jnp.where(keep, alpha * (l_old + pvs[j][HEAD_DIM:HEAD_DIM + 1, :]), l_old)
        acc_sc[t, rows, :] = jnp.where(keep, alpha * (acc_old + pvs[j][:HEAD_DIM, :]), acc_old)
        m_sc[j] = jnp.where(keep, m_new, m_prev)

    @pl.when(jnp.logical_not(keep))
    def _():
        for j in range(2 * N_HEADS):
            h, t = j // 2, j % 2
            rows = slice(h * HEAD_DIM, (h + 1) * HEAD_DIM)
            st_sc[...] = jnp.dot(k, qm_sc[j], preferred_element_type=F32)
            run = st_sc[0:DA_CHUNK, :]
            for c in range(1, tk // DA_CHUNK):
                run = jnp.maximum(run, st_sc[c * DA_CHUNK:(c + 1) * DA_CHUNK, :])
            m_prev = m_sc[j]
            m_new = jnp.maximum(m_prev, jnp.max(run, axis=0, keepdims=True))
            alpha = jnp.exp2(m_prev - m_new)
            for c in range(tk // DA_CHUNK):
                cr = slice(c * DA_CHUNK, (c + 1) * DA_CHUNK)
                p_sc[cr, :] = jnp.exp2(st_sc[cr, :] - m_new).astype(BF16)
            pv = jnp.dot(vt_ext(h), p_sc[...], preferred_element_type=F32)
            l_sc[j] = alpha * l_sc[j] + pv[HEAD_DIM:HEAD_DIM + 1, :]
            m_sc[j] = m_new
            acc_sc[t, rows, :] = alpha * acc_sc[t, rows, :] + pv[:HEAD_DIM, :]

    @pl.when(ki == nk - 1)
    def _():
        lp = lamp_ref[...]
        lam = (jnp.exp(jnp.sum(lp[0:1] * lp[1:2], axis=1, keepdims=True))
               - jnp.exp(jnp.sum(lp[2:3] * lp[3:4], axis=1, keepdims=True))) + lambda_init
        o0 = acc_sc[0] / _rows_per_head([l_sc[2 * h] for h in range(N_HEADS)], tq)
        o1 = acc_sc[1] / _rows_per_head([l_sc[2 * h + 1] for h in range(N_HEADS)], tq)
        a = (o0 - lam * o1).T
        y = a * lax.rsqrt(_group_mean_sq(a) + EPS) * g_ref[...]
        o_ref[0] = (y * (1.0 - lambda_init)).astype(BF16)


def _diff_attention(P, QT, VT, lamp, subln_full, lambda_init, q0, nq, tq, k0, nk, tk):
    B = P.shape[0]
    kern = functools.partial(_da_kernel, tq=tq, lambda_init=lambda_init)
    return pl.pallas_call(
        kern,
        grid=(B, nq, nk),
        in_specs=[pl.BlockSpec((8, LANES), lambda b, qi, ki: (0, 0)),
                  pl.BlockSpec((1, GROUP_W, tq), lambda b, qi, ki: (b, 0, q0 + qi)),
                  pl.BlockSpec((1, tk, GROUP_W), lambda b, qi, ki: (b, k0 + ki, CB_DA_K)),
                  pl.BlockSpec((1, GROUP_W, tk), lambda b, qi, ki: (b, 0, k0 + ki)),
                  pl.BlockSpec((1, GROUP_W), lambda b, qi, ki: (0, 0))],
        out_specs=pl.BlockSpec((1, tq, GROUP_W), lambda b, qi, ki: (b, qi, 0)),
        out_shape=jax.ShapeDtypeStruct((B, nq * tq, GROUP_W), BF16),
        scratch_shapes=[pltpu.VMEM((2 * N_HEADS, GROUP_W, tq), BF16),
                        pltpu.VMEM((2 * N_HEADS, 1, tq), F32),
                        pltpu.VMEM((2 * N_HEADS, 1, tq), F32),
                        pltpu.VMEM((2, GROUP_W, tq), F32),
                        pltpu.VMEM((tk, tq), F32),
                        pltpu.VMEM((tk, tq), BF16)],
        compiler_params=_params(("arbitrary", "arbitrary", "arbitrary")),
        name="diff_attention",
    )(lamp, QT, P, VT, subln_full)


def _swa_kernel(sink_ref, q_ref, kp_ref, kc_ref, kn_ref, kx_ref, vp_ref, vc_ref, vn_ref, vx_ref, o_ref,
                *, seq, blk):
    i = pl.program_id(0)
    nloc = blk + 2 * SWA_WINDOW
    nkeys = nloc + kx_ref.shape[1]
    qpos = i * blk + lax.broadcasted_iota(jnp.int32, (blk, nkeys), 0)
    c = lax.broadcasted_iota(jnp.int32, (blk, nkeys), 1)
    kpos = i * blk - SWA_WINDOW + c
    valid = ((kpos >= 0) & (kpos < seq) & (jnp.abs(qpos - kpos) <= SWA_WINDOW) & (qpos < seq)) | (c >= nloc)
    for b in range(q_ref.shape[0]):
        q = q_ref[b]
        kall = jnp.concatenate([kp_ref[b], kc_ref[b], kn_ref[b], kx_ref[b]], axis=0)
        vall = jnp.concatenate([vp_ref[b], vc_ref[b], vn_ref[b], vx_ref[b]], axis=0)
        qs = []
        for g in range(2):
            qg = q[:, g * LANES:(g + 1) * LANES]
            for kvh in range(2):
                qs.append(jnp.where(_head_mask(qg.shape, HEAD_DIM, kvh), qg, jnp.zeros_like(qg)))
        s = lax.dot_general(jnp.concatenate(qs, axis=0), kall, (((1,), (1,)), ((), ())),
                            preferred_element_type=F32)
        vms = [jnp.where(_head_mask(vall.shape, HEAD_DIM, kvh), vall, jnp.zeros_like(vall)) for kvh in range(2)]
        vm = jnp.concatenate(vms, axis=0)
        for g in range(2):
            ps, invs = [], []
            for kvh in range(2):
                j = 2 * g + kvh
                sink = sink_ref[2 * kvh + g] * LOG2E
                sj = jnp.where(valid, s[j * blk:(j + 1) * blk], NEG)
                m = jnp.maximum(jnp.max(sj, axis=1, keepdims=True), sink)
                e = jnp.exp2(sj - m)
                den = jnp.sum(e, axis=1, keepdims=True) + jnp.exp2(sink - m)
                ps.append(e.astype(BF16))
                invs.append(jnp.broadcast_to(1.0 / den, (blk, LANES)))
            out = jnp.dot(jnp.concatenate(ps, axis=1), vm, preferred_element_type=F32)
            o_ref[b, :, g * LANES:(g + 1) * LANES] = (out * _lane_pick(invs[0], invs[1])).astype(BF16)


def _na_kernel(q_ref, k0_ref, k1_ref, k2_ref, kx_ref, v0_ref, v1_ref, v2_ref, vx_ref, bias_ref, o_ref, *, tq):
    nloc = 3 * tq
    for b in range(q_ref.shape[0]):
        q = q_ref[b]
        kall = jnp.concatenate([k0_ref[b], k1_ref[b], k2_ref[b], kx_ref[b]], axis=0)
        vall = jnp.concatenate([v0_ref[b], v1_ref[b], v2_ref[b], vx_ref[b]], axis=0)
        qs = jnp.concatenate(
            [jnp.where(_head_mask(q.shape, HEAD_DIM, h), q, jnp.zeros_like(q)) for h in range(N_HEADS)], axis=0)
        s = lax.dot_general(qs, kall, (((1,), (1,)), ((), ())), preferred_element_type=F32)
        ps, invs = [], []
        for h in range(N_HEADS):
            sh = s[h * tq:(h + 1) * tq]
            s_loc = sh[:, :nloc] + bias_ref[0, h]
            s_ctx = sh[:, nloc:]
            m = jnp.maximum(jnp.max(s_loc, axis=1, keepdims=True), jnp.max(s_ctx, axis=1, keepdims=True))
            e_loc = jnp.exp2(s_loc - m)
            e_ctx = jnp.exp2(s_ctx - m)
            den = jnp.sum(e_loc, axis=1, keepdims=True) + jnp.sum(e_ctx, axis=1, keepdims=True)
            ps.append(jnp.concatenate([e_loc, e_ctx], axis=1).astype(BF16))
            invs.append(jnp.broadcast_to(1.0 / den, (tq, LANES)))
        vm = jnp.concatenate(
            [jnp.where(_head_mask(vall.shape, HEAD_DIM, h), vall, jnp.zeros_like(vall)) for h in range(N_HEADS)],
            axis=0)
        out = jnp.dot(jnp.concatenate(ps, axis=1), vm, preferred_element_type=F32)
        o_ref[b] = (out * _per_head_full(invs)).astype(BF16)


def _na_bias_kernel(rt_ref, o_ref, sv_sc, *, rows):
    tr = NA_TILE_ROWS
    nrt = rows // tr
    nl = 3 * tr * GRID_W
    n_dcol = 2 * NA_WIN_COLS - 1
    rt = rt_ref[0] * LOG2E
    hi = rt.astype(BF16)
    r1 = rt - hi.astype(F32)
    mid = r1.astype(BF16)
    lo = (r1 - mid.astype(F32)).astype(BF16)
    lane = lax.broadcasted_iota(jnp.int32, (GRID_W, nl), 1)
    qc = lax.broadcasted_iota(jnp.int32, (GRID_W, nl), 0)
    kc = lane % GRID_W
    kr = lane // GRID_W
    dc = jnp.clip(kc - qc, 1 - NA_WIN_COLS, NA_WIN_COLS - 1) + (NA_WIN_COLS - 1)
    c0 = jnp.clip(qc - NA_WIN_COLS // 2, 0, GRID_W - NA_WIN_COLS)
    col_ok = (kc >= c0) & (kc < c0 + NA_WIN_COLS)
    irow = lax.broadcasted_iota(jnp.int32, (LANES, nl), 0)
    krl = lax.broadcasted_iota(jnp.int32, (LANES, nl), 1) // GRID_W
    for v, (t_idx, ws) in enumerate(((0, 0), (1, 0), (nrt - 1, nrt - 3))):
        for qr in range(tr):
            r = t_idx * tr + qr
            r0 = min(max(r - NA_WIN_ROWS // 2, 0), rows - NA_WIN_ROWS)
            d_row = jnp.clip(ws * tr + krl - r + (NA_WIN_ROWS - 1), 0, 2 * NA_WIN_ROWS - 2)
            onehot = jnp.where(irow == d_row, 1.0, 0.0).astype(BF16)
            sv_sc[...] = (jnp.dot(hi, onehot, preferred_element_type=F32)
                          + jnp.dot(mid, onehot, preferred_element_type=F32)
                          + jnp.dot(lo, onehot, preferred_element_type=F32))

            def pick(j, acc):
                return jnp.where(dc == j, sv_sc[pl.ds(j, 1), :], acc)

            acc = lax.fori_loop(0, n_dcol, pick, jnp.zeros((GRID_W, nl), F32), unroll=True)
            krow = ws * tr + kr
            ok = col_ok & (krow >= r0) & (krow < r0 + NA_WIN_ROWS)
            o_ref[v, 0, qr * GRID_W:(qr + 1) * GRID_W, :] = jnp.where(ok, acc, NEG)
    o_ref[3, 0] = jnp.full((tr * GRID_W, nl), NEG, F32)


def _na_bias_tables(rpb, rows):
    H, nr, ncol = rpb.shape
    rt = jnp.zeros((H, 32, LANES), F32).at[:, :ncol, :nr].set(jnp.swapaxes(rpb.astype(F32), 1, 2))
    tq = NA_TILE_ROWS * GRID_W
    return pl.pallas_call(
        functools.partial(_na_bias_kernel, rows=rows),
        grid=(H,),
        in_specs=[pl.BlockSpec((1, 32, LANES), lambda h: (h, 0, 0))],
        out_specs=pl.BlockSpec((4, 1, tq, 3 * tq), lambda h: (0, h, 0, 0)),
        out_shape=jax.ShapeDtypeStruct((4, H, tq, 3 * tq), F32),
        scratch_shapes=[pltpu.VMEM((32, 3 * tq), F32)],
        compiler_params=_params(("arbitrary",)),
        name="na_bias",
    )(rt)


def _ret_direction(q, k, v, state_ref, lgs_ref, lgv, forward, base):
    C = q.shape[0]
    i = lax.broadcasted_iota(jnp.int32, (C, C), 0)
    j = lax.broadcasted_iota(jnp.int32, (C, C), 1)
    dist = (i - j if forward else j - i).astype(F32)
    keep = dist >= 0 if forward else dist > 0
    dist = jnp.where(keep, dist, 0.0)
    qs = jnp.concatenate(
        [jnp.where(_head_mask(q.shape, HEAD_DIM, h), q, jnp.zeros_like(q)) for h in range(N_HEADS)], axis=0)
    s = lax.dot_general(qs, k, (((1,), (1,)), ((), ())), preferred_element_type=F32)
    atts = []
    for h in range(N_HEADS):
        decay = jnp.where(keep, jnp.exp(dist * lgs_ref[base + h]), 0.0)
        atts.append((s[h * C:(h + 1) * C] * decay).astype(BF16))
    vm = jnp.concatenate(
        [jnp.where(_head_mask(v.shape, HEAD_DIM, h), v, jnp.zeros_like(v)) for h in range(N_HEADS)], axis=0)
    intra = jnp.dot(jnp.concatenate(atts, axis=1), vm, preferred_element_type=F32)
    r = lax.broadcasted_iota(jnp.int32, (C, 1), 0).astype(F32)
    xi = jnp.exp((r + 1.0 if forward else C - r) * lgv)
    zeta = jnp.exp((C - 1.0 - r if forward else r) * lgv)
    state = state_ref[...]
    cross = jnp.dot((q.astype(F32) * xi).astype(BF16), state.astype(BF16), preferred_element_type=F32)
    kz_t = (k.astype(F32) * zeta).T.astype(BF16)
    u = jnp.dot(kz_t, v, preferred_element_type=F32)
    rr = lax.broadcasted_iota(jnp.int32, u.shape, 0) // HEAD_DIM
    cc = lax.broadcasted_iota(jnp.int32, u.shape, 1) // HEAD_DIM
    state_ref[...] = jnp.where(rr == cc, jnp.exp(C * lgv) * state + u, 0.0)
    return intra + cross


def _ret_kernel(lgs_ref, lgv_ref, qf_ref, kf_ref, vf_ref, qb_ref, kb_ref, vb_ref, of_ref, ob_ref, sf_sc, sb_sc):
    @pl.when(pl.program_id(0) == 0)
    def _():
        sf_sc[...] = jnp.zeros(sf_sc.shape, F32)
        sb_sc[...] = jnp.zeros(sb_sc.shape, F32)

    for b in range(qf_ref.shape[0]):
        of_ref[b] = _ret_direction(qf_ref[b], kf_ref[b], vf_ref[b], sf_sc.at[b], lgs_ref, lgv_ref[0:1], True, 0)
        ob_ref[b] = _ret_direction(qb_ref[b], kb_ref[b], vb_ref[b], sb_sc.at[b], lgs_ref, lgv_ref[1:2], False,
                                   N_HEADS)


def _local_kernel(sink_ref, lgs_ref, lgv_ref, *refs, seq, tile):
    swa_in, na_in, ret_in = refs[0:9], refs[9:19], refs[19:25]
    yb_ref, yn_ref, of_ref, ob_ref, sf_sc, sb_sc = refs[25:]
    _swa_kernel(sink_ref, *swa_in, yb_ref, seq=seq, blk=tile)
    _na_kernel(*na_in, yn_ref, tq=tile)
    _ret_kernel(lgs_ref, lgv_ref, *ret_in, of_ref, ob_ref, sf_sc, sb_sc)


def _local_mixers(P, sink, bias_tabs, lgs, lgv, seq, ctx):
    B, T, _ = P.shape
    tile = LOCAL_TILE
    W = SWA_WINDOW
    nt, nst, nw = T // tile, seq // tile, T // W
    nc = ctx // tile
    kern = functools.partial(_local_kernel, seq=seq, tile=tile)

    def rows(cb, width=GROUP_W, index=lambda i: i, size=tile):
        return pl.BlockSpec((B, size, width), lambda i: (0, index(i), cb))

    def ctx_rows(cb, width):
        return pl.BlockSpec((B, ctx, width), lambda i: (0, seq // ctx, cb))

    before = lambda i: jnp.clip(i * (tile // W) - 1, 0, nw - 1)
    after = lambda i: jnp.clip((i + 1) * (tile // W), 0, nw - 1)
    swa = [rows(CB_SWA_Q)]
    for cb in (CB_SWA_K128, CB_SWA_V128):
        swa += [rows(cb, LANES, before, W), rows(cb, LANES), rows(cb, LANES, after, W), ctx_rows(cb, LANES)]

    win = lambda i: jnp.clip(i - 1, 0, nst - 3)
    na = [rows(CB_NA_Q)]
    for cb in (CB_NA_K, CB_NA_V):
        na += [rows(cb, index=lambda i, o=o: win(i) + o) for o in range(3)] + [ctx_rows(cb, GROUP_W)]
    na.append(pl.BlockSpec((1, N_HEADS, tile, 3 * tile), lambda i: (jnp.where(i >= nst, 3, i - win(i)), 0, 0, 0)))

    fwd = lambda n: jnp.where(n < nc, nst + n, n - nc)
    bwd = lambda n: nt - 1 - n
    ret = [rows(cb, index=order) for order in (fwd, bwd) for cb in (CB_RET_Q, CB_RET_K, CB_RET_V)]

    tok = lambda order: pl.BlockSpec((B, tile, GROUP_W), lambda i: (0, order(i), 0))
    ident = lambda i: i
    return pl.pallas_call(
        kern,
        grid=(nt,),
        in_specs=[pl.BlockSpec(memory_space=pltpu.SMEM), pl.BlockSpec(memory_space=pltpu.SMEM),
                  pl.BlockSpec((8, GROUP_W), lambda i: (0, 0))] + swa + na + ret,
        out_specs=[tok(ident), tok(ident), tok(fwd), tok(bwd)],
        out_shape=[jax.ShapeDtypeStruct((B, T, GROUP_W), BF16)] * 2 + [jax.ShapeDtypeStruct((B, T, GROUP_W), F32)] * 2,
        scratch_shapes=[pltpu.VMEM((B, GROUP_W, GROUP_W), F32)] * 2,
        compiler_params=_params(("arbitrary",)),
        name="local_mixers",
    )(sink, lgs, lgv, *([P] * 9), *([P] * 9), bias_tabs, *([P] * 6))


def _outproj_kernel(*refs, tm, seq, n_exp):
    with_router = n_exp > 0
    if with_router:
        (ya_ref, yb_ref, yn_ref, of_ref, ob_ref, gt_ref, x_ref, mod_ref, w_ref, gpost_ref, gpre_ref, r_ref,
         x1_ref, h2_ref, eid_ref, gw_ref) = refs
    else:
        (ya_ref, yb_ref, yn_ref, of_ref, ob_ref, gt_ref, x_ref, mod_ref, w_ref, gpost_ref, gpre_ref,
         x1_ref, h2_ref) = refs
    i = pl.program_id(1)
    rows = i * tm + lax.broadcasted_iota(jnp.int32, (tm, 1), 0)
    is_ctx = rows >= seq
    o = of_ref[0] + ob_ref[0]
    yr = o * lax.rsqrt(_group_mean_sq(o) + EPS) * _silu(gt_ref[0].astype(F32))
    y = (jnp.dot(ya_ref[0], w_ref[0:256, :], preferred_element_type=F32)
         + jnp.dot(yb_ref[0], w_ref[256:512, :], preferred_element_type=F32)
         + jnp.dot(yn_ref[0], w_ref[512:768, :], preferred_element_type=F32)
         + jnp.dot(yr.astype(BF16), w_ref[768:1024, :], preferred_element_type=F32))
    yn = y * lax.rsqrt(jnp.mean(y * y, axis=-1, keepdims=True) + EPS) * gpost_ref[...]
    x1 = x_ref[0] + _row_mod(mod_ref, is_ctx, 2) * yn
    x1_ref[0] = x1
    h = x1 * lax.rsqrt(jnp.mean(x1 * x1, axis=-1, keepdims=True) + EPS) * gpre_ref[...]
    h = h * (1.0 + _row_mod(mod_ref, is_ctx, 4)) + _row_mod(mod_ref, is_ctx, 3)
    h2_ref[0] = h.astype(h2_ref.dtype)
    if with_router:
        r = r_ref[...]
        r_hi = r.astype(BF16)
        r_lo = (r - r_hi.astype(F32)).astype(BF16)
        h_hi = h.astype(BF16)
        h_lo = (h - h_hi.astype(F32)).astype(BF16)
        dn = (((1,), (1,)), ((), ()))
        lt = (lax.dot_general(r_hi, h_hi, dn, preferred_element_type=F32)
              + lax.dot_general(r_hi, h_lo, dn, preferred_element_type=F32)
              + lax.dot_general(r_lo, h_hi, dn, preferred_element_type=F32))
        e = lax.broadcasted_iota(jnp.int32, lt.shape, 0)
        lt = jnp.where(e < n_exp, lt, NEG)
        v1 = jnp.max(lt, axis=0, keepdims=True)
        i1 = jnp.min(jnp.where(lt == v1, e, ROUTER_ROWS), axis=0, keepdims=True)
        rest = jnp.where(e == i1, NEG, lt)
        v2 = jnp.max(rest, axis=0, keepdims=True)
        i2 = jnp.min(jnp.where(rest == v2, e, ROUTER_ROWS), axis=0, keepdims=True)
        e2 = jnp.exp(v2 - v1)
        w1 = 1.0 / (1.0 + e2)
        w2 = e2 / (1.0 + e2)
        eid_ref[0] = jnp.where(e == 0, i1, jnp.where(e == 1, i2, 0))[:8]
        wrow = jnp.where(e == 0, w1, jnp.where(e == 1, w2, 0.0))
        gw_ref[0] = jnp.concatenate([wrow, jnp.zeros((LANES - ROUTER_ROWS, tm), F32)], axis=0).T


def _outproj(ya, yb, yn, of, ob, P, xa, modv, w_out_b, g_post, g_pre2, rows, seq, tm, router=None, n_exp=0):
    B, _, D = xa.shape
    with_router = router is not None
    kern = functools.partial(_outproj_kernel, tm=tm, seq=seq, n_exp=n_exp)
    tok = lambda w: pl.BlockSpec((1, tm, w), lambda b, i: (b, i, 0))
    in_specs = [tok(GROUP_W), tok(GROUP_W), tok(GROUP_W), tok(GROUP_W), tok(GROUP_W),
                pl.BlockSpec((1, tm, GROUP_W), lambda b, i: (b, i, CB_RET_G)),
                tok(D),
                pl.BlockSpec((1, 2, 8, D), lambda b, i: (b, 0, 0, 0)),
                pl.BlockSpec((D, D), lambda b, i: (0, 0)),
                pl.BlockSpec((1, D), lambda b, i: (0, 0)),
                pl.BlockSpec((1, D), lambda b, i: (0, 0))]
    args = [ya, yb, yn, of, ob, P, xa, modv, w_out_b, g_post.reshape(1, D), g_pre2.reshape(1, D)]
    out_specs = [tok(D), tok(D)]
    out_shape = [jax.ShapeDtypeStruct((B, rows, D), F32),
                 jax.ShapeDtypeStruct((B, rows, D), F32 if with_router else BF16)]
    if with_router:
        in_specs.append(pl.BlockSpec((ROUTER_ROWS, D), lambda b, i: (0, 0)))
        args.append(router)
        out_specs += [pl.BlockSpec((1, 8, tm), lambda b, i: (b, 0, i)), tok(LANES)]
        out_shape += [jax.ShapeDtypeStruct((B, 8, rows), jnp.int32),
                      jax.ShapeDtypeStruct((B, rows, LANES), F32)]
    return pl.pallas_call(
        kern,
        grid=(B, rows // tm),
        in_specs=in_specs,
        out_specs=out_specs,
        out_shape=out_shape,
        compiler_params=_params(("arbitrary", "arbitrary")),
        name="outproj_router" if with_router else "outproj",
    )(*args)


def _ffn_kernel(h_ref, x_ref, mod_ref, wg_ref, wu_ref, wd_ref, gpost_ref, o_ref, *, tm, seq, fc):
    i = pl.program_id(1)
    rows = i * tm + lax.broadcasted_iota(jnp.int32, (tm, 1), 0)
    is_ctx = rows >= seq
    h = h_ref[0]
    acc = jnp.zeros((tm, o_ref.shape[-1]), F32)
    for c in range(wg_ref.shape[1] // fc):
        g = jnp.dot(h, wg_ref[:, c * fc:(c + 1) * fc], preferred_element_type=F32)
        u = jnp.dot(h, wu_ref[:, c * fc:(c + 1) * fc], preferred_element_type=F32)
        a = (_silu(g) * u).astype(BF16)
        acc = acc + jnp.dot(a, wd_ref[c * fc:(c + 1) * fc, :], preferred_element_type=F32)
    yn = acc * lax.rsqrt(jnp.mean(acc * acc, axis=-1, keepdims=True) + EPS) * gpost_ref[...]
    o_ref[0] = x_ref[0] + _row_mod(mod_ref, is_ctx, 5) * yn


def _dense_ffn(h2, x1, modv, wg, wu, wd, g_post, seq, tm):
    B, T, D = x1.shape
    F = wg.shape[1]
    kern = functools.partial(_ffn_kernel, tm=tm, seq=seq, fc=256)
    const = lambda shape: pl.BlockSpec(shape, lambda b, i: (0, 0), pipeline_mode=pl.Buffered(1))
    return pl.pallas_call(
        kern,
        grid=(B, T // tm),
        in_specs=[pl.BlockSpec((1, tm, D), lambda b, i: (b, i, 0)),
                  pl.BlockSpec((1, tm, D), lambda b, i: (b, i, 0)),
                  pl.BlockSpec((1, 2, 8, D), lambda b, i: (b, 0, 0, 0)),
                  const((D, F)), const((D, F)), const((F, D)),
                  pl.BlockSpec((1, D), lambda b, i: (0, 0))],
        out_specs=pl.BlockSpec((1, tm, D), lambda b, i: (b, i, 0)),
        out_shape=jax.ShapeDtypeStruct((B, T, D), F32),
        compiler_params=_params(("arbitrary", "arbitrary")),
        name="dense_ffn",
    )(h2, x1, modv, wg, wu, wd, g_post.reshape(1, D))


def _row_copy(src_hbm, row, dst_vmem, r, sem):
    return pltpu.make_async_copy(src_hbm.at[pl.ds(row, 1)], dst_vmem.at[pl.ds(r, 1)], sem)


def _moe_ffn_kernel(te_ref, tv_ref, src0_ref, srcn_ref, h_hbm, wg_ref, wu_ref, wd_ref, o_ref,
                    xg_sc, xb_sc, acc_sc, sem, *, tr):
    t = pl.program_id(0)
    f = pl.program_id(1)
    nt = pl.num_programs(0)
    nf = pl.num_programs(1)
    slot = t % 2

    def start_gather(src_ref, s):
        def body(r, c):
            _row_copy(h_hbm, src_ref[0, 0, r], xg_sc.at[s], r, sem.at[s]).start()
            return c

        lax.fori_loop(0, tr, body, 0, unroll=8)

    @pl.when((f == 0) & (t == 0) & (tv_ref[0] > 0))
    def _():
        start_gather(src0_ref, 0)

    @pl.when(f == 0)
    def _():
        acc_sc[...] = jnp.zeros(acc_sc.shape, F32)

    @pl.when((f == 0) & (tv_ref[t] > 0))
    def _():
        pltpu.make_async_copy(h_hbm.at[pl.ds(0, tr)], xg_sc.at[slot], sem.at[slot]).wait()
        xb_sc[...] = xg_sc[slot].astype(BF16)

    @pl.when((f == 0) & (t + 1 < nt) & (tv_ref[jnp.minimum(t + 1, nt - 1)] > 0))
    def _():
        start_gather(srcn_ref, 1 - slot)

    @pl.when(tv_ref[t] > 0)
    def _():
        x = xb_sc[...]
        g = jnp.dot(x, wg_ref[0], preferred_element_type=F32)
        u = jnp.dot(x, wu_ref[0], preferred_element_type=F32)
        a = (_silu(g) * u).astype(BF16)
        acc_sc[...] += jnp.dot(a, wd_ref[0], preferred_element_type=F32)

    @pl.when(f == nf - 1)
    def _():
        o_ref[...] = acc_sc[...]


def _moe_ffn(h_flat, src, tile_expert, tile_valid, wg, wu, wd, n_rows, tr, tf):
    D = h_flat.shape[1]
    F = wg.shape[2]
    nt = n_rows // tr
    return pl.pallas_call(
        functools.partial(_moe_ffn_kernel, tr=tr),
        grid_spec=pltpu.PrefetchScalarGridSpec(
            num_scalar_prefetch=2,
            grid=(nt, F // tf),
            in_specs=[pl.BlockSpec((1, 1, tr), lambda t, f, te, tv: (0, 0, 0), memory_space=pltpu.SMEM),
                      pl.BlockSpec((1, 1, tr), lambda t, f, te, tv: (jnp.minimum(t + 1, nt - 1), 0, 0),
                                   memory_space=pltpu.SMEM),
                      pl.BlockSpec(memory_space=pl.ANY),
                      pl.BlockSpec((1, D, tf), lambda t, f, te, tv: (te[t], 0, f)),
                      pl.BlockSpec((1, D, tf), lambda t, f, te, tv: (te[t], 0, f)),
                      pl.BlockSpec((1, tf, D), lambda t, f, te, tv: (te[t], f, 0))],
            out_specs=pl.BlockSpec((tr, D), lambda t, f, te, tv: (t, 0)),
            scratch_shapes=[pltpu.VMEM((2, tr, D), h_flat.dtype), pltpu.VMEM((tr, D), BF16),
                            pltpu.VMEM((tr, D), F32), pltpu.SemaphoreType.DMA((2,))]),
        out_shape=jax.ShapeDtypeStruct((n_rows, D), F32),
        compiler_params=_params(("arbitrary", "arbitrary")),
        name="moe_ffn",
    )(tile_expert, tile_valid, src.reshape(nt, 1, tr), src.reshape(nt, 1, tr), h_flat, wg, wu, wd)


def _combine_kernel(pos_ref, ys_hbm, gw_ref, x_ref, mod_ref, gpost_ref, o_ref, y1_sc, y2_sc, sem, *, tm, seq):
    def issue(r, c):
        _row_copy(ys_hbm, pos_ref[0, 0, r], y1_sc, r, sem.at[0]).start(priority=0)
        _row_copy(ys_hbm, pos_ref[0, 1, r], y2_sc, r, sem.at[1]).start(priority=1)
        return c

    lax.fori_loop(0, tm, issue, 0, unroll=8)
    pltpu.make_async_copy(ys_hbm.at[pl.ds(0, tm)], y1_sc, sem.at[0]).wait()
    pltpu.make_async_copy(ys_hbm.at[pl.ds(0, tm)], y2_sc, sem.at[1]).wait()
    rows = pl.program_id(1) * tm + lax.broadcasted_iota(jnp.int32, (tm, 1), 0)
    gw = gw_ref[0]
    y = gw[:, 0:1] * y1_sc[...] + gw[:, 1:2] * y2_sc[...]
    yn = y * lax.rsqrt(jnp.mean(y * y, axis=-1, keepdims=True) + EPS) * gpost_ref[...]
    o_ref[0] = x_ref[0] + _row_mod(mod_ref, rows >= seq, 5) * yn


def _moe_combine(pos, ys, gw, x1, modv, g_post, seq, tm):
    B, R, D = x1.shape
    ntile = R // tm
    kern = functools.partial(_combine_kernel, tm=tm, seq=seq)
    return pl.pallas_call(
        kern,
        grid=(B, ntile),
        in_specs=[pl.BlockSpec((1, TOP_K, tm), lambda b, i: (b * ntile + i, 0, 0), memory_space=pltpu.SMEM),
                  pl.BlockSpec(memory_space=pl.ANY),
                  pl.BlockSpec((1, tm, LANES), lambda b, i: (b, i, 0)),
                  pl.BlockSpec((1, tm, D), lambda b, i: (b, i, 0)),
                  pl.BlockSpec((1, 2, 8, D), lambda b, i: (b, 0, 0, 0)),
                  pl.BlockSpec((1, D), lambda b, i: (0, 0))],
        out_specs=pl.BlockSpec((1, tm, D), lambda b, i: (b, i, 0)),
        out_shape=jax.ShapeDtypeStruct((B, R, D), F32),
        scratch_shapes=[pltpu.VMEM((tm, D), F32), pltpu.VMEM((tm, D), F32), pltpu.SemaphoreType.DMA((2,))],
        compiler_params=_params(("arbitrary", "arbitrary")),
        name="moe_combine",
    )(pos.reshape(B * ntile, tm, TOP_K).transpose(0, 2, 1), ys, gw, x1, modv, g_post.reshape(1, D))


def _moe_routing(eid, n_exp, tr):
    N = eid.shape[0]
    e_flat = eid.reshape(-1)
    onehot = (e_flat[:, None] == jnp.arange(n_exp)[None, :]).astype(jnp.int32)
    rank = jnp.sum((jnp.cumsum(onehot, axis=0) - 1) * onehot, axis=1)
    counts = jnp.sum(onehot, axis=0)
    padded = ((counts + tr - 1) // tr) * tr
    ends = jnp.cumsum(padded)
    starts = ends - padded
    dest = starts[e_flat] + rank
    n_rows = (TOP_K * N // tr + n_exp) * tr
    src = jnp.zeros((n_rows,), jnp.int32).at[dest].set(jnp.arange(TOP_K * N, dtype=jnp.int32) // TOP_K)
    tile_start = jnp.arange(n_rows // tr, dtype=jnp.int32) * tr
    tile_expert = jnp.minimum(jnp.sum((tile_start[:, None] >= ends[None, :]).astype(jnp.int32), axis=1), n_exp - 1)
    tile_valid = (tile_start < ends[-1]).astype(jnp.int32)
    pos = dest.reshape(N, TOP_K).astype(jnp.int32)
    return src, pos, tile_expert, tile_valid, n_rows


def _swa_perm():
    idx = []
    for g in range(2):
        for kvh in range(2):
            idx += [(kvh * 2 + g) * HEAD_DIM + d for d in range(HEAD_DIM)]
    return jnp.array(idx, jnp.int32)


def kernel(x, c, ctx, c_ctx, w_mod, b_mod, g_attn_pre, g_attn_post, g_ffn_pre, g_ffn_post, w_in, w_out,
           da_lambda_q1, da_lambda_k1, da_lambda_q2, da_lambda_k2, da_subln, swa_sink, na_rpb,
           ret_gamma_fwd, ret_gamma_bwd, ffn_w_gate, ffn_w_up, ffn_w_down,
           moe_router, moe_w_gate, moe_w_up, moe_w_down):
    B, S, D = x.shape
    CTX = ctx.shape[1]
    T = S + CTX
    L = w_mod.shape[0]
    n_exp = moe_router.shape[-1]
    rows = S // GRID_W
    assert CTX == LOCAL_TILE and S % LOCAL_TILE == 0 and rows >= 3 * NA_TILE_ROWS, (S, CTX)
    assert B + 1 <= 8 and w_in.shape[-1] == IN_WIDTH and w_out.shape[1] == 4 * GROUP_W and D % LANES == 0
    assert n_exp <= ROUTER_ROWS and (TOP_K * B * S) % EXPERT_ROW_TILE == 0

    def token_tile(n_rows):
        return next(t for t in TOKEN_TILES if n_rows % t == 0)

    c8 = jnp.zeros((8, D), F32).at[:B].set(c).at[B].set(c_ctx)
    mod = _modulation(c8, w_mod, b_mod).reshape(L, 8, 6, D)
    mod = jnp.pad(mod, ((0, 0), (0, 0), (0, 2), (0, 0)))
    modv = jnp.stack([mod[:, :B], jnp.broadcast_to(mod[:, B:B + 1], (L, B, 8, D))], axis=2)

    tabs = _rope_tables(S, CTX)
    perm = _swa_perm()
    swa0 = CB_SWA_Q * 256
    xa = jnp.concatenate([x, ctx], axis=1)

    for l in range(L):
        lambda_init = 0.8 - 0.6 * math.exp(-0.3 * l)
        last = l == L - 1
        w_in_l = w_in[l]
        w_in_b = jnp.concatenate([w_in_l[:, :swa0], w_in_l[:, swa0:swa0 + 256][:, perm], w_in_l[:, swa0 + 256:]],
                                 axis=1).astype(BF16)
        w_out_l = w_out[l]
        w_out_b = jnp.concatenate([w_out_l[:256], w_out_l[256:512][perm], w_out_l[512:]], axis=0).astype(BF16)

        P, QT, VT = _inproj(xa, modv[l], g_attn_pre[l], w_in_b, tabs, S, token_tile(T))

        lamp = jnp.zeros((8, LANES), F32)
        for r, v in enumerate((da_lambda_q1[l], da_lambda_k1[l], da_lambda_q2[l], da_lambda_k2[l])):
            lamp = lamp.at[r, :DA_QK].set(v)
        subln_full = jnp.tile(da_subln[l], N_HEADS).reshape(1, GROUP_W)
        tq_da = next(t for t in (1024, 512, 256) if S % t == 0)
        tk_da = next(t for t in (768, 512, 256) if T % t == 0)
        ya = _diff_attention(P, QT, VT, lamp, subln_full, lambda_init, 0, S // tq_da, tq_da, 0, T // tk_da, tk_da)
        if not last:
            ya_ctx = _diff_attention(P, QT, VT, lamp, subln_full, lambda_init, S // CTX, 1, CTX, S // CTX, 1, CTX)
            ya = jnp.concatenate([ya, ya_ctx], axis=1)
        lg = jnp.stack([jax.nn.log_sigmoid(ret_gamma_fwd[l].astype(F32)),
                        jax.nn.log_sigmoid(ret_gamma_bwd[l].astype(F32))])
        lgv = jnp.zeros((8, GROUP_W), F32).at[:2].set(jnp.repeat(lg, HEAD_DIM, axis=1))
        yb, yn, of, ob = _local_mixers(P, swa_sink[l].astype(F32), _na_bias_tables(na_rpb[l], rows),
                                       lg.reshape(-1), lgv, S, CTX)

        R = S if last else T
        e = l // 2
        if l % 2 == 0:
            x1, h2 = _outproj(ya, yb, yn, of, ob, P, xa, modv[l], w_out_b, g_attn_post[l], g_ffn_pre[l],
                              R, S, token_tile(R))
            xa = _dense_ffn(h2, x1, modv[l], ffn_w_gate[e].astype(BF16), ffn_w_up[e].astype(BF16),
                            ffn_w_down[e].astype(BF16), g_ffn_post[l], S, token_tile(R))
        else:
            router = jnp.zeros((ROUTER_ROWS, D), F32).at[:n_exp].set(moe_router[e].T)
            x1, h2, eid, gw = _outproj(ya, yb, yn, of, ob, P, xa, modv[l], w_out_b, g_attn_post[l],
                                       g_ffn_pre[l], R, S, token_tile(R), router=router, n_exp=n_exp)
            tr = EXPERT_ROW_TILE
            src, pos, tile_expert, tile_valid, n_rows = _moe_routing(
                jnp.swapaxes(eid[:, :TOP_K, :], 1, 2).reshape(B * R, TOP_K), n_exp, tr)
            ys = _moe_ffn(h2.reshape(B * R, D), src, tile_expert, tile_valid, moe_w_gate[e].astype(BF16),
                          moe_w_up[e].astype(BF16), moe_w_down[e].astype(BF16), n_rows, tr,
                          next(t for t in EXPERT_F_TILES if moe_w_gate.shape[-1] % t == 0))
            xa = _moe_combine(pos, ys, gw, x1, modv[l], g_ffn_post[l], S,
                              next(t for t in (512, 256) if R % t == 0))
    return xa[:, :S]
```

```python
import functools
import math

import jax
import jax.numpy as jnp
from jax import lax
from jax.experimental import pallas as pl
from jax.experimental.pallas import tpu as pltpu

F32 = jnp.float32
BF16 = jnp.bfloat16

GRID_W = 64
HEAD_DIM = 64
N_HEADS = 4
GROUP_W = N_HEADS * HEAD_DIM
DA_QK = 32
SWA_WINDOW = 128
NA_WIN_ROWS = 8
NA_WIN_COLS = 16
NA_TILE_ROWS = 4
LOCAL_TILE = NA_TILE_ROWS * GRID_W
ROPE_BASE = 10000.0
TOP_K = 2
EPS = 1e-6
NEG = -1e30
LOG2E = 1.4426950408889634
LANES = 128
ONES_ROWS = 16
ROUTER_ROWS = 16
TOKEN_TILES = (768, 512, 256)
EXPERT_F_TILES = (1792, 512, 256)
DMA_ISSUE_UNROLL = 16
EXPERT_ROW_TILE = 512
DA_MAX_JUMP = 64.0
DA_CHUNK = 128
V7X_VMEM_BYTES = 64 * 1024 * 1024
VMEM_LIMIT = V7X_VMEM_BYTES * 7 // 8

CB_DA_Q, CB_DA_K, CB_DA_V = 0, 1, 2
CB_SWA_Q = 3
CB_SWA_K128, CB_SWA_V128 = 8, 9
CB_NA_Q, CB_NA_K, CB_NA_V = 5, 6, 7
CB_RET_Q, CB_RET_K, CB_RET_V, CB_RET_G = 8, 9, 10, 11
IN_WIDTH = 3072


def _params(sem):
    return pltpu.CompilerParams(dimension_semantics=sem, vmem_limit_bytes=VMEM_LIMIT)


def _silu(v):
    return v / (1.0 + jnp.exp(-v))


def _head_mask(shape, head_w, h, dim=1):
    return lax.broadcasted_iota(jnp.int32, shape, dim) // head_w == h


def _lane_pick(a, b):
    lane = lax.broadcasted_iota(jnp.int32, a.shape, 1)
    return jnp.where(lane < HEAD_DIM, a, b)


def _per_head_full(vals):
    return jnp.concatenate([_lane_pick(vals[0], vals[1]), _lane_pick(vals[2], vals[3])], axis=1)


def _group_mean_sq(a):
    sq = a * a
    r = lax.broadcasted_iota(jnp.int32, (GROUP_W, GROUP_W), 0) // HEAD_DIM
    c = lax.broadcasted_iota(jnp.int32, (GROUP_W, GROUP_W), 1) // HEAD_DIM
    g = jnp.where(r == c, 1.0, 0.0).astype(BF16)
    hi = sq.astype(BF16)
    lo = (sq - hi.astype(F32)).astype(BF16)
    tot = jnp.dot(hi, g, preferred_element_type=F32) + jnp.dot(lo, g, preferred_element_type=F32)
    return tot * (1.0 / HEAD_DIM)


def _row_mod(mod_ref, rows_are_ctx, j):
    return jnp.where(rows_are_ctx, mod_ref[0, 1, j:j + 1, :], mod_ref[0, 0, j:j + 1, :])


def _mod_kernel(c_ref, w_ref, b_ref, o_ref):
    s = _silu(c_ref[...])
    o_ref[0] = jnp.dot(s.astype(BF16), w_ref[0].astype(BF16), preferred_element_type=F32) + b_ref[0]


def _modulation(c8, w_mod, b_mod):
    L, D, W = w_mod.shape
    tn = 1536
    return pl.pallas_call(
        _mod_kernel,
        grid=(L, W // tn),
        in_specs=[pl.BlockSpec((8, D), lambda l, j: (0, 0)),
                  pl.BlockSpec((1, D, tn), lambda l, j: (l, 0, j)),
                  pl.BlockSpec((1, 1, tn), lambda l, j: (l, 0, j))],
        out_specs=pl.BlockSpec((1, 8, tn), lambda l, j: (l, 0, j)),
        out_shape=jax.ShapeDtypeStruct((L, 8, W), F32),
        compiler_params=_params(("arbitrary", "arbitrary")),
        name="modulation",
    )(c8, w_mod, b_mod.reshape(L, 1, W))


def _lane_block_ops():
    da_s = DA_QK ** -0.5 * LOG2E
    s64 = HEAD_DIM ** -0.5
    ops = []
    ops += [("r32", da_s)] * 2 + [("r32", 1.0)] * 2 + [(None, 1.0)] * 2
    ops += [("r64", s64 * LOG2E)] * 2 + [("r64", 1.0)] + [(None, 1.0)]
    ops += [(None, s64 * LOG2E)] * 2 + [(None, 1.0)] * 4
    ops += [("r64", 1.0)] * 2 + [("r64", s64)] * 2 + [(None, 1.0)] * 4
    return ops


def _rope(x, cos, sin_signed, w):
    lane = lax.broadcasted_iota(jnp.int32, x.shape, 1)
    first = (lane % (2 * w)) < w
    xr = jnp.where(first, pltpu.roll(x, LANES - w, 1), pltpu.roll(x, w, 1))
    return x * cos + xr * sin_signed


def _inproj_kernel(x_ref, mod_ref, g_ref, w_ref, c32_ref, s32_ref, c64_ref, s64_ref, o_ref, qt_ref, vt_ref,
                   *, tm, seq):
    i = pl.program_id(1)
    x = x_ref[0]
    rows = i * tm + lax.broadcasted_iota(jnp.int32, (tm, 1), 0)
    is_ctx = rows >= seq
    ms = jnp.mean(x * x, axis=-1, keepdims=True)
    h = x * lax.rsqrt(ms + EPS) * g_ref[...]
    h = h * (1.0 + _row_mod(mod_ref, is_ctx, 1)) + _row_mod(mod_ref, is_ctx, 0)
    hb = h.astype(BF16)
    ops = _lane_block_ops()
    for cb in range(IN_WIDTH // 256):
        p = jnp.dot(hb, w_ref[:, cb * 256:(cb + 1) * 256], preferred_element_type=F32)
        halves = []
        for hf in range(2):
            kind, scale = ops[cb * 2 + hf]
            ph = p[:, hf * LANES:(hf + 1) * LANES]
            if kind == "r32":
                ph = _rope(ph, c32_ref[...], s32_ref[...], DA_QK // 4)
            elif kind == "r64":
                ph = _rope(ph, c64_ref[...], s64_ref[...], HEAD_DIM // 4)
            if scale != 1.0:
                ph = ph * scale
            halves.append(ph)
        full = jnp.concatenate(halves, axis=1)
        o_ref[0, :, cb * 256:(cb + 1) * 256] = full.astype(BF16)
        if cb == CB_DA_Q:
            qt_ref[0] = full.T.astype(BF16)
        elif cb == CB_DA_V:
            vt_ref[0] = full.T.astype(BF16)


def _inproj(xa, modv, g, w_in_b, tabs, seq, tm):
    B, T, D = xa.shape
    kern = functools.partial(_inproj_kernel, tm=tm, seq=seq)
    tab_spec = pl.BlockSpec((tm, LANES), lambda b, i: (i, 0))
    return pl.pallas_call(
        kern,
        grid=(B, T // tm),
        in_specs=[pl.BlockSpec((1, tm, D), lambda b, i: (b, i, 0)),
                  pl.BlockSpec((1, 2, 8, D), lambda b, i: (b, 0, 0, 0)),
                  pl.BlockSpec((1, D), lambda b, i: (0, 0)),
                  pl.BlockSpec((D, IN_WIDTH), lambda b, i: (0, 0)),
                  tab_spec, tab_spec, tab_spec, tab_spec],
        out_specs=[pl.BlockSpec((1, tm, IN_WIDTH), lambda b, i: (b, i, 0)),
                   pl.BlockSpec((1, GROUP_W, tm), lambda b, i: (b, 0, i)),
                   pl.BlockSpec((1, GROUP_W, tm), lambda b, i: (b, 0, i))],
        out_shape=[jax.ShapeDtypeStruct((B, T, IN_WIDTH), BF16),
                   jax.ShapeDtypeStruct((B, GROUP_W, T), BF16),
                   jax.ShapeDtypeStruct((B, GROUP_W, T), BF16)],
        compiler_params=_params(("arbitrary", "arbitrary")),
        name="inproj",
    )(xa, modv, g.reshape(1, D), w_in_b, *tabs)


def _rope_tables(seq, ctx):
    t = jnp.arange(seq)
    row = (t // GRID_W).astype(F32)
    col = (t % GRID_W).astype(F32)
    out = []
    for dh in (DA_QK, HEAD_DIM):
        half = dh // 2
        quarter = half // 2
        lane = jnp.arange(LANES)
        d = lane % dh
        use_col = (d // half) == 1
        idx = (d % quarter).astype(F32)
        inv = ROPE_BASE ** (-idx * 2.0 / half)
        pos = jnp.where(use_col[None, :], col[:, None], row[:, None])
        ang = pos * inv[None, :]
        first = (d % half) < quarter
        cos = jnp.cos(ang)
        sin = jnp.where(first[None, :], -jnp.sin(ang), jnp.sin(ang))
        cos = jnp.concatenate([cos, jnp.ones((ctx, LANES), F32)], axis=0)
        sin = jnp.concatenate([sin, jnp.zeros((ctx, LANES), F32)], axis=0)
        out += [cos, sin]
    return out


def _rows_per_head(vals, tq):
    return jnp.concatenate([jnp.broadcast_to(v, (HEAD_DIM, tq)) for v in vals], axis=0)


def _da_kernel(lamp_ref, qt_ref, k_ref, vt_ref, g_ref, o_ref, qm_sc, m_sc, l_sc, acc_sc, st_sc, p_sc,
               *, tq, lambda_init):
    ki = pl.program_id(2)
    nk = pl.num_programs(2)

    @pl.when(ki == 0)
    def _():
        qt = qt_ref[0]
        for j in range(2 * N_HEADS):
            qm = jnp.where(_head_mask(qt.shape, DA_QK, j, 0), qt, jnp.zeros_like(qt))
            qm_sc[j] = qm
            m_sc[j] = jnp.max(jnp.dot(k_ref[0, 0:DA_CHUNK, :], qm, preferred_element_type=F32), axis=0,
                              keepdims=True)
        l_sc[...] = jnp.zeros(l_sc.shape, F32)
        acc_sc[...] = jnp.zeros(acc_sc.shape, F32)

    k = k_ref[0]
    vt = vt_ref[0]
    ones = jnp.ones((ONES_ROWS, vt.shape[1]), BF16)

    tk = k.shape[0]

    def vt_ext(h):
        return jnp.concatenate([vt[h * HEAD_DIM:(h + 1) * HEAD_DIM, :], ones], axis=0)

    jump = None
    pvs, mxs = [], []
    for j in range(2 * N_HEADS):
        st = jnp.dot(k, qm_sc[j], preferred_element_type=F32)
        m_prev = m_sc[j]
        mx = jnp.max(st, axis=0, keepdims=True)
        p = jnp.exp2(st - m_prev).astype(BF16)
        pvs.append(jnp.dot(vt_ext(j // 2), p, preferred_element_type=F32))
        mxs.append(mx)
        jump = mx - m_prev if jump is None else jnp.maximum(jump, mx - m_prev)
    keep = jnp.max(jump) <= DA_MAX_JUMP
    for j in range(2 * N_HEADS):
        h, t = j // 2, j % 2
        rows = slice(h * HEAD_DIM, (h + 1) * HEAD_DIM)
        m_prev = m_sc[j]
        m_new = jnp.maximum(m_prev, mxs[j])
        alpha = jnp.exp2(m_prev - m_new)
        l_old = l_sc[j]
        acc_old = acc_sc[t, rows, :]
        l_sc[j] = jnp.where(keep, alpha * (l_old + pvs[j][HEAD_DIM:HEAD_DIM + 1, :]), l_old)
        acc_sc[t, rows, :] = jnp.where(keep, alpha * (acc_old + pvs[j][:HEAD_DIM, :]), acc_old)
        m_sc[j] = jnp.where(keep, m_new, m_prev)

    @pl.when(jnp.logical_not(keep))
    def _():
        for j in range(2 * N_HEADS):
            h, t = j // 2, j % 2
            rows = slice(h * HEAD_DIM, (h + 1) * HEAD_DIM)
            st_sc[...] = jnp.dot(k, qm_sc[j], preferred_element_type=F32)
            run = st_sc[0:DA_CHUNK, :]
            for c in range(1, tk // DA_CHUNK):
                run = jnp.maximum(run, st_sc[c * DA_CHUNK:(c + 1) * DA_CHUNK, :])
            m_prev = m_sc[j]
            m_new = jnp.maximum(m_prev, jnp.max(run, axis=0, keepdims=True))
            alpha = jnp.exp2(m_prev - m_new)
            for c in range(tk // DA_CHUNK):
                cr = slice(c * DA_CHUNK, (c + 1) * DA_CHUNK)
                p_sc[cr, :] = jnp.exp2(st_sc[cr, :] - m_new).astype(BF16)
            pv = jnp.dot(vt_ext(h), p_sc[...], preferred_element_type=F32)
            l_sc[j] = alpha * l_sc[j] + pv[HEAD_DIM:HEAD_DIM + 1, :]
            m_sc[j] = m_new
            acc_sc[t, rows, :] = alpha * acc_sc[t, rows, :] + pv[:HEAD_DIM, :]

    @pl.when(ki == nk - 1)
    def _():
        lp = lamp_ref[...]
        lam = (jnp.exp(jnp.sum(lp[0:1] * lp[1:2], axis=1, keepdims=True))
               - jnp.exp(jnp.sum(lp[2:3] * lp[3:4], axis=1, keepdims=True))) + lambda_init
        o0 = acc_sc[0] / _rows_per_head([l_sc[2 * h] for h in range(N_HEADS)], tq)
        o1 = acc_sc[1] / _rows_per_head([l_sc[2 * h + 1] for h in range(N_HEADS)], tq)
        a = (o0 - lam * o1).T
        y = a * lax.rsqrt(_group_mean_sq(a) + EPS) * g_ref[...]
        o_ref[0] = (y * (1.0 - lambda_init)).astype(BF16)


def _diff_attention(P, QT, VT, lamp, subln_full, lambda_init, q0, nq, tq, k0, nk, tk):
    B = P.shape[0]
    kern = functools.partial(_da_kernel, tq=tq, lambda_init=lambda_init)
    return pl.pallas_call(
        kern,
        grid=(B, nq, nk),
        in_specs=[pl.BlockSpec((8, LANES), lambda b, qi, ki: (0, 0)),
                  pl.BlockSpec((1, GROUP_W, tq), lambda b, qi, ki: (b, 0, q0 + qi)),
                  pl.BlockSpec((1, tk, GROUP_W), lambda b, qi, ki: (b, k0 + ki, CB_DA_K)),
                  pl.BlockSpec((1, GROUP_W, tk), lambda b, qi, ki: (b, 0, k0 + ki)),
                  pl.BlockSpec((1, GROUP_W), lambda b, qi, ki: (0, 0))],
        out_specs=pl.BlockSpec((1, tq, GROUP_W), lambda b, qi, ki: (b, qi, 0)),
        out_shape=jax.ShapeDtypeStruct((B, nq * tq, GROUP_W), BF16),
        scratch_shapes=[pltpu.VMEM((2 * N_HEADS, GROUP_W, tq), BF16),
                        pltpu.VMEM((2 * N_HEADS, 1, tq), F32),
                        pltpu.VMEM((2 * N_HEADS, 1, tq), F32),
                        pltpu.VMEM((2, GROUP_W, tq), F32),
                        pltpu.VMEM((tk, tq), F32),
                        pltpu.VMEM((tk, tq), BF16)],
        compiler_params=_params(("arbitrary", "arbitrary", "arbitrary")),
        name="diff_attention",
    )(lamp, QT, P, VT, subln_full)


def _swa_kernel(sink_ref, q_ref, kp_ref, kc_ref, kn_ref, kx_ref, vp_ref, vc_ref, vn_ref, vx_ref, o_ref,
                *, seq, blk):
    i = pl.program_id(0)
    nloc = blk + 2 * SWA_WINDOW
    nkeys = nloc + kx_ref.shape[1]
    qpos = i * blk + lax.broadcasted_iota(jnp.int32, (blk, nkeys), 0)
    c = lax.broadcasted_iota(jnp.int32, (blk, nkeys), 1)
    kpos = i * blk - SWA_WINDOW + c
    valid = ((kpos >= 0) & (kpos < seq) & (jnp.abs(qpos - kpos) <= SWA_WINDOW) & (qpos < seq)) | (c >= nloc)
    for b in range(q_ref.shape[0]):
        q = q_ref[b]
        kall = jnp.concatenate([kp_ref[b], kc_ref[b], kn_ref[b], kx_ref[b]], axis=0)
        vall = jnp.concatenate([vp_ref[b], vc_ref[b], vn_ref[b], vx_ref[b]], axis=0)
        qs = []
        for g in range(2):
            qg = q[:, g * LANES:(g + 1) * LANES]
            for kvh in range(2):
                qs.append(jnp.where(_head_mask(qg.shape, HEAD_DIM, kvh), qg, jnp.zeros_like(qg)))
        s = lax.dot_general(jnp.concatenate(qs, axis=0), kall, (((1,), (1,)), ((), ())),
                            preferred_element_type=F32)
        vms = [jnp.where(_head_mask(vall.shape, HEAD_DIM, kvh), vall, jnp.zeros_like(vall)) for kvh in range(2)]
        vm = jnp.concatenate(vms, axis=0)
        for g in range(2):
            ps, invs = [], []
            for kvh in range(2):
                j = 2 * g + kvh
                sink = sink_ref[2 * kvh + g] * LOG2E
                sj = jnp.where(valid, s[j * blk:(j + 1) * blk], NEG)
                m = jnp.maximum(jnp.max(sj, axis=1, keepdims=True), sink)
                e = jnp.exp2(sj - m)
                den = jnp.sum(e, axis=1, keepdims=True) + jnp.exp2(sink - m)
                ps.append(e.astype(BF16))
                invs.append(jnp.broadcast_to(1.0 / den, (blk, LANES)))
            out = jnp.dot(jnp.concatenate(ps, axis=1), vm, preferred_element_type=F32)
            o_ref[b, :, g * LANES:(g + 1) * LANES] = (out * _lane_pick(invs[0], invs[1])).astype(BF16)


def _na_kernel(q_ref, k0_ref, k1_ref, k2_ref, kx_ref, v0_ref, v1_ref, v2_ref, vx_ref, bias_ref, o_ref, *, tq):
    nloc = 3 * tq
    for b in range(q_ref.shape[0]):
        q = q_ref[b]
        kall = jnp.concatenate([k0_ref[b], k1_ref[b], k2_ref[b], kx_ref[b]], axis=0)
        vall = jnp.concatenate([v0_ref[b], v1_ref[b], v2_ref[b], vx_ref[b]], axis=0)
        qs = jnp.concatenate(
            [jnp.where(_head_mask(q.shape, HEAD_DIM, h), q, jnp.zeros_like(q)) for h in range(N_HEADS)], axis=0)
        s = lax.dot_general(qs, kall, (((1,), (1,)), ((), ())), preferred_element_type=F32)
        ps, invs = [], []
        for h in range(N_HEADS):
            sh = s[h * tq:(h + 1) * tq]
            s_loc = sh[:, :nloc] + bias_ref[0, h]
            s_ctx = sh[:, nloc:]
            m = jnp.maximum(jnp.max(s_loc, axis=1, keepdims=True), jnp.max(s_ctx, axis=1, keepdims=True))
            e_loc = jnp.exp2(s_loc - m)
            e_ctx = jnp.exp2(s_ctx - m)
            den = jnp.sum(e_loc, axis=1, keepdims=True) + jnp.sum(e_ctx, axis=1, keepdims=True)
            ps.append(jnp.concatenate([e_loc, e_ctx], axis=1).astype(BF16))
            invs.append(jnp.broadcast_to(1.0 / den, (tq, LANES)))
        vm = jnp.concatenate(
            [jnp.where(_head_mask(vall.shape, HEAD_DIM, h), vall, jnp.zeros_like(vall)) for h in range(N_HEADS)],
            axis=0)
        out = jnp.dot(jnp.concatenate(ps, axis=1), vm, preferred_element_type=F32)
        o_ref[b] = (out * _per_head_full(invs)).astype(BF16)


def _na_bias_kernel(rt_ref, o_ref, sv_sc, *, rows):
    tr = NA_TILE_ROWS
    nrt = rows // tr
    nl = 3 * tr * GRID_W
    n_dcol = 2 * NA_WIN_COLS - 1
    rt = rt_ref[0] * LOG2E
    hi = rt.astype(BF16)
    r1 = rt - hi.astype(F32)
    mid = r1.astype(BF16)
    lo = (r1 - mid.astype(F32)).astype(BF16)
    lane = lax.broadcasted_iota(jnp.int32, (GRID_W, nl), 1)
    qc = lax.broadcasted_iota(jnp.int32, (GRID_W, nl), 0)
    kc = lane % GRID_W
    kr = lane // GRID_W
    dc = jnp.clip(kc - qc, 1 - NA_WIN_COLS, NA_WIN_COLS - 1) + (NA_WIN_COLS - 1)
    c0 = jnp.clip(qc - NA_WIN_COLS // 2, 0, GRID_W - NA_WIN_COLS)
    col_ok = (kc >= c0) & (kc < c0 + NA_WIN_COLS)
    irow = lax.broadcasted_iota(jnp.int32, (LANES, nl), 0)
    krl = lax.broadcasted_iota(jnp.int32, (LANES, nl), 1) // GRID_W
    for v, (t_idx, ws) in enumerate(((0, 0), (1, 0), (nrt - 1, nrt - 3))):
        for qr in range(tr):
            r = t_idx * tr + qr
            r0 = min(max(r - NA_WIN_ROWS // 2, 0), rows - NA_WIN_ROWS)
            d_row = jnp.clip(ws * tr + krl - r + (NA_WIN_ROWS - 1), 0, 2 * NA_WIN_ROWS - 2)
            onehot = jnp.where(irow == d_row, 1.0, 0.0).astype(BF16)
            sv_sc[...] = (jnp.dot(hi, onehot, preferred_element_type=F32)
                          + jnp.dot(mid, onehot, preferred_element_type=F32)
                          + jnp.dot(lo, onehot, preferred_element_type=F32))

            def pick(j, acc):
                return jnp.where(dc == j, sv_sc[pl.ds(j, 1), :], acc)

            acc = lax.fori_loop(0, n_dcol, pick, jnp.zeros((GRID_W, nl), F32), unroll=True)
            krow = ws * tr + kr
            ok = col_ok & (krow >= r0) & (krow < r0 + NA_WIN_ROWS)
            o_ref[v, 0, qr * GRID_W:(qr + 1) * GRID_W, :] = jnp.where(ok, acc, NEG)
    o_ref[3, 0] = jnp.full((tr * GRID_W, nl), NEG, F32)


def _na_bias_tables(rpb, rows):
    H, nr, ncol = rpb.shape
    rt = jnp.zeros((H, 32, LANES), F32).at[:, :ncol, :nr].set(jnp.swapaxes(rpb.astype(F32), 1, 2))
    tq = NA_TILE_ROWS * GRID_W
    return pl.pallas_call(
        functools.partial(_na_bias_kernel, rows=rows),
        grid=(H,),
        in_specs=[pl.BlockSpec((1, 32, LANES), lambda h: (h, 0, 0))],
        out_specs=pl.BlockSpec((4, 1, tq, 3 * tq), lambda h: (0, h, 0, 0)),
        out_shape=jax.ShapeDtypeStruct((4, H, tq, 3 * tq), F32),
        scratch_shapes=[pltpu.VMEM((32, 3 * tq), F32)],
        compiler_params=_params(("arbitrary",)),
        name="na_bias",
    )(rt)


def _ret_direction(q, k, v, state_ref, lgs_ref, lgv, forward, base):
    C = q.shape[0]
    i = lax.broadcasted_iota(jnp.int32, (C, C), 0)
    j = lax.broadcasted_iota(jnp.int32, (C, C), 1)
    dist = (i - j if forward else j - i).astype(F32)
    keep = dist >= 0 if forward else dist > 0
    dist = jnp.where(keep, dist, 0.0)
    qs = jnp.concatenate(
        [jnp.where(_head_mask(q.shape, HEAD_DIM, h), q, jnp.zeros_like(q)) for h in range(N_HEADS)], axis=0)
    s = lax.dot_general(qs, k, (((1,), (1,)), ((), ())), preferred_element_type=F32)
    atts = []
    for h in range(N_HEADS):
        decay = jnp.where(keep, jnp.exp(dist * lgs_ref[base + h]), 0.0)
        atts.append((s[h * C:(h + 1) * C] * decay).astype(BF16))
    vm = jnp.concatenate(
        [jnp.where(_head_mask(v.shape, HEAD_DIM, h), v, jnp.zeros_like(v)) for h in range(N_HEADS)], axis=0)
    intra = jnp.dot(jnp.concatenate(atts, axis=1), vm, preferred_element_type=F32)
    r = lax.broadcasted_iota(jnp.int32, (C, 1), 0).astype(F32)
    xi = jnp.exp((r + 1.0 if forward else C - r) * lgv)
    zeta = jnp.exp((C - 1.0 - r if forward else r) * lgv)
    state = state_ref[...]
    cross = jnp.dot((q.astype(F32) * xi).astype(BF16), state.astype(BF16), preferred_element_type=F32)
    kz_t = (k.astype(F32) * zeta).T.astype(BF16)
    u = jnp.dot(kz_t, v, preferred_element_type=F32)
    rr = lax.broadcasted_iota(jnp.int32, u.shape, 0) // HEAD_DIM
    cc = lax.broadcasted_iota(jnp.int32, u.shape, 1) // HEAD_DIM
    state_ref[...] = jnp.where(rr == cc, jnp.exp(C * lgv) * state + u, 0.0)
    return intra + cross


def _ret_kernel(lgs_ref, lgv_ref, qf_ref, kf_ref, vf_ref, qb_ref, kb_ref, vb_ref, of_ref, ob_ref, sf_sc, sb_sc):
    @pl.when(pl.program_id(0) == 0)
    def _():
        sf_sc[...] = jnp.zeros(sf_sc.shape, F32)
        sb_sc[...] = jnp.zeros(sb_sc.shape, F32)

    for b in range(qf_ref.shape[0]):
        of_ref[b] = _ret_direction(qf_ref[b], kf_ref[b], vf_ref[b], sf_sc.at[b], lgs_ref, lgv_ref[0:1], True, 0)
        ob_ref[b] = _ret_direction(qb_ref[b], kb_ref[b], vb_ref[b], sb_sc.at[b], lgs_ref, lgv_ref[1:2], False,
                                   N_HEADS)


def _local_kernel(sink_ref, lgs_ref, lgv_ref, *refs, seq, tile):
    swa_in, na_in, ret_in = refs[0:9], refs[9:19], refs[19:25]
    yb_ref, yn_ref, of_ref, ob_ref, sf_sc, sb_sc = refs[25:]
    _swa_kernel(sink_ref, *swa_in, yb_ref, seq=seq, blk=tile)
    _na_kernel(*na_in, yn_ref, tq=tile)
    _ret_kernel(lgs_ref, lgv_ref, *ret_in, of_ref, ob_ref, sf_sc, sb_sc)


def _local_mixers(P, sink, bias_tabs, lgs, lgv, seq, ctx):
    B, T, _ = P.shape
    tile = LOCAL_TILE
    W = SWA_WINDOW
    nt, nst, nw = T // tile, seq // tile, T // W
    nc = ctx // tile
    kern = functools.partial(_local_kernel, seq=seq, tile=tile)

    def rows(cb, width=GROUP_W, index=lambda i: i, size=tile):
        return pl.BlockSpec((B, size, width), lambda i: (0, index(i), cb))

    def ctx_rows(cb, width):
        return pl.BlockSpec((B, ctx, width), lambda i: (0, seq // ctx, cb))

    before = lambda i: jnp.clip(i * (tile // W) - 1, 0, nw - 1)
    after = lambda i: jnp.clip((i + 1) * (tile // W), 0, nw - 1)
    swa = [rows(CB_SWA_Q)]
    for cb in (CB_SWA_K128, CB_SWA_V128):
        swa += [rows(cb, LANES, before, W), rows(cb, LANES), rows(cb, LANES, after, W), ctx_rows(cb, LANES)]

    win = lambda i: jnp.clip(i - 1, 0, nst - 3)
    na = [rows(CB_NA_Q)]
    for cb in (CB_NA_K, CB_NA_V):
        na += [rows(cb, index=lambda i, o=o: win(i) + o) for o in range(3)] + [ctx_rows(cb, GROUP_W)]
    na.append(pl.BlockSpec((1, N_HEADS, tile, 3 * tile), lambda i: (jnp.where(i >= nst, 3, i - win(i)), 0, 0, 0)))

    fwd = lambda n: jnp.where(n < nc, nst + n, n - nc)
    bwd = lambda n: nt - 1 - n
    ret = [rows(cb, index=order) for order in (fwd, bwd) for cb in (CB_RET_Q, CB_RET_K, CB_RET_V)]

    tok = lambda order: pl.BlockSpec((B, tile, GROUP_W), lambda i: (0, order(i), 0))
    ident = lambda i: i
    return pl.pallas_call(
        kern,
        grid=(nt,),
        in_specs=[pl.BlockSpec(memory_space=pltpu.SMEM), pl.BlockSpec(memory_space=pltpu.SMEM),
                  pl.BlockSpec((8, GROUP_W), lambda i: (0, 0))] + swa + na + ret,
        out_specs=[tok(ident), tok(ident), tok(fwd), tok(bwd)],
        out_shape=[jax.ShapeDtypeStruct((B, T, GROUP_W), BF16)] * 2 + [jax.ShapeDtypeStruct((B, T, GROUP_W), F32)] * 2,
        scratch_shapes=[pltpu.VMEM((B, GROUP_W, GROUP_W), F32)] * 2,
        compiler_params=_params(("arbitrary",)),
        name="local_mixers",
    )(sink, lgs, lgv, *([P] * 9), *([P] * 9), bias_tabs, *([P] * 6))


def _outproj_kernel(*refs, tm, seq, n_exp):
    with_router = n_exp > 0
    if with_router:
        (ya_ref, yb_ref, yn_ref, of_ref, ob_ref, gt_ref, x_ref, mod_ref, w_ref, gpost_ref, gpre_ref, r_ref,
         x1_ref, h2_ref, eid_ref, gw_ref) = refs
    else:
        (ya_ref, yb_ref, yn_ref, of_ref, ob_ref, gt_ref, x_ref, mod_ref, w_ref, gpost_ref, gpre_ref,
         x1_ref, h2_ref) = refs
    i = pl.program_id(1)
    rows = i * tm + lax.broadcasted_iota(jnp.int32, (tm, 1), 0)
    is_ctx = rows >= seq
    o = of_ref[0] + ob_ref[0]
    yr = o * lax.rsqrt(_group_mean_sq(o) + EPS) * _silu(gt_ref[0].astype(F32))
    y = (jnp.dot(ya_ref[0], w_ref[0:256, :], preferred_element_type=F32)
         + jnp.dot(yb_ref[0], w_ref[256:512, :], preferred_element_type=F32)
         + jnp.dot(yn_ref[0], w_ref[512:768, :], preferred_element_type=F32)
         + jnp.dot(yr.astype(BF16), w_ref[768:1024, :], preferred_element_type=F32))
    yn = y * lax.rsqrt(jnp.mean(y * y, axis=-1, keepdims=True) + EPS) * gpost_ref[...]
    x1 = x_ref[0] + _row_mod(mod_ref, is_ctx, 2) * yn
    x1_ref[0] = x1
    h = x1 * lax.rsqrt(jnp.mean(x1 * x1, axis=-1, keepdims=True) + EPS) * gpre_ref[...]
    h = h * (1.0 + _row_mod(mod_ref, is_ctx, 4)) + _row_mod(mod_ref, is_ctx, 3)
    h2_ref[0] = h.astype(h2_ref.dtype)
    if with_router:
        r = r_ref[...]
        r_hi = r.astype(BF16)
        r_lo = (r - r_hi.astype(F32)).astype(BF16)
        h_hi = h.astype(BF16)
        h_lo = (h - h_hi.astype(F32)).astype(BF16)
        dn = (((1,), (1,)), ((), ()))
        lt = (lax.dot_general(r_hi, h_hi, dn, preferred_element_type=F32)
              + lax.dot_general(r_hi, h_lo, dn, preferred_element_type=F32)
              + lax.dot_general(r_lo, h_hi, dn, preferred_element_type=F32))
        e = lax.broadcasted_iota(jnp.int32, lt.shape, 0)
        lt = jnp.where(e < n_exp, lt, NEG)
        v1 = jnp.max(lt, axis=0, keepdims=True)
        i1 = jnp.min(jnp.where(lt == v1, e, ROUTER_ROWS), axis=0, keepdims=True)
        rest = jnp.where(e == i1, NEG, lt)
        v2 = jnp.max(rest, axis=0, keepdims=True)
        i2 = jnp.min(jnp.where(rest == v2, e, ROUTER_ROWS), axis=0, keepdims=True)
        e2 = jnp.exp(v2 - v1)
        w1 = 1.0 / (1.0 + e2)
        w2 = e2 / (1.0 + e2)
        eid_ref[0] = jnp.where(e == 0, i1, jnp.where(e == 1, i2, 0))[:8]
        wrow = jnp.where(e == 0, w1, jnp.where(e == 1, w2, 0.0))
        gw_ref[0] = jnp.concatenate([wrow, jnp.zeros((LANES - ROUTER_ROWS, tm), F32)], axis=0).T


def _outproj(ya, yb, yn, of, ob, P, xa, modv, w_out_b, g_post, g_pre2, rows, seq, tm, router=None, n_exp=0):
    B, _, D = xa.shape
    with_router = router is not None
    kern = functools.partial(_outproj_kernel, tm=tm, seq=seq, n_exp=n_exp)
    tok = lambda w: pl.BlockSpec((1, tm, w), lambda b, i: (b, i, 0))
    in_specs = [tok(GROUP_W), tok(GROUP_W), tok(GROUP_W), tok(GROUP_W), tok(GROUP_W),
                pl.BlockSpec((1, tm, GROUP_W), lambda b, i: (b, i, CB_RET_G)),
                tok(D),
                pl.BlockSpec((1, 2, 8, D), lambda b, i: (b, 0, 0, 0)),
                pl.BlockSpec((D, D), lambda b, i: (0, 0)),
                pl.BlockSpec((1, D), lambda b, i: (0, 0)),
                pl.BlockSpec((1, D), lambda b, i: (0, 0))]
    args = [ya, yb, yn, of, ob, P, xa, modv, w_out_b, g_post.reshape(1, D), g_pre2.reshape(1, D)]
    out_specs = [tok(D), tok(D)]
    out_shape = [jax.ShapeDtypeStruct((B, rows, D), F32),
                 jax.ShapeDtypeStruct((B, rows, D), F32 if with_router else BF16)]
    if with_router:
        in_specs.append(pl.BlockSpec((ROUTER_ROWS, D), lambda b, i: (0, 0)))
        args.append(router)
        out_specs += [pl.BlockSpec((1, 8, tm), lambda b, i: (b, 0, i)), tok(LANES)]
        out_shape += [jax.ShapeDtypeStruct((B, 8, rows), jnp.int32),
                      jax.ShapeDtypeStruct((B, rows, LANES), F32)]
    return pl.pallas_call(
        kern,
        grid=(B, rows // tm),
        in_specs=in_specs,
        out_specs=out_specs,
        out_shape=out_shape,
        compiler_params=_params(("arbitrary", "arbitrary")),
        name="outproj_router" if with_router else "outproj",
    )(*args)


def _ffn_kernel(h_ref, x_ref, mod_ref, wg_ref, wu_ref, wd_ref, gpost_ref, o_ref, *, tm, seq, fc):
    i = pl.program_id(1)
    rows = i * tm + lax.broadcasted_iota(jnp.int32, (tm, 1), 0)
    is_ctx = rows >= seq
    h = h_ref[0]
    acc = jnp.zeros((tm, o_ref.shape[-1]), F32)
    for c in range(wg_ref.shape[1] // fc):
        g = jnp.dot(h, wg_ref[:, c * fc:(c + 1) * fc], preferred_element_type=F32)
        u = jnp.dot(h, wu_ref[:, c * fc:(c + 1) * fc], preferred_element_type=F32)
        a = (_silu(g) * u).astype(BF16)
        acc = acc + jnp.dot(a, wd_ref[c * fc:(c + 1) * fc, :], preferred_element_type=F32)
    yn = acc * lax.rsqrt(jnp.mean(acc * acc, axis=-1, keepdims=True) + EPS) * gpost_ref[...]
    o_ref[0] = x_ref[0] + _row_mod(mod_ref, is_ctx, 5) * yn


def _dense_ffn(h2, x1, modv, wg, wu, wd, g_post, seq, tm):
    B, T, D = x1.shape
    F = wg.shape[1]
    kern = functools.partial(_ffn_kernel, tm=tm, seq=seq, fc=256)
    const = lambda shape: pl.BlockSpec(shape, lambda b, i: (0, 0), pipeline_mode=pl.Buffered(1))
    return pl.pallas_call(
        kern,
        grid=(B, T // tm),
        in_specs=[pl.BlockSpec((1, tm, D), lambda b, i: (b, i, 0)),
                  pl.BlockSpec((1, tm, D), lambda b, i: (b, i, 0)),
                  pl.BlockSpec((1, 2, 8, D), lambda b, i: (b, 0, 0, 0)),
                  const((D, F)), const((D, F)), const((F, D)),
                  pl.BlockSpec((1, D), lambda b, i: (0, 0))],
        out_specs=pl.BlockSpec((1, tm, D), lambda b, i: (b, i, 0)),
        out_shape=jax.ShapeDtypeStruct((B, T, D), F32),
        compiler_params=_params(("arbitrary", "arbitrary")),
        name="dense_ffn",
    )(h2, x1, modv, wg, wu, wd, g_post.reshape(1, D))


def _row_copy(src_hbm, row, dst_vmem, r, sem):
    return pltpu.make_async_copy(src_hbm.at[pl.ds(row, 1)], dst_vmem.at[pl.ds(r, 1)], sem)


def _moe_ffn_kernel(te_ref, tv_ref, src0_ref, srcn_ref, h_hbm, wg_ref, wu_ref, wd_ref, o_ref,
                    xg_sc, xb_sc, acc_sc, sem, *, tr):
    t = pl.program_id(0)
    f = pl.program_id(1)
    nt = pl.num_programs(0)
    nf = pl.num_programs(1)
    slot = t % 2

    def start_gather(src_ref, s):
        def body(r, c):
            _row_copy(h_hbm, src_ref[0, 0, r], xg_sc.at[s], r, sem.at[s]).start()
            return c

        lax.fori_loop(0, tr, body, 0, unroll=DMA_ISSUE_UNROLL)

    @pl.when((f == 0) & (t == 0) & (tv_ref[0] > 0))
    def _():
        start_gather(src0_ref, 0)

    @pl.when(f == 0)
    def _():
        acc_sc[...] = jnp.zeros(acc_sc.shape, F32)

    @pl.when((f == 0) & (tv_ref[t] > 0))
    def _():
        pltpu.make_async_copy(h_hbm.at[pl.ds(0, tr)], xg_sc.at[slot], sem.at[slot]).wait()
        xb_sc[...] = xg_sc[slot].astype(BF16)

    @pl.when((f == 0) & (t + 1 < nt) & (tv_ref[jnp.minimum(t + 1, nt - 1)] > 0))
    def _():
        start_gather(srcn_ref, 1 - slot)

    @pl.when(tv_ref[t] > 0)
    def _():
        x = xb_sc[...]
        g = jnp.dot(x, wg_ref[0], preferred_element_type=F32)
        u = jnp.dot(x, wu_ref[0], preferred_element_type=F32)
        a = (_silu(g) * u).astype(BF16)
        acc_sc[...] += jnp.dot(a, wd_ref[0], preferred_element_type=F32)

    @pl.when(f == nf - 1)
    def _():
        o_ref[...] = acc_sc[...]


def _moe_ffn(h_flat, src, tile_expert, tile_valid, wg, wu, wd, n_rows, tr, tf):
    D = h_flat.shape[1]
    F = wg.shape[2]
    nt = n_rows // tr
    return pl.pallas_call(
        functools.partial(_moe_ffn_kernel, tr=tr),
        grid_spec=pltpu.PrefetchScalarGridSpec(
            num_scalar_prefetch=2,
            grid=(nt, F // tf),
            in_specs=[pl.BlockSpec((1, 1, tr), lambda t, f, te, tv: (0, 0, 0), memory_space=pltpu.SMEM),
                      pl.BlockSpec((1, 1, tr), lambda t, f, te, tv: (jnp.minimum(t + 1, nt - 1), 0, 0),
                                   memory_space=pltpu.SMEM),
                      pl.BlockSpec(memory_space=pl.ANY),
                      pl.BlockSpec((1, D, tf), lambda t, f, te, tv: (te[t], 0, f)),
                      pl.BlockSpec((1, D, tf), lambda t, f, te, tv: (te[t], 0, f)),
                      pl.BlockSpec((1, tf, D), lambda t, f, te, tv: (te[t], f, 0))],
            out_specs=pl.BlockSpec((tr, D), lambda t, f, te, tv: (t, 0)),
            scratch_shapes=[pltpu.VMEM((2, tr, D), h_flat.dtype), pltpu.VMEM((tr, D), BF16),
                            pltpu.VMEM((tr, D), F32), pltpu.SemaphoreType.DMA((2,))]),
        out_shape=jax.ShapeDtypeStruct((n_rows, D), F32),
        compiler_params=_params(("arbitrary", "arbitrary")),
        name="moe_ffn",
    )(tile_expert, tile_valid, src.reshape(nt, 1, tr), src.reshape(nt, 1, tr), h_flat, wg, wu, wd)


def _combine_kernel(pos_ref, ys_hbm, gw_ref, x_ref, mod_ref, gpost_ref, o_ref, y1_sc, y2_sc, sem, *, tm, seq):
    def issue(r, c):
        _row_copy(ys_hbm, pos_ref[0, 0, r], y1_sc, r, sem.at[0]).start(priority=0)
        _row_copy(ys_hbm, pos_ref[0, 1, r], y2_sc, r, sem.at[1]).start(priority=1)
        return c

    lax.fori_loop(0, tm, issue, 0, unroll=DMA_ISSUE_UNROLL)
    pltpu.make_async_copy(ys_hbm.at[pl.ds(0, tm)], y1_sc, sem.at[0]).wait()
    pltpu.make_async_copy(ys_hbm.at[pl.ds(0, tm)], y2_sc, sem.at[1]).wait()
    rows = pl.program_id(1) * tm + lax.broadcasted_iota(jnp.int32, (tm, 1), 0)
    gw = gw_ref[0]
    y = gw[:, 0:1] * y1_sc[...] + gw[:, 1:2] * y2_sc[...]
    yn = y * lax.rsqrt(jnp.mean(y * y, axis=-1, keepdims=True) + EPS) * gpost_ref[...]
    o_ref[0] = x_ref[0] + _row_mod(mod_ref, rows >= seq, 5) * yn


def _moe_combine(pos, ys, gw, x1, modv, g_post, seq, tm):
    B, R, D = x1.shape
    ntile = R // tm
    kern = functools.partial(_combine_kernel, tm=tm, seq=seq)
    return pl.pallas_call(
        kern,
        grid=(B, ntile),
        in_specs=[pl.BlockSpec((1, TOP_K, tm), lambda b, i: (b * ntile + i, 0, 0), memory_space=pltpu.SMEM),
                  pl.BlockSpec(memory_space=pl.ANY),
                  pl.BlockSpec((1, tm, LANES), lambda b, i: (b, i, 0)),
                  pl.BlockSpec((1, tm, D), lambda b, i: (b, i, 0)),
                  pl.BlockSpec((1, 2, 8, D), lambda b, i: (b, 0, 0, 0)),
                  pl.BlockSpec((1, D), lambda b, i: (0, 0))],
        out_specs=pl.BlockSpec((1, tm, D), lambda b, i: (b, i, 0)),
        out_shape=jax.ShapeDtypeStruct((B, R, D), F32),
        scratch_shapes=[pltpu.VMEM((tm, D), F32), pltpu.VMEM((tm, D), F32), pltpu.SemaphoreType.DMA((2,))],
        compiler_params=_params(("arbitrary", "arbitrary")),
        name="moe_combine",
    )(pos.reshape(B * ntile, tm, TOP_K).transpose(0, 2, 1), ys, gw, x1, modv, g_post.reshape(1, D))


def _moe_routing(eid, n_exp, tr):
    N = eid.shape[0]
    e_flat = eid.reshape(-1)
    onehot = (e_flat[:, None] == jnp.arange(n_exp)[None, :]).astype(jnp.int32)
    rank = jnp.sum((jnp.cumsum(onehot, axis=0) - 1) * onehot, axis=1)
    counts = jnp.sum(onehot, axis=0)
    padded = ((counts + tr - 1) // tr) * tr
    ends = jnp.cumsum(padded)
    starts = ends - padded
    dest = starts[e_flat] + rank
    n_rows = (TOP_K * N // tr + n_exp) * tr
    src = jnp.zeros((n_rows,), jnp.int32).at[dest].set(jnp.arange(TOP_K * N, dtype=jnp.int32) // TOP_K)
    tile_start = jnp.arange(n_rows // tr, dtype=jnp.int32) * tr
    tile_expert = jnp.minimum(jnp.sum((tile_start[:, None] >= ends[None, :]).astype(jnp.int32), axis=1), n_exp - 1)
    tile_valid = (tile_start < ends[-1]).astype(jnp.int32)
    pos = dest.reshape(N, TOP_K).astype(jnp.int32)
    return src, pos, tile_expert, tile_valid, n_rows


def _swa_perm():
    idx = []
    for g in range(2):
        for kvh in range(2):
            idx += [(kvh * 2 + g) * HEAD_DIM + d for d in range(HEAD_DIM)]
    return jnp.array(idx, jnp.int32)


def kernel(x, c, ctx, c_ctx, w_mod, b_mod, g_attn_pre, g_attn_post, g_ffn_pre, g_ffn_post, w_in, w_out,
           da_lambda_q1, da_lambda_k1, da_lambda_q2, da_lambda_k2, da_subln, swa_sink, na_rpb,
           ret_gamma_fwd, ret_gamma_bwd, ffn_w_gate, ffn_w_up, ffn_w_down,
           moe_router, moe_w_gate, moe_w_up, moe_w_down):
    B, S, D = x.shape
    CTX = ctx.shape[1]
    T = S + CTX
    L = w_mod.shape[0]
    n_exp = moe_router.shape[-1]
    rows = S // GRID_W
    assert CTX == LOCAL_TILE and S % LOCAL_TILE == 0 and rows >= 3 * NA_TILE_ROWS, (S, CTX)
    assert B + 1 <= 8 and w_in.shape[-1] == IN_WIDTH and w_out.shape[1] == 4 * GROUP_W and D % LANES == 0
    assert n_exp <= ROUTER_ROWS and (TOP_K * B * S) % EXPERT_ROW_TILE == 0

    def token_tile(n_rows):
        return next(t for t in TOKEN_TILES if n_rows % t == 0)

    c8 = jnp.zeros((8, D), F32).at[:B].set(c).at[B].set(c_ctx)
    mod = _modulation(c8, w_mod, b_mod).reshape(L, 8, 6, D)
    mod = jnp.pad(mod, ((0, 0), (0, 0), (0, 2), (0, 0)))
    modv = jnp.stack([mod[:, :B], jnp.broadcast_to(mod[:, B:B + 1], (L, B, 8, D))], axis=2)

    tabs = _rope_tables(S, CTX)
    perm = _swa_perm()
    swa0 = CB_SWA_Q * 256
    xa = jnp.concatenate([x, ctx], axis=1)

    for l in range(L):
        lambda_init = 0.8 - 0.6 * math.exp(-0.3 * l)
        last = l == L - 1
        w_in_l = w_in[l]
        w_in_b = jnp.concatenate([w_in_l[:, :swa0], w_in_l[:, swa0:swa0 + 256][:, perm], w_in_l[:, swa0 + 256:]],
                                 axis=1).astype(BF16)
        w_out_l = w_out[l]
        w_out_b = jnp.concatenate([w_out_l[:256], w_out_l[256:512][perm], w_out_l[512:]], axis=0).astype(BF16)

        P, QT, VT = _inproj(xa, modv[l], g_attn_pre[l], w_in_b, tabs, S, token_tile(T))

        lamp = jnp.zeros((8, LANES), F32)
        for r, v in enumerate((da_lambda_q1[l], da_lambda_k1[l], da_lambda_q2[l], da_lambda_k2[l])):
            lamp = lamp.at[r, :DA_QK].set(v)
        subln_full = jnp.tile(da_subln[l], N_HEADS).reshape(1, GROUP_W)
        tq_da = next(t for t in (1024, 512, 256) if S % t == 0)
        tk_da = next(t for t in (768, 512, 256) if T % t == 0)
        ya = _diff_attention(P, QT, VT, lamp, subln_full, lambda_init, 0, S // tq_da, tq_da, 0, T // tk_da, tk_da)
        if not last:
            ya_ctx = _diff_attention(P, QT, VT, lamp, subln_full, lambda_init, S // CTX, 1, CTX, S // CTX, 1, CTX)
            ya = jnp.concatenate([ya, ya_ctx], axis=1)
        lg = jnp.stack([jax.nn.log_sigmoid(ret_gamma_fwd[l].astype(F32)),
                        jax.nn.log_sigmoid(ret_gamma_bwd[l].astype(F32))])
        lgv = jnp.zeros((8, GROUP_W), F32).at[:2].set(jnp.repeat(lg, HEAD_DIM, axis=1))
        yb, yn, of, ob = _local_mixers(P, swa_sink[l].astype(F32), _na_bias_tables(na_rpb[l], rows),
                                       lg.reshape(-1), lgv, S, CTX)

        R = S if last else T
        e = l // 2
        if l % 2 == 0:
            x1, h2 = _outproj(ya, yb, yn, of, ob, P, xa, modv[l], w_out_b, g_attn_post[l], g_ffn_pre[l],
                              R, S, token_tile(R))
            xa = _dense_ffn(h2, x1, modv[l], ffn_w_gate[e].astype(BF16), ffn_w_up[e].astype(BF16),
                            ffn_w_down[e].astype(BF16), g_ffn_post[l], S, token_tile(R))
        else:
            router = jnp.zeros((ROUTER_ROWS, D), F32).at[:n_exp].set(moe_router[e].T)
            x1, h2, eid, gw = _outproj(ya, yb, yn, of, ob, P, xa, modv[l], w_out_b, g_attn_post[l],
                                       g_ffn_pre[l], R, S, token_tile(R), router=router, n_exp=n_exp)
            tr = EXPERT_ROW_TILE
            src, pos, tile_expert, tile_valid, n_rows = _moe_routing(
                jnp.swapaxes(eid[:, :TOP_K, :], 1, 2).reshape(B * R, TOP_K), n_exp, tr)
            ys = _moe_ffn(h2.reshape(B * R, D), src, tile_expert, tile_valid, moe_w_gate[e].astype(BF16),
                          moe_w_up[e].astype(BF16), moe_w_down[e].astype(BF16), n_rows, tr,
                          next(t for t in EXPERT_F_TILES if moe_w_gate.shape[-1] % t == 0))
            xa = _moe_combine(pos, ys, gw, x1, modv[l], g_ffn_post[l], S,
                              next(t for t in (1024, 512, 256) if R % t == 0))
    return xa[:, :S]
```

```python
import functools
import math

import jax
import jax.numpy as jnp
from jax import lax
from jax.experimental import pallas as pl
from jax.experimental.pallas import tpu as pltpu

F32 = jnp.float32
BF16 = jnp.bfloat16

GRID_W = 64
HEAD_DIM = 64
N_HEADS = 4
GROUP_W = N_HEADS * HEAD_DIM
DA_QK = 32
SWA_WINDOW = 128
NA_WIN_ROWS = 8
NA_WIN_COLS = 16
NA_TILE_ROWS = 4
LOCAL_TILE = NA_TILE_ROWS * GRID_W
ROPE_BASE = 10000.0
TOP_K = 2
EPS = 1e-6
NEG = -1e30
LOG2E = 1.4426950408889634
LANES = 128
ONES_ROWS = 16
ROUTER_ROWS = 16
TOKEN_TILES = (768, 512, 256)
EXPERT_F_TILES = (1792, 512, 256)
EXPERT_ROW_TILE = 512
DA_MAX_JUMP = 64.0
DA_CHUNK = 128
V7X_VMEM_BYTES = 64 * 1024 * 1024
VMEM_LIMIT = V7X_VMEM_BYTES * 7 // 8

CB_DA_Q, CB_DA_K, CB_DA_V = 0, 1, 2
CB_SWA_Q = 3
CB_SWA_K128, CB_SWA_V128 = 8, 9
CB_NA_Q, CB_NA_K, CB_NA_V = 5, 6, 7
CB_RET_Q, CB_RET_K, CB_RET_V, CB_RET_G = 8, 9, 10, 11
IN_WIDTH = 3072


def _params(sem):
    return pltpu.CompilerParams(dimension_semantics=sem, vmem_limit_bytes=VMEM_LIMIT)


def _silu(v):
    return v / (1.0 + jnp.exp(-v))


def _head_mask(shape, head_w, h, dim=1):
    return lax.broadcasted_iota(jnp.int32, shape, dim) // head_w == h


def _lane_pick(a, b):
    lane = lax.broadcasted_iota(jnp.int32, a.shape, 1)
    return jnp.where(lane < HEAD_DIM, a, b)


def _per_head_full(vals):
    return jnp.concatenate([_lane_pick(vals[0], vals[1]), _lane_pick(vals[2], vals[3])], axis=1)


def _group_mean_sq(a):
    sq = a * a
    r = lax.broadcasted_iota(jnp.int32, (GROUP_W, GROUP_W), 0) // HEAD_DIM
    c = lax.broadcasted_iota(jnp.int32, (GROUP_W, GROUP_W), 1) // HEAD_DIM
    g = jnp.where(r == c, 1.0, 0.0).astype(BF16)
    hi = sq.astype(BF16)
    lo = (sq - hi.astype(F32)).astype(BF16)
    tot = jnp.dot(hi, g, preferred_element_type=F32) + jnp.dot(lo, g, preferred_element_type=F32)
    return tot * (1.0 / HEAD_DIM)


def _row_mod(mod_ref, rows_are_ctx, j):
    return jnp.where(rows_are_ctx, mod_ref[0, 1, j:j + 1, :], mod_ref[0, 0, j:j + 1, :])


def _mod_kernel(c_ref, w_ref, b_ref, o_ref):
    s = _silu(c_ref[...])
    o_ref[0] = jnp.dot(s.astype(BF16), w_ref[0].astype(BF16), preferred_element_type=F32) + b_ref[0]


def _modulation(c8, w_mod, b_mod):
    L, D, W = w_mod.shape
    tn = 1536
    return pl.pallas_call(
        _mod_kernel,
        grid=(L, W // tn),
        in_specs=[pl.BlockSpec((8, D), lambda l, j: (0, 0)),
                  pl.BlockSpec((1, D, tn), lambda l, j: (l, 0, j)),
                  pl.BlockSpec((1, 1, tn), lambda l, j: (l, 0, j))],
        out_specs=pl.BlockSpec((1, 8, tn), lambda l, j: (l, 0, j)),
        out_shape=jax.ShapeDtypeStruct((L, 8, W), F32),
        compiler_params=_params(("arbitrary", "arbitrary")),
        name="modulation",
    )(c8, w_mod, b_mod.reshape(L, 1, W))


def _lane_block_ops():
    da_s = DA_QK ** -0.5 * LOG2E
    s64 = HEAD_DIM ** -0.5
    ops = []
    ops += [("r32", da_s)] * 2 + [("r32", 1.0)] * 2 + [(None, 1.0)] * 2
    ops += [("r64", s64 * LOG2E)] * 2 + [("r64", 1.0)] + [(None, 1.0)]
    ops += [(None, s64 * LOG2E)] * 2 + [(None, 1.0)] * 4
    ops += [("r64", 1.0)] * 2 + [("r64", s64)] * 2 + [(None, 1.0)] * 4
    return ops


def _rope(x, cos, sin_signed, w):
    lane = lax.broadcasted_iota(jnp.int32, x.shape, 1)
    first = (lane % (2 * w)) < w
    xr = jnp.where(first, pltpu.roll(x, LANES - w, 1), pltpu.roll(x, w, 1))
    return x * cos + xr * sin_signed


def _inproj_kernel(x_ref, mod_ref, g_ref, w_ref, c32_ref, s32_ref, c64_ref, s64_ref, o_ref, qt_ref, vt_ref,
                   *, tm, seq):
    i = pl.program_id(1)
    x = x_ref[0]
    rows = i * tm + lax.broadcasted_iota(jnp.int32, (tm, 1), 0)
    is_ctx = rows >= seq
    ms = jnp.mean(x * x, axis=-1, keepdims=True)
    h = x * lax.rsqrt(ms + EPS) * g_ref[...]
    h = h * (1.0 + _row_mod(mod_ref, is_ctx, 1)) + _row_mod(mod_ref, is_ctx, 0)
    hb = h.astype(BF16)
    ops = _lane_block_ops()
    for cb in range(IN_WIDTH // 256):
        p = jnp.dot(hb, w_ref[:, cb * 256:(cb + 1) * 256], preferred_element_type=F32)
        halves = []
        for hf in range(2):
            kind, scale = ops[cb * 2 + hf]
            ph = p[:, hf * LANES:(hf + 1) * LANES]
            if kind == "r32":
                ph = _rope(ph, c32_ref[...], s32_ref[...], DA_QK // 4)
            elif kind == "r64":
                ph = _rope(ph, c64_ref[...], s64_ref[...], HEAD_DIM // 4)
            if scale != 1.0:
                ph = ph * scale
            halves.append(ph)
        full = jnp.concatenate(halves, axis=1)
        o_ref[0, :, cb * 256:(cb + 1) * 256] = full.astype(BF16)
        if cb == CB_DA_Q:
            qt_ref[0] = full.T.astype(BF16)
        elif cb == CB_DA_V:
            vt_ref[0] = full.T.astype(BF16)


def _inproj(xa, modv, g, w_in_b, tabs, seq, tm):
    B, T, D = xa.shape
    kern = functools.partial(_inproj_kernel, tm=tm, seq=seq)
    tab_spec = pl.BlockSpec((tm, LANES), lambda b, i: (i, 0))
    return pl.pallas_call(
        kern,
        grid=(B, T // tm),
        in_specs=[pl.BlockSpec((1, tm, D), lambda b, i: (b, i, 0)),
                  pl.BlockSpec((1, 2, 8, D), lambda b, i: (b, 0, 0, 0)),
                  pl.BlockSpec((1, D), lambda b, i: (0, 0)),
                  pl.BlockSpec((D, IN_WIDTH), lambda b, i: (0, 0)),
                  tab_spec, tab_spec, tab_spec, tab_spec],
        out_specs=[pl.BlockSpec((1, tm, IN_WIDTH), lambda b, i: (b, i, 0)),
                   pl.BlockSpec((1, GROUP_W, tm), lambda b, i: (b, 0, i)),
                   pl.BlockSpec((1, GROUP_W, tm), lambda b, i: (b, 0, i))],
        out_shape=[jax.ShapeDtypeStruct((B, T, IN_WIDTH), BF16),
                   jax.ShapeDtypeStruct((B, GROUP_W, T), BF16),
                   jax.ShapeDtypeStruct((B, GROUP_W, T), BF16)],
        compiler_params=_params(("arbitrary", "arbitrary")),
        name="inproj",
    )(xa, modv, g.reshape(1, D), w_in_b, *tabs)


def _rope_tables(seq, ctx):
    t = jnp.arange(seq)
    row = (t // GRID_W).astype(F32)
    col = (t % GRID_W).astype(F32)
    out = []
    for dh in (DA_QK, HEAD_DIM):
        half = dh // 2
        quarter = half // 2
        lane = jnp.arange(LANES)
        d = lane % dh
        use_col = (d // half) == 1
        idx = (d % quarter).astype(F32)
        inv = ROPE_BASE ** (-idx * 2.0 / half)
        pos = jnp.where(use_col[None, :], col[:, None], row[:, None])
        ang = pos * inv[None, :]
        first = (d % half) < quarter
        cos = jnp.cos(ang)
        sin = jnp.where(first[None, :], -jnp.sin(ang), jnp.sin(ang))
        cos = jnp.concatenate([cos, jnp.ones((ctx, LANES), F32)], axis=0)
        sin = jnp.concatenate([sin, jnp.zeros((ctx, LANES), F32)], axis=0)
        out += [cos, sin]
    return out


def _rows_per_head(vals, tq):
    return jnp.concatenate([jnp.broadcast_to(v, (HEAD_DIM, tq)) for v in vals], axis=0)


def _da_kernel(lamp_ref, qt_ref, k_ref, vt_ref, g_ref, o_ref, qm_sc, m_sc, l_sc, acc_sc, st_sc, p_sc,
               *, tq, lambda_init):
    ki = pl.program_id(2)
    nk = pl.num_programs(2)

    @pl.when(ki == 0)
    def _():
        qt = qt_ref[0]
        for j in range(2 * N_HEADS):
            qm = jnp.where(_head_mask(qt.shape, DA_QK, j, 0), qt, jnp.zeros_like(qt))
            qm_sc[j] = qm
            m_sc[j] = jnp.max(jnp.dot(k_ref[0, 0:DA_CHUNK, :], qm, preferred_element_type=F32), axis=0,
                              keepdims=True)
        l_sc[...] = jnp.zeros(l_sc.shape, F32)
        acc_sc[...] = jnp.zeros(acc_sc.shape, F32)

    k = k_ref[0]
    vt = vt_ref[0]
    ones = jnp.ones((ONES_ROWS, vt.shape[1]), BF16)

    tk = k.shape[0]

    def vt_ext(h):
        return jnp.concatenate([vt[h * HEAD_DIM:(h + 1) * HEAD_DIM, :], ones], axis=0)

    jump = None
    pvs, mxs = [], []
    for j in range(2 * N_HEADS):
        st = jnp.dot(k, qm_sc[j], preferred_element_type=F32)
        m_prev = m_sc[j]
        mx = jnp.max(st, axis=0, keepdims=True)
        p = jnp.exp2(st - m_prev).astype(BF16)
        pvs.append(jnp.dot(vt_ext(j // 2), p, preferred_element_type=F32))
        mxs.append(mx)
        jump = mx - m_prev if jump is None else jnp.maximum(jump, mx - m_prev)
    keep = jnp.max(jump) <= DA_MAX_JUMP
    for j in range(2 * N_HEADS):
        h, t = j // 2, j % 2
        rows = slice(h * HEAD_DIM, (h + 1) * HEAD_DIM)
        m_prev = m_sc[j]
        m_new = jnp.maximum(m_prev, mxs[j])
        alpha = jnp.exp2(m_prev - m_new)
        l_old = l_sc[j]
        acc_old = acc_sc[t, rows, :]
        l_sc[j] = jnp.where(keep, alpha * (l_old + pvs[j][HEAD_DIM:HEAD_DIM + 1, :]), l_old)
        acc_sc[t, rows, :] = jnp.where(keep, alpha * (acc_old + pvs[j][:HEAD_DIM, :]), acc_old)
        m_sc[j] = jnp.where(keep, m_new, m_prev)

    @pl.when(jnp.logical_not(keep))
    def _():
        for j in range(2 * N_HEADS):
            h, t = j // 2, j % 2
            rows = slice(h * HEAD_DIM, (h + 1) * HEAD_DIM)
            st_sc[...] = jnp.dot(k, qm_sc[j], preferred_element_type=F32)
            run = st_sc[0:DA_CHUNK, :]
            for c in range(1, tk // DA_CHUNK):
                run = jnp.maximum(run, st_sc[c * DA_CHUNK:(c + 1) * DA_CHUNK, :])
            m_prev = m_sc[j]
            m_new = jnp.maximum(m_prev, jnp.max(run, axis=0, keepdims=True))
            alpha = jnp.exp2(m_prev - m_new)
            for c in range(tk // DA_CHUNK):
                cr = slice(c * DA_CHUNK, (c + 1) * DA_CHUNK)
                p_sc[cr, :] = jnp.exp2(st_sc[cr, :] - m_new).astype(BF16)
            pv = jnp.dot(vt_ext(h), p_sc[...], preferred_element_type=F32)
            l_sc[j] = alpha * l_sc[j] + pv[HEAD_DIM:HEAD_DIM + 1, :]
            m_sc[j] = m_new
            acc_sc[t, rows, :] = alpha * acc_sc[t, rows, :] + pv[:HEAD_DIM, :]

    @pl.when(ki == nk - 1)
    def _():
        lp = lamp_ref[...]
        lam = (jnp.exp(jnp.sum(lp[0:1] * lp[1:2], axis=1, keepdims=True))
               - jnp.exp(jnp.sum(lp[2:3] * lp[3:4], axis=1, keepdims=True))) + lambda_init
        o0 = acc_sc[0] / _rows_per_head([l_sc[2 * h] for h in range(N_HEADS)], tq)
        o1 = acc_sc[1] / _rows_per_head([l_sc[2 * h + 1] for h in range(N_HEADS)], tq)
        a = (o0 - lam * o1).T
        y = a * lax.rsqrt(_group_mean_sq(a) + EPS) * g_ref[...]
        o_ref[0] = (y * (1.0 - lambda_init)).astype(BF16)


def _diff_attention(P, QT, VT, lamp, subln_full, lambda_init, q0, nq, tq, k0, nk, tk):
    B = P.shape[0]
    kern = functools.partial(_da_kernel, tq=tq, lambda_init=lambda_init)
    return pl.pallas_call(
        kern,
        grid=(B, nq, nk),
        in_specs=[pl.BlockSpec((8, LANES), lambda b, qi, ki: (0, 0)),
                  pl.BlockSpec((1, GROUP_W, tq), lambda b, qi, ki: (b, 0, q0 + qi)),
                  pl.BlockSpec((1, tk, GROUP_W), lambda b, qi, ki: (b, k0 + ki, CB_DA_K)),
                  pl.BlockSpec((1, GROUP_W, tk), lambda b, qi, ki: (b, 0, k0 + ki)),
                  pl.BlockSpec((1, GROUP_W), lambda b, qi, ki: (0, 0))],
        out_specs=pl.BlockSpec((1, tq, GROUP_W), lambda b, qi, ki: (b, qi, 0)),
        out_shape=jax.ShapeDtypeStruct((B, nq * tq, GROUP_W), BF16),
        scratch_shapes=[pltpu.VMEM((2 * N_HEADS, GROUP_W, tq), BF16),
                        pltpu.VMEM((2 * N_HEADS, 1, tq), F32),
                        pltpu.VMEM((2 * N_HEADS, 1, tq), F32),
                        pltpu.VMEM((2, GROUP_W, tq), F32),
                        pltpu.VMEM((tk, tq), F32),
                        pltpu.VMEM((tk, tq), BF16)],
        compiler_params=_params(("arbitrary", "arbitrary", "arbitrary")),
        name="diff_attention",
    )(lamp, QT, P, VT, subln_full)


def _swa_kernel(sink_ref, q_ref, kp_ref, kc_ref, kn_ref, kx_ref, vp_ref, vc_ref, vn_ref, vx_ref, o_ref,
                *, seq, blk):
    i = pl.program_id(0)
    nloc = blk + 2 * SWA_WINDOW
    nkeys = nloc + kx_ref.shape[1]
    qpos = i * blk + lax.broadcasted_iota(jnp.int32, (blk, nkeys), 0)
    c = lax.broadcasted_iota(jnp.int32, (blk, nkeys), 1)
    kpos = i * blk - SWA_WINDOW + c
    valid = ((kpos >= 0) & (kpos < seq) & (jnp.abs(qpos - kpos) <= SWA_WINDOW) & (qpos < seq)) | (c >= nloc)
    for b in range(q_ref.shape[0]):
        q = q_ref[b]
        kall = jnp.concatenate([kp_ref[b], kc_ref[b], kn_ref[b], kx_ref[b]], axis=0)
        vall = jnp.concatenate([vp_ref[b], vc_ref[b], vn_ref[b], vx_ref[b]], axis=0)
        qs = []
        for g in range(2):
            qg = q[:, g * LANES:(g + 1) * LANES]
            for kvh in range(2):
                qs.append(jnp.where(_head_mask(qg.shape, HEAD_DIM, kvh), qg, jnp.zeros_like(qg)))
        s = lax.dot_general(jnp.concatenate(qs, axis=0), kall, (((1,), (1,)), ((), ())),
                            preferred_element_type=F32)
        vms = [jnp.where(_head_mask(vall.shape, HEAD_DIM, kvh), vall, jnp.zeros_like(vall)) for kvh in range(2)]
        vm = jnp.concatenate(vms, axis=0)
        for g in range(2):
            ps, invs = [], []
            for kvh in range(2):
                j = 2 * g + kvh
                sink = sink_ref[2 * kvh + g] * LOG2E
                sj = jnp.where(valid, s[j * blk:(j + 1) * blk], NEG)
                m = jnp.maximum(jnp.max(sj, axis=1, keepdims=True), sink)
                e = jnp.exp2(sj - m)
                den = jnp.sum(e, axis=1, keepdims=True) + jnp.exp2(sink - m)
                ps.append(e.astype(BF16))
                invs.append(jnp.broadcast_to(1.0 / den, (blk, LANES)))
            out = jnp.dot(jnp.concatenate(ps, axis=1), vm, preferred_element_type=F32)
            o_ref[b, :, g * LANES:(g + 1) * LANES] = (out * _lane_pick(invs[0], invs[1])).astype(BF16)


def _na_kernel(q_ref, k0_ref, k1_ref, k2_ref, kx_ref, v0_ref, v1_ref, v2_ref, vx_ref, bias_ref, o_ref, *, tq):
    nloc = 3 * tq
    for b in range(q_ref.shape[0]):
        q = q_ref[b]
        kall = jnp.concatenate([k0_ref[b], k1_ref[b], k2_ref[b], kx_ref[b]], axis=0)
        vall = jnp.concatenate([v0_ref[b], v1_ref[b], v2_ref[b], vx_ref[b]], axis=0)
        qs = jnp.concatenate(
            [jnp.where(_head_mask(q.shape, HEAD_DIM, h), q, jnp.zeros_like(q)) for h in range(N_HEADS)], axis=0)
        s = lax.dot_general(qs, kall, (((1,), (1,)), ((), ())), preferred_element_type=F32)
        ps, invs = [], []
        for h in range(N_HEADS):
            sh = s[h * tq:(h + 1) * tq]
            s_loc = sh[:, :nloc] + bias_ref[0, h]
            s_ctx = sh[:, nloc:]
            m = jnp.maximum(jnp.max(s_loc, axis=1, keepdims=True), jnp.max(s_ctx, axis=1, keepdims=True))
            e_loc = jnp.exp2(s_loc - m)
            e_ctx = jnp.exp2(s_ctx - m)
            den = jnp.sum(e_loc, axis=1, keepdims=True) + jnp.sum(e_ctx, axis=1, keepdims=True)
            ps.append(jnp.concatenate([e_loc, e_ctx], axis=1).astype(BF16))
            invs.append(jnp.broadcast_to(1.0 / den, (tq, LANES)))
        vm = jnp.concatenate(
            [jnp.where(_head_mask(vall.shape, HEAD_DIM, h), vall, jnp.zeros_like(vall)) for h in range(N_HEADS)],
            axis=0)
        out = jnp.dot(jnp.concatenate(ps, axis=1), vm, preferred_element_type=F32)
        o_ref[b] = (out * _per_head_full(invs)).astype(BF16)


def _na_bias_kernel(rt_ref, o_ref, sv_sc, *, rows):
    tr = NA_TILE_ROWS
    nrt = rows // tr
    nl = 3 * tr * GRID_W
    n_dcol = 2 * NA_WIN_COLS - 1
    rt = rt_ref[0] * LOG2E
    hi = rt.astype(BF16)
    r1 = rt - hi.astype(F32)
    mid = r1.astype(BF16)
    lo = (r1 - mid.astype(F32)).astype(BF16)
    lane = lax.broadcasted_iota(jnp.int32, (GRID_W, nl), 1)
    qc = lax.broadcasted_iota(jnp.int32, (GRID_W, nl), 0)
    kc = lane % GRID_W
    kr = lane // GRID_W
    dc = jnp.clip(kc - qc, 1 - NA_WIN_COLS, NA_WIN_COLS - 1) + (NA_WIN_COLS - 1)
    c0 = jnp.clip(qc - NA_WIN_COLS // 2, 0, GRID_W - NA_WIN_COLS)
    col_ok = (kc >= c0) & (kc < c0 + NA_WIN_COLS)
    irow = lax.broadcasted_iota(jnp.int32, (LANES, nl), 0)
    krl = lax.broadcasted_iota(jnp.int32, (LANES, nl), 1) // GRID_W
    for v, (t_idx, ws) in enumerate(((0, 0), (1, 0), (nrt - 1, nrt - 3))):
        for qr in range(tr):
            r = t_idx * tr + qr
            r0 = min(max(r - NA_WIN_ROWS // 2, 0), rows - NA_WIN_ROWS)
            d_row = jnp.clip(ws * tr + krl - r + (NA_WIN_ROWS - 1), 0, 2 * NA_WIN_ROWS - 2)
            onehot = jnp.where(irow == d_row, 1.0, 0.0).astype(BF16)
            sv_sc[...] = (jnp.dot(hi, onehot, preferred_element_type=F32)
                          + jnp.dot(mid, onehot, preferred_element_type=F32)
                          + jnp.dot(lo, onehot, preferred_element_type=F32))

            def pick(j, acc):
                return jnp.where(dc == j, sv_sc[pl.ds(j, 1), :], acc)

            acc = lax.fori_loop(0, n_dcol, pick, jnp.zeros((GRID_W, nl), F32), unroll=True)
            krow = ws * tr + kr
            ok = col_ok & (krow >= r0) & (krow < r0 + NA_WIN_ROWS)
            o_ref[v, 0, qr * GRID_W:(qr + 1) * GRID_W, :] = jnp.where(ok, acc, NEG)
    o_ref[3, 0] = jnp.full((tr * GRID_W, nl), NEG, F32)


def _na_bias_tables(rpb, rows):
    H, nr, ncol = rpb.shape
    rt = jnp.zeros((H, 32, LANES), F32).at[:, :ncol, :nr].set(jnp.swapaxes(rpb.astype(F32), 1, 2))
    tq = NA_TILE_ROWS * GRID_W
    return pl.pallas_call(
        functools.partial(_na_bias_kernel, rows=rows),
        grid=(H,),
        in_specs=[pl.BlockSpec((1, 32, LANES), lambda h: (h, 0, 0))],
        out_specs=pl.BlockSpec((4, 1, tq, 3 * tq), lambda h: (0, h, 0, 0)),
        out_shape=jax.ShapeDtypeStruct((4, H, tq, 3 * tq), F32),
        scratch_shapes=[pltpu.VMEM((32, 3 * tq), F32)],
        compiler_params=_params(("arbitrary",)),
        name="na_bias",
    )(rt)


def _ret_direction(q, k, v, state_ref, lgs_ref, lgv, forward, base):
    C = q.shape[0]
    i = lax.broadcasted_iota(jnp.int32, (C, C), 0)
    j = lax.broadcasted_iota(jnp.int32, (C, C), 1)
    dist = (i - j if forward else j - i).astype(F32)
    keep = dist >= 0 if forward else dist > 0
    dist = jnp.where(keep, dist, 0.0)
    qs = jnp.concatenate(
        [jnp.where(_head_mask(q.shape, HEAD_DIM, h), q, jnp.zeros_like(q)) for h in range(N_HEADS)], axis=0)
    s = lax.dot_general(qs, k, (((1,), (1,)), ((), ())), preferred_element_type=F32)
    atts = []
    for h in range(N_HEADS):
        decay = jnp.where(keep, jnp.exp(dist * lgs_ref[base + h]), 0.0)
        atts.append((s[h * C:(h + 1) * C] * decay).astype(BF16))
    vm = jnp.concatenate(
        [jnp.where(_head_mask(v.shape, HEAD_DIM, h), v, jnp.zeros_like(v)) for h in range(N_HEADS)], axis=0)
    intra = jnp.dot(jnp.concatenate(atts, axis=1), vm, preferred_element_type=F32)
    r = lax.broadcasted_iota(jnp.int32, (C, 1), 0).astype(F32)
    xi = jnp.exp((r + 1.0 if forward else C - r) * lgv)
    zeta = jnp.exp((C - 1.0 - r if forward else r) * lgv)
    state = state_ref[...]
    cross = jnp.dot((q.astype(F32) * xi).astype(BF16), state.astype(BF16), preferred_element_type=F32)
    kz_t = (k.astype(F32) * zeta).T.astype(BF16)
    u = jnp.dot(kz_t, v, preferred_element_type=F32)
    rr = lax.broadcasted_iota(jnp.int32, u.shape, 0) // HEAD_DIM
    cc = lax.broadcasted_iota(jnp.int32, u.shape, 1) // HEAD_DIM
    state_ref[...] = jnp.where(rr == cc, jnp.exp(C * lgv) * state + u, 0.0)
    return intra + cross


def _ret_kernel(lgs_ref, lgv_ref, qf_ref, kf_ref, vf_ref, qb_ref, kb_ref, vb_ref, of_ref, ob_ref, sf_sc, sb_sc):
    @pl.when(pl.program_id(0) == 0)
    def _():
        sf_sc[...] = jnp.zeros(sf_sc.shape, F32)
        sb_sc[...] = jnp.zeros(sb_sc.shape, F32)

    for b in range(qf_ref.shape[0]):
        of_ref[b] = _ret_direction(qf_ref[b], kf_ref[b], vf_ref[b], sf_sc.at[b], lgs_ref, lgv_ref[0:1], True, 0)
        ob_ref[b] = _ret_direction(qb_ref[b], kb_ref[b], vb_ref[b], sb_sc.at[b], lgs_ref, lgv_ref[1:2], False,
                                   N_HEADS)


def _local_kernel(sink_ref, lgs_ref, lgv_ref, *refs, seq, tile):
    swa_in, na_in, ret_in = refs[0:9], refs[9:19], refs[19:25]
    yb_ref, yn_ref, of_ref, ob_ref, sf_sc, sb_sc = refs[25:]
    _swa_kernel(sink_ref, *swa_in, yb_ref, seq=seq, blk=tile)
    _na_kernel(*na_in, yn_ref, tq=tile)
    _ret_kernel(lgs_ref, lgv_ref, *ret_in, of_ref, ob_ref, sf_sc, sb_sc)


def _local_mixers(P, sink, bias_tabs, lgs, lgv, seq, ctx):
    B, T, _ = P.shape
    tile = LOCAL_TILE
    W = SWA_WINDOW
    nt, nst, nw = T // tile, seq // tile, T // W
    nc = ctx // tile
    kern = functools.partial(_local_kernel, seq=seq, tile=tile)

    def rows(cb, width=GROUP_W, index=lambda i: i, size=tile):
        return pl.BlockSpec((B, size, width), lambda i: (0, index(i), cb))

    def ctx_rows(cb, width):
        return pl.BlockSpec((B, ctx, width), lambda i: (0, seq // ctx, cb))

    before = lambda i: jnp.clip(i * (tile // W) - 1, 0, nw - 1)
    after = lambda i: jnp.clip((i + 1) * (tile // W), 0, nw - 1)
    swa = [rows(CB_SWA_Q)]
    for cb in (CB_SWA_K128, CB_SWA_V128):
        swa += [rows(cb, LANES, before, W), rows(cb, LANES), rows(cb, LANES, after, W), ctx_rows(cb, LANES)]

    win = lambda i: jnp.clip(i - 1, 0, nst - 3)
    na = [rows(CB_NA_Q)]
    for cb in (CB_NA_K, CB_NA_V):
        na += [rows(cb, index=lambda i, o=o: win(i) + o) for o in range(3)] + [ctx_rows(cb, GROUP_W)]
    na.append(pl.BlockSpec((1, N_HEADS, tile, 3 * tile), lambda i: (jnp.where(i >= nst, 3, i - win(i)), 0, 0, 0)))

    fwd = lambda n: jnp.where(n < nc, nst + n, n - nc)
    bwd = lambda n: nt - 1 - n
    ret = [rows(cb, index=order) for order in (fwd, bwd) for cb in (CB_RET_Q, CB_RET_K, CB_RET_V)]

    tok = lambda order: pl.BlockSpec((B, tile, GROUP_W), lambda i: (0, order(i), 0))
    ident = lambda i: i
    return pl.pallas_call(
        kern,
        grid=(nt,),
        in_specs=[pl.BlockSpec(memory_space=pltpu.SMEM), pl.BlockSpec(memory_space=pltpu.SMEM),
                  pl.BlockSpec((8, GROUP_W), lambda i: (0, 0))] + swa + na + ret,
        out_specs=[tok(ident), tok(ident), tok(fwd), tok(bwd)],
        out_shape=[jax.ShapeDtypeStruct((B, T, GROUP_W), BF16)] * 2 + [jax.ShapeDtypeStruct((B, T, GROUP_W), F32)] * 2,
        scratch_shapes=[pltpu.VMEM((B, GROUP_W, GROUP_W), F32)] * 2,
        compiler_params=_params(("arbitrary",)),
        name="local_mixers",
    )(sink, lgs, lgv, *([P] * 9), *([P] * 9), bias_tabs, *([P] * 6))


def _outproj_kernel(*refs, tm, seq, n_exp):
    with_router = n_exp > 0
    if with_router:
        (ya_ref, yb_ref, yn_ref, of_ref, ob_ref, gt_ref, x_ref, mod_ref, w_ref, gpost_ref, gpre_ref, r_ref,
         x1_ref, h2_ref, eid_ref, gw_ref) = refs
    else:
        (ya_ref, yb_ref, yn_ref, of_ref, ob_ref, gt_ref, x_ref, mod_ref, w_ref, gpost_ref, gpre_ref,
         x1_ref, h2_ref) = refs
    i = pl.program_id(1)
    rows = i * tm + lax.broadcasted_iota(jnp.int32, (tm, 1), 0)
    is_ctx = rows >= seq
    o = of_ref[0] + ob_ref[0]
    yr = o * lax.rsqrt(_group_mean_sq(o) + EPS) * _silu(gt_ref[0].astype(F32))
    y = (jnp.dot(ya_ref[0], w_ref[0:256, :], preferred_element_type=F32)
         + jnp.dot(yb_ref[0], w_ref[256:512, :], preferred_element_type=F32)
         + jnp.dot(yn_ref[0], w_ref[512:768, :], preferred_element_type=F32)
         + jnp.dot(yr.astype(BF16), w_ref[768:1024, :], preferred_element_type=F32))
    yn = y * lax.rsqrt(jnp.mean(y * y, axis=-1, keepdims=True) + EPS) * gpost_ref[...]
    x1 = x_ref[0] + _row_mod(mod_ref, is_ctx, 2) * yn
    x1_ref[0] = x1
    h = x1 * lax.rsqrt(jnp.mean(x1 * x1, axis=-1, keepdims=True) + EPS) * gpre_ref[...]
    h = h * (1.0 + _row_mod(mod_ref, is_ctx, 4)) + _row_mod(mod_ref, is_ctx, 3)
    h2_ref[0] = h.astype(h2_ref.dtype)
    if with_router:
        r = r_ref[...]
        r_hi = r.astype(BF16)
        r_lo = (r - r_hi.astype(F32)).astype(BF16)
        h_hi = h.astype(BF16)
        h_lo = (h - h_hi.astype(F32)).astype(BF16)
        dn = (((1,), (1,)), ((), ()))
        lt = (lax.dot_general(r_hi, h_hi, dn, preferred_element_type=F32)
              + lax.dot_general(r_hi, h_lo, dn, preferred_element_type=F32)
              + lax.dot_general(r_lo, h_hi, dn, preferred_element_type=F32))
        e = lax.broadcasted_iota(jnp.int32, lt.shape, 0)
        lt = jnp.where(e < n_exp, lt, NEG)
        v1 = jnp.max(lt, axis=0, keepdims=True)
        i1 = jnp.min(jnp.where(lt == v1, e, ROUTER_ROWS), axis=0, keepdims=True)
        rest = jnp.where(e == i1, NEG, lt)
        v2 = jnp.max(rest, axis=0, keepdims=True)
        i2 = jnp.min(jnp.where(rest == v2, e, ROUTER_ROWS), axis=0, keepdims=True)
        e2 = jnp.exp(v2 - v1)
        w1 = 1.0 / (1.0 + e2)
        w2 = e2 / (1.0 + e2)
        eid_ref[0] = jnp.where(e == 0, i1, jnp.where(e == 1, i2, 0))[:8]
        wrow = jnp.where(e == 0, w1, jnp.where(e == 1, w2, 0.0))
        gw_ref[0] = jnp.concatenate([wrow, jnp.zeros((LANES - ROUTER_ROWS, tm), F32)], axis=0).T


def _outproj(ya, yb, yn, of, ob, P, xa, modv, w_out_b, g_post, g_pre2, rows, seq, tm, router=None, n_exp=0):
    B, _, D = xa.shape
    with_router = router is not None
    kern = functools.partial(_outproj_kernel, tm=tm, seq=seq, n_exp=n_exp)
    tok = lambda w: pl.BlockSpec((1, tm, w), lambda b, i: (b, i, 0))
    in_specs = [tok(GROUP_W), tok(GROUP_W), tok(GROUP_W), tok(GROUP_W), tok(GROUP_W),
                pl.BlockSpec((1, tm, GROUP_W), lambda b, i: (b, i, CB_RET_G)),
                tok(D),
                pl.BlockSpec((1, 2, 8, D), lambda b, i: (b, 0, 0, 0)),
                pl.BlockSpec((D, D), lambda b, i: (0, 0)),
                pl.BlockSpec((1, D), lambda b, i: (0, 0)),
                pl.BlockSpec((1, D), lambda b, i: (0, 0))]
    args = [ya, yb, yn, of, ob, P, xa, modv, w_out_b, g_post.reshape(1, D), g_pre2.reshape(1, D)]
    out_specs = [tok(D), tok(D)]
    out_shape = [jax.ShapeDtypeStruct((B, rows, D), F32),
                 jax.ShapeDtypeStruct((B, rows, D), F32 if with_router else BF16)]
    if with_router:
        in_specs.append(pl.BlockSpec((ROUTER_ROWS, D), lambda b, i: (0, 0)))
        args.append(router)
        out_specs += [pl.BlockSpec((1, 8, tm), lambda b, i: (b, 0, i)), tok(LANES)]
        out_shape += [jax.ShapeDtypeStruct((B, 8, rows), jnp.int32),
                      jax.ShapeDtypeStruct((B, rows, LANES), F32)]
    return pl.pallas_call(
        kern,
        grid=(B, rows // tm),
        in_specs=in_specs,
        out_specs=out_specs,
        out_shape=out_shape,
        compiler_params=_params(("arbitrary", "arbitrary")),
        name="outproj_router" if with_router else "outproj",
    )(*args)


def _ffn_kernel(h_ref, x_ref, mod_ref, wg_ref, wu_ref, wd_ref, gpost_ref, o_ref, *, tm, seq, fc):
    i = pl.program_id(1)
    rows = i * tm + lax.broadcasted_iota(jnp.int32, (tm, 1), 0)
    is_ctx = rows >= seq
    h = h_ref[0]
    acc = jnp.zeros((tm, o_ref.shape[-1]), F32)
    for c in range(wg_ref.shape[1] // fc):
        g = jnp.dot(h, wg_ref[:, c * fc:(c + 1) * fc], preferred_element_type=F32)
        u = jnp.dot(h, wu_ref[:, c * fc:(c + 1) * fc], preferred_element_type=F32)
        a = (_silu(g) * u).astype(BF16)
        acc = acc + jnp.dot(a, wd_ref[c * fc:(c + 1) * fc, :], preferred_element_type=F32)
    yn = acc * lax.rsqrt(jnp.mean(acc * acc, axis=-1, keepdims=True) + EPS) * gpost_ref[...]
    o_ref[0] = x_ref[0] + _row_mod(mod_ref, is_ctx, 5) * yn


def _dense_ffn(h2, x1, modv, wg, wu, wd, g_post, seq, tm):
    B, T, D = x1.shape
    F = wg.shape[1]
    kern = functools.partial(_ffn_kernel, tm=tm, seq=seq, fc=256)
    const = lambda shape: pl.BlockSpec(shape, lambda b, i: (0, 0), pipeline_mode=pl.Buffered(1))
    return pl.pallas_call(
        kern,
        grid=(B, T // tm),
        in_specs=[pl.BlockSpec((1, tm, D), lambda b, i: (b, i, 0)),
                  pl.BlockSpec((1, tm, D), lambda b, i: (b, i, 0)),
                  pl.BlockSpec((1, 2, 8, D), lambda b, i: (b, 0, 0, 0)),
                  const((D, F)), const((D, F)), const((F, D)),
                  pl.BlockSpec((1, D), lambda b, i: (0, 0))],
        out_specs=pl.BlockSpec((1, tm, D), lambda b, i: (b, i, 0)),
        out_shape=jax.ShapeDtypeStruct((B, T, D), F32),
        compiler_params=_params(("arbitrary", "arbitrary")),
        name="dense_ffn",
    )(h2, x1, modv, wg, wu, wd, g_post.reshape(1, D))


def _row_copy(src_hbm, row, dst_vmem, r, sem):
    return pltpu.make_async_copy(src_hbm.at[pl.ds(row, 1)], dst_vmem.at[pl.ds(r, 1)], sem)


def _moe_ffn_kernel(te_ref, tv_ref, src0_ref, srcn_ref, h_hbm, wg_ref, wu_ref, wd_ref, o_ref,
                    xg_sc, xb_sc, acc_sc, sem, *, tr):
    t = pl.program_id(0)
    f = pl.program_id(1)
    nt = pl.num_programs(0)
    nf = pl.num_programs(1)
    slot = t % 2

    def start_gather(src_ref, s):
        def body(r, c):
            _row_copy(h_hbm, src_ref[0, 0, r], xg_sc.at[s], r, sem.at[s]).start(priority=1)
            return c

        lax.fori_loop(0, tr, body, 0, unroll=8)

    @pl.when((f == 0) & (t == 0) & (tv_ref[0] > 0))
    def _():
        start_gather(src0_ref, 0)

    @pl.when(f == 0)
    def _():
        acc_sc[...] = jnp.zeros(acc_sc.shape, F32)

    @pl.when((f == 0) & (tv_ref[t] > 0))
    def _():
        pltpu.make_async_copy(h_hbm.at[pl.ds(0, tr)], xg_sc.at[slot], sem.at[slot]).wait()
        xb_sc[...] = xg_sc[slot].astype(BF16)

    @pl.when((f == 0) & (t + 1 < nt) & (tv_ref[jnp.minimum(t + 1, nt - 1)] > 0))
    def _():
        start_gather(srcn_ref, 1 - slot)

    @pl.when(tv_ref[t] > 0)
    def _():
        x = xb_sc[...]
        g = jnp.dot(x, wg_ref[0], preferred_element_type=F32)
        u = jnp.dot(x, wu_ref[0], preferred_element_type=F32)
        a = (_silu(g) * u).astype(BF16)
        acc_sc[...] += jnp.dot(a, wd_ref[0], preferred_element_type=F32)

    @pl.when(f == nf - 1)
    def _():
        o_ref[...] = acc_sc[...]


def _moe_ffn(h_flat, src, tile_expert, tile_valid, wg, wu, wd, n_rows, tr, tf):
    D = h_flat.shape[1]
    F = wg.shape[2]
    nt = n_rows // tr
    return pl.pallas_call(
        functools.partial(_moe_ffn_kernel, tr=tr),
        grid_spec=pltpu.PrefetchScalarGridSpec(
            num_scalar_prefetch=2,
            grid=(nt, F // tf),
            in_specs=[pl.BlockSpec((1, 1, tr), lambda t, f, te, tv: (0, 0, 0), memory_space=pltpu.SMEM),
                      pl.BlockSpec((1, 1, tr), lambda t, f, te, tv: (jnp.minimum(t + 1, nt - 1), 0, 0),
                                   memory_space=pltpu.SMEM),
                      pl.BlockSpec(memory_space=pl.ANY),
                      pl.BlockSpec((1, D, tf), lambda t, f, te, tv: (te[t], 0, f)),
                      pl.BlockSpec((1, D, tf), lambda t, f, te, tv: (te[t], 0, f)),
                      pl.BlockSpec((1, tf, D), lambda t, f, te, tv: (te[t], f, 0))],
            out_specs=pl.BlockSpec((tr, D), lambda t, f, te, tv: (t, 0)),
            scratch_shapes=[pltpu.VMEM((2, tr, D), h_flat.dtype), pltpu.VMEM((tr, D), BF16),
                            pltpu.VMEM((tr, D), F32), pltpu.SemaphoreType.DMA((2,))]),
        out_shape=jax.ShapeDtypeStruct((n_rows, D), F32),
        compiler_params=_params(("arbitrary", "arbitrary")),
        name="moe_ffn",
    )(tile_expert, tile_valid, src.reshape(nt, 1, tr), src.reshape(nt, 1, tr), h_flat, wg, wu, wd)


def _combine_kernel(pos_ref, ys_hbm, gw_ref, x_ref, mod_ref, gpost_ref, o_ref, y1_sc, y2_sc, sem, *, tm, seq):
    def issue(r, c):
        _row_copy(ys_hbm, pos_ref[0, 0, r], y1_sc, r, sem.at[0]).start(priority=0)
        _row_copy(ys_hbm, pos_ref[0, 1, r], y2_sc, r, sem.at[1]).start(priority=1)
        return c

    lax.fori_loop(0, tm, issue, 0, unroll=8)
    pltpu.make_async_copy(ys_hbm.at[pl.ds(0, tm)], y1_sc, sem.at[0]).wait()
    pltpu.make_async_copy(ys_hbm.at[pl.ds(0, tm)], y2_sc, sem.at[1]).wait()
    rows = pl.program_id(1) * tm + lax.broadcasted_iota(jnp.int32, (tm, 1), 0)
    gw = gw_ref[0]
    y = gw[:, 0:1] * y1_sc[...] + gw[:, 1:2] * y2_sc[...]
    yn = y * lax.rsqrt(jnp.mean(y * y, axis=-1, keepdims=True) + EPS) * gpost_ref[...]
    o_ref[0] = x_ref[0] + _row_mod(mod_ref, rows >= seq, 5) * yn


def _moe_combine(pos, ys, gw, x1, modv, g_post, seq, tm):
    B, R, D = x1.shape
    ntile = R // tm
    kern = functools.partial(_combine_kernel, tm=tm, seq=seq)
    return pl.pallas_call(
        kern,
        grid=(B, ntile),
        in_specs=[pl.BlockSpec((1, TOP_K, tm), lambda b, i: (b * ntile + i, 0, 0), memory_space=pltpu.SMEM),
                  pl.BlockSpec(memory_space=pl.ANY),
                  pl.BlockSpec((1, tm, LANES), lambda b, i: (b, i, 0)),
                  pl.BlockSpec((1, tm, D), lambda b, i: (b, i, 0)),
                  pl.BlockSpec((1, 2, 8, D), lambda b, i: (b, 0, 0, 0)),
                  pl.BlockSpec((1, D), lambda b, i: (0, 0))],
        out_specs=pl.BlockSpec((1, tm, D), lambda b, i: (b, i, 0)),
        out_shape=jax.ShapeDtypeStruct((B, R, D), F32),
        scratch_shapes=[pltpu.VMEM((tm, D), F32), pltpu.VMEM((tm, D), F32), pltpu.SemaphoreType.DMA((2,))],
        compiler_params=_params(("arbitrary", "arbitrary")),
        name="moe_combine",
    )(pos.reshape(B * ntile, tm, TOP_K).transpose(0, 2, 1), ys, gw, x1, modv, g_post.reshape(1, D))


def _moe_routing(eid, n_exp, tr):
    N = eid.shape[0]
    e_flat = eid.reshape(-1)
    onehot = (e_flat[:, None] == jnp.arange(n_exp)[None, :]).astype(jnp.int32)
    rank = jnp.sum((jnp.cumsum(onehot, axis=0) - 1) * onehot, axis=1)
    counts = jnp.sum(onehot, axis=0)
    padded = ((counts + tr - 1) // tr) * tr
    ends = jnp.cumsum(padded)
    starts = ends - padded
    dest = starts[e_flat] + rank
    n_rows = (TOP_K * N // tr + n_exp) * tr
    src = jnp.zeros((n_rows,), jnp.int32).at[dest].set(jnp.arange(TOP_K * N, dtype=jnp.int32) // TOP_K)
    tile_start = jnp.arange(n_rows // tr, dtype=jnp.int32) * tr
    tile_expert = jnp.minimum(jnp.sum((tile_start[:, None] >= ends[None, :]).astype(jnp.int32), axis=1), n_exp - 1)
    tile_valid = (tile_start < ends[-1]).astype(jnp.int32)
    pos = dest.reshape(N, TOP_K).astype(jnp.int32)
    return src, pos, tile_expert, tile_valid, n_rows


def _swa_perm():
    idx = []
    for g in range(2):
        for kvh in range(2):
            idx += [(kvh * 2 + g) * HEAD_DIM + d for d in range(HEAD_DIM)]
    return jnp.array(idx, jnp.int32)


def kernel(x, c, ctx, c_ctx, w_mod, b_mod, g_attn_pre, g_attn_post, g_ffn_pre, g_ffn_post, w_in, w_out,
           da_lambda_q1, da_lambda_k1, da_lambda_q2, da_lambda_k2, da_subln, swa_sink, na_rpb,
           ret_gamma_fwd, ret_gamma_bwd, ffn_w_gate, ffn_w_up, ffn_w_down,
           moe_router, moe_w_gate, moe_w_up, moe_w_down):
    B, S, D = x.shape
    CTX = ctx.shape[1]
    T = S + CTX
    L = w_mod.shape[0]
    n_exp = moe_router.shape[-1]
    rows = S // GRID_W
    assert CTX == LOCAL_TILE and S % LOCAL_TILE == 0 and rows >= 3 * NA_TILE_ROWS, (S, CTX)
    assert B + 1 <= 8 and w_in.shape[-1] == IN_WIDTH and w_out.shape[1] == 4 * GROUP_W and D % LANES == 0
    assert n_exp <= ROUTER_ROWS and (TOP_K * B * S) % EXPERT_ROW_TILE == 0

    def token_tile(n_rows):
        return next(t for t in TOKEN_TILES if n_rows % t == 0)

    c8 = jnp.zeros((8, D), F32).at[:B].set(c).at[B].set(c_ctx)
    mod = _modulation(c8, w_mod, b_mod).reshape(L, 8, 6, D)
    mod = jnp.pad(mod, ((0, 0), (0, 0), (0, 2), (0, 0)))
    modv = jnp.stack([mod[:, :B], jnp.broadcast_to(mod[:, B:B + 1], (L, B, 8, D))], axis=2)

    tabs = _rope_tables(S, CTX)
    perm = _swa_perm()
    swa0 = CB_SWA_Q * 256
    xa = jnp.concatenate([x, ctx], axis=1)

    for l in range(L):
        lambda_init = 0.8 - 0.6 * math.exp(-0.3 * l)
        last = l == L - 1
        w_in_l = w_in[l]
        w_in_b = jnp.concatenate([w_in_l[:, :swa0], w_in_l[:, swa0:swa0 + 256][:, perm], w_in_l[:, swa0 + 256:]],
                                 axis=1).astype(BF16)
        w_out_l = w_out[l]
        w_out_b = jnp.concatenate([w_out_l[:256], w_out_l[256:512][perm], w_out_l[512:]], axis=0).astype(BF16)

        P, QT, VT = _inproj(xa, modv[l], g_attn_pre[l], w_in_b, tabs, S, token_tile(T))

        lamp = jnp.zeros((8, LANES), F32)
        for r, v in enumerate((da_lambda_q1[l], da_lambda_k1[l], da_lambda_q2[l], da_lambda_k2[l])):
            lamp = lamp.at[r, :DA_QK].set(v)
        subln_full = jnp.tile(da_subln[l], N_HEADS).reshape(1, GROUP_W)
        tq_da = next(t for t in (1024, 512, 256) if S % t == 0)
        tk_da = next(t for t in (768, 512, 256) if T % t == 0)
        ya = _diff_attention(P, QT, VT, lamp, subln_full, lambda_init, 0, S // tq_da, tq_da, 0, T // tk_da, tk_da)
        if not last:
            ya_ctx = _diff_attention(P, QT, VT, lamp, subln_full, lambda_init, S // CTX, 1, CTX, S // CTX, 1, CTX)
            ya = jnp.concatenate([ya, ya_ctx], axis=1)
        lg = jnp.stack([jax.nn.log_sigmoid(ret_gamma_fwd[l].astype(F32)),
                        jax.nn.log_sigmoid(ret_gamma_bwd[l].astype(F32))])
        lgv = jnp.zeros((8, GROUP_W), F32).at[:2].set(jnp.repeat(lg, HEAD_DIM, axis=1))
        yb, yn, of, ob = _local_mixers(P, swa_sink[l].astype(F32), _na_bias_tables(na_rpb[l], rows),
                                       lg.reshape(-1), lgv, S, CTX)

        R = S if last else T
        e = l // 2
        if l % 2 == 0:
            x1, h2 = _outproj(ya, yb, yn, of, ob, P, xa, modv[l], w_out_b, g_attn_post[l], g_ffn_pre[l],
                              R, S, token_tile(R))
            xa = _dense_ffn(h2, x1, modv[l], ffn_w_gate[e].astype(BF16), ffn_w_up[e].astype(BF16),
                            ffn_w_down[e].astype(BF16), g_ffn_post[l], S, token_tile(R))
        else:
            router = jnp.zeros((ROUTER_ROWS, D), F32).at[:n_exp].set(moe_router[e].T)
            x1, h2, eid, gw = _outproj(ya, yb, yn, of, ob, P, xa, modv[l], w_out_b, g_attn_post[l],
                                       g_ffn_pre[l], R, S, token_tile(R), router=router, n_exp=n_exp)
            tr = EXPERT_ROW_TILE
            src, pos, tile_expert, tile_valid, n_rows = _moe_routing(
                jnp.swapaxes(eid[:, :TOP_K, :], 1, 2).reshape(B * R, TOP_K), n_exp, tr)
            ys = _moe_ffn(h2.reshape(B * R, D), src, tile_expert, tile_valid, moe_w_gate[e].astype(BF16),
                          moe_w_up[e].astype(BF16), moe_w_down[e].astype(BF16), n_rows, tr,
                          next(t for t in EXPERT_F_TILES if moe_w_gate.shape[-1] % t == 0))
            xa = _moe_combine(pos, ys, gw, x1, modv[l], g_ffn_post[l], S,
                              next(t for t in (1024, 512, 256) if R % t == 0))
    return xa[:, :S]
```
